```python
import math
import jax
import jax.numpy as jnp
from jax import lax
import numpy as np

D_MODEL = 2048
BATCH = 16
SEQ = 256
DEPTH = 2
DEC_BATCH = 8
DEC_SEQ = 2048
PAST_LEN = 256

GRID_W = 64
N_EVEN = (DEPTH + 1) // 2
N_ODD = DEPTH // 2
Q_BLOCK = 128
ROPE_THETA = 10000.0
NORM_EPS = 1e-6

A_HEADS = 8
A_KV_HEADS = 2
A_HEAD_DIM = 128
A_Q = A_HEADS * A_HEAD_DIM
A_KV = A_KV_HEADS * A_HEAD_DIM
A_IN = A_Q + 2 * A_KV
A_OUT = A_Q

B_HEADS = 16
B_HEAD_DIM = 64
B_W = B_HEADS * B_HEAD_DIM
B_DECAY_LORA = 64
B_AAA_LORA = 64
B_GATE_LORA = 128
B_IN = 3 * B_W + 2 * B_DECAY_LORA + 2 * B_AAA_LORA + B_GATE_LORA
B_GN_EPS = 64e-5

C_HEADS = 8
C_HEAD_DIM = 128
C_W = C_HEADS * C_HEAD_DIM
C_CHUNK = 64
C_IN = 4 * C_W + 4 * C_HEADS

D_HEADS = 8
D_NOPE = 128
D_ROPE = 64
D_V = 128
D_QK = D_NOPE + D_ROPE
D_KV_RANK = 512
D_Q = D_HEADS * D_QK
D_IN = D_Q + D_KV_RANK + D_ROPE
D_OUT = D_HEADS * D_V

AB_IN = A_IN + B_IN
AB_OUT = A_OUT + B_W
CD_IN = C_IN + D_IN
CD_OUT = C_W + D_OUT

PEER_HEADS = 8
N_KEYS = 128
N_EXPERTS = N_KEYS * N_KEYS
PEER_QDIM = 128
PEER_HALF = PEER_QDIM // 2
PEER_TOPK = 16
PEER_BLOCK = 128

kernel_name = 'hybrid_prefix_diffusion_step'


def rms_norm(x, g):
    xf = x.astype(jnp.float32)
    y = xf * lax.rsqrt(jnp.mean(xf * xf, axis=-1, keepdims=True) + NORM_EPS)
    return (y * g.astype(jnp.float32)).astype(x.dtype)


def l2norm(x):
    return x * lax.rsqrt(jnp.sum(x * x, axis=-1, keepdims=True) + 1e-6)


def shift_prev(x):
    return jnp.pad(x, ((0, 0), (1, 0), (0, 0)))[:, :-1]


def shift_next(x):
    return jnp.pad(x, ((0, 0), (0, 1), (0, 0)))[:, 1:]


def axial_rope(n, rot_dim):
    rows = n // GRID_W
    row = jnp.repeat(jnp.arange(rows, dtype=jnp.float32), GRID_W)
    col = jnp.tile(jnp.arange(GRID_W, dtype=jnp.float32), rows)
    n_freq = rot_dim // 4
    inv = ROPE_THETA ** (-jnp.arange(n_freq, dtype=jnp.float32) / n_freq)
    ang = jnp.concatenate([row[:, None] * inv, col[:, None] * inv], axis=-1)
    return jnp.cos(ang), jnp.sin(ang)


def apply_rope(x, cos, sin):
    xf = x.astype(jnp.float32).reshape(x.shape[:-1] + (-1, 2))
    x1, x2 = xf[..., 0], xf[..., 1]
    c, s = cos[None, :, None, :], sin[None, :, None, :]
    out = jnp.stack([x1 * c - x2 * s, x1 * s + x2 * c], axis=-1)
    return out.reshape(x.shape).astype(x.dtype)


def blocked_attention(q, k, v):
    bn, nq, h, dq = q.shape
    hkv, dv = k.shape[2], v.shape[-1]
    grp = h // hkv
    scale = dq ** -0.5
    kf, vf = k.astype(jnp.float32), v.astype(jnp.float32)
    qb = q.astype(jnp.float32).reshape(bn, nq // Q_BLOCK, Q_BLOCK, hkv, grp, dq).transpose(1, 0, 2, 3, 4, 5)

    def one_block(qi):
        s = jnp.einsum('bqkgd,bskd->bkgqs', qi, kf) * scale
        p = jax.nn.softmax(s, axis=-1)
        return jnp.einsum('bkgqs,bskd->bqkgd', p, vf)

    o = lax.map(one_block, qb)
    return o.transpose(1, 0, 2, 3, 4, 5).reshape(bn, nq, h, dv).astype(q.dtype)


def gqa_mixer(pa, q_norm_g, k_norm_g, ctx_kv):
    bn, t, _ = pa.shape
    q = rms_norm(pa[..., :A_Q].reshape(bn, t, A_HEADS, A_HEAD_DIM), q_norm_g)
    k = rms_norm(pa[..., A_Q:A_Q + A_KV].reshape(bn, t, A_KV_HEADS, A_HEAD_DIM), k_norm_g)
    v = pa[..., A_Q + A_KV:].reshape(bn, t, A_KV_HEADS, A_HEAD_DIM)
    if ctx_kv is None:
        return blocked_attention(q, k, v).reshape(bn, t, A_OUT), (k, v)
    cos, sin = axial_rope(t, A_HEAD_DIM)
    q, k_lat = apply_rope(q, cos, sin), apply_rope(k, cos, sin)
    k_all = jnp.concatenate([k_lat, ctx_kv[0]], axis=1)
    v_all = jnp.concatenate([v, ctx_kv[1]], axis=1)
    return blocked_attention(q, k_all, v_all).reshape(bn, t, A_OUT), None


def rwkv_scan(r, w, k, v, kk, a, s0):
    def step(s, inp):
        r_t, w_t, k_t, v_t, kk_t, a_t = inp
        s_kk = jnp.einsum('bhvk,bhk->bhv', s, kk_t)
        s = (s * w_t[:, :, None, :] - s_kk[..., None] * (kk_t * a_t)[:, :, None, :]
             + v_t[..., None] * k_t[:, :, None, :])
        return s, jnp.einsum('bhvk,bhk->bhv', s, r_t)

    xs = tuple(jnp.swapaxes(x, 0, 1) for x in (r, w, k, v, kk, a))
    s_fin, ys = lax.scan(step, s0, xs)
    return jnp.swapaxes(ys, 0, 1), s_fin


def rwkv_mixer(pb, mu, w0, w2, a0, a2, g2, k_k, k_a, r_k, ln_w, ln_b, s0):
    bn, t, _ = pb.shape
    f = pb.astype(jnp.float32)
    f = f + mu[0] * (shift_prev(f) - f) + mu[1] * (shift_next(f) - f)
    r, k, v = f[..., :B_W], f[..., B_W:2 * B_W], f[..., 2 * B_W:3 * B_W]
    off = 3 * B_W
    w_lo = f[..., off:off + 2 * B_DECAY_LORA].reshape(bn, t, 2, B_DECAY_LORA)
    off += 2 * B_DECAY_LORA
    a_lo = f[..., off:off + 2 * B_AAA_LORA].reshape(bn, t, 2, B_AAA_LORA)
    off += 2 * B_AAA_LORA
    g_lo = f[..., off:]
    w = -jax.nn.softplus(-(w0 + jnp.einsum('btdr,drc->btdc', jnp.tanh(w_lo), w2))) - 0.5
    decay = jnp.exp(-jnp.exp(w))
    a = jax.nn.sigmoid(a0 + jnp.einsum('btdr,drc->btdc', a_lo, a2))
    g = jax.nn.sigmoid(g_lo) @ g2

    def hd(x):
        return x.reshape(x.shape[:-1] + (B_HEADS, B_HEAD_DIM))

    kk = hd(k * k_k)
    kk = kk / jnp.maximum(jnp.sqrt(jnp.sum(kk * kk, axis=-1, keepdims=True)), 1e-12)
    k_dir = k[:, :, None, :] * (1.0 + (a - 1.0) * k_a)
    if s0 is None:
        s0 = jnp.zeros((bn, 2, B_HEADS, B_HEAD_DIM, B_HEAD_DIM), jnp.float32)
    outs, finals = [], []
    for d in range(2):
        seqs = (hd(r), hd(decay[:, :, d]), hd(k_dir[:, :, d]), hd(v), kk, hd(a[:, :, d]))
        if d == 1:
            seqs = tuple(jnp.flip(x, 1) for x in seqs)
        y_d, s_d = rwkv_scan(*seqs, s0[:, d].astype(jnp.float32))
        outs.append(jnp.flip(y_d, 1) if d == 1 else y_d)
        finals.append(s_d)
    y = outs[0] + outs[1]
    mean = jnp.mean(y, axis=-1, keepdims=True)
    var = jnp.var(y, axis=-1, keepdims=True)
    y = ((y - mean) * lax.rsqrt(var + B_GN_EPS)).reshape(bn, t, B_W) * ln_w + ln_b
    bonus = jnp.sum(hd(r) * hd(k) * r_k, axis=-1, keepdims=True) * hd(v)
    out = (y + bonus.reshape(bn, t, B_W)) * g
    return out.astype(pb.dtype), jnp.stack(finals, axis=1).astype(pb.dtype)


def chunk_gated_delta(q, k, v, beta, g, s0):
    bn, h, t, _ = q.shape
    dv = v.shape[-1]
    L = C_CHUNK
    n = t // L
    q, k, v = (x.reshape(bn, h, n, L, x.shape[-1]) for x in (q, k, v))
    beta = beta.reshape(bn, h, n, L)
    gc = jnp.cumsum(g.reshape(bn, h, n, L), axis=-1)
    incl = jnp.tril(jnp.ones((L, L), dtype=bool))
    strict = jnp.tril(jnp.ones((L, L), dtype=bool), k=-1)
    decay_mask = jnp.exp(jnp.where(incl, gc[..., :, None] - gc[..., None, :], -jnp.inf))
    kb = k * beta[..., None]
    a_mat = jnp.where(strict, jnp.einsum('bhnid,bhnjd->bhnij', kb, k) * decay_mask, 0.0)
    rhs = jnp.concatenate([v * beta[..., None], kb * jnp.exp(gc)[..., None]], axis=-1)
    sol = lax.linalg.triangular_solve(a_mat, rhs, left_side=True, lower=True, unit_diagonal=True)
    u, w = sol[..., :dv], sol[..., dv:]
    qk = jnp.einsum('bhnid,bhnjd->bhnij', q, k) * decay_mask

    def step(s, inp):
        q_c, k_c, u_c, w_c, gc_c, qk_c = inp
        v_new = u_c - jnp.einsum('bhlk,bhkv->bhlv', w_c, s)
        o = (jnp.einsum('bhlk,bhkv->bhlv', q_c * jnp.exp(gc_c)[..., None], s)
             + jnp.einsum('bhij,bhjv->bhiv', qk_c, v_new))
        g_end = gc_c[..., -1]
        s = (s * jnp.exp(g_end)[..., None, None]
             + jnp.einsum('bhlk,bhlv->bhkv', k_c * jnp.exp(g_end[..., None] - gc_c)[..., None], v_new))
        return s, o

    xs = tuple(jnp.moveaxis(x, 2, 0) for x in (q, k, u, w, gc, qk))
    s_fin, o = lax.scan(step, s0, xs)
    return jnp.moveaxis(o, 0, 2).reshape(bn, h, t, dv), s_fin


def gdn_mixer(pc, conv_w, a_log, dt_bias, norm_g, s0):
    bn, t, _ = pc.shape
    f = pc.astype(jnp.float32)
    qkv = f[..., :3 * C_W]
    qkv = jax.nn.silu(conv_w[0] * shift_prev(qkv) + conv_w[1] * qkv + conv_w[2] * shift_next(qkv))

    def hd(x):
        return x.reshape(x.shape[:-1] + (C_HEADS, C_HEAD_DIM))

    q = l2norm(hd(qkv[..., :C_W])) * (C_HEAD_DIM ** -0.5)
    k = l2norm(hd(qkv[..., C_W:2 * C_W]))
    v = hd(qkv[..., 2 * C_W:])
    z = hd(f[..., 3 * C_W:4 * C_W])
    off = 4 * C_W
    beta = jax.nn.sigmoid(f[..., off:off + 2 * C_HEADS].reshape(bn, t, 2, C_HEADS))
    off += 2 * C_HEADS
    g = -jnp.exp(a_log) * jax.nn.softplus(f[..., off:off + 2 * C_HEADS].reshape(bn, t, 2, C_HEADS) + dt_bias)
    if s0 is None:
        s0 = jnp.zeros((bn, 2, C_HEADS, C_HEAD_DIM, C_HEAD_DIM), jnp.float32)

    def bh(x):
        return jnp.swapaxes(x, 1, 2)

    outs, finals = [], []
    for d in range(2):
        seqs = (bh(q), bh(k), bh(v), bh(beta[:, :, d]), bh(g[:, :, d]))
        if d == 1:
            seqs = tuple(jnp.flip(x, 2) for x in seqs)
        o_d, s_d = chunk_gated_delta(*seqs, s0[:, d].astype(jnp.float32))
        outs.append(jnp.flip(o_d, 2) if d == 1 else o_d)
        finals.append(s_d)
    y = bh(outs[0] + outs[1])
    y = rms_norm(y, norm_g) * jax.nn.silu(z)
    return y.reshape(bn, t, C_W).astype(pc.dtype), jnp.stack(finals, axis=1).astype(pc.dtype)


def mla_mixer(pd, kv_norm_g, w_kv_b, ctx):
    bn, t, _ = pd.shape
    q = pd[..., :D_Q].reshape(bn, t, D_HEADS, D_QK)
    ckv = rms_norm(pd[..., D_Q:D_Q + D_KV_RANK], kv_norm_g)
    k_rope = pd[..., D_Q + D_KV_RANK:]

    def keys_values(ckv_, k_rope_):
        kv = (ckv_ @ w_kv_b).reshape(ckv_.shape[:2] + (D_HEADS, D_NOPE + D_V))
        k_nope, vv = kv[..., :D_NOPE], kv[..., D_NOPE:]
        k_r = jnp.broadcast_to(k_rope_[:, :, None, :], k_nope.shape[:3] + (D_ROPE,)).astype(k_nope.dtype)
        return jnp.concatenate([k_nope, k_r], axis=-1), vv

    if ctx is None:
        k, v = keys_values(ckv, k_rope)
        return blocked_attention(q, k, v).reshape(bn, t, D_OUT), (ckv, k_rope)
    cos, sin = axial_rope(t, D_ROPE)
    q = jnp.concatenate([q[..., :D_NOPE], apply_rope(q[..., D_NOPE:], cos, sin)], axis=-1)
    k_rope = apply_rope(k_rope[:, :, None, :], cos, sin)[:, :, 0]
    k, v = keys_values(ckv, k_rope)
    k_c, v_c = keys_values(ctx[0], ctx[1])
    out = blocked_attention(q, jnp.concatenate([k, k_c], axis=1), jnp.concatenate([v, v_c], axis=1))
    return out.reshape(bn, t, D_OUT), None


def peer_ffn(h, w_q, sub_keys, u_tab, v_tab):
    bn, t, d = h.shape
    n_tok = bn * t
    xf = h.reshape(n_tok, d)
    q = (xf @ w_q).astype(jnp.float32).reshape(n_tok, PEER_HEADS, 2, PEER_HALF)
    s = jnp.einsum('nhpd,hpkd->nhpk', q, sub_keys.astype(jnp.float32))
    s_top, i_top = lax.top_k(s, PEER_TOPK)
    cand = (s_top[:, :, 0, :, None] + s_top[:, :, 1, None, :]).reshape(n_tok, PEER_HEADS, PEER_TOPK * PEER_TOPK)
    cand_idx = (i_top[:, :, 0, :, None] * N_KEYS + i_top[:, :, 1, None, :]).reshape(n_tok, PEER_HEADS, PEER_TOPK * PEER_TOPK)
    best, pos = lax.top_k(cand, PEER_TOPK)
    idx = jnp.take_along_axis(cand_idx, pos, axis=-1)
    gate = jax.nn.softmax(best, axis=-1).astype(h.dtype)
    nb = n_tok // PEER_BLOCK
    hk = PEER_HEADS * PEER_TOPK

    def one_block(args):
        xb, ib, gb = args
        act = jax.nn.gelu(jnp.einsum('td,ted->te', xb, u_tab[ib]), approximate=False)
        return jnp.einsum('te,ted->td', act * gb, v_tab[ib])

    out = lax.map(one_block, (xf.reshape(nb, PEER_BLOCK, d), idx.reshape(nb, PEER_BLOCK, hk), gate.reshape(nb, PEER_BLOCK, hk)))
    return out.reshape(bn, t, d)


def modulation(cond, w_ada, b_ada):
    m = jax.nn.silu(cond) @ w_ada + b_ada
    return jnp.split(m[:, None, :], 6, axis=-1)


def modulate(x, g, shift, scale):
    return rms_norm(x, g) * (1.0 + scale) + shift


def trunk_layer(x, mod, norm1, norm2, w_in, w_out, mixer, ctx, peer_args):
    sh1, sc1, g1, sh2, sc2, g2 = mod
    mixed, ctx_out = mixer(modulate(x, norm1, sh1, sc1) @ w_in, ctx)
    x = x + g1 * (mixed @ w_out)
    x = x + g2 * peer_ffn(modulate(x, norm2, sh2, sc2), *peer_args)
    return x, ctx_out


def setup_inputs(seed: int = 0) -> dict:
    key = jax.random.key(seed)
    keys = iter(jax.random.split(key, 48))
    f32 = jnp.float32

    def nrm(shape, std):
        return std * jax.random.normal(next(keys), shape, f32)

    def gain(shape):
        return 1.0 + 0.05 * jax.random.normal(next(keys), shape, f32)

    def unif(shape, lo, hi):
        return jax.random.uniform(next(keys), shape, f32, lo, hi)

    dt = jnp.exp(unif((N_ODD, 2, C_HEADS), math.log(1e-3), math.log(1e-1)))
    return {
        'x_prompt': nrm((BATCH, SEQ, D_MODEL), 1.0),
        'x_sample': nrm((DEC_BATCH, DEC_SEQ, D_MODEL), 1.0),
        'cache_attn_k': nrm((DEC_BATCH, N_EVEN, PAST_LEN, A_KV_HEADS, A_HEAD_DIM), 1.0),
        'cache_attn_v': nrm((DEC_BATCH, N_EVEN, PAST_LEN, A_KV_HEADS, A_HEAD_DIM), 1.0),
        'state_rwkv': nrm((DEC_BATCH, N_EVEN, 2, B_HEADS, B_HEAD_DIM, B_HEAD_DIM), 0.3),
        'state_gdn': nrm((DEC_BATCH, N_ODD, 2, C_HEADS, C_HEAD_DIM, C_HEAD_DIM), 0.1),
        'cache_mla_ckv': nrm((DEC_BATCH, N_ODD, PAST_LEN, D_KV_RANK), 1.0),
        'cache_mla_krope': nrm((DEC_BATCH, N_ODD, PAST_LEN, D_ROPE), 1.0),
        'c': nrm((DEC_BATCH, D_MODEL), 1.0),
        'c_ctx': nrm((D_MODEL,), 1.0),
        'norm1_g': gain((DEPTH, D_MODEL)),
        'norm2_g': gain((DEPTH, D_MODEL)),
        'ada_w': nrm((DEPTH, D_MODEL, 6 * D_MODEL), 0.5 * D_MODEL ** -0.5),
        'ada_b': nrm((DEPTH, 6 * D_MODEL), 0.01),
        'w_in_ab': nrm((N_EVEN, D_MODEL, AB_IN), D_MODEL ** -0.5),
        'w_out_ab': nrm((N_EVEN, AB_OUT, D_MODEL), AB_OUT ** -0.5),
        'attn_q_norm': gain((N_EVEN, A_HEAD_DIM)),
        'attn_k_norm': gain((N_EVEN, A_HEAD_DIM)),
        'rwkv_mu': unif((N_EVEN, 2, B_IN), 0.0, 0.5),
        'rwkv_w0': unif((N_EVEN, 2, B_W), -5.0, 1.0),
        'rwkv_w2': nrm((N_EVEN, 2, B_DECAY_LORA, B_W), 0.1),
        'rwkv_a0': nrm((N_EVEN, 2, B_W), 0.5),
        'rwkv_a2': nrm((N_EVEN, 2, B_AAA_LORA, B_W), 0.1),
        'rwkv_g2': nrm((N_EVEN, B_GATE_LORA, B_W), B_GATE_LORA ** -0.5),
        'rwkv_k_k': 0.85 + nrm((N_EVEN, B_W), 0.05),
        'rwkv_k_a': gain((N_EVEN, B_W)),
        'rwkv_r_k': nrm((N_EVEN, B_HEADS, B_HEAD_DIM), 0.1),
        'rwkv_ln_w': gain((N_EVEN, B_W)),
        'rwkv_ln_b': nrm((N_EVEN, B_W), 0.01),
        'w_in_cd': nrm((N_ODD, D_MODEL, CD_IN), D_MODEL ** -0.5),
        'w_out_cd': nrm((N_ODD, CD_OUT, D_MODEL), CD_OUT ** -0.5),
        'gdn_conv': nrm((N_ODD, 3, 3 * C_W), 0.5),
        'gdn_a_log': jnp.log(unif((N_ODD, 2, C_HEADS), 1.0, 16.0)),
        'gdn_dt_bias': dt + jnp.log(-jnp.expm1(-dt)),
        'gdn_norm_g': gain((N_ODD, C_HEAD_DIM)),
        'mla_kv_norm_g': gain((N_ODD, D_KV_RANK)),
        'mla_w_kv_b': nrm((N_ODD, D_KV_RANK, D_HEADS * (D_NOPE + D_V)), D_KV_RANK ** -0.5),
        'peer_wq': nrm((DEPTH, D_MODEL, PEER_HEADS * PEER_QDIM), D_MODEL ** -0.5),
        'peer_keys': nrm((DEPTH, PEER_HEADS, 2, N_KEYS, PEER_HALF), PEER_HALF ** -0.5),
        'peer_u': nrm((DEPTH, N_EXPERTS, D_MODEL), D_MODEL ** -0.5),
        'peer_v': nrm((DEPTH, N_EXPERTS, D_MODEL), 0.3),
        'final_norm_g': gain((D_MODEL,)),
    }


def reference(x_prompt, x_sample, cache_attn_k, cache_attn_v, state_rwkv, state_gdn, cache_mla_ckv, cache_mla_krope,
              c, c_ctx, norm1_g, norm2_g, ada_w, ada_b, w_in_ab, w_out_ab, attn_q_norm, attn_k_norm,
              rwkv_mu, rwkv_w0, rwkv_w2, rwkv_a0, rwkv_a2, rwkv_g2, rwkv_k_k, rwkv_k_a, rwkv_r_k, rwkv_ln_w, rwkv_ln_b,
              w_in_cd, w_out_cd, gdn_conv, gdn_a_log, gdn_dt_bias, gdn_norm_g, mla_kv_norm_g, mla_w_kv_b,
              peer_wq, peer_keys, peer_u, peer_v, final_norm_g):
    yp, ys = x_prompt, x_sample
    ks_out, vs_out, srwkv_out, sgdn_out, ckv_out, krope_out = [], [], [], [], [], []
    for li in range(DEPTH):
        j = li // 2
        mod_p = modulation(c_ctx[None, :], ada_w[li], ada_b[li])
        mod_s = modulation(c, ada_w[li], ada_b[li])
        peer_args = (peer_wq[li], peer_keys[li], peer_u[li], peer_v[li])
        if li % 2 == 0:
            def mixer(p, ctx, j=j):
                pa, pb = p[..., :A_IN], p[..., A_IN:]
                oa, kv = gqa_mixer(pa, attn_q_norm[j], attn_k_norm[j], None if ctx is None else ctx[:2])
                ob, s_fin = rwkv_mixer(pb, rwkv_mu[j], rwkv_w0[j], rwkv_w2[j], rwkv_a0[j], rwkv_a2[j], rwkv_g2[j],
                                       rwkv_k_k[j], rwkv_k_a[j], rwkv_r_k[j], rwkv_ln_w[j], rwkv_ln_b[j],
                                       None if ctx is None else ctx[2])
                return jnp.concatenate([oa, ob], axis=-1), (kv, s_fin)
            w_in, w_out = w_in_ab[j], w_out_ab[j]
            cached = (cache_attn_k[:, j], cache_attn_v[:, j], state_rwkv[:, j])
        else:
            def mixer(p, ctx, j=j):
                pc, pd = p[..., :C_IN], p[..., C_IN:]
                oc, s_fin = gdn_mixer(pc, gdn_conv[j], gdn_a_log[j], gdn_dt_bias[j], gdn_norm_g[j],
                                      None if ctx is None else ctx[0])
                od, lat = mla_mixer(pd, mla_kv_norm_g[j], mla_w_kv_b[j], None if ctx is None else ctx[1:])
                return jnp.concatenate([oc, od], axis=-1), (s_fin, lat)
            w_in, w_out = w_in_cd[j], w_out_cd[j]
            cached = (state_gdn[:, j], cache_mla_ckv[:, j], cache_mla_krope[:, j])
        yp, ctx_out = trunk_layer(yp, mod_p, norm1_g[li], norm2_g[li], w_in, w_out, mixer, None, peer_args)
        ys, _ = trunk_layer(ys, mod_s, norm1_g[li], norm2_g[li], w_in, w_out, mixer, cached, peer_args)
        if li % 2 == 0:
            (k_c, v_c), s_c = ctx_out
            ks_out.append(k_c)
            vs_out.append(v_c)
            srwkv_out.append(s_c)
        else:
            s_c, (ckv_c, kr_c) = ctx_out
            sgdn_out.append(s_c)
            ckv_out.append(ckv_c)
            krope_out.append(kr_c)
    y_prompt = rms_norm(yp, final_norm_g)
    y_sample = rms_norm(ys, final_norm_g)
    new_attn_k = jnp.stack(ks_out, axis=1)
    new_attn_v = jnp.stack(vs_out, axis=1)
    new_state_rwkv = jnp.stack(srwkv_out, axis=1)
    new_state_gdn = jnp.stack(sgdn_out, axis=1)
    new_mla_ckv = jnp.stack(ckv_out, axis=1)
    new_mla_krope = jnp.stack(krope_out, axis=1)
    return (y_prompt, y_sample, new_attn_k, new_attn_v, new_state_rwkv, new_state_gdn, new_mla_ckv, new_mla_krope)
```

```python
import functools
import math

import numpy as np
import jax
import jax.numpy as jnp
from jax import lax
from jax.experimental import pallas as pl
from jax.experimental.pallas import tpu as pltpu

F32 = jnp.float32
BF16 = jnp.bfloat16

D_MODEL = 2048
GRID_W = 64
ROPE_THETA = 10000.0
NORM_EPS = 1e-6
A_HEADS, A_KV_HEADS, A_HEAD_DIM = 8, 2, 128
A_Q = A_HEADS * A_HEAD_DIM
A_KV = A_KV_HEADS * A_HEAD_DIM
A_IN = A_Q + 2 * A_KV
B_HEADS, B_HEAD_DIM = 16, 64
B_W = B_HEADS * B_HEAD_DIM
B_DECAY_LORA, B_AAA_LORA, B_GATE_LORA = 64, 64, 128
B_IN = 3 * B_W + 2 * B_DECAY_LORA + 2 * B_AAA_LORA + B_GATE_LORA
B_GN_EPS = 64e-5
C_HEADS, C_HEAD_DIM = 8, 128
C_W = C_HEADS * C_HEAD_DIM
C_IN = 4 * C_W + 4 * C_HEADS
D_HEADS, D_NOPE, D_ROPE, D_V = 8, 128, 64, 128
D_QK = D_NOPE + D_ROPE
D_KV_RANK = 512
D_Q = D_HEADS * D_QK
PEER_HEADS, N_KEYS, PEER_QDIM, PEER_TOPK = 8, 128, 128, 16
PEER_HALF = PEER_QDIM // 2
N_EXPERTS = N_KEYS * N_KEYS

LANES = 128
VMEM_LIMIT = 48 * 1024 * 1024
VMEM_LIMIT_BIG = 56 * 1024 * 1024

CHUNK = 64

NN = ((1,), (0,))
NT = ((1,), (1,))
TN = ((0,), (0,))


def _dg(a, b, dims=NN):
    return lax.dot_general(a, b, (dims, ((), ())), preferred_element_type=F32)


def _split(x):
    hi = x.astype(BF16)
    lo = (x - hi.astype(F32)).astype(BF16)
    return hi, lo


def _dot_hp(a, b, dims=NN):
    ah, al = _split(a)
    bh, bl = _split(b)
    return _dg(ah, bh, dims) + (_dg(ah, bl, dims) + _dg(al, bh, dims))


def _dot_bf(a, b, dims=NN):
    return _dg(a.astype(BF16), b.astype(BF16), dims)


def _sigmoid(x):
    return 1.0 / (1.0 + jnp.exp(-x))


def _softplus(x):
    return jnp.maximum(x, 0.0) + jnp.log(1.0 + jnp.exp(-jnp.abs(x)))


def _cparams(sem, vmem=VMEM_LIMIT):
    return pltpu.CompilerParams(dimension_semantics=sem, vmem_limit_bytes=vmem)


def _pick_tile(n, cap):
    best = None
    t = LANES
    while t <= min(n, cap):
        if n % t == 0:
            best = t
        t += LANES
    return best if best is not None else n


def _mm_norm_kernel(*refs, has_norm, has_mod, shift_slot, scale_slot, hp, emit_xn):
    it = iter(refs)
    x_ref = next(it)
    g_ref = next(it) if has_norm else None
    m_ref = next(it) if has_mod else None
    wh_ref = next(it)
    wl_ref = next(it) if hp else None
    o_ref = next(it)
    xn_ref = next(it) if emit_xn else None
    xh_ref = next(it)
    xl_ref = next(it) if hp else None

    @pl.when(pl.program_id(1) == 0)
    def _():
        xv = x_ref[...].astype(F32)
        if has_norm:
            xv = xv * lax.rsqrt(jnp.mean(xv * xv, axis=-1, keepdims=True) + NORM_EPS) * g_ref[...]
        if has_mod:
            xv = xv * (1.0 + m_ref[scale_slot:scale_slot + 1, :]) + m_ref[shift_slot:shift_slot + 1, :]
        hi = xv.astype(BF16)
        xh_ref[...] = hi
        if hp:
            xl_ref[...] = (xv - hi.astype(F32)).astype(BF16)
        if emit_xn:
            xn_ref[...] = xv.astype(xn_ref.dtype)

    acc = _dg(xh_ref[...], wh_ref[...])
    if hp:
        acc = acc + (_dg(xh_ref[...], wl_ref[...]) + _dg(xl_ref[...], wh_ref[...]))
    o_ref[...] = acc


def _mm_norm(x, w, *, k=None, xcol=0, norm_g=None, mods=None, slots=(0, 1), rows_per_mod=None,
             hp=False, emit_xn=None, tm=512, tn_cap=1664):
    M = x.shape[0]
    K = k if k is not None else x.shape[1]
    ws = tuple(w) if isinstance(w, (tuple, list)) else (w,)
    N = ws[0].shape[1]
    tm = min(tm, M)
    if mods is not None and mods.shape[0] > 1:
        tm = min(tm, rows_per_mod)
    tn = _pick_tile(N, tn_cap)
    has_norm = norm_g is not None
    has_mod = mods is not None
    rpm = rows_per_mod if rows_per_mod is not None else M

    in_specs = [pl.BlockSpec((tm, K), lambda i, j: (i, xcol))]
    args = [x]
    if has_norm:
        in_specs.append(pl.BlockSpec((1, K), lambda i, j: (0, 0)))
        args.append(norm_g.reshape(1, K).astype(F32))
    if has_mod:
        if mods.shape[0] > 1:
            in_specs.append(pl.BlockSpec((None, 6, K), lambda i, j: ((i * tm) // rpm, 0, 0)))
        else:
            in_specs.append(pl.BlockSpec((None, 6, K), lambda i, j: (0, 0, 0)))
        args.append(mods)
    for wi in ws:
        in_specs.append(pl.BlockSpec((K, tn), lambda i, j: (0, j)))
        args.append(wi)
    out_shape = [jax.ShapeDtypeStruct((M, N), F32)]
    out_specs = [pl.BlockSpec((tm, tn), lambda i, j: (i, j))]
    if emit_xn is not None:
        out_shape.append(jax.ShapeDtypeStruct((M, K), emit_xn))
        out_specs.append(pl.BlockSpec((tm, K), lambda i, j: (i, 0)))
    scratch = [pltpu.VMEM((tm, K), BF16)]
    if hp:
        scratch.append(pltpu.VMEM((tm, K), BF16))
    kern = functools.partial(_mm_norm_kernel, has_norm=has_norm, has_mod=has_mod, shift_slot=slots[0],
                             scale_slot=slots[1], hp=hp, emit_xn=emit_xn is not None)
    outs = pl.pallas_call(
        kern, grid=(M // tm, N // tn), in_specs=in_specs, out_specs=out_specs, out_shape=out_shape,
        scratch_shapes=scratch, compiler_params=_cparams(("parallel", "arbitrary")))(*args)
    return outs if emit_xn is not None else outs[0]


def _mm_res_kernel(*refs, n_in, gate_slot):
    a_refs = refs[:n_in]
    w_refs = refs[n_in:2 * n_in]
    res_ref, m_ref, o_ref = refs[2 * n_in:]
    acc = None
    for a_ref, w_ref in zip(a_refs, w_refs):
        part = _dg(a_ref[...].astype(BF16), w_ref[...])
        acc = part if acc is None else acc + part
    o_ref[...] = res_ref[...] + m_ref[gate_slot:gate_slot + 1, :] * acc


def _mm_res(a_list, w_list, res, mods, gate_slot, rows_per_mod, tm=512, tn=1024):
    M, N = res.shape
    tm = min(tm, M, rows_per_mod) if mods.shape[0] > 1 else min(tm, M)
    tn = min(tn, N)
    n_in = len(a_list)
    in_specs, args = [], []
    for a in a_list:
        in_specs.append(pl.BlockSpec((tm, a.shape[1]), lambda i, j: (i, 0)))
        args.append(a)
    for w in w_list:
        in_specs.append(pl.BlockSpec((w.shape[0], tn), lambda i, j: (0, j)))
        args.append(w)
    in_specs.append(pl.BlockSpec((tm, tn), lambda i, j: (i, j)))
    args.append(res)
    if mods.shape[0] > 1:
        in_specs.append(pl.BlockSpec((None, 6, tn), lambda i, j: ((i * tm) // rows_per_mod, 0, j)))
    else:
        in_specs.append(pl.BlockSpec((None, 6, tn), lambda i, j: (0, 0, j)))
    args.append(mods)
    return pl.pallas_call(
        functools.partial(_mm_res_kernel, n_in=n_in, gate_slot=gate_slot),
        grid=(M // tm, N // tn), in_specs=in_specs,
        out_specs=pl.BlockSpec((tm, tn), lambda i, j: (i, j)),
        out_shape=jax.ShapeDtypeStruct((M, N), F32),
        compiler_params=_cparams(("parallel", "arbitrary")))(*args)


def _rmsnorm_kernel(x_ref, g_ref, o_ref):
    xv = x_ref[...]
    o_ref[...] = xv * lax.rsqrt(jnp.mean(xv * xv, axis=-1, keepdims=True) + NORM_EPS) * g_ref[...]


def _rmsnorm(x, g, tm=512):
    M, K = x.shape
    tm = min(tm, M)
    return pl.pallas_call(
        _rmsnorm_kernel, grid=(M // tm,),
        in_specs=[pl.BlockSpec((tm, K), lambda i: (i, 0)), pl.BlockSpec((1, K), lambda i: (0, 0))],
        out_specs=pl.BlockSpec((tm, K), lambda i: (i, 0)),
        out_shape=jax.ShapeDtypeStruct((M, K), F32),
        compiler_params=_cparams(("parallel",)))(x, g.reshape(1, K))


def _rope_tables(T, rot_dim):
    rows = T // GRID_W
    row = jnp.repeat(jnp.arange(rows, dtype=F32), GRID_W)
    col = jnp.tile(jnp.arange(GRID_W, dtype=F32), rows)
    n_freq = rot_dim // 4
    inv = ROPE_THETA ** (-jnp.arange(n_freq, dtype=F32) / n_freq)
    ang = jnp.concatenate([row[:, None] * inv, col[:, None] * inv], axis=-1)
    cos = jnp.repeat(jnp.cos(ang), 2, axis=-1)
    sin = jnp.repeat(jnp.sin(ang), 2, axis=-1) * jnp.tile(jnp.array([-1.0, 1.0], F32), rot_dim // 2)
    pad = LANES - rot_dim
    if pad:
        cos = jnp.concatenate([cos, jnp.ones((T, pad), F32)], axis=-1)
        sin = jnp.concatenate([sin, jnp.zeros((T, pad), F32)], axis=-1)
    return cos, sin


def _rope(y, cosf, sinf):
    lane = lax.broadcasted_iota(jnp.int32, y.shape, 1)
    even = (lane & 1) == 0
    width = y.shape[1]
    swap = jnp.where(even, pltpu.roll(y, width - 1, 1), pltpu.roll(y, 1, 1))
    return y * cosf + swap * sinf


def _norm_rope_kernel(*refs, nblk, norm, rope):
    it = iter(refs)
    x_ref = next(it)
    g_ref = next(it) if norm else None
    cos_ref = next(it) if rope else None
    sin_ref = next(it) if rope else None
    o_ref = next(it)
    for c in range(nblk):
        sl = slice(c * LANES, (c + 1) * LANES)
        y = x_ref[:, sl]
        if norm:
            y = y * lax.rsqrt(jnp.mean(y * y, axis=-1, keepdims=True) + NORM_EPS) * g_ref[:, sl]
        if rope:
            y = _rope(y, cos_ref[...], sin_ref[...])
        o_ref[:, sl] = y


def _norm_rope(x, nblk, T, gains=None, tables=None, tt=256):
    M = x.shape[0]
    tt = min(tt, T)
    W = nblk * LANES
    in_specs = [pl.BlockSpec((tt, W), lambda i: (i, 0))]
    args = [x]
    if gains is not None:
        in_specs.append(pl.BlockSpec((1, W), lambda i: (0, 0)))
        args.append(gains.reshape(1, W))
    if tables is not None:
        nt = T // tt
        for t in tables:
            in_specs.append(pl.BlockSpec((tt, LANES), lambda i: (i % nt, 0)))
            args.append(t)
    return pl.pallas_call(
        functools.partial(_norm_rope_kernel, nblk=nblk, norm=gains is not None, rope=tables is not None),
        grid=(M // tt,), in_specs=in_specs,
        out_specs=pl.BlockSpec((tt, W), lambda i: (i, 0)),
        out_shape=jax.ShapeDtypeStruct((M, W), F32),
        compiler_params=_cparams(("parallel",)))(*args)


def _attn_kernel(*refs, scale, has2):
    it = iter(refs)
    q_ref = next(it)
    q2_ref = next(it) if has2 else None
    k_ref = next(it)
    k2_ref = next(it) if has2 else None
    v_ref = next(it)
    o_ref = next(it)
    s = _dot_bf(q_ref[...] * scale, k_ref[...], NT)
    if has2:
        s = s + _dot_bf(q2_ref[...] * scale, k2_ref[...], NT)
    m = jnp.max(s, axis=-1, keepdims=True)
    p = jnp.exp(s - m)
    l = jnp.sum(p, axis=-1, keepdims=True)
    o_ref[...] = _dot_bf(p, v_ref[...]) / l


def _attention(B, T, Tk, H, scale, q, k, v, q2=None, k2=None, tq=256):
    tq = min(tq, T)
    nq = T // tq
    has2 = q2 is not None

    def qspec(cf):
        return pl.BlockSpec((tq, LANES), lambda b, h, i: (b * nq + i, cf(h)))

    def kspec(cf):
        return pl.BlockSpec((Tk, LANES), lambda b, h, i: (b, cf(h)))

    in_specs, args = [qspec(q[1])], [q[0]]
    if has2:
        in_specs.append(qspec(q2[1]))
        args.append(q2[0])
    in_specs.append(kspec(k[1]))
    args.append(k[0])
    if has2:
        in_specs.append(kspec(k2[1]))
        args.append(k2[0])
    in_specs.append(kspec(v[1]))
    args.append(v[0])
    return pl.pallas_call(
        functools.partial(_attn_kernel, scale=scale, has2=has2),
        grid=(B, H, nq), in_specs=in_specs,
        out_specs=pl.BlockSpec((tq, LANES), lambda b, h, i: (b * nq + i, h)),
        out_shape=jax.ShapeDtypeStruct((B * T, H * LANES), F32),
        compiler_params=_cparams(("parallel", "arbitrary", "arbitrary")))(*args)


def _tri_masks(L, rev):
    i = lax.broadcasted_iota(jnp.int32, (L, L), 0)
    j = lax.broadcasted_iota(jnp.int32, (L, L), 1)
    if rev:
        return j >= i, j > i, i == j
    return j <= i, j < i, i == j


def _inv_unit_tri(a, eye_f, L):
    n = -a
    t = eye_f + n
    p = n
    for _ in range(int(math.log2(L)) - 1):
        p = _dot_hp(p, p)
        t = t + _dot_hp(t, p)
    return t


def _shifted(f, prev_ref, next_ref, tt, T):
    i = pl.program_id(0)
    first = (i * tt) % T == 0
    last = ((i + 1) * tt) % T == 0
    prev = jnp.where(first, 0.0, prev_ref[7:8, :])
    nxt = jnp.where(last, 0.0, next_ref[0:1, :])
    rows = lax.broadcasted_iota(jnp.int32, f.shape, 0)
    fp = jnp.where(rows == 0, prev, pltpu.roll(f, 1, 0))
    fn = jnp.where(rows == tt - 1, nxt, pltpu.roll(f, tt - 1, 0))
    return fp, fn


def _halo_specs(tt, width, M, col_fn):
    r8 = tt // 8
    last8 = M // 8 - 1
    return [
        pl.BlockSpec((tt, width), lambda *g: (g[0], col_fn(*g))),
        pl.BlockSpec((8, width), lambda *g: (jnp.maximum(g[0] * r8 - 1, 0), col_fn(*g))),
        pl.BlockSpec((8, width), lambda *g: (jnp.minimum((g[0] + 1) * r8, last8), col_fn(*g))),
    ]


NPAIR = B_HEADS // 2


def _pair_sum_matrix():
    r = lax.broadcasted_iota(jnp.int32, (LANES, LANES), 0) // B_HEAD_DIM
    c = lax.broadcasted_iota(jnp.int32, (LANES, LANES), 1) // B_HEAD_DIM
    return (r == c).astype(F32)


def _rwkv_prep_kernel(x_ref, prev_ref, next_ref, mu_ref, w0_ref, w2_ref, a0_ref, a2_ref, g2_ref, kk_ref_in,
                      r_o, k_o, v_o, kk_o, g_o, lw0_o, lw1_o, a0_o, a1_o, *, tt, T):
    f = x_ref[...]
    fp, fn = _shifted(f, prev_ref, next_ref, tt, T)
    f = f + mu_ref[0:1, :] * (fp - f) + mu_ref[1:2, :] * (fn - f)
    r = f[:, :B_W]
    k = f[:, B_W:2 * B_W]
    v = f[:, 2 * B_W:3 * B_W]
    off = 3 * B_W
    w_lo = f[:, off:off + 2 * B_DECAY_LORA]
    off += 2 * B_DECAY_LORA
    a_lo = f[:, off:off + 2 * B_AAA_LORA]
    off += 2 * B_AAA_LORA
    g_lo = f[:, off:]
    th = jnp.tanh(w_lo)
    logw, aa = [], []
    for d in range(2):
        wpre = w0_ref[d:d + 1, :] + _dot_hp(th[:, d * B_DECAY_LORA:(d + 1) * B_DECAY_LORA], w2_ref[d])
        wv = -_softplus(-wpre) - 0.5
        logw.append(-jnp.exp(wv))
        aa.append(_sigmoid(a0_ref[d:d + 1, :] + _dot_hp(a_lo[:, d * B_AAA_LORA:(d + 1) * B_AAA_LORA], a2_ref[d])))
    g = _dot_hp(_sigmoid(g_lo), g2_ref[...])
    kk = k * kk_ref_in[...]
    gsum = _pair_sum_matrix()
    for p in range(NPAIR):
        sl = slice(p * LANES, (p + 1) * LANES)
        kp = kk[:, sl]
        ssq = _dot_hp(kp * kp, gsum)
        kk_o[p] = kp / jnp.maximum(jnp.sqrt(ssq), 1e-12)
        r_o[p] = r[:, sl]
        k_o[p] = k[:, sl]
        v_o[p] = v[:, sl]
        g_o[p] = g[:, sl]
        lw0_o[p] = logw[0][:, sl]
        lw1_o[p] = logw[1][:, sl]
        a0_o[p] = aa[0][:, sl]
        a1_o[p] = aa[1][:, sl]


def _rwkv_prep(pb, B, T, mu, w0, w2, a0, a2, g2, k_k, tt=128):
    M = pb.shape[0]
    tt = min(tt, T)
    nt = T // tt
    full = lambda shape: pl.BlockSpec(shape, lambda i: tuple(0 for _ in shape))
    in_specs = _halo_specs(tt, B_IN, M, lambda i: 0) + [
        full((2, B_IN)), full((2, B_W)), full((2, B_DECAY_LORA, B_W)), full((2, B_W)),
        full((2, B_AAA_LORA, B_W)), full((B_GATE_LORA, B_W)), full((1, B_W))]
    ospec = pl.BlockSpec((None, NPAIR, tt, LANES), lambda i: (i // nt, 0, i % nt, 0))
    oshape = jax.ShapeDtypeStruct((B, NPAIR, T, LANES), F32)
    return pl.pallas_call(
        functools.partial(_rwkv_prep_kernel, tt=tt, T=T),
        grid=(M // tt,), in_specs=in_specs, out_specs=[ospec] * 9, out_shape=[oshape] * 9,
        compiler_params=_cparams(("parallel",)))(pb, pb, pb, mu, w0, w2, a0, a2, g2, k_k.reshape(1, B_W))


def _rwkv_head_chunk(r, lw, kd, v, kk, a, s, masks, L, rev):
    incl, strict, eye = masks
    incl_f = incl.astype(F32)
    logp = _dot_hp(incl_f, lw)
    logp_ex = logp - lw
    ninv = jnp.exp(-logp)
    b = kk * a
    kt = kk * jnp.exp(logp_ex)
    rt = r * jnp.exp(logp)
    lhs = jnp.concatenate([kt, rt], axis=0)
    rhs = jnp.concatenate([b * ninv, kd * ninv], axis=0)
    gm = _dot_hp(lhs, rhs, NT)
    a_b = jnp.where(strict, gm[:L, :L], 0.0)
    a_k = jnp.where(strict, gm[:L, L:], 0.0)
    b_r = jnp.where(incl, gm[L:, :L], 0.0)
    k_r = jnp.where(incl, gm[L:, L:], 0.0)
    tinv = _inv_unit_tri(a_b, eye.astype(F32), L)
    ls = _dot_hp(lhs, s, NT)
    u = -_dot_hp(tinv, ls[:L] + _dot_hp(a_k, v))
    uv = jnp.concatenate([u, v], axis=0)
    y = ls[L:] + _dot_hp(jnp.concatenate([b_r, k_r], axis=1), uv)
    last = 0 if rev else L - 1
    pl_row = logp[last:last + 1, :]
    rem = jnp.exp(pl_row - logp)
    s_new = s * jnp.exp(pl_row) + _dot_hp(uv, jnp.concatenate([b * rem, kd * rem], axis=0), TN)
    return y, s_new


def _rwkv_scan_kernel(r_ref, v_ref, kk_ref, k_ref, lw_ref, a_ref, ka_ref, s0_ref, y_ref, sf_ref, s_ref, *, L, rev):
    c = pl.program_id(1)

    @pl.when(c == 0)
    def _():
        s_ref[...] = s0_ref[...]

    masks = _tri_masks(L, rev)

    def pair(p, carry):
        r2, v2, kk2, k2, lw2, a2 = r_ref[p], v_ref[p], kk_ref[p], k_ref[p], lw_ref[p], a_ref[p]
        kd2 = k2 * (1.0 + (a2 - 1.0) * ka_ref[p])
        ys = []
        for half in range(2):
            sl = slice(half * B_HEAD_DIM, (half + 1) * B_HEAD_DIM)
            h = 2 * p + half
            y, s_new = _rwkv_head_chunk(r2[:, sl], lw2[:, sl], kd2[:, sl], v2[:, sl], kk2[:, sl], a2[:, sl],
                                        s_ref[h], masks, L, rev)
            s_ref[h] = s_new
            ys.append(y)
        y_ref[p] = jnp.concatenate(ys, axis=1)
        return carry

    lax.fori_loop(0, NPAIR, pair, 0)

    @pl.when(c == pl.num_programs(1) - 1)
    def _():
        sf_ref[...] = s_ref[...]


def _rwkv_scan(r, v, kk, k, lw, a, k_a, s0, rev, L=CHUNK):
    B, _, T, _ = r.shape
    nc = T // L
    cmap = (lambda b, c: (b, 0, nc - 1 - c, 0)) if rev else (lambda b, c: (b, 0, c, 0))
    seq = pl.BlockSpec((None, NPAIR, L, LANES), cmap)
    st = pl.BlockSpec((None, B_HEADS, B_HEAD_DIM, B_HEAD_DIM), lambda b, c: (b, 0, 0, 0))
    return pl.pallas_call(
        functools.partial(_rwkv_scan_kernel, L=L, rev=rev),
        grid=(B, nc),
        in_specs=[seq] * 6 + [pl.BlockSpec((NPAIR, 1, LANES), lambda b, c: (0, 0, 0)), st],
        out_specs=[seq, st],
        out_shape=[jax.ShapeDtypeStruct((B, NPAIR, T, LANES), F32),
                   jax.ShapeDtypeStruct((B, B_HEADS, B_HEAD_DIM, B_HEAD_DIM), F32)],
        scratch_shapes=[pltpu.VMEM((B_HEADS, B_HEAD_DIM, B_HEAD_DIM), F32)],
        compiler_params=_cparams(("parallel", "arbitrary")))(r, v, kk, k, lw, a, k_a.reshape(NPAIR, 1, LANES), s0)


def _rwkv_post_kernel(y0_ref, y1_ref, r_ref, k_ref, v_ref, g_ref, rk_ref, lnw_ref, lnb_ref, o_ref):
    gsum = _pair_sum_matrix()
    inv_n = 1.0 / B_HEAD_DIM
    for p in range(NPAIR):
        y = y0_ref[p] + y1_ref[p]
        mean = _dot_hp(y, gsum) * inv_n
        yc = y - mean
        var = _dot_hp(yc * yc, gsum) * inv_n
        yn = yc * lax.rsqrt(var + B_GN_EPS) * lnw_ref[p] + lnb_ref[p]
        bonus = _dot_hp(r_ref[p] * k_ref[p] * rk_ref[p], gsum) * v_ref[p]
        o_ref[:, p * LANES:(p + 1) * LANES] = (yn + bonus) * g_ref[p]


def _rwkv_post(y0, y1, r, k, v, g, r_k, ln_w, ln_b, tt=256):
    B, _, T, _ = r.shape
    tt = min(tt, T)
    nt = T // tt
    seq = pl.BlockSpec((None, NPAIR, tt, LANES), lambda i: (i // nt, 0, i % nt, 0))
    par = pl.BlockSpec((NPAIR, 1, LANES), lambda i: (0, 0, 0))
    return pl.pallas_call(
        _rwkv_post_kernel, grid=(B * nt,),
        in_specs=[seq] * 6 + [par] * 3,
        out_specs=pl.BlockSpec((tt, B_W), lambda i: (i, 0)),
        out_shape=jax.ShapeDtypeStruct((B * T, B_W), F32),
        compiler_params=_cparams(("parallel",)))(
            y0, y1, r, k, v, g, r_k.reshape(NPAIR, 1, LANES), ln_w.reshape(NPAIR, 1, LANES),
            ln_b.reshape(NPAIR, 1, LANES))


def _rwkv_mixer(pb, B, T, prm, s0):
    r, k, v, kk, g, lw0, lw1, a0, a1 = _rwkv_prep(pb, B, T, prm['mu'], prm['w0'], prm['w2'], prm['a0'],
                                                   prm['a2'], prm['g2'], prm['k_k'])
    if s0 is None:
        s0 = jnp.zeros((B, 2, B_HEADS, B_HEAD_DIM, B_HEAD_DIM), F32)
    y0, sf0 = _rwkv_scan(r, v, kk, k, lw0, a0, prm['k_a'], s0[:, 0], rev=False)
    y1, sf1 = _rwkv_scan(r, v, kk, k, lw1, a1, prm['k_a'], s0[:, 1], rev=True)
    out = _rwkv_post(y0, y1, r, k, v, g, prm['r_k'], prm['ln_w'], prm['ln_b'])
    return out, jnp.stack([sf0, sf1], axis=1)


def _gdn_prep_kernel(x_ref, prev_ref, next_ref, cw_ref, o_ref, *, tt, T):
    c = pl.program_id(1)
    x = x_ref[...]
    xp, xn = _shifted(x, prev_ref, next_ref, tt, T)
    y = cw_ref[0:1, :] * xp + cw_ref[1:2, :] * x + cw_ref[2:3, :] * xn
    y = y * _sigmoid(y)

    @pl.when(c == 2)
    def _():
        o_ref[...] = y

    @pl.when(c < 2)
    def _():
        scale = jnp.where(c == 0, C_HEAD_DIM ** -0.5, 1.0).astype(F32)
        for h in range(C_HEADS):
            sl = slice(h * C_HEAD_DIM, (h + 1) * C_HEAD_DIM)
            yh = y[:, sl]
            o_ref[:, sl] = yh * lax.rsqrt(jnp.sum(yh * yh, axis=-1, keepdims=True) + 1e-6) * scale


def _gdn_prep(pc, T, conv_w, tt=256):
    M = pc.shape[0]
    tt = min(tt, T)
    in_specs = _halo_specs(tt, C_W, M, lambda i, c: c) + [pl.BlockSpec((3, C_W), lambda i, c: (0, c))]
    return pl.pallas_call(
        functools.partial(_gdn_prep_kernel, tt=tt, T=T),
        grid=(M // tt, 3), in_specs=in_specs,
        out_specs=pl.BlockSpec((tt, C_W), lambda i, c: (i, c)),
        out_shape=jax.ShapeDtypeStruct((M, 3 * C_W), F32),
        compiler_params=_cparams(("parallel", "arbitrary")))(pc, pc, pc, conv_w)


def _gdn_bg_kernel(x_ref, alog_ref, dtb_ref, o_ref):
    x = x_ref[...]
    lane = lax.broadcasted_iota(jnp.int32, x.shape, 1)
    beta = _sigmoid(x)
    g = -jnp.exp(alog_ref[...]) * _softplus(x + dtb_ref[...])
    o_ref[...] = jnp.where(lane < 2 * C_HEADS, beta, g)


def _gdn_bg(pc, a_log, dt_bias, tt=512):
    M = pc.shape[0]
    tt = min(tt, M)
    nb = 2 * C_HEADS
    pad = lambda p: jnp.concatenate([jnp.zeros((nb,), F32), p.reshape(nb), jnp.zeros((LANES - 2 * nb,), F32)]).reshape(1, LANES)
    return pl.pallas_call(
        _gdn_bg_kernel, grid=(M // tt,),
        in_specs=[pl.BlockSpec((tt, LANES), lambda i: (i, 4 * C_W // LANES)),
                  pl.BlockSpec((1, LANES), lambda i: (0, 0)), pl.BlockSpec((1, LANES), lambda i: (0, 0))],
        out_specs=pl.BlockSpec((tt, LANES), lambda i: (i, 0)),
        out_shape=jax.ShapeDtypeStruct((M, LANES), F32),
        compiler_params=_cparams(("parallel",)))(pc, pad(a_log), pad(dt_bias))


def _gdn_head_chunk(q, k, v, beta, g_col, g_row, s, masks, L, rev):
    incl, strict, eye = masks
    incl_f = incl.astype(F32)
    i = lax.broadcasted_iota(jnp.int32, (L, L), 0)
    j = lax.broadcasted_iota(jnp.int32, (L, L), 1)
    cum_t = ((i >= j) if rev else (i <= j)).astype(F32)
    gc = _dot_hp(incl_f, jnp.broadcast_to(g_col, (L, LANES)))
    gc_r = _dot_hp(jnp.broadcast_to(g_row, (8, L)), cum_t)[0:1, :]
    decay = jnp.exp(jnp.where(incl, gc[:, :L] - gc_r, -1e30))
    kb = k * beta
    a_mat = jnp.where(strict, _dot_hp(kb, k, NT) * decay, 0.0)
    tinv = _inv_unit_tri(a_mat, eye.astype(F32), L)
    rhs = jnp.concatenate([v * beta, kb * jnp.exp(gc)], axis=1)
    sol = _dot_hp(tinv, rhs)
    u, w = sol[:, :C_HEAD_DIM], sol[:, C_HEAD_DIM:]
    qk = jnp.where(incl, _dot_hp(q, k, NT) * decay, 0.0)
    v_new = u - _dot_hp(w, s)
    o = _dot_hp(q * jnp.exp(gc), s) + _dot_hp(qk, v_new)
    last = 0 if rev else L - 1
    g_end = gc[last:last + 1, :]
    s_new = s * jnp.exp(g_end) + _dot_hp(k * jnp.exp(g_end - gc), v_new, TN)
    return o, s_new


def _gdn_scan_kernel(q_ref, k_ref, v_ref, bcol_ref, gcol_ref, grow_ref, s0_ref, o_ref, sf_ref, s_ref, *, L, rev):
    c = pl.program_id(1)

    @pl.when(c == 0)
    def _():
        s_ref[...] = s0_ref[...]

    masks = _tri_masks(L, rev)
    bcol = bcol_ref[...]
    gcol = gcol_ref[...]
    lane = lax.broadcasted_iota(jnp.int32, bcol.shape, 1)

    def head(h, carry):
        hs = pl.ds(pl.multiple_of(h * C_HEAD_DIM, C_HEAD_DIM), C_HEAD_DIM)
        sel = lane == h
        beta = jnp.sum(jnp.where(sel, bcol, 0.0), axis=1, keepdims=True)
        g_col = jnp.sum(jnp.where(sel, gcol, 0.0), axis=1, keepdims=True)
        g_row = grow_ref[pl.ds(h, 1), :]
        o, s_new = _gdn_head_chunk(q_ref[:, hs], k_ref[:, hs], v_ref[:, hs], beta, g_col, g_row, s_ref[h],
                                   masks, L, rev)
        o_ref[:, hs] = o
        s_ref[h] = s_new
        return carry

    lax.fori_loop(0, C_HEADS, head, 0)

    @pl.when(c == pl.num_programs(1) - 1)
    def _():
        sf_ref[...] = s_ref[...]


def _gdn_scan(qkv, bcol, gcol, grow, s0, B, T, rev, L=CHUNK):
    nc = T // L
    cidx = (lambda c: nc - 1 - c) if rev else (lambda c: c)
    seq = lambda col: pl.BlockSpec((L, C_W), lambda b, c: (b * nc + cidx(c), col))
    small = pl.BlockSpec((L, C_HEADS), lambda b, c: (b * nc + cidx(c), 0))
    st = pl.BlockSpec((None, C_HEADS, C_HEAD_DIM, C_HEAD_DIM), lambda b, c: (b, 0, 0, 0))
    return pl.pallas_call(
        functools.partial(_gdn_scan_kernel, L=L, rev=rev),
        grid=(B, nc),
        in_specs=[seq(0), seq(1), seq(2), small, small,
                  pl.BlockSpec((None, C_HEADS, L), lambda b, c: (b * nc + cidx(c), 0, 0)), st],
        out_specs=[pl.BlockSpec((L, C_W), lambda b, c: (b * nc + cidx(c), 0)), st],
        out_shape=[jax.ShapeDtypeStruct((B * T, C_W), F32),
                   jax.ShapeDtypeStruct((B, C_HEADS, C_HEAD_DIM, C_HEAD_DIM), F32)],
        scratch_shapes=[pltpu.VMEM((C_HEADS, C_HEAD_DIM, C_HEAD_DIM), F32)],
        compiler_params=_cparams(("parallel", "arbitrary")))(qkv, qkv, qkv, bcol, gcol, grow, s0)


def _gdn_post_kernel(o0_ref, o1_ref, z_ref, g_ref, out_ref):
    for h in range(C_HEADS):
        sl = slice(h * C_HEAD_DIM, (h + 1) * C_HEAD_DIM)
        y = o0_ref[:, sl] + o1_ref[:, sl]
        y = y * lax.rsqrt(jnp.mean(y * y, axis=-1, keepdims=True) + NORM_EPS) * g_ref[...]
        z = z_ref[:, sl]
        out_ref[:, sl] = y * (z * _sigmoid(z))


def _gdn_post(o0, o1, pc, norm_g, tt=256):
    M = o0.shape[0]
    tt = min(tt, M)
    return pl.pallas_call(
        _gdn_post_kernel, grid=(M // tt,),
        in_specs=[pl.BlockSpec((tt, C_W), lambda i: (i, 0)), pl.BlockSpec((tt, C_W), lambda i: (i, 0)),
                  pl.BlockSpec((tt, C_W), lambda i: (i, 3)), pl.BlockSpec((1, C_HEAD_DIM), lambda i: (0, 0))],
        out_specs=pl.BlockSpec((tt, C_W), lambda i: (i, 0)),
        out_shape=jax.ShapeDtypeStruct((M, C_W), F32),
        compiler_params=_cparams(("parallel",)))(o0, o1, pc, norm_g.reshape(1, C_HEAD_DIM))


def _gdn_mixer(pc, B, T, prm, s0, L=CHUNK):
    M = pc.shape[0]
    nc = T // L
    qkv = _gdn_prep(pc, T, prm['conv'])
    bg = _gdn_bg(pc, prm['a_log'], prm['dt_bias'])
    if s0 is None:
        s0 = jnp.zeros((B, 2, C_HEADS, C_HEAD_DIM, C_HEAD_DIM), F32)
    outs, finals = [], []
    for d in range(2):
        bcol = bg[:, d * C_HEADS:(d + 1) * C_HEADS]
        gcol = bg[:, 2 * C_HEADS + d * C_HEADS:2 * C_HEADS + (d + 1) * C_HEADS]
        grow = gcol.reshape(B * nc, L, C_HEADS).transpose(0, 2, 1)
        o, sf = _gdn_scan(qkv, bcol, gcol, grow, s0[:, d], B, T, rev=(d == 1))
        outs.append(o)
        finals.append(sf)
    y = _gdn_post(outs[0], outs[1], pc, prm['norm_g'])
    return y, jnp.stack(finals, axis=1)


def _top16(tile, need_mask):
    rows = lax.broadcasted_iota(jnp.int32, tile.shape, 0).astype(F32)
    big = float(tile.shape[0])
    cur = tile
    removed = jnp.zeros(tile.shape, dtype=jnp.bool_) if need_mask else None
    vals = []
    for _ in range(PEER_TOPK):
        m = jnp.max(cur, axis=0, keepdims=True)
        vals.append(m)
        idx = jnp.min(jnp.where(cur == m, rows, big), axis=0, keepdims=True)
        hit = rows == idx
        if need_mask:
            removed = jnp.logical_or(removed, hit)
        cur = jnp.where(hit, -jnp.inf, cur)
    return vals, removed


def _peer_topk_kernel(q_ref, kh_ref, kl_ref, s0_ref, s1_ref, st_ref, sc_ref):
    qh, ql = _split(q_ref[...])
    kh = kh_ref[...]
    sc_ref[...] = _dg(kh, qh, NT) + (_dg(kh, ql, NT) + _dg(kl_ref[...], qh, NT))

    def head(h, carry):
        t0 = sc_ref[pl.ds(pl.multiple_of(h * 2 * N_KEYS, N_KEYS), N_KEYS), :]
        t1 = sc_ref[pl.ds(pl.multiple_of(h * 2 * N_KEYS + N_KEYS, N_KEYS), N_KEYS), :]
        a, ra = _top16(t0, True)
        b, rb = _top16(t1, True)
        s0_ref[h] = jnp.where(ra, t0, -jnp.inf)
        s1_ref[h] = jnp.where(rb, t1, -jnp.inf)
        bmat = jnp.concatenate(b, axis=0)
        cand = jnp.concatenate([a[i] + bmat for i in range(PEER_TOPK)], axis=0)
        best, _ = _top16(cand, False)
        z = jnp.zeros_like(best[0])
        for n in range(PEER_TOPK):
            z = z + jnp.exp(best[n] - best[0])
        zero = jnp.zeros_like(z)
        st_ref[h] = jnp.concatenate([best[PEER_TOPK - 1], a[0], b[0], 1.0 / z, zero, zero, zero, zero], axis=0)
        return carry

    lax.fori_loop(0, PEER_HEADS, head, 0)


def _peer_topk(q, kdt_hi, kdt_lo, tt=256):
    M = q.shape[0]
    tt = min(tt, M)
    R = PEER_HEADS * 2 * N_KEYS
    big = pl.BlockSpec((PEER_HEADS, N_KEYS, tt), lambda i: (0, 0, i))
    return pl.pallas_call(
        _peer_topk_kernel, grid=(M // tt,),
        in_specs=[pl.BlockSpec((tt, PEER_HEADS * PEER_QDIM), lambda i: (i, 0)),
                  pl.BlockSpec((R, PEER_HEADS * PEER_QDIM), lambda i: (0, 0)),
                  pl.BlockSpec((R, PEER_HEADS * PEER_QDIM), lambda i: (0, 0))],
        out_specs=[big, big, pl.BlockSpec((PEER_HEADS, 8, tt), lambda i: (0, 0, i))],
        out_shape=[jax.ShapeDtypeStruct((PEER_HEADS, N_KEYS, M), F32),
                   jax.ShapeDtypeStruct((PEER_HEADS, N_KEYS, M), F32),
                   jax.ShapeDtypeStruct((PEER_HEADS, 8, M), F32)],
        scratch_shapes=[pltpu.VMEM((R, tt), F32)],
        compiler_params=_cparams(("parallel",)))(q, kdt_hi, kdt_lo)


def _peer_expert_kernel(x_ref, u_ref, vt_ref, s0_ref, s1_ref, st_ref, res_ref, m_ref, o_ref,
                        acc_ref, e0_ref, e1_ref, *, ni, gate_slot):
    e = pl.program_id(1)

    @pl.when(e == 0)
    def _():
        acc_ref[...] = jnp.zeros_like(acc_ref)
        for h in range(PEER_HEADS):
            e0_ref[h] = jnp.exp(s0_ref[h] - st_ref[h, 1:2, :]) * st_ref[h, 3:4, :]
            e1_ref[h] = jnp.exp(s1_ref[h] - st_ref[h, 2:3, :])

    act = _dg(u_ref[...], x_ref[...], NT)
    act = 0.5 * act * (1.0 + lax.erf(act * (2.0 ** -0.5)))
    pieces = []
    for ii in range(ni):
        i = e * ni + ii
        w = None
        for h in range(PEER_HEADS):
            total = s1_ref[h] + s0_ref[h, pl.ds(i, 1), :]
            contrib = jnp.where(total >= st_ref[h, 0:1, :], e1_ref[h], 0.0) * e0_ref[h, pl.ds(i, 1), :]
            w = contrib if w is None else w + contrib
        pieces.append(w)
    wmat = jnp.concatenate(pieces, axis=0) if ni > 1 else pieces[0]
    acc_ref[...] += _dg(vt_ref[...], (act * wmat).astype(BF16))

    @pl.when(e == pl.num_programs(1) - 1)
    def _():
        o_ref[...] = res_ref[...] + m_ref[gate_slot:gate_slot + 1, :] * acc_ref[...].T


def _peer_experts(xn, u_bf, vt_bf, s0m, s1m, st, res, mods, gate_slot, rows_per_mod, tt=512, eb=512):
    M, D = res.shape
    tt = min(tt, M, rows_per_mod) if mods.shape[0] > 1 else min(tt, M)
    ni = eb // N_KEYS
    if mods.shape[0] > 1:
        mspec = pl.BlockSpec((None, 6, D), lambda i, e: ((i * tt) // rows_per_mod, 0, 0))
    else:
        mspec = pl.BlockSpec((None, 6, D), lambda i, e: (0, 0, 0))
    big = pl.BlockSpec((PEER_HEADS, N_KEYS, tt), lambda i, e: (0, 0, i))
    return pl.pallas_call(
        functools.partial(_peer_expert_kernel, ni=ni, gate_slot=gate_slot),
        grid=(M // tt, N_EXPERTS // eb),
        in_specs=[pl.BlockSpec((tt, D), lambda i, e: (i, 0)),
                  pl.BlockSpec((eb, D), lambda i, e: (e, 0)),
                  pl.BlockSpec((D, eb), lambda i, e: (0, e)),
                  big, big, pl.BlockSpec((PEER_HEADS, 8, tt), lambda i, e: (0, 0, i)),
                  pl.BlockSpec((tt, D), lambda i, e: (i, 0)), mspec],
        out_specs=pl.BlockSpec((tt, D), lambda i, e: (i, 0)),
        out_shape=jax.ShapeDtypeStruct((M, D), F32),
        scratch_shapes=[pltpu.VMEM((D, tt), F32), pltpu.VMEM((PEER_HEADS, N_KEYS, tt), F32),
                        pltpu.VMEM((PEER_HEADS, N_KEYS, tt), F32)],
        compiler_params=_cparams(("parallel", "arbitrary"), VMEM_LIMIT_BIG))(
            xn, u_bf, vt_bf, s0m, s1m, st, res, mods)


def _peer(x, mods, rows_per_mod, norm_g, prm):
    q, xn = _mm_norm(x, prm['wq'], norm_g=norm_g, mods=mods, slots=(3, 4), rows_per_mod=rows_per_mod,
                     hp=True, emit_xn=BF16)
    s0m, s1m, st = _peer_topk(q, prm['kdt_hi'], prm['kdt_lo'])
    return _peer_experts(xn, prm['u'], prm['vt'], s0m, s1m, st, x, mods, 5, rows_per_mod)


def _peer_params(wq, keys, u_tab, v_tab):
    kd = jnp.zeros((PEER_HEADS * 2, N_KEYS, PEER_HEADS * 2, PEER_HALF), F32)
    hp_idx = jnp.arange(PEER_HEADS * 2)
    kd = kd.at[hp_idx, :, hp_idx, :].set(keys.reshape(PEER_HEADS * 2, N_KEYS, PEER_HALF))
    kd = kd.reshape(PEER_HEADS * 2 * N_KEYS, PEER_HEADS * PEER_QDIM)
    kd_hi, kd_lo = _split(kd)
    return dict(wq=_split(wq), kdt_hi=kd_hi, kdt_lo=kd_lo, u=u_tab.astype(BF16), vt=v_tab.T.astype(BF16))


def _layer_even(x, mods, B, T, norm1, w_a, w_b, w_out, qk_gains, rwkv_prm, ctx):
    rpm = T
    pa = _mm_norm(x, w_a, norm_g=norm1, mods=mods, slots=(0, 1), rows_per_mod=rpm)
    pb = _mm_norm(x, w_b, norm_g=norm1, mods=mods, slots=(0, 1), rows_per_mod=rpm)
    nqk = A_HEADS + A_KV_HEADS
    grp = A_HEADS // A_KV_HEADS
    scale = A_HEAD_DIM ** -0.5
    if ctx is None:
        qk = _norm_rope(pa, nqk, T, gains=qk_gains)
        oa = _attention(B, T, T, A_HEADS, scale, (qk, lambda h: h), (qk, lambda h: A_HEADS + h // grp),
                        (pa, lambda h: nqk + h // grp))
        ctx_out = (qk[:, A_Q:A_Q + A_KV], pa[:, A_Q + A_KV:])
        s0 = None
    else:
        cache_k, cache_v, s0 = ctx
        P = cache_k.shape[1]
        qk = _norm_rope(pa, nqk, T, gains=qk_gains, tables=_rope_tables(T, A_HEAD_DIM))
        k_all = jnp.concatenate([qk[:, A_Q:].reshape(B, T, A_KV), cache_k.reshape(B, P, A_KV)], axis=1)
        v_all = jnp.concatenate([pa[:, A_Q + A_KV:].reshape(B, T, A_KV), cache_v.reshape(B, P, A_KV)], axis=1)
        Tk = T + P
        oa = _attention(B, T, Tk, A_HEADS, scale, (qk, lambda h: h),
                        (k_all.reshape(B * Tk, A_KV), lambda h: h // grp),
                        (v_all.reshape(B * Tk, A_KV), lambda h: h // grp))
        ctx_out = None
    ob, s_fin = _rwkv_mixer(pb, B, T, rwkv_prm, s0)
    x = _mm_res([oa, ob], w_out, x, mods, 2, rpm)
    return x, ctx_out, s_fin


D_QROPE_BLK = 0
D_KROPE_BLK = D_HEADS
D_NROPE_BLK = D_HEADS + 1
D_CKV_BLK = 12
D_QNOPE_BLK = 16
D_PROJ = 24 * LANES


def _mla_in_weight(w_d):
    K = w_d.shape[0]
    wq = w_d[:, :D_Q].reshape(K, D_HEADS, D_QK)
    z64 = jnp.zeros((K, D_HEADS, LANES - D_ROPE), w_d.dtype)
    q_rope = jnp.concatenate([wq[:, :, D_NOPE:], z64], axis=-1).reshape(K, D_HEADS * LANES)
    q_nope = wq[:, :, :D_NOPE].reshape(K, D_HEADS * LANES)
    k_rope = jnp.concatenate([w_d[:, D_Q + D_KV_RANK:], jnp.zeros((K, LANES - D_ROPE), w_d.dtype)], axis=-1)
    padz = jnp.zeros((K, (D_CKV_BLK - D_NROPE_BLK) * LANES), w_d.dtype)
    return jnp.concatenate([q_rope, k_rope, padz, w_d[:, D_Q:D_Q + D_KV_RANK], q_nope], axis=-1)


def _layer_odd(x, mods, B, T, norm1, w_c, w_d, w_out, gdn_prm, kv_norm_g, w_kv_b, ctx):
    rpm = T
    pc = _mm_norm(x, w_c, norm_g=norm1, mods=mods, slots=(0, 1), rows_per_mod=rpm)
    pd = _mm_norm(x, w_d, norm_g=norm1, mods=mods, slots=(0, 1), rows_per_mod=rpm)
    scale = D_QK ** -0.5
    ckv_col = D_CKV_BLK * LANES // D_KV_RANK
    kv, ckv_n = _mm_norm(pd, w_kv_b, k=D_KV_RANK, xcol=ckv_col, norm_g=kv_norm_g, emit_xn=F32)
    kcol = lambda h: 2 * h
    vcol = lambda h: 2 * h + 1
    if ctx is None:
        od = _attention(B, T, T, D_HEADS, scale, (pd, lambda h: D_QNOPE_BLK + h), (kv, kcol), (kv, vcol),
                        q2=(pd, lambda h: D_QROPE_BLK + h), k2=(pd, lambda h: D_KROPE_BLK))
        ctx_out = (ckv_n, pd[:, D_KROPE_BLK * LANES:D_KROPE_BLK * LANES + D_ROPE])
        s0 = None
    else:
        s0, c_ckv, c_krope = ctx
        P = c_ckv.shape[1]
        Tk = T + P
        roped = _norm_rope(pd, D_NROPE_BLK, T, tables=_rope_tables(T, D_ROPE))
        kv_ctx = _mm_norm(c_ckv.reshape(B * P, D_KV_RANK), w_kv_b)
        kv_all = jnp.concatenate([kv.reshape(B, T, -1), kv_ctx.reshape(B, P, -1)], axis=1).reshape(B * Tk, -1)
        kr_lat = roped[:, D_KROPE_BLK * LANES:].reshape(B, T, LANES)
        kr_ctx = jnp.concatenate([c_krope, jnp.zeros((B, P, LANES - D_ROPE), F32)], axis=-1)
        kr_all = jnp.concatenate([kr_lat, kr_ctx], axis=1).reshape(B * Tk, LANES)
        od = _attention(B, T, Tk, D_HEADS, scale, (pd, lambda h: D_QNOPE_BLK + h), (kv_all, kcol), (kv_all, vcol),
                        q2=(roped, lambda h: D_QROPE_BLK + h), k2=(kr_all, lambda h: 0))
        ctx_out = None
    oc, s_fin = _gdn_mixer(pc, B, T, gdn_prm, s0)
    x = _mm_res([oc, od], w_out, x, mods, 2, rpm)
    return x, ctx_out, s_fin


def _modulation(cond, w_ada, b_ada):
    n = cond.shape[0]
    rows = ((n + 7) // 8) * 8
    a = cond * _sigmoid(cond)
    a = jnp.concatenate([a, jnp.zeros((rows - n, cond.shape[1]), F32)], axis=0)
    m = _mm_norm(a, _split(w_ada), hp=True, tn_cap=1024)[:n] + b_ada
    return m.reshape(n, 6, D_MODEL)


def kernel(x_prompt, x_sample, cache_attn_k, cache_attn_v, state_rwkv, state_gdn, cache_mla_ckv, cache_mla_krope,
           c, c_ctx, norm1_g, norm2_g, ada_w, ada_b, w_in_ab, w_out_ab, attn_q_norm, attn_k_norm,
           rwkv_mu, rwkv_w0, rwkv_w2, rwkv_a0, rwkv_a2, rwkv_g2, rwkv_k_k, rwkv_k_a, rwkv_r_k, rwkv_ln_w, rwkv_ln_b,
           w_in_cd, w_out_cd, gdn_conv, gdn_a_log, gdn_dt_bias, gdn_norm_g, mla_kv_norm_g, mla_w_kv_b,
           peer_wq, peer_keys, peer_u, peer_v, final_norm_g):
    Bp, Tp, D = x_prompt.shape
    Bs, Ts, _ = x_sample.shape
    depth = norm1_g.shape[0]
    xp = x_prompt.reshape(Bp * Tp, D)
    xs = x_sample.reshape(Bs * Ts, D)
    cond = jnp.concatenate([c_ctx[None, :], c], axis=0)
    outs = dict(k=[], v=[], rwkv=[], gdn=[], ckv=[], krope=[])
    for li in range(depth):
        j = li // 2
        mods = _modulation(cond, ada_w[li], ada_b[li])
        mods_p, mods_s = mods[:1], mods[1:]
        peer_prm = _peer_params(peer_wq[li], peer_keys[li], peer_u[li], peer_v[li])
        if li % 2 == 0:
            w_in = w_in_ab[j].astype(BF16)
            w_a, w_b = w_in[:, :A_IN], w_in[:, A_IN:]
            w_o = w_out_ab[j].astype(BF16)
            w_out = [w_o[:A_Q], w_o[A_Q:]]
            gains = jnp.concatenate([jnp.tile(attn_q_norm[j], A_HEADS), jnp.tile(attn_k_norm[j], A_KV_HEADS)])
            rp = dict(mu=rwkv_mu[j], w0=rwkv_w0[j], w2=rwkv_w2[j], a0=rwkv_a0[j], a2=rwkv_a2[j], g2=rwkv_g2[j],
                      k_k=rwkv_k_k[j], k_a=rwkv_k_a[j], r_k=rwkv_r_k[j], ln_w=rwkv_ln_w[j], ln_b=rwkv_ln_b[j])
            xp, (k_c, v_c), s_c = _layer_even(xp, mods_p, Bp, Tp, norm1_g[li], w_a, w_b, w_out, gains, rp, None)
            xs, _, _ = _layer_even(xs, mods_s, Bs, Ts, norm1_g[li], w_a, w_b, w_out, gains, rp,
                                   (cache_attn_k[:, j], cache_attn_v[:, j], state_rwkv[:, j]))
            outs['k'].append(k_c.reshape(Bp, Tp, A_KV_HEADS, A_HEAD_DIM))
            outs['v'].append(v_c.reshape(Bp, Tp, A_KV_HEADS, A_HEAD_DIM))
            outs['rwkv'].append(s_c)
        else:
            w_in = w_in_cd[j]
            w_c = jnp.concatenate([w_in[:, :C_IN], jnp.zeros((D, LANES - 4 * C_HEADS), F32)], axis=-1).astype(BF16)
            w_d = _mla_in_weight(w_in[:, C_IN:]).astype(BF16)
            w_o = w_out_cd[j].astype(BF16)
            w_out = [w_o[:C_W], w_o[C_W:]]
            gp = dict(conv=gdn_conv[j], a_log=gdn_a_log[j], dt_bias=gdn_dt_bias[j], norm_g=gdn_norm_g[j])
            w_kv_b = mla_w_kv_b[j].astype(BF16)
            xp, (ckv_c, kr_c), s_c = _layer_odd(xp, mods_p, Bp, Tp, norm1_g[li], w_c, w_d, w_out, gp,
                                                mla_kv_norm_g[j], w_kv_b, None)
            xs, _, _ = _layer_odd(xs, mods_s, Bs, Ts, norm1_g[li], w_c, w_d, w_out, gp, mla_kv_norm_g[j], w_kv_b,
                                  (state_gdn[:, j], cache_mla_ckv[:, j], cache_mla_krope[:, j]))
            outs['gdn'].append(s_c)
            outs['ckv'].append(ckv_c.reshape(Bp, Tp, D_KV_RANK))
            outs['krope'].append(kr_c.reshape(Bp, Tp, D_ROPE))
        xp = _peer(xp, mods_p, Tp, norm2_g[li], peer_prm)
        xs = _peer(xs, mods_s, Ts, norm2_g[li], peer_prm)
    y_prompt = _rmsnorm(xp, final_norm_g).reshape(Bp, Tp, D)
    y_sample = _rmsnorm(xs, final_norm_g).reshape(Bs, Ts, D)
    st = lambda name: jnp.stack(outs[name], axis=1)
    return (y_prompt, y_sample, st('k'), st('v'), st('rwkv'), st('gdn'), st('ckv'), st('krope'))
```

```python
import functools
import math

import numpy as np
import jax
import jax.numpy as jnp
from jax import lax
from jax.experimental import pallas as pl
from jax.experimental.pallas import tpu as pltpu

F32 = jnp.float32
BF16 = jnp.bfloat16

D_MODEL = 2048
GRID_W = 64
ROPE_THETA = 10000.0
NORM_EPS = 1e-6
A_HEADS, A_KV_HEADS, A_HEAD_DIM = 8, 2, 128
A_Q = A_HEADS * A_HEAD_DIM
A_KV = A_KV_HEADS * A_HEAD_DIM
A_IN = A_Q + 2 * A_KV
B_HEADS, B_HEAD_DIM = 16, 64
B_W = B_HEADS * B_HEAD_DIM
B_DECAY_LORA, B_AAA_LORA, B_GATE_LORA = 64, 64, 128
B_IN = 3 * B_W + 2 * B_DECAY_LORA + 2 * B_AAA_LORA + B_GATE_LORA
B_GN_EPS = 64e-5
C_HEADS, C_HEAD_DIM = 8, 128
C_W = C_HEADS * C_HEAD_DIM
C_IN = 4 * C_W + 4 * C_HEADS
D_HEADS, D_NOPE, D_ROPE, D_V = 8, 128, 64, 128
D_QK = D_NOPE + D_ROPE
D_KV_RANK = 512
D_Q = D_HEADS * D_QK
PEER_HEADS, N_KEYS, PEER_QDIM, PEER_TOPK = 8, 128, 128, 16
PEER_HALF = PEER_QDIM // 2
N_EXPERTS = N_KEYS * N_KEYS

LANES = 128
MXU_K = 256
VMEM_LIMIT = 48 * 1024 * 1024
VMEM_LIMIT_BIG = 56 * 1024 * 1024

CHUNK = 64

NN = ((1,), (0,))
NT = ((1,), (1,))
TN = ((0,), (0,))


def _dg(a, b, dims=NN):
    return lax.dot_general(a, b, (dims, ((), ())), preferred_element_type=F32)


def _split(x):
    hi = x.astype(BF16)
    lo = (x - hi.astype(F32)).astype(BF16)
    return hi, lo


def _dot_hp(a, b, dims=NN):
    ah, al = _split(a)
    bh, bl = _split(b)
    return _dg(ah, bh, dims) + (_dg(ah, bl, dims) + _dg(al, bh, dims))


def _dot_bf(a, b, dims=NN):
    return _dg(a.astype(BF16), b.astype(BF16), dims)


def _sigmoid(x):
    return 1.0 / (1.0 + jnp.exp(-x))


def _softplus(x):
    return jnp.maximum(x, 0.0) + jnp.log(1.0 + jnp.exp(-jnp.abs(x)))


def _cparams(sem, vmem=VMEM_LIMIT):
    return pltpu.CompilerParams(dimension_semantics=sem, vmem_limit_bytes=vmem)


def _pick_tile(n, cap):
    best = None
    t = LANES
    while t <= min(n, cap):
        if n % t == 0:
            best = t
        t += LANES
    return best if best is not None else n


def _mm_norm_kernel(*refs, has_norm, has_mod, shift_slot, scale_slot, hp, emit_xn):
    it = iter(refs)
    x_ref = next(it)
    g_ref = next(it) if has_norm else None
    m_ref = next(it) if has_mod else None
    wh_ref = next(it)
    wl_ref = next(it) if hp else None
    o_ref = next(it)
    xn_ref = next(it) if emit_xn else None
    xh_ref = next(it)
    xl_ref = next(it) if hp else None

    @pl.when(pl.program_id(1) == 0)
    def _():
        xv = x_ref[...].astype(F32)
        if has_norm:
            xv = xv * lax.rsqrt(jnp.mean(xv * xv, axis=-1, keepdims=True) + NORM_EPS) * g_ref[...]
        if has_mod:
            xv = xv * (1.0 + m_ref[scale_slot:scale_slot + 1, :]) + m_ref[shift_slot:shift_slot + 1, :]
        hi = xv.astype(BF16)
        xh_ref[...] = hi
        if hp:
            xl_ref[...] = (xv - hi.astype(F32)).astype(BF16)
        if emit_xn:
            xn_ref[...] = xv.astype(xn_ref.dtype)

    acc = _dg(xh_ref[...], wh_ref[...])
    if hp:
        acc = acc + (_dg(xh_ref[...], wl_ref[...]) + _dg(xl_ref[...], wh_ref[...]))
    o_ref[...] = acc


def _mm_norm(x, w, *, k=None, xcol=0, norm_g=None, mods=None, slots=(0, 1), rows_per_mod=None,
             hp=False, emit_xn=None, tm=512, tn_cap=1664):
    M = x.shape[0]
    K = k if k is not None else x.shape[1]
    ws = tuple(w) if isinstance(w, (tuple, list)) else (w,)
    N = ws[0].shape[1]
    tm = min(tm, M)
    if mods is not None and mods.shape[0] > 1:
        tm = min(tm, rows_per_mod)
    tn = _pick_tile(N, tn_cap)
    has_norm = norm_g is not None
    has_mod = mods is not None
    rpm = rows_per_mod if rows_per_mod is not None else M

    in_specs = [pl.BlockSpec((tm, K), lambda i, j: (i, xcol))]
    args = [x]
    if has_norm:
        in_specs.append(pl.BlockSpec((1, K), lambda i, j: (0, 0)))
        args.append(norm_g.reshape(1, K).astype(F32))
    if has_mod:
        if mods.shape[0] > 1:
            in_specs.append(pl.BlockSpec((None, 6, K), lambda i, j: ((i * tm) // rpm, 0, 0)))
        else:
            in_specs.append(pl.BlockSpec((None, 6, K), lambda i, j: (0, 0, 0)))
        args.append(mods)
    for wi in ws:
        in_specs.append(pl.BlockSpec((K, tn), lambda i, j: (0, j)))
        args.append(wi)
    out_shape = [jax.ShapeDtypeStruct((M, N), F32)]
    out_specs = [pl.BlockSpec((tm, tn), lambda i, j: (i, j))]
    if emit_xn is not None:
        out_shape.append(jax.ShapeDtypeStruct((M, K), emit_xn))
        out_specs.append(pl.BlockSpec((tm, K), lambda i, j: (i, 0)))
    scratch = [pltpu.VMEM((tm, K), BF16)]
    if hp:
        scratch.append(pltpu.VMEM((tm, K), BF16))
    kern = functools.partial(_mm_norm_kernel, has_norm=has_norm, has_mod=has_mod, shift_slot=slots[0],
                             scale_slot=slots[1], hp=hp, emit_xn=emit_xn is not None)
    outs = pl.pallas_call(
        kern, grid=(M // tm, N // tn), in_specs=in_specs, out_specs=out_specs, out_shape=out_shape,
        scratch_shapes=scratch, compiler_params=_cparams(("parallel", "arbitrary")))(*args)
    return outs if emit_xn is not None else outs[0]


def _mm_res_kernel(*refs, n_in, gate_slot):
    a_refs = refs[:n_in]
    w_refs = refs[n_in:2 * n_in]
    res_ref, m_ref, o_ref = refs[2 * n_in:]
    acc = None
    for a_ref, w_ref in zip(a_refs, w_refs):
        part = _dg(a_ref[...].astype(BF16), w_ref[...])
        acc = part if acc is None else acc + part
    o_ref[...] = res_ref[...] + m_ref[gate_slot:gate_slot + 1, :] * acc


def _mm_res(a_list, w_list, res, mods, gate_slot, rows_per_mod, tm=512, tn=1024):
    M, N = res.shape
    tm = min(tm, M, rows_per_mod) if mods.shape[0] > 1 else min(tm, M)
    tn = min(tn, N)
    n_in = len(a_list)
    in_specs, args = [], []
    for a in a_list:
        in_specs.append(pl.BlockSpec((tm, a.shape[1]), lambda i, j: (i, 0)))
        args.append(a)
    for w in w_list:
        in_specs.append(pl.BlockSpec((w.shape[0], tn), lambda i, j: (0, j)))
        args.append(w)
    in_specs.append(pl.BlockSpec((tm, tn), lambda i, j: (i, j)))
    args.append(res)
    if mods.shape[0] > 1:
        in_specs.append(pl.BlockSpec((None, 6, tn), lambda i, j: ((i * tm) // rows_per_mod, 0, j)))
    else:
        in_specs.append(pl.BlockSpec((None, 6, tn), lambda i, j: (0, 0, j)))
    args.append(mods)
    return pl.pallas_call(
        functools.partial(_mm_res_kernel, n_in=n_in, gate_slot=gate_slot),
        grid=(M // tm, N // tn), in_specs=in_specs,
        out_specs=pl.BlockSpec((tm, tn), lambda i, j: (i, j)),
        out_shape=jax.ShapeDtypeStruct((M, N), F32),
        compiler_params=_cparams(("parallel", "arbitrary")))(*args)


def _rmsnorm_kernel(x_ref, g_ref, o_ref):
    xv = x_ref[...]
    o_ref[...] = xv * lax.rsqrt(jnp.mean(xv * xv, axis=-1, keepdims=True) + NORM_EPS) * g_ref[...]


def _rmsnorm(x, g, tm=512):
    M, K = x.shape
    tm = min(tm, M)
    return pl.pallas_call(
        _rmsnorm_kernel, grid=(M // tm,),
        in_specs=[pl.BlockSpec((tm, K), lambda i: (i, 0)), pl.BlockSpec((1, K), lambda i: (0, 0))],
        out_specs=pl.BlockSpec((tm, K), lambda i: (i, 0)),
        out_shape=jax.ShapeDtypeStruct((M, K), F32),
        compiler_params=_cparams(("parallel",)))(x, g.reshape(1, K))


def _rope_tables(T, rot_dim):
    rows = T // GRID_W
    row = jnp.repeat(jnp.arange(rows, dtype=F32), GRID_W)
    col = jnp.tile(jnp.arange(GRID_W, dtype=F32), rows)
    n_freq = rot_dim // 4
    inv = ROPE_THETA ** (-jnp.arange(n_freq, dtype=F32) / n_freq)
    ang = jnp.concatenate([row[:, None] * inv, col[:, None] * inv], axis=-1)
    cos = jnp.repeat(jnp.cos(ang), 2, axis=-1)
    sin = jnp.repeat(jnp.sin(ang), 2, axis=-1) * jnp.tile(jnp.array([-1.0, 1.0], F32), rot_dim // 2)
    pad = LANES - rot_dim
    if pad:
        cos = jnp.concatenate([cos, jnp.ones((T, pad), F32)], axis=-1)
        sin = jnp.concatenate([sin, jnp.zeros((T, pad), F32)], axis=-1)
    return cos, sin


def _rope(y, cosf, sinf):
    lane = lax.broadcasted_iota(jnp.int32, y.shape, 1)
    even = (lane & 1) == 0
    width = y.shape[1]
    swap = jnp.where(even, pltpu.roll(y, width - 1, 1), pltpu.roll(y, 1, 1))
    return y * cosf + swap * sinf


def _norm_rope_kernel(*refs, nblk, norm, rope):
    it = iter(refs)
    x_ref = next(it)
    g_ref = next(it) if norm else None
    cos_ref = next(it) if rope else None
    sin_ref = next(it) if rope else None
    o_ref = next(it)
    for c in range(nblk):
        sl = slice(c * LANES, (c + 1) * LANES)
        y = x_ref[:, sl]
        if norm:
            y = y * lax.rsqrt(jnp.mean(y * y, axis=-1, keepdims=True) + NORM_EPS) * g_ref[:, sl]
        if rope:
            y = _rope(y, cos_ref[...], sin_ref[...])
        o_ref[:, sl] = y


def _norm_rope(x, nblk, T, gains=None, tables=None, tt=256):
    M = x.shape[0]
    tt = min(tt, T)
    W = nblk * LANES
    in_specs = [pl.BlockSpec((tt, W), lambda i: (i, 0))]
    args = [x]
    if gains is not None:
        in_specs.append(pl.BlockSpec((1, W), lambda i: (0, 0)))
        args.append(gains.reshape(1, W))
    if tables is not None:
        nt = T // tt
        for t in tables:
            in_specs.append(pl.BlockSpec((tt, LANES), lambda i: (i % nt, 0)))
            args.append(t)
    return pl.pallas_call(
        functools.partial(_norm_rope_kernel, nblk=nblk, norm=gains is not None, rope=tables is not None),
        grid=(M // tt,), in_specs=in_specs,
        out_specs=pl.BlockSpec((tt, W), lambda i: (i, 0)),
        out_shape=jax.ShapeDtypeStruct((M, W), F32),
        compiler_params=_cparams(("parallel",)))(*args)


def _attn_kernel(*refs, scale, has2):
    it = iter(refs)
    q_ref = next(it)
    q2_ref = next(it) if has2 else None
    k_ref = next(it)
    k2_ref = next(it) if has2 else None
    v_ref = next(it)
    o_ref = next(it)
    s = _dot_bf(q_ref[...] * scale, k_ref[...], NT)
    if has2:
        s = s + _dot_bf(q2_ref[...] * scale, k2_ref[...], NT)
    m = jnp.max(s, axis=-1, keepdims=True)
    p = jnp.exp(s - m)
    l = jnp.sum(p, axis=-1, keepdims=True)
    o_ref[...] = _dot_bf(p, v_ref[...]) / l


def _attention(B, T, Tk, H, scale, q, k, v, q2=None, k2=None, tq=256):
    tq = min(tq, T)
    nq = T // tq
    has2 = q2 is not None

    def qspec(cf):
        return pl.BlockSpec((tq, LANES), lambda b, h, i: (b * nq + i, cf(h)))

    def kspec(cf):
        return pl.BlockSpec((Tk, LANES), lambda b, h, i: (b, cf(h)))

    in_specs, args = [qspec(q[1])], [q[0]]
    if has2:
        in_specs.append(qspec(q2[1]))
        args.append(q2[0])
    in_specs.append(kspec(k[1]))
    args.append(k[0])
    if has2:
        in_specs.append(kspec(k2[1]))
        args.append(k2[0])
    in_specs.append(kspec(v[1]))
    args.append(v[0])
    return pl.pallas_call(
        functools.partial(_attn_kernel, scale=scale, has2=has2),
        grid=(B, H, nq), in_specs=in_specs,
        out_specs=pl.BlockSpec((tq, LANES), lambda b, h, i: (b * nq + i, h)),
        out_shape=jax.ShapeDtypeStruct((B * T, H * LANES), F32),
        compiler_params=_cparams(("parallel", "arbitrary", "arbitrary")))(*args)


def _tri_masks(L, rev):
    i = lax.broadcasted_iota(jnp.int32, (L, L), 0)
    j = lax.broadcasted_iota(jnp.int32, (L, L), 1)
    if rev:
        return j >= i, j > i, i == j
    return j <= i, j < i, i == j


def _inv_unit_tri(mats, eye_f, L):
    pows = [[-a for a in mats]]
    for _ in range(int(math.log2(L)) - 1):
        pows.append([_dot_hp(p, p) for p in pows[-1]])
    terms = [[eye_f + p for p in level] for level in pows]
    while len(terms) > 1:
        nxt = [[_dot_hp(x, y) for x, y in zip(terms[i], terms[i + 1])] for i in range(0, len(terms) - 1, 2)]
        if len(terms) % 2:
            nxt.append(terms[-1])
        terms = nxt
    return terms[0]


def _shifted(f, prev_ref, next_ref, tt, T):
    i = pl.program_id(0)
    first = (i * tt) % T == 0
    last = ((i + 1) * tt) % T == 0
    prev = jnp.where(first, 0.0, prev_ref[7:8, :])
    nxt = jnp.where(last, 0.0, next_ref[0:1, :])
    rows = lax.broadcasted_iota(jnp.int32, f.shape, 0)
    fp = jnp.where(rows == 0, prev, pltpu.roll(f, 1, 0))
    fn = jnp.where(rows == tt - 1, nxt, pltpu.roll(f, tt - 1, 0))
    return fp, fn


def _halo_specs(tt, width, M, col_fn):
    r8 = tt // 8
    last8 = M // 8 - 1
    return [
        pl.BlockSpec((tt, width), lambda *g: (g[0], col_fn(*g))),
        pl.BlockSpec((8, width), lambda *g: (jnp.maximum(g[0] * r8 - 1, 0), col_fn(*g))),
        pl.BlockSpec((8, width), lambda *g: (jnp.minimum((g[0] + 1) * r8, last8), col_fn(*g))),
    ]


NPAIR = B_HEADS // 2


def _pair_sum_matrix():
    r = lax.broadcasted_iota(jnp.int32, (LANES, LANES), 0) // B_HEAD_DIM
    c = lax.broadcasted_iota(jnp.int32, (LANES, LANES), 1) // B_HEAD_DIM
    return (r == c).astype(F32)


def _rwkv_prep_kernel(x_ref, prev_ref, next_ref, mu_ref, w0_ref, w2_ref, a0_ref, a2_ref, g2_ref, kk_ref_in,
                      r_o, k_o, v_o, kk_o, g_o, lw0_o, lw1_o, a0_o, a1_o, *, tt, T):
    f = x_ref[...]
    fp, fn = _shifted(f, prev_ref, next_ref, tt, T)
    f = f + mu_ref[0:1, :] * (fp - f) + mu_ref[1:2, :] * (fn - f)
    r = f[:, :B_W]
    k = f[:, B_W:2 * B_W]
    v = f[:, 2 * B_W:3 * B_W]
    off = 3 * B_W
    w_lo = f[:, off:off + 2 * B_DECAY_LORA]
    off += 2 * B_DECAY_LORA
    a_lo = f[:, off:off + 2 * B_AAA_LORA]
    off += 2 * B_AAA_LORA
    g_lo = f[:, off:]
    th = jnp.tanh(w_lo)
    logw, aa = [], []
    for d in range(2):
        wpre = w0_ref[d:d + 1, :] + _dot_hp(th[:, d * B_DECAY_LORA:(d + 1) * B_DECAY_LORA], w2_ref[d])
        wv = -_softplus(-wpre) - 0.5
        logw.append(-jnp.exp(wv))
        aa.append(_sigmoid(a0_ref[d:d + 1, :] + _dot_hp(a_lo[:, d * B_AAA_LORA:(d + 1) * B_AAA_LORA], a2_ref[d])))
    g = _dot_hp(_sigmoid(g_lo), g2_ref[...])
    kk = k * kk_ref_in[...]
    gsum = _pair_sum_matrix()
    for p in range(NPAIR):
        sl = slice(p * LANES, (p + 1) * LANES)
        kp = kk[:, sl]
        ssq = _dot_hp(kp * kp, gsum)
        kk_o[p] = kp / jnp.maximum(jnp.sqrt(ssq), 1e-12)
        r_o[p] = r[:, sl]
        k_o[p] = k[:, sl]
        v_o[p] = v[:, sl]
        g_o[p] = g[:, sl]
        lw0_o[p] = logw[0][:, sl]
        lw1_o[p] = logw[1][:, sl]
        a0_o[p] = aa[0][:, sl]
        a1_o[p] = aa[1][:, sl]


def _rwkv_prep(pb, B, T, mu, w0, w2, a0, a2, g2, k_k, tt=128):
    M = pb.shape[0]
    tt = min(tt, T)
    nt = T // tt
    full = lambda shape: pl.BlockSpec(shape, lambda i: tuple(0 for _ in shape))
    in_specs = _halo_specs(tt, B_IN, M, lambda i: 0) + [
        full((2, B_IN)), full((2, B_W)), full((2, B_DECAY_LORA, B_W)), full((2, B_W)),
        full((2, B_AAA_LORA, B_W)), full((B_GATE_LORA, B_W)), full((1, B_W))]
    ospec = pl.BlockSpec((None, NPAIR, tt, LANES), lambda i: (i // nt, 0, i % nt, 0))
    oshape = jax.ShapeDtypeStruct((B, NPAIR, T, LANES), F32)
    return pl.pallas_call(
        functools.partial(_rwkv_prep_kernel, tt=tt, T=T),
        grid=(M // tt,), in_specs=in_specs, out_specs=[ospec] * 9, out_shape=[oshape] * 9,
        compiler_params=_cparams(("parallel",)))(pb, pb, pb, mu, w0, w2, a0, a2, g2, k_k.reshape(1, B_W))


def _rwkv_chunk(r2, lw2, kd2, v2, kk2, a2, states, masks, L, rev):
    incl, strict, eye = masks
    incl_f = incl.astype(F32)
    eye_f = eye.astype(F32)
    i2 = lax.broadcasted_iota(jnp.int32, (L, 2 * L), 0)
    j2 = lax.broadcasted_iota(jnp.int32, (L, 2 * L), 1) % L
    incl2 = (j2 >= i2) if rev else (j2 <= i2)
    last = 0 if rev else L - 1
    halves = (slice(0, B_HEAD_DIM), slice(B_HEAD_DIM, 2 * B_HEAD_DIM))
    logp2 = [_dot_hp(incl_f, lw) for lw in lw2]
    lhs, rhs, rem_rhs, dec, vs = [], [], [], [], []
    for lp, lw, r, kd, v, kk, a in zip(logp2, lw2, r2, kd2, v2, kk2, a2):
        ninv = jnp.exp(-lp)
        b = kk * a
        kt = kk * jnp.exp(lp - lw)
        rt = r * jnp.exp(lp)
        bn, kdn = b * ninv, kd * ninv
        pl_row = lp[last:last + 1, :]
        rem = jnp.exp(pl_row - lp)
        bh, kh = b * rem, kd * rem
        dpl = jnp.exp(pl_row)
        for sl in halves:
            lhs.append(jnp.concatenate([kt[:, sl], rt[:, sl]], axis=0))
            rhs.append(jnp.concatenate([bn[:, sl], kdn[:, sl]], axis=0))
            rem_rhs.append(jnp.concatenate([bh[:, sl], kh[:, sl]], axis=0))
            dec.append(dpl[:, sl])
            vs.append(v[:, sl])
    gm = [_dot_hp(x, y, NT) for x, y in zip(lhs, rhs)]
    ls = [_dot_hp(x, s, NT) for x, s in zip(lhs, states)]
    akv = [_dot_hp(jnp.where(strict, g[:L, L:], 0.0), v) for g, v in zip(gm, vs)]
    tinv = _inv_unit_tri([jnp.where(strict, g[:L, :L], 0.0) for g in gm], eye_f, L)
    u = [-_dot_hp(t, l[:L] + x) for t, l, x in zip(tinv, ls, akv)]
    uv = [jnp.concatenate([x, v], axis=0) for x, v in zip(u, vs)]
    y = [l[L:] + _dot_hp(jnp.where(incl2, g[L:, :], 0.0), x) for l, g, x in zip(ls, gm, uv)]
    s_new = [s * d + _dot_hp(x, m, TN) for s, d, x, m in zip(states, dec, uv, rem_rhs)]
    y2 = [jnp.concatenate([y[2 * p], y[2 * p + 1]], axis=1) for p in range(len(r2))]
    return y2, s_new


def _rwkv_scan_kernel(r_ref, v_ref, kk_ref, k_ref, lw_ref, a_ref, ka_ref, s0_ref, y_ref, sf_ref, s_ref, *, L, rev):
    c = pl.program_id(1)

    @pl.when(c == 0)
    def _():
        s_ref[...] = s0_ref[...]

    masks = _tri_masks(L, rev)
    pairs = range(NPAIR)
    a2 = [a_ref[p] for p in pairs]
    kd2 = [k_ref[p] * (1.0 + (a - 1.0) * ka_ref[p]) for p, a in zip(pairs, a2)]
    y2, s_new = _rwkv_chunk([r_ref[p] for p in pairs], [lw_ref[p] for p in pairs], kd2, [v_ref[p] for p in pairs],
                            [kk_ref[p] for p in pairs], a2, [s_ref[h] for h in range(B_HEADS)], masks, L, rev)
    for p in pairs:
        y_ref[p] = y2[p]
    for h in range(B_HEADS):
        s_ref[h] = s_new[h]

    @pl.when(c == pl.num_programs(1) - 1)
    def _():
        sf_ref[...] = s_ref[...]


def _rwkv_scan(r, v, kk, k, lw, a, k_a, s0, rev, L=CHUNK):
    B, _, T, _ = r.shape
    nc = T // L
    cmap = (lambda b, c: (b, 0, nc - 1 - c, 0)) if rev else (lambda b, c: (b, 0, c, 0))
    seq = pl.BlockSpec((None, NPAIR, L, LANES), cmap)
    st = pl.BlockSpec((None, B_HEADS, B_HEAD_DIM, B_HEAD_DIM), lambda b, c: (b, 0, 0, 0))
    return pl.pallas_call(
        functools.partial(_rwkv_scan_kernel, L=L, rev=rev),
        grid=(B, nc),
        in_specs=[seq] * 6 + [pl.BlockSpec((NPAIR, 1, LANES), lambda b, c: (0, 0, 0)), st],
        out_specs=[seq, st],
        out_shape=[jax.ShapeDtypeStruct((B, NPAIR, T, LANES), F32),
                   jax.ShapeDtypeStruct((B, B_HEADS, B_HEAD_DIM, B_HEAD_DIM), F32)],
        scratch_shapes=[pltpu.VMEM((B_HEADS, B_HEAD_DIM, B_HEAD_DIM), F32)],
        compiler_params=_cparams(("parallel", "arbitrary")))(r, v, kk, k, lw, a, k_a.reshape(NPAIR, 1, LANES), s0)


def _rwkv_post_kernel(y0_ref, y1_ref, r_ref, k_ref, v_ref, g_ref, rk_ref, lnw_ref, lnb_ref, o_ref):
    gsum = _pair_sum_matrix()
    inv_n = 1.0 / B_HEAD_DIM
    for p in range(NPAIR):
        y = y0_ref[p] + y1_ref[p]
        mean = _dot_hp(y, gsum) * inv_n
        yc = y - mean
        var = _dot_hp(yc * yc, gsum) * inv_n
        yn = yc * lax.rsqrt(var + B_GN_EPS) * lnw_ref[p] + lnb_ref[p]
        bonus = _dot_hp(r_ref[p] * k_ref[p] * rk_ref[p], gsum) * v_ref[p]
        o_ref[:, p * LANES:(p + 1) * LANES] = (yn + bonus) * g_ref[p]


def _rwkv_post(y0, y1, r, k, v, g, r_k, ln_w, ln_b, tt=256):
    B, _, T, _ = r.shape
    tt = min(tt, T)
    nt = T // tt
    seq = pl.BlockSpec((None, NPAIR, tt, LANES), lambda i: (i // nt, 0, i % nt, 0))
    par = pl.BlockSpec((NPAIR, 1, LANES), lambda i: (0, 0, 0))
    return pl.pallas_call(
        _rwkv_post_kernel, grid=(B * nt,),
        in_specs=[seq] * 6 + [par] * 3,
        out_specs=pl.BlockSpec((tt, B_W), lambda i: (i, 0)),
        out_shape=jax.ShapeDtypeStruct((B * T, B_W), F32),
        compiler_params=_cparams(("parallel",)))(
            y0, y1, r, k, v, g, r_k.reshape(NPAIR, 1, LANES), ln_w.reshape(NPAIR, 1, LANES),
            ln_b.reshape(NPAIR, 1, LANES))


def _rwkv_mixer(pb, B, T, prm, s0):
    r, k, v, kk, g, lw0, lw1, a0, a1 = _rwkv_prep(pb, B, T, prm['mu'], prm['w0'], prm['w2'], prm['a0'],
                                                   prm['a2'], prm['g2'], prm['k_k'])
    if s0 is None:
        s0 = jnp.zeros((B, 2, B_HEADS, B_HEAD_DIM, B_HEAD_DIM), F32)
    y0, sf0 = _rwkv_scan(r, v, kk, k, lw0, a0, prm['k_a'], s0[:, 0], rev=False)
    y1, sf1 = _rwkv_scan(r, v, kk, k, lw1, a1, prm['k_a'], s0[:, 1], rev=True)
    out = _rwkv_post(y0, y1, r, k, v, g, prm['r_k'], prm['ln_w'], prm['ln_b'])
    return out, jnp.stack([sf0, sf1], axis=1)


def _gdn_prep_kernel(x_ref, prev_ref, next_ref, cw_ref, o_ref, *, tt, T):
    c = pl.program_id(1)
    x = x_ref[...]
    xp, xn = _shifted(x, prev_ref, next_ref, tt, T)
    y = cw_ref[0:1, :] * xp + cw_ref[1:2, :] * x + cw_ref[2:3, :] * xn
    y = y * _sigmoid(y)

    @pl.when(c == 2)
    def _():
        o_ref[...] = y

    @pl.when(c < 2)
    def _():
        scale = jnp.where(c == 0, C_HEAD_DIM ** -0.5, 1.0).astype(F32)
        for h in range(C_HEADS):
            sl = slice(h * C_HEAD_DIM, (h + 1) * C_HEAD_DIM)
            yh = y[:, sl]
            o_ref[:, sl] = yh * lax.rsqrt(jnp.sum(yh * yh, axis=-1, keepdims=True) + 1e-6) * scale


def _gdn_prep(pc, T, conv_w, tt=256):
    M = pc.shape[0]
    tt = min(tt, T)
    in_specs = _halo_specs(tt, C_W, M, lambda i, c: c) + [pl.BlockSpec((3, C_W), lambda i, c: (0, c))]
    return pl.pallas_call(
        functools.partial(_gdn_prep_kernel, tt=tt, T=T),
        grid=(M // tt, 3), in_specs=in_specs,
        out_specs=pl.BlockSpec((tt, C_W), lambda i, c: (i, c)),
        out_shape=jax.ShapeDtypeStruct((M, 3 * C_W), F32),
        compiler_params=_cparams(("parallel", "arbitrary")))(pc, pc, pc, conv_w)


def _gdn_bg_kernel(x_ref, alog_ref, dtb_ref, o_ref):
    x = x_ref[...]
    lane = lax.broadcasted_iota(jnp.int32, x.shape, 1)
    beta = _sigmoid(x)
    g = -jnp.exp(alog_ref[...]) * _softplus(x + dtb_ref[...])
    o_ref[...] = jnp.where(lane < 2 * C_HEADS, beta, g)


def _gdn_bg(pc, a_log, dt_bias, tt=512):
    M = pc.shape[0]
    tt = min(tt, M)
    nb = 2 * C_HEADS
    pad = lambda p: jnp.concatenate([jnp.zeros((nb,), F32), p.reshape(nb), jnp.zeros((LANES - 2 * nb,), F32)]).reshape(1, LANES)
    return pl.pallas_call(
        _gdn_bg_kernel, grid=(M // tt,),
        in_specs=[pl.BlockSpec((tt, LANES), lambda i: (i, 4 * C_W // LANES)),
                  pl.BlockSpec((1, LANES), lambda i: (0, 0)), pl.BlockSpec((1, LANES), lambda i: (0, 0))],
        out_specs=pl.BlockSpec((tt, LANES), lambda i: (i, 0)),
        out_shape=jax.ShapeDtypeStruct((M, LANES), F32),
        compiler_params=_cparams(("parallel",)))(pc, pad(a_log), pad(dt_bias))


def _gdn_chunk(qs, ks, vs, bcol, gcol, grow, states, masks, L, rev):
    incl, strict, eye = masks
    incl_f = incl.astype(F32)
    eye_f = eye.astype(F32)
    nh = len(qs)
    i = lax.broadcasted_iota(jnp.int32, (L, L), 0)
    j = lax.broadcasted_iota(jnp.int32, (L, L), 1)
    cum_t = ((i >= j) if rev else (i <= j)).astype(F32)
    gc_cols = _dot_hp(incl_f, gcol)
    gc_rows = _dot_hp(grow, cum_t)
    last = 0 if rev else L - 1
    gcs, decays, kbs, rhss, kends, g_ends = [], [], [], [], [], []
    for h in range(nh):
        gc = jnp.broadcast_to(gc_cols[:, h:h + 1], (L, LANES))
        decays.append(jnp.exp(jnp.where(incl, gc[:, :L] - gc_rows[h:h + 1, :], -1e30)))
        beta = bcol[:, h:h + 1]
        kb = ks[h] * beta
        kbs.append(kb)
        rhss.append(jnp.concatenate([vs[h] * beta, kb * jnp.exp(gc)], axis=1))
        g_end = gc[last:last + 1, :]
        kends.append(ks[h] * jnp.exp(g_end - gc))
        g_ends.append(jnp.exp(g_end))
        gcs.append(jnp.exp(gc))
    kk = [_dot_hp(kb, k, NT) for kb, k in zip(kbs, ks)]
    qk = [_dot_hp(q, k, NT) for q, k in zip(qs, ks)]
    qs_s = [_dot_hp(q * e, s) for q, e, s in zip(qs, gcs, states)]
    tinv = _inv_unit_tri([jnp.where(strict, x * d, 0.0) for x, d in zip(kk, decays)], eye_f, L)
    sol = [_dot_hp(t, r) for t, r in zip(tinv, rhss)]
    v_new = [x[:, :C_HEAD_DIM] - _dot_hp(x[:, C_HEAD_DIM:], s) for x, s in zip(sol, states)]
    o = [a + _dot_hp(jnp.where(incl, x * d, 0.0), vn) for a, x, d, vn in zip(qs_s, qk, decays, v_new)]
    s_new = [s * ge + _dot_hp(ke, vn, TN) for s, ge, ke, vn in zip(states, g_ends, kends, v_new)]
    return o, s_new


def _gdn_scan_kernel(q_ref, k_ref, v_ref, bcol_ref, gcol_ref, grow_ref, s0_ref, o_ref, sf_ref, s_ref, *, L, rev):
    c = pl.program_id(1)

    @pl.when(c == 0)
    def _():
        s_ref[...] = s0_ref[...]

    masks = _tri_masks(L, rev)
    heads = range(C_HEADS)
    hs = [slice(h * C_HEAD_DIM, (h + 1) * C_HEAD_DIM) for h in heads]
    o, s_new = _gdn_chunk([q_ref[:, sl] for sl in hs], [k_ref[:, sl] for sl in hs], [v_ref[:, sl] for sl in hs],
                          bcol_ref[...], gcol_ref[...], grow_ref[...], [s_ref[h] for h in heads], masks, L, rev)
    for h in heads:
        o_ref[:, hs[h]] = o[h]
        s_ref[h] = s_new[h]

    @pl.when(c == pl.num_programs(1) - 1)
    def _():
        sf_ref[...] = s_ref[...]


def _gdn_scan(qkv, bcol, gcol, grow, s0, B, T, rev, L=CHUNK):
    nc = T // L
    cidx = (lambda c: nc - 1 - c) if rev else (lambda c: c)
    seq = lambda col: pl.BlockSpec((L, C_W), lambda b, c: (b * nc + cidx(c), col))
    small = pl.BlockSpec((L, C_HEADS), lambda b, c: (b * nc + cidx(c), 0))
    st = pl.BlockSpec((None, C_HEADS, C_HEAD_DIM, C_HEAD_DIM), lambda b, c: (b, 0, 0, 0))
    return pl.pallas_call(
        functools.partial(_gdn_scan_kernel, L=L, rev=rev),
        grid=(B, nc),
        in_specs=[seq(0), seq(1), seq(2), small, small,
                  pl.BlockSpec((None, C_HEADS, L), lambda b, c: (b * nc + cidx(c), 0, 0)), st],
        out_specs=[pl.BlockSpec((L, C_W), lambda b, c: (b * nc + cidx(c), 0)), st],
        out_shape=[jax.ShapeDtypeStruct((B * T, C_W), F32),
                   jax.ShapeDtypeStruct((B, C_HEADS, C_HEAD_DIM, C_HEAD_DIM), F32)],
        scratch_shapes=[pltpu.VMEM((C_HEADS, C_HEAD_DIM, C_HEAD_DIM), F32)],
        compiler_params=_cparams(("parallel", "arbitrary")))(qkv, qkv, qkv, bcol, gcol, grow, s0)


def _gdn_post_kernel(o0_ref, o1_ref, z_ref, g_ref, out_ref):
    for h in range(C_HEADS):
        sl = slice(h * C_HEAD_DIM, (h + 1) * C_HEAD_DIM)
        y = o0_ref[:, sl] + o1_ref[:, sl]
        y = y * lax.rsqrt(jnp.mean(y * y, axis=-1, keepdims=True) + NORM_EPS) * g_ref[...]
        z = z_ref[:, sl]
        out_ref[:, sl] = y * (z * _sigmoid(z))


def _gdn_post(o0, o1, pc, norm_g, tt=256):
    M = o0.shape[0]
    tt = min(tt, M)
    return pl.pallas_call(
        _gdn_post_kernel, grid=(M // tt,),
        in_specs=[pl.BlockSpec((tt, C_W), lambda i: (i, 0)), pl.BlockSpec((tt, C_W), lambda i: (i, 0)),
                  pl.BlockSpec((tt, C_W), lambda i: (i, 3)), pl.BlockSpec((1, C_HEAD_DIM), lambda i: (0, 0))],
        out_specs=pl.BlockSpec((tt, C_W), lambda i: (i, 0)),
        out_shape=jax.ShapeDtypeStruct((M, C_W), F32),
        compiler_params=_cparams(("parallel",)))(o0, o1, pc, norm_g.reshape(1, C_HEAD_DIM))


def _gdn_mixer(pc, B, T, prm, s0, L=CHUNK):
    M = pc.shape[0]
    nc = T // L
    qkv = _gdn_prep(pc, T, prm['conv'])
    bg = _gdn_bg(pc, prm['a_log'], prm['dt_bias'])
    if s0 is None:
        s0 = jnp.zeros((B, 2, C_HEADS, C_HEAD_DIM, C_HEAD_DIM), F32)
    outs, finals = [], []
    for d in range(2):
        bcol = bg[:, d * C_HEADS:(d + 1) * C_HEADS]
        gcol = bg[:, 2 * C_HEADS + d * C_HEADS:2 * C_HEADS + (d + 1) * C_HEADS]
        grow = gcol.reshape(B * nc, L, C_HEADS).transpose(0, 2, 1)
        o, sf = _gdn_scan(qkv, bcol, gcol, grow, s0[:, d], B, T, rev=(d == 1))
        outs.append(o)
        finals.append(sf)
    y = _gdn_post(outs[0], outs[1], pc, prm['norm_g'])
    return y, jnp.stack(finals, axis=1)


def _top_values(tile, count, need_mask):
    rows = lax.broadcasted_iota(jnp.int32, tile.shape, 0).astype(F32)
    big = float(tile.shape[0])
    cur = tile
    removed = jnp.zeros(tile.shape, dtype=jnp.bool_) if need_mask else None
    vals = []
    for _ in range(count):
        m = jnp.max(cur, axis=0, keepdims=True)
        vals.append(m)
        idx = jnp.min(jnp.where(cur == m, rows, big), axis=0, keepdims=True)
        hit = rows == idx
        if need_mask:
            removed = jnp.logical_or(removed, hit)
        cur = jnp.where(hit, -jnp.inf, cur)
    return vals, removed


PEER_GRP = MXU_K // PEER_HALF
PEER_NGRP = 2 * PEER_HEADS // PEER_GRP


def _peer_topk_kernel(q_ref, kh_ref, kl_ref, s0_ref, s1_ref, st_ref, sc_ref):
    rows_g = PEER_GRP * N_KEYS
    for g in range(PEER_NGRP):
        qh, ql = _split(q_ref[:, g * MXU_K:(g + 1) * MXU_K])
        kh = kh_ref[g * rows_g:(g + 1) * rows_g, :]
        kl = kl_ref[g * rows_g:(g + 1) * rows_g, :]
        sc_ref[g * rows_g:(g + 1) * rows_g, :] = _dg(kh, qh, NT) + (_dg(kh, ql, NT) + _dg(kl, qh, NT))

    def head(h, carry):
        t0 = sc_ref[pl.ds(pl.multiple_of(h * 2 * N_KEYS, N_KEYS), N_KEYS), :]
        t1 = sc_ref[pl.ds(pl.multiple_of(h * 2 * N_KEYS + N_KEYS, N_KEYS), N_KEYS), :]
        a, ra = _top_values(t0, PEER_TOPK, True)
        b, rb = _top_values(t1, PEER_TOPK, True)
        s0_ref[h] = jnp.where(ra, t0, -jnp.inf)
        s1_ref[h] = jnp.where(rb, t1, -jnp.inf)
        bmat = jnp.concatenate(b, axis=0)
        cand = jnp.concatenate([a[i] + bmat for i in range(PEER_TOPK)], axis=0)
        best, _ = _top_values(cand, PEER_TOPK + 1, False)
        z = jnp.zeros_like(best[0])
        for n in range(PEER_TOPK):
            z = z + jnp.exp(best[n] - best[0])
        tau = 0.5 * (best[PEER_TOPK - 1] + best[PEER_TOPK])
        zero = jnp.zeros_like(z)
        st_ref[h] = jnp.concatenate([tau, a[0], b[0], 1.0 / z, zero, zero, zero, zero], axis=0)
        return carry

    lax.fori_loop(0, PEER_HEADS, head, 0)


def _peer_topk(q, kdt_hi, kdt_lo, tt=256):
    M = q.shape[0]
    tt = min(tt, M)
    R = PEER_HEADS * 2 * N_KEYS
    big = pl.BlockSpec((PEER_HEADS, N_KEYS, tt), lambda i: (0, 0, i))
    return pl.pallas_call(
        _peer_topk_kernel, grid=(M // tt,),
        in_specs=[pl.BlockSpec((tt, PEER_HEADS * PEER_QDIM), lambda i: (i, 0)),
                  pl.BlockSpec((R, MXU_K), lambda i: (0, 0)),
                  pl.BlockSpec((R, MXU_K), lambda i: (0, 0))],
        out_specs=[big, big, pl.BlockSpec((PEER_HEADS, 8, tt), lambda i: (0, 0, i))],
        out_shape=[jax.ShapeDtypeStruct((PEER_HEADS, N_KEYS, M), F32),
                   jax.ShapeDtypeStruct((PEER_HEADS, N_KEYS, M), F32),
                   jax.ShapeDtypeStruct((PEER_HEADS, 8, M), F32)],
        scratch_shapes=[pltpu.VMEM((R, tt), F32)],
        compiler_params=_cparams(("parallel",)))(q, kdt_hi, kdt_lo)


def _peer_expert_kernel(x_ref, u_ref, vt_ref, s0_ref, s1_ref, st_ref, res_ref, m_ref, o_ref,
                        acc_ref, thr_ref, e0_ref, e1_ref, *, ni, gate_slot):
    e = pl.program_id(1)

    @pl.when(e == 0)
    def _():
        acc_ref[...] = jnp.zeros_like(acc_ref)
        for h in range(PEER_HEADS):
            s0 = s0_ref[h]
            thr_ref[h] = st_ref[h, 0:1, :] - s0
            e0_ref[h] = jnp.exp(s0 - st_ref[h, 1:2, :]) * st_ref[h, 3:4, :]
            e1_ref[h] = jnp.exp(s1_ref[h] - st_ref[h, 2:3, :])

    x = x_ref[...]
    pieces = []
    for ii in range(ni):
        i = e * ni + ii
        act = _dg(u_ref[ii * N_KEYS:(ii + 1) * N_KEYS, :], x, NT)
        act = 0.5 * act * (1.0 + lax.erf(act * (2.0 ** -0.5)))
        w = None
        for h in range(PEER_HEADS):
            keep = s1_ref[h] >= thr_ref[h, pl.ds(i, 1), :]
            contrib = jnp.where(keep, e1_ref[h], 0.0) * e0_ref[h, pl.ds(i, 1), :]
            w = contrib if w is None else w + contrib
        pieces.append((act * w).astype(BF16))
    amat = jnp.concatenate(pieces, axis=0) if ni > 1 else pieces[0]
    acc_ref[...] += _dg(vt_ref[...], amat)

    @pl.when(e == pl.num_programs(1) - 1)
    def _():
        o_ref[...] = res_ref[...] + m_ref[gate_slot:gate_slot + 1, :] * acc_ref[...].T


def _peer_experts(xn, u_bf, vt_bf, s0m, s1m, st, res, mods, gate_slot, rows_per_mod, tt=512, eb=1024):
    M, D = res.shape
    tt = min(tt, M, rows_per_mod) if mods.shape[0] > 1 else min(tt, M)
    ni = eb // N_KEYS
    once = pl.Buffered(1)
    if mods.shape[0] > 1:
        mspec = pl.BlockSpec((None, 6, D), lambda i, e: ((i * tt) // rows_per_mod, 0, 0), pipeline_mode=once)
    else:
        mspec = pl.BlockSpec((None, 6, D), lambda i, e: (0, 0, 0), pipeline_mode=once)
    big = pl.BlockSpec((PEER_HEADS, N_KEYS, tt), lambda i, e: (0, 0, i), pipeline_mode=once)
    return pl.pallas_call(
        functools.partial(_peer_expert_kernel, ni=ni, gate_slot=gate_slot),
        grid=(M // tt, N_EXPERTS // eb),
        in_specs=[pl.BlockSpec((tt, D), lambda i, e: (i, 0), pipeline_mode=once),
                  pl.BlockSpec((eb, D), lambda i, e: (e, 0)),
                  pl.BlockSpec((D, eb), lambda i, e: (0, e)),
                  big, big, pl.BlockSpec((PEER_HEADS, 8, tt), lambda i, e: (0, 0, i), pipeline_mode=once),
                  pl.BlockSpec((tt, D), lambda i, e: (i, 0), pipeline_mode=once), mspec],
        out_specs=pl.BlockSpec((tt, D), lambda i, e: (i, 0)),
        out_shape=jax.ShapeDtypeStruct((M, D), F32),
        scratch_shapes=[pltpu.VMEM((D, tt), F32)] + [pltpu.VMEM((PEER_HEADS, N_KEYS, tt), F32)] * 3,
        compiler_params=_cparams(("parallel", "arbitrary"), VMEM_LIMIT_BIG))(
            xn, u_bf, vt_bf, s0m, s1m, st, res, mods)


def _peer(x, mods, rows_per_mod, norm_g, prm):
    q, xn = _mm_norm(x, prm['wq'], norm_g=norm_g, mods=mods, slots=(3, 4), rows_per_mod=rows_per_mod,
                     hp=True, emit_xn=BF16)
    s0m, s1m, st = _peer_topk(q, prm['kdt_hi'], prm['kdt_lo'])
    return _peer_experts(xn, prm['u'], prm['vt'], s0m, s1m, st, x, mods, 5, rows_per_mod)


def _peer_params(wq, keys, u_tab, v_tab):
    nset = 2 * PEER_HEADS
    kd = jnp.zeros((nset, N_KEYS, PEER_GRP, PEER_HALF), F32)
    sets = jnp.arange(nset)
    kd = kd.at[sets, :, sets % PEER_GRP, :].set(keys.reshape(nset, N_KEYS, PEER_HALF))
    kd_hi, kd_lo = _split(kd.reshape(nset * N_KEYS, MXU_K))
    return dict(wq=_split(wq), kdt_hi=kd_hi, kdt_lo=kd_lo, u=u_tab.astype(BF16), vt=v_tab.T.astype(BF16))


def _layer_even(x, mods, B, T, norm1, w_a, w_b, w_out, qk_gains, rwkv_prm, ctx):
    rpm = T
    pa = _mm_norm(x, w_a, norm_g=norm1, mods=mods, slots=(0, 1), rows_per_mod=rpm)
    pb = _mm_norm(x, w_b, norm_g=norm1, mods=mods, slots=(0, 1), rows_per_mod=rpm)
    nqk = A_HEADS + A_KV_HEADS
    grp = A_HEADS // A_KV_HEADS
    scale = A_HEAD_DIM ** -0.5
    if ctx is None:
        qk = _norm_rope(pa, nqk, T, gains=qk_gains)
        oa = _attention(B, T, T, A_HEADS, scale, (qk, lambda h: h), (qk, lambda h: A_HEADS + h // grp),
                        (pa, lambda h: nqk + h // grp))
        ctx_out = (qk[:, A_Q:A_Q + A_KV], pa[:, A_Q + A_KV:])
        s0 = None
    else:
        cache_k, cache_v, s0 = ctx
        P = cache_k.shape[1]
        qk = _norm_rope(pa, nqk, T, gains=qk_gains, tables=_rope_tables(T, A_HEAD_DIM))
        k_all = jnp.concatenate([qk[:, A_Q:].reshape(B, T, A_KV), cache_k.reshape(B, P, A_KV)], axis=1)
        v_all = jnp.concatenate([pa[:, A_Q + A_KV:].reshape(B, T, A_KV), cache_v.reshape(B, P, A_KV)], axis=1)
        Tk = T + P
        oa = _attention(B, T, Tk, A_HEADS, scale, (qk, lambda h: h),
                        (k_all.reshape(B * Tk, A_KV), lambda h: h // grp),
                        (v_all.reshape(B * Tk, A_KV), lambda h: h // grp))
        ctx_out = None
    ob, s_fin = _rwkv_mixer(pb, B, T, rwkv_prm, s0)
    x = _mm_res([oa, ob], w_out, x, mods, 2, rpm)
    return x, ctx_out, s_fin


D_QROPE_BLK = 0
D_KROPE_BLK = D_HEADS
D_NROPE_BLK = D_HEADS + 1
D_CKV_BLK = 12
D_QNOPE_BLK = 16
D_PROJ = 24 * LANES


def _mla_in_weight(w_d):
    K = w_d.shape[0]
    wq = w_d[:, :D_Q].reshape(K, D_HEADS, D_QK)
    z64 = jnp.zeros((K, D_HEADS, LANES - D_ROPE), w_d.dtype)
    q_rope = jnp.concatenate([wq[:, :, D_NOPE:], z64], axis=-1).reshape(K, D_HEADS * LANES)
    q_nope = wq[:, :, :D_NOPE].reshape(K, D_HEADS * LANES)
    k_rope = jnp.concatenate([w_d[:, D_Q + D_KV_RANK:], jnp.zeros((K, LANES - D_ROPE), w_d.dtype)], axis=-1)
    padz = jnp.zeros((K, (D_CKV_BLK - D_NROPE_BLK) * LANES), w_d.dtype)
    return jnp.concatenate([q_rope, k_rope, padz, w_d[:, D_Q:D_Q + D_KV_RANK], q_nope], axis=-1)


def _layer_odd(x, mods, B, T, norm1, w_c, w_d, w_out, gdn_prm, kv_norm_g, w_kv_b, ctx):
    rpm = T
    pc = _mm_norm(x, w_c, norm_g=norm1, mods=mods, slots=(0, 1), rows_per_mod=rpm)
    pd = _mm_norm(x, w_d, norm_g=norm1, mods=mods, slots=(0, 1), rows_per_mod=rpm)
    scale = D_QK ** -0.5
    ckv_col = D_CKV_BLK * LANES // D_KV_RANK
    kv, ckv_n = _mm_norm(pd, w_kv_b, k=D_KV_RANK, xcol=ckv_col, norm_g=kv_norm_g, emit_xn=F32)
    kcol = lambda h: 2 * h
    vcol = lambda h: 2 * h + 1
    if ctx is None:
        od = _attention(B, T, T, D_HEADS, scale, (pd, lambda h: D_QNOPE_BLK + h), (kv, kcol), (kv, vcol),
                        q2=(pd, lambda h: D_QROPE_BLK + h), k2=(pd, lambda h: D_KROPE_BLK))
        ctx_out = (ckv_n, pd[:, D_KROPE_BLK * LANES:D_KROPE_BLK * LANES + D_ROPE])
        s0 = None
    else:
        s0, c_ckv, c_krope = ctx
        P = c_ckv.shape[1]
        Tk = T + P
        roped = _norm_rope(pd, D_NROPE_BLK, T, tables=_rope_tables(T, D_ROPE))
        kv_ctx = _mm_norm(c_ckv.reshape(B * P, D_KV_RANK), w_kv_b)
        kv_all = jnp.concatenate([kv.reshape(B, T, -1), kv_ctx.reshape(B, P, -1)], axis=1).reshape(B * Tk, -1)
        kr_lat = roped[:, D_KROPE_BLK * LANES:].reshape(B, T, LANES)
        kr_ctx = jnp.concatenate([c_krope, jnp.zeros((B, P, LANES - D_ROPE), F32)], axis=-1)
        kr_all = jnp.concatenate([kr_lat, kr_ctx], axis=1).reshape(B * Tk, LANES)
        od = _attention(B, T, Tk, D_HEADS, scale, (pd, lambda h: D_QNOPE_BLK + h), (kv_all, kcol), (kv_all, vcol),
                        q2=(roped, lambda h: D_QROPE_BLK + h), k2=(kr_all, lambda h: 0))
        ctx_out = None
    oc, s_fin = _gdn_mixer(pc, B, T, gdn_prm, s0)
    x = _mm_res([oc, od], w_out, x, mods, 2, rpm)
    return x, ctx_out, s_fin


def _modulation(cond, w_ada, b_ada):
    n = cond.shape[0]
    rows = ((n + 7) // 8) * 8
    a = cond * _sigmoid(cond)
    a = jnp.concatenate([a, jnp.zeros((rows - n, cond.shape[1]), F32)], axis=0)
    m = _mm_norm(a, _split(w_ada), hp=True, tn_cap=1024)[:n] + b_ada
    return m.reshape(n, 6, D_MODEL)


def kernel(x_prompt, x_sample, cache_attn_k, cache_attn_v, state_rwkv, state_gdn, cache_mla_ckv, cache_mla_krope,
           c, c_ctx, norm1_g, norm2_g, ada_w, ada_b, w_in_ab, w_out_ab, attn_q_norm, attn_k_norm,
           rwkv_mu, rwkv_w0, rwkv_w2, rwkv_a0, rwkv_a2, rwkv_g2, rwkv_k_k, rwkv_k_a, rwkv_r_k, rwkv_ln_w, rwkv_ln_b,
           w_in_cd, w_out_cd, gdn_conv, gdn_a_log, gdn_dt_bias, gdn_norm_g, mla_kv_norm_g, mla_w_kv_b,
           peer_wq, peer_keys, peer_u, peer_v, final_norm_g):
    Bp, Tp, D = x_prompt.shape
    Bs, Ts, _ = x_sample.shape
    depth = norm1_g.shape[0]
    xp = x_prompt.reshape(Bp * Tp, D)
    xs = x_sample.reshape(Bs * Ts, D)
    cond = jnp.concatenate([c_ctx[None, :], c], axis=0)
    outs = dict(k=[], v=[], rwkv=[], gdn=[], ckv=[], krope=[])
    for li in range(depth):
        j = li // 2
        mods = _modulation(cond, ada_w[li], ada_b[li])
        mods_p, mods_s = mods[:1], mods[1:]
        peer_prm = _peer_params(peer_wq[li], peer_keys[li], peer_u[li], peer_v[li])
        if li % 2 == 0:
            w_in = w_in_ab[j].astype(BF16)
            w_a, w_b = w_in[:, :A_IN], w_in[:, A_IN:]
            w_o = w_out_ab[j].astype(BF16)
            w_out = [w_o[:A_Q], w_o[A_Q:]]
            gains = jnp.concatenate([jnp.tile(attn_q_norm[j], A_HEADS), jnp.tile(attn_k_norm[j], A_KV_HEADS)])
            rp = dict(mu=rwkv_mu[j], w0=rwkv_w0[j], w2=rwkv_w2[j], a0=rwkv_a0[j], a2=rwkv_a2[j], g2=rwkv_g2[j],
                      k_k=rwkv_k_k[j], k_a=rwkv_k_a[j], r_k=rwkv_r_k[j], ln_w=rwkv_ln_w[j], ln_b=rwkv_ln_b[j])
            xp, (k_c, v_c), s_c = _layer_even(xp, mods_p, Bp, Tp, norm1_g[li], w_a, w_b, w_out, gains, rp, None)
            xs, _, _ = _layer_even(xs, mods_s, Bs, Ts, norm1_g[li], w_a, w_b, w_out, gains, rp,
                                   (cache_attn_k[:, j], cache_attn_v[:, j], state_rwkv[:, j]))
            outs['k'].append(k_c.reshape(Bp, Tp, A_KV_HEADS, A_HEAD_DIM))
            outs['v'].append(v_c.reshape(Bp, Tp, A_KV_HEADS, A_HEAD_DIM))
            outs['rwkv'].append(s_c)
        else:
            w_in = w_in_cd[j]
            w_c = jnp.concatenate([w_in[:, :C_IN], jnp.zeros((D, LANES - 4 * C_HEADS), F32)], axis=-1).astype(BF16)
            w_d = _mla_in_weight(w_in[:, C_IN:]).astype(BF16)
            w_o = w_out_cd[j].astype(BF16)
            w_out = [w_o[:C_W], w_o[C_W:]]
            gp = dict(conv=gdn_conv[j], a_log=gdn_a_log[j], dt_bias=gdn_dt_bias[j], norm_g=gdn_norm_g[j])
            w_kv_b = mla_w_kv_b[j].astype(BF16)
            xp, (ckv_c, kr_c), s_c = _layer_odd(xp, mods_p, Bp, Tp, norm1_g[li], w_c, w_d, w_out, gp,
                                                mla_kv_norm_g[j], w_kv_b, None)
            xs, _, _ = _layer_odd(xs, mods_s, Bs, Ts, norm1_g[li], w_c, w_d, w_out, gp, mla_kv_norm_g[j], w_kv_b,
                                  (state_gdn[:, j], cache_mla_ckv[:, j], cache_mla_krope[:, j]))
            outs['gdn'].append(s_c)
            outs['ckv'].append(ckv_c.reshape(Bp, Tp, D_KV_RANK))
            outs['krope'].append(kr_c.reshape(Bp, Tp, D_ROPE))
        xp = _peer(xp, mods_p, Tp, norm2_g[li], peer_prm)
        xs = _peer(xs, mods_s, Ts, norm2_g[li], peer_prm)
    y_prompt = _rmsnorm(xp, final_norm_g).reshape(Bp, Tp, D)
    y_sample = _rmsnorm(xs, final_norm_g).reshape(Bs, Ts, D)
    st = lambda name: jnp.stack(outs[name], axis=1)
    return (y_prompt, y_sample, st('k'), st('v'), st('rwkv'), st('gdn'), st('ckv'), st('krope'))
```

```python
import functools
import math

import numpy as np
import jax
import jax.numpy as jnp
from jax import lax
from jax.experimental import pallas as pl
from jax.experimental.pallas import tpu as pltpu

F32 = jnp.float32
BF16 = jnp.bfloat16

D_MODEL = 2048
GRID_W = 64
ROPE_THETA = 10000.0
NORM_EPS = 1e-6
A_HEADS, A_KV_HEADS, A_HEAD_DIM = 8, 2, 128
A_Q = A_HEADS * A_HEAD_DIM
A_KV = A_KV_HEADS * A_HEAD_DIM
A_IN = A_Q + 2 * A_KV
B_HEADS, B_HEAD_DIM = 16, 64
B_W = B_HEADS * B_HEAD_DIM
B_DECAY_LORA, B_AAA_LORA, B_GATE_LORA = 64, 64, 128
B_IN = 3 * B_W + 2 * B_DECAY_LORA + 2 * B_AAA_LORA + B_GATE_LORA
B_GN_EPS = 64e-5
C_HEADS, C_HEAD_DIM = 8, 128
C_W = C_HEADS * C_HEAD_DIM
C_IN = 4 * C_W + 4 * C_HEADS
D_HEADS, D_NOPE, D_ROPE, D_V = 8, 128, 64, 128
D_QK = D_NOPE + D_ROPE
D_KV_RANK = 512
D_Q = D_HEADS * D_QK
PEER_HEADS, N_KEYS, PEER_QDIM, PEER_TOPK = 8, 128, 128, 16
PEER_HALF = PEER_QDIM // 2
N_EXPERTS = N_KEYS * N_KEYS

LANES = 128
MXU_K = 256
VMEM_LIMIT = 48 * 1024 * 1024
VMEM_LIMIT_BIG = 56 * 1024 * 1024

CHUNK = 64

NN = ((1,), (0,))
NT = ((1,), (1,))
TN = ((0,), (0,))


def _dg(a, b, dims=NN):
    return lax.dot_general(a, b, (dims, ((), ())), preferred_element_type=F32)


def _split(x):
    hi = x.astype(BF16)
    lo = (x - hi.astype(F32)).astype(BF16)
    return hi, lo


def _dot_hp(a, b, dims=NN):
    ah, al = _split(a)
    bh, bl = _split(b)
    return _dg(ah, bh, dims) + (_dg(ah, bl, dims) + _dg(al, bh, dims))


def _dot_bf(a, b, dims=NN):
    return _dg(a.astype(BF16), b.astype(BF16), dims)


def _sigmoid(x):
    return 1.0 / (1.0 + jnp.exp(-x))


def _softplus(x):
    return jnp.maximum(x, 0.0) + jnp.log(1.0 + jnp.exp(-jnp.abs(x)))


def _cparams(sem, vmem=VMEM_LIMIT):
    return pltpu.CompilerParams(dimension_semantics=sem, vmem_limit_bytes=vmem)


def _pick_tile(n, cap):
    best = None
    t = LANES
    while t <= min(n, cap):
        if n % t == 0:
            best = t
        t += LANES
    return best if best is not None else n


def _mm_norm_kernel(*refs, has_norm, has_mod, shift_slot, scale_slot, hp, emit_xn):
    it = iter(refs)
    x_ref = next(it)
    g_ref = next(it) if has_norm else None
    m_ref = next(it) if has_mod else None
    wh_ref = next(it)
    wl_ref = next(it) if hp else None
    o_ref = next(it)
    xn_ref = next(it) if emit_xn else None
    xh_ref = next(it)
    xl_ref = next(it) if hp else None

    @pl.when(pl.program_id(1) == 0)
    def _():
        xv = x_ref[...].astype(F32)
        if has_norm:
            xv = xv * lax.rsqrt(jnp.mean(xv * xv, axis=-1, keepdims=True) + NORM_EPS) * g_ref[...]
        if has_mod:
            xv = xv * (1.0 + m_ref[scale_slot:scale_slot + 1, :]) + m_ref[shift_slot:shift_slot + 1, :]
        hi = xv.astype(BF16)
        xh_ref[...] = hi
        if hp:
            xl_ref[...] = (xv - hi.astype(F32)).astype(BF16)
        if emit_xn:
            xn_ref[...] = xv.astype(xn_ref.dtype)

    acc = _dg(xh_ref[...], wh_ref[...])
    if hp:
        acc = acc + (_dg(xh_ref[...], wl_ref[...]) + _dg(xl_ref[...], wh_ref[...]))
    o_ref[...] = acc


def _mm_norm(x, w, *, k=None, xcol=0, norm_g=None, mods=None, slots=(0, 1), rows_per_mod=None,
             hp=False, emit_xn=None, tm=512, tn_cap=1664):
    M = x.shape[0]
    K = k if k is not None else x.shape[1]
    ws = tuple(w) if isinstance(w, (tuple, list)) else (w,)
    N = ws[0].shape[1]
    tm = min(tm, M)
    if mods is not None and mods.shape[0] > 1:
        tm = min(tm, rows_per_mod)
    tn = _pick_tile(N, tn_cap)
    has_norm = norm_g is not None
    has_mod = mods is not None
    rpm = rows_per_mod if rows_per_mod is not None else M

    in_specs = [pl.BlockSpec((tm, K), lambda i, j: (i, xcol))]
    args = [x]
    if has_norm:
        in_specs.append(pl.BlockSpec((1, K), lambda i, j: (0, 0)))
        args.append(norm_g.reshape(1, K).astype(F32))
    if has_mod:
        if mods.shape[0] > 1:
            in_specs.append(pl.BlockSpec((None, 6, K), lambda i, j: ((i * tm) // rpm, 0, 0)))
        else:
            in_specs.append(pl.BlockSpec((None, 6, K), lambda i, j: (0, 0, 0)))
        args.append(mods)
    for wi in ws:
        in_specs.append(pl.BlockSpec((K, tn), lambda i, j: (0, j)))
        args.append(wi)
    out_shape = [jax.ShapeDtypeStruct((M, N), F32)]
    out_specs = [pl.BlockSpec((tm, tn), lambda i, j: (i, j))]
    if emit_xn is not None:
        out_shape.append(jax.ShapeDtypeStruct((M, K), emit_xn))
        out_specs.append(pl.BlockSpec((tm, K), lambda i, j: (i, 0)))
    scratch = [pltpu.VMEM((tm, K), BF16)]
    if hp:
        scratch.append(pltpu.VMEM((tm, K), BF16))
    kern = functools.partial(_mm_norm_kernel, has_norm=has_norm, has_mod=has_mod, shift_slot=slots[0],
                             scale_slot=slots[1], hp=hp, emit_xn=emit_xn is not None)
    outs = pl.pallas_call(
        kern, grid=(M // tm, N // tn), in_specs=in_specs, out_specs=out_specs, out_shape=out_shape,
        scratch_shapes=scratch, compiler_params=_cparams(("parallel", "arbitrary")))(*args)
    return outs if emit_xn is not None else outs[0]


def _mm_res_kernel(*refs, n_in, gate_slot):
    a_refs = refs[:n_in]
    w_refs = refs[n_in:2 * n_in]
    res_ref, m_ref, o_ref = refs[2 * n_in:]
    acc = None
    for a_ref, w_ref in zip(a_refs, w_refs):
        part = _dg(a_ref[...].astype(BF16), w_ref[...])
        acc = part if acc is None else acc + part
    o_ref[...] = res_ref[...] + m_ref[gate_slot:gate_slot + 1, :] * acc


def _mm_res(a_list, w_list, res, mods, gate_slot, rows_per_mod, tm=512, tn=1024):
    M, N = res.shape
    tm = min(tm, M, rows_per_mod) if mods.shape[0] > 1 else min(tm, M)
    tn = min(tn, N)
    n_in = len(a_list)
    in_specs, args = [], []
    for a in a_list:
        in_specs.append(pl.BlockSpec((tm, a.shape[1]), lambda i, j: (i, 0)))
        args.append(a)
    for w in w_list:
        in_specs.append(pl.BlockSpec((w.shape[0], tn), lambda i, j: (0, j)))
        args.append(w)
    in_specs.append(pl.BlockSpec((tm, tn), lambda i, j: (i, j)))
    args.append(res)
    if mods.shape[0] > 1:
        in_specs.append(pl.BlockSpec((None, 6, tn), lambda i, j: ((i * tm) // rows_per_mod, 0, j)))
    else:
        in_specs.append(pl.BlockSpec((None, 6, tn), lambda i, j: (0, 0, j)))
    args.append(mods)
    return pl.pallas_call(
        functools.partial(_mm_res_kernel, n_in=n_in, gate_slot=gate_slot),
        grid=(M // tm, N // tn), in_specs=in_specs,
        out_specs=pl.BlockSpec((tm, tn), lambda i, j: (i, j)),
        out_shape=jax.ShapeDtypeStruct((M, N), F32),
        compiler_params=_cparams(("parallel", "arbitrary")))(*args)


def _rmsnorm_kernel(x_ref, g_ref, o_ref):
    xv = x_ref[...]
    o_ref[...] = xv * lax.rsqrt(jnp.mean(xv * xv, axis=-1, keepdims=True) + NORM_EPS) * g_ref[...]


def _rmsnorm(x, g, tm=512):
    M, K = x.shape
    tm = min(tm, M)
    return pl.pallas_call(
        _rmsnorm_kernel, grid=(M // tm,),
        in_specs=[pl.BlockSpec((tm, K), lambda i: (i, 0)), pl.BlockSpec((1, K), lambda i: (0, 0))],
        out_specs=pl.BlockSpec((tm, K), lambda i: (i, 0)),
        out_shape=jax.ShapeDtypeStruct((M, K), F32),
        compiler_params=_cparams(("parallel",)))(x, g.reshape(1, K))


def _rope_tables(T, rot_dim):
    rows = T // GRID_W
    row = jnp.repeat(jnp.arange(rows, dtype=F32), GRID_W)
    col = jnp.tile(jnp.arange(GRID_W, dtype=F32), rows)
    n_freq = rot_dim // 4
    inv = ROPE_THETA ** (-jnp.arange(n_freq, dtype=F32) / n_freq)
    ang = jnp.concatenate([row[:, None] * inv, col[:, None] * inv], axis=-1)
    cos = jnp.repeat(jnp.cos(ang), 2, axis=-1)
    sin = jnp.repeat(jnp.sin(ang), 2, axis=-1) * jnp.tile(jnp.array([-1.0, 1.0], F32), rot_dim // 2)
    pad = LANES - rot_dim
    if pad:
        cos = jnp.concatenate([cos, jnp.ones((T, pad), F32)], axis=-1)
        sin = jnp.concatenate([sin, jnp.zeros((T, pad), F32)], axis=-1)
    return cos, sin


def _rope(y, cosf, sinf):
    lane = lax.broadcasted_iota(jnp.int32, y.shape, 1)
    even = (lane & 1) == 0
    width = y.shape[1]
    swap = jnp.where(even, pltpu.roll(y, width - 1, 1), pltpu.roll(y, 1, 1))
    return y * cosf + swap * sinf


def _norm_rope_kernel(*refs, nblk, norm, rope):
    it = iter(refs)
    x_ref = next(it)
    g_ref = next(it) if norm else None
    cos_ref = next(it) if rope else None
    sin_ref = next(it) if rope else None
    o_ref = next(it)
    for c in range(nblk):
        sl = slice(c * LANES, (c + 1) * LANES)
        y = x_ref[:, sl]
        if norm:
            y = y * lax.rsqrt(jnp.mean(y * y, axis=-1, keepdims=True) + NORM_EPS) * g_ref[:, sl]
        if rope:
            y = _rope(y, cos_ref[...], sin_ref[...])
        o_ref[:, sl] = y


def _norm_rope(x, nblk, T, gains=None, tables=None, tt=256):
    M = x.shape[0]
    tt = min(tt, T)
    W = nblk * LANES
    in_specs = [pl.BlockSpec((tt, W), lambda i: (i, 0))]
    args = [x]
    if gains is not None:
        in_specs.append(pl.BlockSpec((1, W), lambda i: (0, 0)))
        args.append(gains.reshape(1, W))
    if tables is not None:
        nt = T // tt
        for t in tables:
            in_specs.append(pl.BlockSpec((tt, LANES), lambda i: (i % nt, 0)))
            args.append(t)
    return pl.pallas_call(
        functools.partial(_norm_rope_kernel, nblk=nblk, norm=gains is not None, rope=tables is not None),
        grid=(M // tt,), in_specs=in_specs,
        out_specs=pl.BlockSpec((tt, W), lambda i: (i, 0)),
        out_shape=jax.ShapeDtypeStruct((M, W), F32),
        compiler_params=_cparams(("parallel",)))(*args)


def _attn_kernel(*refs, scale, has2):
    it = iter(refs)
    q_ref = next(it)
    q2_ref = next(it) if has2 else None
    k_ref = next(it)
    k2_ref = next(it) if has2 else None
    v_ref = next(it)
    o_ref = next(it)
    s = _dot_bf(q_ref[...] * scale, k_ref[...], NT)
    if has2:
        s = s + _dot_bf(q2_ref[...] * scale, k2_ref[...], NT)
    m = jnp.max(s, axis=-1, keepdims=True)
    p = jnp.exp(s - m)
    l = jnp.sum(p, axis=-1, keepdims=True)
    o_ref[...] = _dot_bf(p, v_ref[...]) / l


def _attention(B, T, Tk, H, scale, q, k, v, q2=None, k2=None, tq=256):
    tq = min(tq, T)
    nq = T // tq
    has2 = q2 is not None

    def qspec(cf):
        return pl.BlockSpec((tq, LANES), lambda b, h, i: (b * nq + i, cf(h)))

    def kspec(cf):
        return pl.BlockSpec((Tk, LANES), lambda b, h, i: (b, cf(h)))

    in_specs, args = [qspec(q[1])], [q[0]]
    if has2:
        in_specs.append(qspec(q2[1]))
        args.append(q2[0])
    in_specs.append(kspec(k[1]))
    args.append(k[0])
    if has2:
        in_specs.append(kspec(k2[1]))
        args.append(k2[0])
    in_specs.append(kspec(v[1]))
    args.append(v[0])
    return pl.pallas_call(
        functools.partial(_attn_kernel, scale=scale, has2=has2),
        grid=(B, H, nq), in_specs=in_specs,
        out_specs=pl.BlockSpec((tq, LANES), lambda b, h, i: (b * nq + i, h)),
        out_shape=jax.ShapeDtypeStruct((B * T, H * LANES), F32),
        compiler_params=_cparams(("parallel", "arbitrary", "arbitrary")))(*args)


def _tri_masks(L, rev):
    i = lax.broadcasted_iota(jnp.int32, (L, L), 0)
    j = lax.broadcasted_iota(jnp.int32, (L, L), 1)
    if rev:
        return j >= i, j > i, i == j
    return j <= i, j < i, i == j


def _inv_unit_tri(mats, eye_f, L):
    pows = [[-a for a in mats]]
    for _ in range(int(math.log2(L)) - 1):
        pows.append([_dot_hp(p, p) for p in pows[-1]])
    terms = [[eye_f + p for p in level] for level in pows]
    while len(terms) > 1:
        nxt = [[_dot_hp(x, y) for x, y in zip(terms[i], terms[i + 1])] for i in range(0, len(terms) - 1, 2)]
        if len(terms) % 2:
            nxt.append(terms[-1])
        terms = nxt
    return terms[0]


def _shifted(f, prev_ref, next_ref, tt, T):
    i = pl.program_id(0)
    first = (i * tt) % T == 0
    last = ((i + 1) * tt) % T == 0
    prev = jnp.where(first, 0.0, prev_ref[7:8, :])
    nxt = jnp.where(last, 0.0, next_ref[0:1, :])
    rows = lax.broadcasted_iota(jnp.int32, f.shape, 0)
    fp = jnp.where(rows == 0, prev, pltpu.roll(f, 1, 0))
    fn = jnp.where(rows == tt - 1, nxt, pltpu.roll(f, tt - 1, 0))
    return fp, fn


def _halo_specs(tt, width, M, col_fn):
    r8 = tt // 8
    last8 = M // 8 - 1
    return [
        pl.BlockSpec((tt, width), lambda *g: (g[0], col_fn(*g))),
        pl.BlockSpec((8, width), lambda *g: (jnp.maximum(g[0] * r8 - 1, 0), col_fn(*g))),
        pl.BlockSpec((8, width), lambda *g: (jnp.minimum((g[0] + 1) * r8, last8), col_fn(*g))),
    ]


NPAIR = B_HEADS // 2


def _pair_sum_matrix():
    r = lax.broadcasted_iota(jnp.int32, (LANES, LANES), 0) // B_HEAD_DIM
    c = lax.broadcasted_iota(jnp.int32, (LANES, LANES), 1) // B_HEAD_DIM
    return (r == c).astype(F32)


def _rwkv_prep_kernel(x_ref, prev_ref, next_ref, mu_ref, w0_ref, w2_ref, a0_ref, a2_ref, g2_ref, kk_ref_in,
                      r_o, k_o, v_o, kk_o, g_o, lw0_o, lw1_o, a0_o, a1_o, *, tt, T):
    f = x_ref[...]
    fp, fn = _shifted(f, prev_ref, next_ref, tt, T)
    f = f + mu_ref[0:1, :] * (fp - f) + mu_ref[1:2, :] * (fn - f)
    r = f[:, :B_W]
    k = f[:, B_W:2 * B_W]
    v = f[:, 2 * B_W:3 * B_W]
    off = 3 * B_W
    w_lo = f[:, off:off + 2 * B_DECAY_LORA]
    off += 2 * B_DECAY_LORA
    a_lo = f[:, off:off + 2 * B_AAA_LORA]
    off += 2 * B_AAA_LORA
    g_lo = f[:, off:]
    th = jnp.tanh(w_lo)
    logw, aa = [], []
    for d in range(2):
        wpre = w0_ref[d:d + 1, :] + _dot_hp(th[:, d * B_DECAY_LORA:(d + 1) * B_DECAY_LORA], w2_ref[d])
        wv = -_softplus(-wpre) - 0.5
        logw.append(-jnp.exp(wv))
        aa.append(_sigmoid(a0_ref[d:d + 1, :] + _dot_hp(a_lo[:, d * B_AAA_LORA:(d + 1) * B_AAA_LORA], a2_ref[d])))
    g = _dot_hp(_sigmoid(g_lo), g2_ref[...])
    kk = k * kk_ref_in[...]
    gsum = _pair_sum_matrix()
    for p in range(NPAIR):
        sl = slice(p * LANES, (p + 1) * LANES)
        kp = kk[:, sl]
        ssq = _dot_hp(kp * kp, gsum)
        kk_o[p] = kp / jnp.maximum(jnp.sqrt(ssq), 1e-12)
        r_o[p] = r[:, sl]
        k_o[p] = k[:, sl]
        v_o[p] = v[:, sl]
        g_o[p] = g[:, sl]
        lw0_o[p] = logw[0][:, sl]
        lw1_o[p] = logw[1][:, sl]
        a0_o[p] = aa[0][:, sl]
        a1_o[p] = aa[1][:, sl]


def _rwkv_prep(pb, B, T, mu, w0, w2, a0, a2, g2, k_k, tt=128):
    M = pb.shape[0]
    tt = min(tt, T)
    nt = T // tt
    full = lambda shape: pl.BlockSpec(shape, lambda i: tuple(0 for _ in shape))
    in_specs = _halo_specs(tt, B_IN, M, lambda i: 0) + [
        full((2, B_IN)), full((2, B_W)), full((2, B_DECAY_LORA, B_W)), full((2, B_W)),
        full((2, B_AAA_LORA, B_W)), full((B_GATE_LORA, B_W)), full((1, B_W))]
    ospec = pl.BlockSpec((None, NPAIR, tt, LANES), lambda i: (i // nt, 0, i % nt, 0))
    oshape = jax.ShapeDtypeStruct((B, NPAIR, T, LANES), F32)
    return pl.pallas_call(
        functools.partial(_rwkv_prep_kernel, tt=tt, T=T),
        grid=(M // tt,), in_specs=in_specs, out_specs=[ospec] * 9, out_shape=[oshape] * 9,
        compiler_params=_cparams(("parallel",)))(pb, pb, pb, mu, w0, w2, a0, a2, g2, k_k.reshape(1, B_W))


def _rwkv_chunk(r2, lw2, kd2, v2, kk2, a2, states, masks, L, rev):
    incl, strict, eye = masks
    incl_f = incl.astype(F32)
    eye_f = eye.astype(F32)
    i2 = lax.broadcasted_iota(jnp.int32, (L, 2 * L), 0)
    j2 = lax.broadcasted_iota(jnp.int32, (L, 2 * L), 1) % L
    incl2 = (j2 >= i2) if rev else (j2 <= i2)
    last = 0 if rev else L - 1
    halves = (slice(0, B_HEAD_DIM), slice(B_HEAD_DIM, 2 * B_HEAD_DIM))
    logp2 = [_dot_hp(incl_f, lw) for lw in lw2]
    lhs, rhs, rem_rhs, dec, vs = [], [], [], [], []
    for lp, lw, r, kd, v, kk, a in zip(logp2, lw2, r2, kd2, v2, kk2, a2):
        ninv = jnp.exp(-lp)
        b = kk * a
        kt = kk * jnp.exp(lp - lw)
        rt = r * jnp.exp(lp)
        bn, kdn = b * ninv, kd * ninv
        pl_row = lp[last:last + 1, :]
        rem = jnp.exp(pl_row - lp)
        bh, kh = b * rem, kd * rem
        dpl = jnp.exp(pl_row)
        for sl in halves:
            lhs.append(jnp.concatenate([kt[:, sl], rt[:, sl]], axis=0))
            rhs.append(jnp.concatenate([bn[:, sl], kdn[:, sl]], axis=0))
            rem_rhs.append(jnp.concatenate([bh[:, sl], kh[:, sl]], axis=0))
            dec.append(dpl[:, sl])
            vs.append(v[:, sl])
    gm = [_dot_hp(x, y, NT) for x, y in zip(lhs, rhs)]
    ls = [_dot_hp(x, s, NT) for x, s in zip(lhs, states)]
    akv = [_dot_hp(jnp.where(strict, g[:L, L:], 0.0), v) for g, v in zip(gm, vs)]
    tinv = _inv_unit_tri([jnp.where(strict, g[:L, :L], 0.0) for g in gm], eye_f, L)
    u = [-_dot_hp(t, l[:L] + x) for t, l, x in zip(tinv, ls, akv)]
    uv = [jnp.concatenate([x, v], axis=0) for x, v in zip(u, vs)]
    y = [l[L:] + _dot_hp(jnp.where(incl2, g[L:, :], 0.0), x) for l, g, x in zip(ls, gm, uv)]
    s_new = [s * d + _dot_hp(x, m, TN) for s, d, x, m in zip(states, dec, uv, rem_rhs)]
    y2 = [jnp.concatenate([y[2 * p], y[2 * p + 1]], axis=1) for p in range(len(r2))]
    return y2, s_new


def _rwkv_scan_kernel(r_ref, v_ref, kk_ref, k_ref, lw_ref, a_ref, ka_ref, s0_ref, y_ref, sf_ref, s_ref, *, L, rev):
    c = pl.program_id(1)

    @pl.when(c == 0)
    def _():
        s_ref[...] = s0_ref[...]

    masks = _tri_masks(L, rev)
    pairs = range(NPAIR)
    a2 = [a_ref[p] for p in pairs]
    kd2 = [k_ref[p] * (1.0 + (a - 1.0) * ka_ref[p]) for p, a in zip(pairs, a2)]
    y2, s_new = _rwkv_chunk([r_ref[p] for p in pairs], [lw_ref[p] for p in pairs], kd2, [v_ref[p] for p in pairs],
                            [kk_ref[p] for p in pairs], a2, [s_ref[h] for h in range(B_HEADS)], masks, L, rev)
    for p in pairs:
        y_ref[p] = y2[p]
    for h in range(B_HEADS):
        s_ref[h] = s_new[h]

    @pl.when(c == pl.num_programs(1) - 1)
    def _():
        sf_ref[...] = s_ref[...]


def _rwkv_scan(r, v, kk, k, lw, a, k_a, s0, rev, L=CHUNK):
    B, _, T, _ = r.shape
    nc = T // L
    cmap = (lambda b, c: (b, 0, nc - 1 - c, 0)) if rev else (lambda b, c: (b, 0, c, 0))
    seq = pl.BlockSpec((None, NPAIR, L, LANES), cmap)
    st = pl.BlockSpec((None, B_HEADS, B_HEAD_DIM, B_HEAD_DIM), lambda b, c: (b, 0, 0, 0))
    return pl.pallas_call(
        functools.partial(_rwkv_scan_kernel, L=L, rev=rev),
        grid=(B, nc),
        in_specs=[seq] * 6 + [pl.BlockSpec((NPAIR, 1, LANES), lambda b, c: (0, 0, 0)), st],
        out_specs=[seq, st],
        out_shape=[jax.ShapeDtypeStruct((B, NPAIR, T, LANES), F32),
                   jax.ShapeDtypeStruct((B, B_HEADS, B_HEAD_DIM, B_HEAD_DIM), F32)],
        scratch_shapes=[pltpu.VMEM((B_HEADS, B_HEAD_DIM, B_HEAD_DIM), F32)],
        compiler_params=_cparams(("parallel", "arbitrary")))(r, v, kk, k, lw, a, k_a.reshape(NPAIR, 1, LANES), s0)


def _rwkv_post_kernel(y0_ref, y1_ref, r_ref, k_ref, v_ref, g_ref, rk_ref, lnw_ref, lnb_ref, o_ref):
    gsum = _pair_sum_matrix()
    inv_n = 1.0 / B_HEAD_DIM
    for p in range(NPAIR):
        y = y0_ref[p] + y1_ref[p]
        mean = _dot_hp(y, gsum) * inv_n
        yc = y - mean
        var = _dot_hp(yc * yc, gsum) * inv_n
        yn = yc * lax.rsqrt(var + B_GN_EPS) * lnw_ref[p] + lnb_ref[p]
        bonus = _dot_hp(r_ref[p] * k_ref[p] * rk_ref[p], gsum) * v_ref[p]
        o_ref[:, p * LANES:(p + 1) * LANES] = (yn + bonus) * g_ref[p]


def _rwkv_post(y0, y1, r, k, v, g, r_k, ln_w, ln_b, tt=256):
    B, _, T, _ = r.shape
    tt = min(tt, T)
    nt = T // tt
    seq = pl.BlockSpec((None, NPAIR, tt, LANES), lambda i: (i // nt, 0, i % nt, 0))
    par = pl.BlockSpec((NPAIR, 1, LANES), lambda i: (0, 0, 0))
    return pl.pallas_call(
        _rwkv_post_kernel, grid=(B * nt,),
        in_specs=[seq] * 6 + [par] * 3,
        out_specs=pl.BlockSpec((tt, B_W), lambda i: (i, 0)),
        out_shape=jax.ShapeDtypeStruct((B * T, B_W), F32),
        compiler_params=_cparams(("parallel",)))(
            y0, y1, r, k, v, g, r_k.reshape(NPAIR, 1, LANES), ln_w.reshape(NPAIR, 1, LANES),
            ln_b.reshape(NPAIR, 1, LANES))


def _rwkv_mixer(pb, B, T, prm, s0):
    r, k, v, kk, g, lw0, lw1, a0, a1 = _rwkv_prep(pb, B, T, prm['mu'], prm['w0'], prm['w2'], prm['a0'],
                                                   prm['a2'], prm['g2'], prm['k_k'])
    if s0 is None:
        s0 = jnp.zeros((B, 2, B_HEADS, B_HEAD_DIM, B_HEAD_DIM), F32)
    y0, sf0 = _rwkv_scan(r, v, kk, k, lw0, a0, prm['k_a'], s0[:, 0], rev=False)
    y1, sf1 = _rwkv_scan(r, v, kk, k, lw1, a1, prm['k_a'], s0[:, 1], rev=True)
    out = _rwkv_post(y0, y1, r, k, v, g, prm['r_k'], prm['ln_w'], prm['ln_b'])
    return out, jnp.stack([sf0, sf1], axis=1)


def _gdn_prep_kernel(x_ref, prev_ref, next_ref, cw_ref, o_ref, *, tt, T):
    c = pl.program_id(1)
    x = x_ref[...]
    xp, xn = _shifted(x, prev_ref, next_ref, tt, T)
    y = cw_ref[0:1, :] * xp + cw_ref[1:2, :] * x + cw_ref[2:3, :] * xn
    y = y * _sigmoid(y)

    @pl.when(c == 2)
    def _():
        o_ref[...] = y

    @pl.when(c < 2)
    def _():
        scale = jnp.where(c == 0, C_HEAD_DIM ** -0.5, 1.0).astype(F32)
        for h in range(C_HEADS):
            sl = slice(h * C_HEAD_DIM, (h + 1) * C_HEAD_DIM)
            yh = y[:, sl]
            o_ref[:, sl] = yh * lax.rsqrt(jnp.sum(yh * yh, axis=-1, keepdims=True) + 1e-6) * scale


def _gdn_prep(pc, T, conv_w, tt=256):
    M = pc.shape[0]
    tt = min(tt, T)
    in_specs = _halo_specs(tt, C_W, M, lambda i, c: c) + [pl.BlockSpec((3, C_W), lambda i, c: (0, c))]
    return pl.pallas_call(
        functools.partial(_gdn_prep_kernel, tt=tt, T=T),
        grid=(M // tt, 3), in_specs=in_specs,
        out_specs=pl.BlockSpec((tt, C_W), lambda i, c: (i, c)),
        out_shape=jax.ShapeDtypeStruct((M, 3 * C_W), F32),
        compiler_params=_cparams(("parallel", "arbitrary")))(pc, pc, pc, conv_w)


def _gdn_bg_kernel(x_ref, alog_ref, dtb_ref, o_ref):
    x = x_ref[...]
    lane = lax.broadcasted_iota(jnp.int32, x.shape, 1)
    beta = _sigmoid(x)
    g = -jnp.exp(alog_ref[...]) * _softplus(x + dtb_ref[...])
    o_ref[...] = jnp.where(lane < 2 * C_HEADS, beta, g)


def _gdn_bg(pc, a_log, dt_bias, tt=512):
    M = pc.shape[0]
    tt = min(tt, M)
    nb = 2 * C_HEADS
    pad = lambda p: jnp.concatenate([jnp.zeros((nb,), F32), p.reshape(nb), jnp.zeros((LANES - 2 * nb,), F32)]).reshape(1, LANES)
    return pl.pallas_call(
        _gdn_bg_kernel, grid=(M // tt,),
        in_specs=[pl.BlockSpec((tt, LANES), lambda i: (i, 4 * C_W // LANES)),
                  pl.BlockSpec((1, LANES), lambda i: (0, 0)), pl.BlockSpec((1, LANES), lambda i: (0, 0))],
        out_specs=pl.BlockSpec((tt, LANES), lambda i: (i, 0)),
        out_shape=jax.ShapeDtypeStruct((M, LANES), F32),
        compiler_params=_cparams(("parallel",)))(pc, pad(a_log), pad(dt_bias))


def _gdn_chunk(qs, ks, vs, bcol, gcol, grow, states, masks, L, rev):
    incl, strict, eye = masks
    incl_f = incl.astype(F32)
    eye_f = eye.astype(F32)
    nh = len(qs)
    i = lax.broadcasted_iota(jnp.int32, (L, L), 0)
    j = lax.broadcasted_iota(jnp.int32, (L, L), 1)
    cum_t = ((i >= j) if rev else (i <= j)).astype(F32)
    gc_cols = _dot_hp(incl_f, gcol)
    gc_rows = _dot_hp(grow, cum_t)
    last = 0 if rev else L - 1
    gcs, decays, kbs, rhss, kends, g_ends = [], [], [], [], [], []
    for h in range(nh):
        gc = jnp.broadcast_to(gc_cols[:, h:h + 1], (L, LANES))
        decays.append(jnp.exp(jnp.where(incl, gc[:, :L] - gc_rows[h:h + 1, :], -1e30)))
        beta = bcol[:, h:h + 1]
        kb = ks[h] * beta
        kbs.append(kb)
        rhss.append(jnp.concatenate([vs[h] * beta, kb * jnp.exp(gc)], axis=1))
        g_end = gc[last:last + 1, :]
        kends.append(ks[h] * jnp.exp(g_end - gc))
        g_ends.append(jnp.exp(g_end))
        gcs.append(jnp.exp(gc))
    kk = [_dot_hp(kb, k, NT) for kb, k in zip(kbs, ks)]
    qk = [_dot_hp(q, k, NT) for q, k in zip(qs, ks)]
    qs_s = [_dot_hp(q * e, s) for q, e, s in zip(qs, gcs, states)]
    tinv = _inv_unit_tri([jnp.where(strict, x * d, 0.0) for x, d in zip(kk, decays)], eye_f, L)
    sol = [_dot_hp(t, r) for t, r in zip(tinv, rhss)]
    v_new = [x[:, :C_HEAD_DIM] - _dot_hp(x[:, C_HEAD_DIM:], s) for x, s in zip(sol, states)]
    o = [a + _dot_hp(jnp.where(incl, x * d, 0.0), vn) for a, x, d, vn in zip(qs_s, qk, decays, v_new)]
    s_new = [s * ge + _dot_hp(ke, vn, TN) for s, ge, ke, vn in zip(states, g_ends, kends, v_new)]
    return o, s_new


def _gdn_scan_kernel(q_ref, k_ref, v_ref, bcol_ref, gcol_ref, grow_ref, s0_ref, o_ref, sf_ref, s_ref, *, L, rev):
    c = pl.program_id(1)

    @pl.when(c == 0)
    def _():
        s_ref[...] = s0_ref[...]

    masks = _tri_masks(L, rev)
    heads = range(C_HEADS)
    hs = [slice(h * C_HEAD_DIM, (h + 1) * C_HEAD_DIM) for h in heads]
    o, s_new = _gdn_chunk([q_ref[:, sl] for sl in hs], [k_ref[:, sl] for sl in hs], [v_ref[:, sl] for sl in hs],
                          bcol_ref[...], gcol_ref[...], grow_ref[...], [s_ref[h] for h in heads], masks, L, rev)
    for h in heads:
        o_ref[:, hs[h]] = o[h]
        s_ref[h] = s_new[h]

    @pl.when(c == pl.num_programs(1) - 1)
    def _():
        sf_ref[...] = s_ref[...]


def _gdn_scan(qkv, bcol, gcol, grow, s0, B, T, rev, L=CHUNK):
    nc = T // L
    cidx = (lambda c: nc - 1 - c) if rev else (lambda c: c)
    seq = lambda col: pl.BlockSpec((L, C_W), lambda b, c: (b * nc + cidx(c), col))
    small = pl.BlockSpec((L, C_HEADS), lambda b, c: (b * nc + cidx(c), 0))
    st = pl.BlockSpec((None, C_HEADS, C_HEAD_DIM, C_HEAD_DIM), lambda b, c: (b, 0, 0, 0))
    return pl.pallas_call(
        functools.partial(_gdn_scan_kernel, L=L, rev=rev),
        grid=(B, nc),
        in_specs=[seq(0), seq(1), seq(2), small, small,
                  pl.BlockSpec((None, C_HEADS, L), lambda b, c: (b * nc + cidx(c), 0, 0)), st],
        out_specs=[pl.BlockSpec((L, C_W), lambda b, c: (b * nc + cidx(c), 0)), st],
        out_shape=[jax.ShapeDtypeStruct((B * T, C_W), F32),
                   jax.ShapeDtypeStruct((B, C_HEADS, C_HEAD_DIM, C_HEAD_DIM), F32)],
        scratch_shapes=[pltpu.VMEM((C_HEADS, C_HEAD_DIM, C_HEAD_DIM), F32)],
        compiler_params=_cparams(("parallel", "arbitrary")))(qkv, qkv, qkv, bcol, gcol, grow, s0)


def _gdn_post_kernel(o0_ref, o1_ref, z_ref, g_ref, out_ref):
    for h in range(C_HEADS):
        sl = slice(h * C_HEAD_DIM, (h + 1) * C_HEAD_DIM)
        y = o0_ref[:, sl] + o1_ref[:, sl]
        y = y * lax.rsqrt(jnp.mean(y * y, axis=-1, keepdims=True) + NORM_EPS) * g_ref[...]
        z = z_ref[:, sl]
        out_ref[:, sl] = y * (z * _sigmoid(z))


def _gdn_post(o0, o1, pc, norm_g, tt=256):
    M = o0.shape[0]
    tt = min(tt, M)
    return pl.pallas_call(
        _gdn_post_kernel, grid=(M // tt,),
        in_specs=[pl.BlockSpec((tt, C_W), lambda i: (i, 0)), pl.BlockSpec((tt, C_W), lambda i: (i, 0)),
                  pl.BlockSpec((tt, C_W), lambda i: (i, 3)), pl.BlockSpec((1, C_HEAD_DIM), lambda i: (0, 0))],
        out_specs=pl.BlockSpec((tt, C_W), lambda i: (i, 0)),
        out_shape=jax.ShapeDtypeStruct((M, C_W), F32),
        compiler_params=_cparams(("parallel",)))(o0, o1, pc, norm_g.reshape(1, C_HEAD_DIM))


def _gdn_mixer(pc, B, T, prm, s0, L=CHUNK):
    M = pc.shape[0]
    nc = T // L
    qkv = _gdn_prep(pc, T, prm['conv'])
    bg = _gdn_bg(pc, prm['a_log'], prm['dt_bias'])
    if s0 is None:
        s0 = jnp.zeros((B, 2, C_HEADS, C_HEAD_DIM, C_HEAD_DIM), F32)
    outs, finals = [], []
    for d in range(2):
        bcol = bg[:, d * C_HEADS:(d + 1) * C_HEADS]
        gcol = bg[:, 2 * C_HEADS + d * C_HEADS:2 * C_HEADS + (d + 1) * C_HEADS]
        grow = gcol.reshape(B * nc, L, C_HEADS).transpose(0, 2, 1)
        o, sf = _gdn_scan(qkv, bcol, gcol, grow, s0[:, d], B, T, rev=(d == 1))
        outs.append(o)
        finals.append(sf)
    y = _gdn_post(outs[0], outs[1], pc, prm['norm_g'])
    return y, jnp.stack(finals, axis=1)


def _top_values(tile, count, need_mask):
    rows = lax.broadcasted_iota(jnp.int32, tile.shape, 0).astype(F32)
    big = float(tile.shape[0])
    cur = tile
    vals = []
    for _ in range(count):
        m = jnp.max(cur, axis=0, keepdims=True)
        vals.append(m)
        idx = jnp.min(jnp.where(cur == m, rows, big), axis=0, keepdims=True)
        cur = jnp.where(rows == idx, -jnp.inf, cur)
    return vals, (cur == -jnp.inf) if need_mask else None


PEER_GRP = MXU_K // PEER_HALF
PEER_NGRP = 2 * PEER_HEADS // PEER_GRP


def _peer_topk_kernel(q_ref, kh_ref, kl_ref, thr_ref, e0_ref, e1_ref, s1_ref, sc_ref):
    rows_g = PEER_GRP * N_KEYS
    for g in range(PEER_NGRP):
        qh, ql = _split(q_ref[:, g * MXU_K:(g + 1) * MXU_K])
        kh = kh_ref[g * rows_g:(g + 1) * rows_g, :]
        kl = kl_ref[g * rows_g:(g + 1) * rows_g, :]
        sc_ref[g * rows_g:(g + 1) * rows_g, :] = _dg(kh, qh, NT) + (_dg(kh, ql, NT) + _dg(kl, qh, NT))

    def head(h, carry):
        t0 = sc_ref[pl.ds(pl.multiple_of(h * 2 * N_KEYS, N_KEYS), N_KEYS), :]
        t1 = sc_ref[pl.ds(pl.multiple_of(h * 2 * N_KEYS + N_KEYS, N_KEYS), N_KEYS), :]
        a, ra = _top_values(t0, PEER_TOPK, True)
        b, rb = _top_values(t1, PEER_TOPK, True)
        s0 = jnp.where(ra, t0, -jnp.inf)
        s1 = jnp.where(rb, t1, -jnp.inf)
        amat = jnp.concatenate(a, axis=0)
        bmat = jnp.concatenate(b, axis=0)
        b_lo = bmat[:8]
        row8 = lax.broadcasted_iota(jnp.int32, b_lo.shape, 0)
        pieces = [a[0] + b_lo, a[0] + bmat[8:], a[1] + b_lo]
        for r in range(2, 8):
            pieces.append(jnp.where(row8 < PEER_TOPK // (r + 1), a[r] + b_lo, -jnp.inf))
        pieces.append(amat[8:] + b[0])
        cand = jnp.concatenate(pieces, axis=0)
        best, _ = _top_values(cand, PEER_TOPK + 1, False)
        z = jnp.zeros_like(best[0])
        for n in range(PEER_TOPK):
            z = z + jnp.exp(best[n] - best[0])
        tau = 0.5 * (best[PEER_TOPK - 1] + best[PEER_TOPK])
        thr = tau - s0
        e0 = jnp.exp(s0 - a[0]) * (1.0 / z)
        e1 = jnp.exp(s1 - b[0])
        for tc in range(t0.shape[1] // LANES):
            cols = slice(tc * LANES, (tc + 1) * LANES)
            thr_ref[h, tc] = thr[:, cols]
            e0_ref[h, tc] = e0[:, cols]
            e1_ref[h, tc] = e1[:, cols]
            s1_ref[h, tc] = s1[:, cols]
        return carry

    lax.fori_loop(0, PEER_HEADS, head, 0)


def _peer_topk(q, kdt_hi, kdt_lo, tt=256):
    M = q.shape[0]
    tt = min(tt, M)
    R = PEER_HEADS * 2 * N_KEYS
    big = pl.BlockSpec((PEER_HEADS, tt // LANES, N_KEYS, LANES), lambda i: (0, i, 0, 0))
    return pl.pallas_call(
        _peer_topk_kernel, grid=(M // tt,),
        in_specs=[pl.BlockSpec((tt, PEER_HEADS * PEER_QDIM), lambda i: (i, 0)),
                  pl.BlockSpec((R, MXU_K), lambda i: (0, 0)),
                  pl.BlockSpec((R, MXU_K), lambda i: (0, 0))],
        out_specs=[big] * 4,
        out_shape=[jax.ShapeDtypeStruct((PEER_HEADS, M // LANES, N_KEYS, LANES), F32)] * 4,
        scratch_shapes=[pltpu.VMEM((R, tt), F32)],
        compiler_params=_cparams(("parallel",)))(q, kdt_hi, kdt_lo)


def _peer_expert_kernel(x_ref, u_ref, vt_ref, thr_ref, e0_ref, e1_ref, s1_ref, res_ref, m_ref, o_ref,
                        acc_ref, act0_ref, act1_ref, am0_ref, am1_ref, *, ni, gate_slot):
    e = pl.program_id(1)
    n_blocks = pl.num_programs(1) - 2
    tt = x_ref.shape[0]

    @pl.when(e == 0)
    def _():
        acc_ref[...] = jnp.zeros_like(acc_ref)
        act0_ref[...] = jnp.zeros_like(act0_ref)
        act1_ref[...] = jnp.zeros_like(act1_ref)
        am0_ref[...] = jnp.zeros_like(am0_ref)
        am1_ref[...] = jnp.zeros_like(am1_ref)

    def stages(act_w, act_r, am_w, am_r):
        first = jnp.clip(e - 1, 0, n_blocks - 1) * ni
        d_rows = acc_ref.shape[0] // ni
        for ii in range(ni):
            i = first + ii
            rows = slice(ii * N_KEYS, (ii + 1) * N_KEYS)
            act_w[rows, :] = _dg(u_ref[rows, :], x_ref[...], NT)
            drows = slice(ii * d_rows, (ii + 1) * d_rows)
            acc_ref[drows, :] += _dg(vt_ref[drows, :], am_r[...])
            for tc in range(tt // LANES):
                cols = slice(tc * LANES, (tc + 1) * LANES)
                w = None
                for h in range(PEER_HEADS):
                    keep = s1_ref[h, tc] >= thr_ref[h, tc, pl.ds(i, 1), :]
                    contrib = jnp.where(keep, e1_ref[h, tc], 0.0) * e0_ref[h, tc, pl.ds(i, 1), :]
                    w = contrib if w is None else w + contrib
                a = act_r[rows, cols]
                a = 0.5 * a * (1.0 + lax.erf(a * (2.0 ** -0.5)))
                am_w[rows, cols] = (a * w).astype(BF16)

    @pl.when(e % 2 == 0)
    def _():
        stages(act0_ref, act1_ref, am1_ref, am0_ref)

    @pl.when(e % 2 == 1)
    def _():
        stages(act1_ref, act0_ref, am0_ref, am1_ref)

    @pl.when(e == pl.num_programs(1) - 1)
    def _():
        o_ref[...] = res_ref[...] + m_ref[gate_slot:gate_slot + 1, :] * acc_ref[...].T


def _peer_experts(xn, u_bf, vt_bf, thr, e0, e1, s1m, res, mods, gate_slot, rows_per_mod, tt=512, eb=1024):
    M, D = res.shape
    tt = min(tt, M, rows_per_mod) if mods.shape[0] > 1 else min(tt, M)
    ni = eb // N_KEYS
    nb = N_EXPERTS // eb
    once = pl.Buffered(1)
    if mods.shape[0] > 1:
        mspec = pl.BlockSpec((None, 6, D), lambda i, e: ((i * tt) // rows_per_mod, 0, 0), pipeline_mode=once)
    else:
        mspec = pl.BlockSpec((None, 6, D), lambda i, e: (0, 0, 0), pipeline_mode=once)
    big = pl.BlockSpec((PEER_HEADS, tt // LANES, N_KEYS, LANES), lambda i, e: (0, i, 0, 0), pipeline_mode=once)
    return pl.pallas_call(
        functools.partial(_peer_expert_kernel, ni=ni, gate_slot=gate_slot),
        grid=(M // tt, nb + 2),
        in_specs=[pl.BlockSpec((tt, D), lambda i, e: (i, 0), pipeline_mode=once),
                  pl.BlockSpec((eb, D), lambda i, e: (jnp.minimum(e, nb - 1), 0)),
                  pl.BlockSpec((D, eb), lambda i, e: (0, jnp.clip(e - 2, 0, nb - 1))),
                  big, big, big, big,
                  pl.BlockSpec((tt, D), lambda i, e: (i, 0), pipeline_mode=once), mspec],
        out_specs=pl.BlockSpec((tt, D), lambda i, e: (i, 0)),
        out_shape=jax.ShapeDtypeStruct((M, D), F32),
        scratch_shapes=[pltpu.VMEM((D, tt), F32), pltpu.VMEM((eb, tt), F32), pltpu.VMEM((eb, tt), F32),
                        pltpu.VMEM((eb, tt), BF16), pltpu.VMEM((eb, tt), BF16)],
        compiler_params=_cparams(("parallel", "arbitrary"), VMEM_LIMIT_BIG))(
            xn, u_bf, vt_bf, thr, e0, e1, s1m, res, mods)


def _peer(x, mods, rows_per_mod, norm_g, prm):
    q, xn = _mm_norm(x, prm['wq'], norm_g=norm_g, mods=mods, slots=(3, 4), rows_per_mod=rows_per_mod,
                     hp=True, emit_xn=BF16)
    thr, e0, e1, s1m = _peer_topk(q, prm['kdt_hi'], prm['kdt_lo'])
    return _peer_experts(xn, prm['u'], prm['vt'], thr, e0, e1, s1m, x, mods, 5, rows_per_mod)


def _peer_params(wq, keys, u_tab, v_tab):
    nset = 2 * PEER_HEADS
    kd = jnp.zeros((nset, N_KEYS, PEER_GRP, PEER_HALF), F32)
    sets = jnp.arange(nset)
    kd = kd.at[sets, :, sets % PEER_GRP, :].set(keys.reshape(nset, N_KEYS, PEER_HALF))
    kd_hi, kd_lo = _split(kd.reshape(nset * N_KEYS, MXU_K))
    return dict(wq=_split(wq), kdt_hi=kd_hi, kdt_lo=kd_lo, u=u_tab.astype(BF16), vt=v_tab.T.astype(BF16))


def _layer_even(x, mods, B, T, norm1, w_a, w_b, w_out, qk_gains, rwkv_prm, ctx):
    rpm = T
    pa = _mm_norm(x, w_a, norm_g=norm1, mods=mods, slots=(0, 1), rows_per_mod=rpm)
    pb = _mm_norm(x, w_b, norm_g=norm1, mods=mods, slots=(0, 1), rows_per_mod=rpm)
    nqk = A_HEADS + A_KV_HEADS
    grp = A_HEADS // A_KV_HEADS
    scale = A_HEAD_DIM ** -0.5
    if ctx is None:
        qk = _norm_rope(pa, nqk, T, gains=qk_gains)
        oa = _attention(B, T, T, A_HEADS, scale, (qk, lambda h: h), (qk, lambda h: A_HEADS + h // grp),
                        (pa, lambda h: nqk + h // grp))
        ctx_out = (qk[:, A_Q:A_Q + A_KV], pa[:, A_Q + A_KV:])
        s0 = None
    else:
        cache_k, cache_v, s0 = ctx
        P = cache_k.shape[1]
        qk = _norm_rope(pa, nqk, T, gains=qk_gains, tables=_rope_tables(T, A_HEAD_DIM))
        k_all = jnp.concatenate([qk[:, A_Q:].reshape(B, T, A_KV), cache_k.reshape(B, P, A_KV)], axis=1)
        v_all = jnp.concatenate([pa[:, A_Q + A_KV:].reshape(B, T, A_KV), cache_v.reshape(B, P, A_KV)], axis=1)
        Tk = T + P
        oa = _attention(B, T, Tk, A_HEADS, scale, (qk, lambda h: h),
                        (k_all.reshape(B * Tk, A_KV), lambda h: h // grp),
                        (v_all.reshape(B * Tk, A_KV), lambda h: h // grp))
        ctx_out = None
    ob, s_fin = _rwkv_mixer(pb, B, T, rwkv_prm, s0)
    x = _mm_res([oa, ob], w_out, x, mods, 2, rpm)
    return x, ctx_out, s_fin


D_QROPE_BLK = 0
D_KROPE_BLK = D_HEADS
D_NROPE_BLK = D_HEADS + 1
D_CKV_BLK = 12
D_QNOPE_BLK = 16
D_PROJ = 24 * LANES


def _mla_in_weight(w_d):
    K = w_d.shape[0]
    wq = w_d[:, :D_Q].reshape(K, D_HEADS, D_QK)
    z64 = jnp.zeros((K, D_HEADS, LANES - D_ROPE), w_d.dtype)
    q_rope = jnp.concatenate([wq[:, :, D_NOPE:], z64], axis=-1).reshape(K, D_HEADS * LANES)
    q_nope = wq[:, :, :D_NOPE].reshape(K, D_HEADS * LANES)
    k_rope = jnp.concatenate([w_d[:, D_Q + D_KV_RANK:], jnp.zeros((K, LANES - D_ROPE), w_d.dtype)], axis=-1)
    padz = jnp.zeros((K, (D_CKV_BLK - D_NROPE_BLK) * LANES), w_d.dtype)
    return jnp.concatenate([q_rope, k_rope, padz, w_d[:, D_Q:D_Q + D_KV_RANK], q_nope], axis=-1)


def _layer_odd(x, mods, B, T, norm1, w_c, w_d, w_out, gdn_prm, kv_norm_g, w_kv_b, ctx):
    rpm = T
    pc = _mm_norm(x, w_c, norm_g=norm1, mods=mods, slots=(0, 1), rows_per_mod=rpm)
    pd = _mm_norm(x, w_d, norm_g=norm1, mods=mods, slots=(0, 1), rows_per_mod=rpm)
    scale = D_QK ** -0.5
    ckv_col = D_CKV_BLK * LANES // D_KV_RANK
    kv, ckv_n = _mm_norm(pd, w_kv_b, k=D_KV_RANK, xcol=ckv_col, norm_g=kv_norm_g, emit_xn=F32)
    kcol = lambda h: 2 * h
    vcol = lambda h: 2 * h + 1
    if ctx is None:
        od = _attention(B, T, T, D_HEADS, scale, (pd, lambda h: D_QNOPE_BLK + h), (kv, kcol), (kv, vcol),
                        q2=(pd, lambda h: D_QROPE_BLK + h), k2=(pd, lambda h: D_KROPE_BLK))
        ctx_out = (ckv_n, pd[:, D_KROPE_BLK * LANES:D_KROPE_BLK * LANES + D_ROPE])
        s0 = None
    else:
        s0, c_ckv, c_krope = ctx
        P = c_ckv.shape[1]
        Tk = T + P
        roped = _norm_rope(pd, D_NROPE_BLK, T, tables=_rope_tables(T, D_ROPE))
        kv_ctx = _mm_norm(c_ckv.reshape(B * P, D_KV_RANK), w_kv_b)
        kv_all = jnp.concatenate([kv.reshape(B, T, -1), kv_ctx.reshape(B, P, -1)], axis=1).reshape(B * Tk, -1)
        kr_lat = roped[:, D_KROPE_BLK * LANES:].reshape(B, T, LANES)
        kr_ctx = jnp.concatenate([c_krope, jnp.zeros((B, P, LANES - D_ROPE), F32)], axis=-1)
        kr_all = jnp.concatenate([kr_lat, kr_ctx], axis=1).reshape(B * Tk, LANES)
        od = _attention(B, T, Tk, D_HEADS, scale, (pd, lambda h: D_QNOPE_BLK + h), (kv_all, kcol), (kv_all, vcol),
                        q2=(roped, lambda h: D_QROPE_BLK + h), k2=(kr_all, lambda h: 0))
        ctx_out = None
    oc, s_fin = _gdn_mixer(pc, B, T, gdn_prm, s0)
    x = _mm_res([oc, od], w_out, x, mods, 2, rpm)
    return x, ctx_out, s_fin


def _modulation(cond, w_ada, b_ada):
    n = cond.shape[0]
    rows = ((n + 7) // 8) * 8
    a = cond * _sigmoid(cond)
    a = jnp.concatenate([a, jnp.zeros((rows - n, cond.shape[1]), F32)], axis=0)
    m = _mm_norm(a, _split(w_ada), hp=True, tn_cap=1024)[:n] + b_ada
    return m.reshape(n, 6, D_MODEL)


def kernel(x_prompt, x_sample, cache_attn_k, cache_attn_v, state_rwkv, state_gdn, cache_mla_ckv, cache_mla_krope,
           c, c_ctx, norm1_g, norm2_g, ada_w, ada_b, w_in_ab, w_out_ab, attn_q_norm, attn_k_norm,
           rwkv_mu, rwkv_w0, rwkv_w2, rwkv_a0, rwkv_a2, rwkv_g2, rwkv_k_k, rwkv_k_a, rwkv_r_k, rwkv_ln_w, rwkv_ln_b,
           w_in_cd, w_out_cd, gdn_conv, gdn_a_log, gdn_dt_bias, gdn_norm_g, mla_kv_norm_g, mla_w_kv_b,
           peer_wq, peer_keys, peer_u, peer_v, final_norm_g):
    Bp, Tp, D = x_prompt.shape
    Bs, Ts, _ = x_sample.shape
    depth = norm1_g.shape[0]
    xp = x_prompt.reshape(Bp * Tp, D)
    xs = x_sample.reshape(Bs * Ts, D)
    cond = jnp.concatenate([c_ctx[None, :], c], axis=0)
    outs = dict(k=[], v=[], rwkv=[], gdn=[], ckv=[], krope=[])
    for li in range(depth):
        j = li // 2
        mods = _modulation(cond, ada_w[li], ada_b[li])
        mods_p, mods_s = mods[:1], mods[1:]
        peer_prm = _peer_params(peer_wq[li], peer_keys[li], peer_u[li], peer_v[li])
        if li % 2 == 0:
            w_in = w_in_ab[j].astype(BF16)
            w_a, w_b = w_in[:, :A_IN], w_in[:, A_IN:]
            w_o = w_out_ab[j].astype(BF16)
            w_out = [w_o[:A_Q], w_o[A_Q:]]
            gains = jnp.concatenate([jnp.tile(attn_q_norm[j], A_HEADS), jnp.tile(attn_k_norm[j], A_KV_HEADS)])
            rp = dict(mu=rwkv_mu[j], w0=rwkv_w0[j], w2=rwkv_w2[j], a0=rwkv_a0[j], a2=rwkv_a2[j], g2=rwkv_g2[j],
                      k_k=rwkv_k_k[j], k_a=rwkv_k_a[j], r_k=rwkv_r_k[j], ln_w=rwkv_ln_w[j], ln_b=rwkv_ln_b[j])
            xp, (k_c, v_c), s_c = _layer_even(xp, mods_p, Bp, Tp, norm1_g[li], w_a, w_b, w_out, gains, rp, None)
            xs, _, _ = _layer_even(xs, mods_s, Bs, Ts, norm1_g[li], w_a, w_b, w_out, gains, rp,
                                   (cache_attn_k[:, j], cache_attn_v[:, j], state_rwkv[:, j]))
            outs['k'].append(k_c.reshape(Bp, Tp, A_KV_HEADS, A_HEAD_DIM))
            outs['v'].append(v_c.reshape(Bp, Tp, A_KV_HEADS, A_HEAD_DIM))
            outs['rwkv'].append(s_c)
        else:
            w_in = w_in_cd[j]
            w_c = jnp.concatenate([w_in[:, :C_IN], jnp.zeros((D, LANES - 4 * C_HEADS), F32)], axis=-1).astype(BF16)
            w_d = _mla_in_weight(w_in[:, C_IN:]).astype(BF16)
            w_o = w_out_cd[j].astype(BF16)
            w_out = [w_o[:C_W], w_o[C_W:]]
            gp = dict(conv=gdn_conv[j], a_log=gdn_a_log[j], dt_bias=gdn_dt_bias[j], norm_g=gdn_norm_g[j])
            w_kv_b = mla_w_kv_b[j].astype(BF16)
            xp, (ckv_c, kr_c), s_c = _layer_odd(xp, mods_p, Bp, Tp, norm1_g[li], w_c, w_d, w_out, gp,
                                                mla_kv_norm_g[j], w_kv_b, None)
            xs, _, _ = _layer_odd(xs, mods_s, Bs, Ts, norm1_g[li], w_c, w_d, w_out, gp, mla_kv_norm_g[j], w_kv_b,
                                  (state_gdn[:, j], cache_mla_ckv[:, j], cache_mla_krope[:, j]))
            outs['gdn'].append(s_c)
            outs['ckv'].append(ckv_c.reshape(Bp, Tp, D_KV_RANK))
            outs['krope'].append(kr_c.reshape(Bp, Tp, D_ROPE))
        xp = _peer(xp, mods_p, Tp, norm2_g[li], peer_prm)
        xs = _peer(xs, mods_s, Ts, norm2_g[li], peer_prm)
    y_prompt = _rmsnorm(xp, final_norm_g).reshape(Bp, Tp, D)
    y_sample = _rmsnorm(xs, final_norm_g).reshape(Bs, Ts, D)
    st = lambda name: jnp.stack(outs[name], axis=1)
    return (y_prompt, y_sample, st('k'), st('v'), st('rwkv'), st('gdn'), st('ckv'), st('krope'))
```

```python
import functools
import math

import numpy as np
import jax
import jax.numpy as jnp
from jax import lax
from jax.experimental import pallas as pl
from jax.experimental.pallas import tpu as pltpu

F32 = jnp.float32
BF16 = jnp.bfloat16

D_MODEL = 2048
GRID_W = 64
ROPE_THETA = 10000.0
NORM_EPS = 1e-6
A_HEADS, A_KV_HEADS, A_HEAD_DIM = 8, 2, 128
A_Q = A_HEADS * A_HEAD_DIM
A_KV = A_KV_HEADS * A_HEAD_DIM
A_IN = A_Q + 2 * A_KV
B_HEADS, B_HEAD_DIM = 16, 64
B_W = B_HEADS * B_HEAD_DIM
B_DECAY_LORA, B_AAA_LORA, B_GATE_LORA = 64, 64, 128
B_IN = 3 * B_W + 2 * B_DECAY_LORA + 2 * B_AAA_LORA + B_GATE_LORA
B_GN_EPS = 64e-5
C_HEADS, C_HEAD_DIM = 8, 128
C_W = C_HEADS * C_HEAD_DIM
C_IN = 4 * C_W + 4 * C_HEADS
D_HEADS, D_NOPE, D_ROPE, D_V = 8, 128, 64, 128
D_QK = D_NOPE + D_ROPE
D_KV_RANK = 512
D_Q = D_HEADS * D_QK
PEER_HEADS, N_KEYS, PEER_QDIM, PEER_TOPK = 8, 128, 128, 16
PEER_HALF = PEER_QDIM // 2
N_EXPERTS = N_KEYS * N_KEYS

LANES = 128
MXU_K = 256
VMEM_LIMIT = 48 * 1024 * 1024
VMEM_LIMIT_BIG = 56 * 1024 * 1024

CHUNK = 64

NN = ((1,), (0,))
NT = ((1,), (1,))
TN = ((0,), (0,))


def _dg(a, b, dims=NN):
    return lax.dot_general(a, b, (dims, ((), ())), preferred_element_type=F32)


def _split(x):
    hi = x.astype(BF16)
    lo = (x - hi.astype(F32)).astype(BF16)
    return hi, lo


def _dot_hp(a, b, dims=NN):
    ah, al = _split(a)
    bh, bl = _split(b)
    return _dg(ah, bh, dims) + (_dg(ah, bl, dims) + _dg(al, bh, dims))


def _dot_bf(a, b, dims=NN):
    return _dg(a.astype(BF16), b.astype(BF16), dims)


def _sigmoid(x):
    return 1.0 / (1.0 + jnp.exp(-x))


def _softplus(x):
    return jnp.maximum(x, 0.0) + jnp.log(1.0 + jnp.exp(-jnp.abs(x)))


def _cparams(sem, vmem=VMEM_LIMIT):
    return pltpu.CompilerParams(dimension_semantics=sem, vmem_limit_bytes=vmem)


def _pick_tile(n, cap):
    best = None
    t = LANES
    while t <= min(n, cap):
        if n % t == 0:
            best = t
        t += LANES
    return best if best is not None else n


def _mm_norm_kernel(*refs, has_norm, has_mod, shift_slot, scale_slot, hp, emit_xn):
    it = iter(refs)
    x_ref = next(it)
    g_ref = next(it) if has_norm else None
    m_ref = next(it) if has_mod else None
    wh_ref = next(it)
    wl_ref = next(it) if hp else None
    o_ref = next(it)
    xn_ref = next(it) if emit_xn else None
    xh_ref = next(it)
    xl_ref = next(it) if hp else None

    @pl.when(pl.program_id(1) == 0)
    def _():
        xv = x_ref[...].astype(F32)
        if has_norm:
            xv = xv * lax.rsqrt(jnp.mean(xv * xv, axis=-1, keepdims=True) + NORM_EPS) * g_ref[...]
        if has_mod:
            xv = xv * (1.0 + m_ref[scale_slot:scale_slot + 1, :]) + m_ref[shift_slot:shift_slot + 1, :]
        hi = xv.astype(BF16)
        xh_ref[...] = hi
        if hp:
            xl_ref[...] = (xv - hi.astype(F32)).astype(BF16)
        if emit_xn:
            xn_ref[...] = xv.astype(xn_ref.dtype)

    acc = _dg(xh_ref[...], wh_ref[...])
    if hp:
        acc = acc + (_dg(xh_ref[...], wl_ref[...]) + _dg(xl_ref[...], wh_ref[...]))
    o_ref[...] = acc


def _mm_norm(x, w, *, k=None, xcol=0, norm_g=None, mods=None, slots=(0, 1), rows_per_mod=None,
             hp=False, emit_xn=None, tm=512, tn_cap=1664):
    M = x.shape[0]
    K = k if k is not None else x.shape[1]
    ws = tuple(w) if isinstance(w, (tuple, list)) else (w,)
    N = ws[0].shape[1]
    tm = min(tm, M)
    if mods is not None and mods.shape[0] > 1:
        tm = min(tm, rows_per_mod)
    tn = _pick_tile(N, tn_cap)
    has_norm = norm_g is not None
    has_mod = mods is not None
    rpm = rows_per_mod if rows_per_mod is not None else M

    in_specs = [pl.BlockSpec((tm, K), lambda i, j: (i, xcol))]
    args = [x]
    if has_norm:
        in_specs.append(pl.BlockSpec((1, K), lambda i, j: (0, 0)))
        args.append(norm_g.reshape(1, K).astype(F32))
    if has_mod:
        if mods.shape[0] > 1:
            in_specs.append(pl.BlockSpec((None, 6, K), lambda i, j: ((i * tm) // rpm, 0, 0)))
        else:
            in_specs.append(pl.BlockSpec((None, 6, K), lambda i, j: (0, 0, 0)))
        args.append(mods)
    for wi in ws:
        in_specs.append(pl.BlockSpec((K, tn), lambda i, j: (0, j)))
        args.append(wi)
    out_shape = [jax.ShapeDtypeStruct((M, N), F32)]
    out_specs = [pl.BlockSpec((tm, tn), lambda i, j: (i, j))]
    if emit_xn is not None:
        out_shape.append(jax.ShapeDtypeStruct((M, K), emit_xn))
        out_specs.append(pl.BlockSpec((tm, K), lambda i, j: (i, 0)))
    scratch = [pltpu.VMEM((tm, K), BF16)]
    if hp:
        scratch.append(pltpu.VMEM((tm, K), BF16))
    kern = functools.partial(_mm_norm_kernel, has_norm=has_norm, has_mod=has_mod, shift_slot=slots[0],
                             scale_slot=slots[1], hp=hp, emit_xn=emit_xn is not None)
    outs = pl.pallas_call(
        kern, grid=(M // tm, N // tn), in_specs=in_specs, out_specs=out_specs, out_shape=out_shape,
        scratch_shapes=scratch, compiler_params=_cparams(("parallel", "arbitrary")))(*args)
    return outs if emit_xn is not None else outs[0]


def _mm_res_kernel(*refs, n_in, gate_slot):
    a_refs = refs[:n_in]
    w_refs = refs[n_in:2 * n_in]
    res_ref, m_ref, o_ref = refs[2 * n_in:]
    acc = None
    for a_ref, w_ref in zip(a_refs, w_refs):
        part = _dg(a_ref[...].astype(BF16), w_ref[...])
        acc = part if acc is None else acc + part
    o_ref[...] = res_ref[...] + m_ref[gate_slot:gate_slot + 1, :] * acc


def _mm_res(a_list, w_list, res, mods, gate_slot, rows_per_mod, tm=512, tn=1024):
    M, N = res.shape
    tm = min(tm, M, rows_per_mod) if mods.shape[0] > 1 else min(tm, M)
    tn = min(tn, N)
    n_in = len(a_list)
    in_specs, args = [], []
    for a in a_list:
        in_specs.append(pl.BlockSpec((tm, a.shape[1]), lambda i, j: (i, 0)))
        args.append(a)
    for w in w_list:
        in_specs.append(pl.BlockSpec((w.shape[0], tn), lambda i, j: (0, j)))
        args.append(w)
    in_specs.append(pl.BlockSpec((tm, tn), lambda i, j: (i, j)))
    args.append(res)
    if mods.shape[0] > 1:
        in_specs.append(pl.BlockSpec((None, 6, tn), lambda i, j: ((i * tm) // rows_per_mod, 0, j)))
    else:
        in_specs.append(pl.BlockSpec((None, 6, tn), lambda i, j: (0, 0, j)))
    args.append(mods)
    return pl.pallas_call(
        functools.partial(_mm_res_kernel, n_in=n_in, gate_slot=gate_slot),
        grid=(M // tm, N // tn), in_specs=in_specs,
        out_specs=pl.BlockSpec((tm, tn), lambda i, j: (i, j)),
        out_shape=jax.ShapeDtypeStruct((M, N), F32),
        compiler_params=_cparams(("parallel", "arbitrary")))(*args)


def _rmsnorm_kernel(x_ref, g_ref, o_ref):
    xv = x_ref[...]
    o_ref[...] = xv * lax.rsqrt(jnp.mean(xv * xv, axis=-1, keepdims=True) + NORM_EPS) * g_ref[...]


def _rmsnorm(x, g, tm=512):
    M, K = x.shape
    tm = min(tm, M)
    return pl.pallas_call(
        _rmsnorm_kernel, grid=(M // tm,),
        in_specs=[pl.BlockSpec((tm, K), lambda i: (i, 0)), pl.BlockSpec((1, K), lambda i: (0, 0))],
        out_specs=pl.BlockSpec((tm, K), lambda i: (i, 0)),
        out_shape=jax.ShapeDtypeStruct((M, K), F32),
        compiler_params=_cparams(("parallel",)))(x, g.reshape(1, K))


def _rope_tables(T, rot_dim):
    rows = T // GRID_W
    row = jnp.repeat(jnp.arange(rows, dtype=F32), GRID_W)
    col = jnp.tile(jnp.arange(GRID_W, dtype=F32), rows)
    n_freq = rot_dim // 4
    inv = ROPE_THETA ** (-jnp.arange(n_freq, dtype=F32) / n_freq)
    ang = jnp.concatenate([row[:, None] * inv, col[:, None] * inv], axis=-1)
    cos = jnp.repeat(jnp.cos(ang), 2, axis=-1)
    sin = jnp.repeat(jnp.sin(ang), 2, axis=-1) * jnp.tile(jnp.array([-1.0, 1.0], F32), rot_dim // 2)
    pad = LANES - rot_dim
    if pad:
        cos = jnp.concatenate([cos, jnp.ones((T, pad), F32)], axis=-1)
        sin = jnp.concatenate([sin, jnp.zeros((T, pad), F32)], axis=-1)
    return cos, sin


def _rope(y, cosf, sinf):
    lane = lax.broadcasted_iota(jnp.int32, y.shape, 1)
    even = (lane & 1) == 0
    width = y.shape[1]
    swap = jnp.where(even, pltpu.roll(y, width - 1, 1), pltpu.roll(y, 1, 1))
    return y * cosf + swap * sinf


def _norm_rope_kernel(*refs, nblk, norm, rope):
    it = iter(refs)
    x_ref = next(it)
    g_ref = next(it) if norm else None
    cos_ref = next(it) if rope else None
    sin_ref = next(it) if rope else None
    o_ref = next(it)
    for c in range(nblk):
        sl = slice(c * LANES, (c + 1) * LANES)
        y = x_ref[:, sl]
        if norm:
            y = y * lax.rsqrt(jnp.mean(y * y, axis=-1, keepdims=True) + NORM_EPS) * g_ref[:, sl]
        if rope:
            y = _rope(y, cos_ref[...], sin_ref[...])
        o_ref[:, sl] = y


def _norm_rope(x, nblk, T, gains=None, tables=None, tt=256):
    M = x.shape[0]
    tt = min(tt, T)
    W = nblk * LANES
    in_specs = [pl.BlockSpec((tt, W), lambda i: (i, 0))]
    args = [x]
    if gains is not None:
        in_specs.append(pl.BlockSpec((1, W), lambda i: (0, 0)))
        args.append(gains.reshape(1, W))
    if tables is not None:
        nt = T // tt
        for t in tables:
            in_specs.append(pl.BlockSpec((tt, LANES), lambda i: (i % nt, 0)))
            args.append(t)
    return pl.pallas_call(
        functools.partial(_norm_rope_kernel, nblk=nblk, norm=gains is not None, rope=tables is not None),
        grid=(M // tt,), in_specs=in_specs,
        out_specs=pl.BlockSpec((tt, W), lambda i: (i, 0)),
        out_shape=jax.ShapeDtypeStruct((M, W), F32),
        compiler_params=_cparams(("parallel",)))(*args)


def _attn_kernel(*refs, scale, has2):
    it = iter(refs)
    q_ref = next(it)
    q2_ref = next(it) if has2 else None
    k_ref = next(it)
    k2_ref = next(it) if has2 else None
    v_ref = next(it)
    o_ref = next(it)
    s = _dot_bf(q_ref[...] * scale, k_ref[...], NT)
    if has2:
        s = s + _dot_bf(q2_ref[...] * scale, k2_ref[...], NT)
    m = jnp.max(s, axis=-1, keepdims=True)
    p = jnp.exp(s - m)
    l = jnp.sum(p, axis=-1, keepdims=True)
    o_ref[...] = _dot_bf(p, v_ref[...]) / l


def _attention(B, T, Tk, H, scale, q, k, v, q2=None, k2=None, tq=512):
    tq = min(tq, T)
    nq = T // tq
    has2 = q2 is not None

    def qspec(cf):
        return pl.BlockSpec((tq, LANES), lambda b, h, i: (b * nq + i, cf(h)))

    def kspec(cf):
        return pl.BlockSpec((Tk, LANES), lambda b, h, i: (b, cf(h)))

    in_specs, args = [qspec(q[1])], [q[0]]
    if has2:
        in_specs.append(qspec(q2[1]))
        args.append(q2[0])
    in_specs.append(kspec(k[1]))
    args.append(k[0])
    if has2:
        in_specs.append(kspec(k2[1]))
        args.append(k2[0])
    in_specs.append(kspec(v[1]))
    args.append(v[0])
    return pl.pallas_call(
        functools.partial(_attn_kernel, scale=scale, has2=has2),
        grid=(B, H, nq), in_specs=in_specs,
        out_specs=pl.BlockSpec((tq, LANES), lambda b, h, i: (b * nq + i, h)),
        out_shape=jax.ShapeDtypeStruct((B * T, H * LANES), F32),
        compiler_params=_cparams(("parallel", "arbitrary", "arbitrary")))(*args)


def _tri_masks(L, rev):
    i = lax.broadcasted_iota(jnp.int32, (L, L), 0)
    j = lax.broadcasted_iota(jnp.int32, (L, L), 1)
    if rev:
        return j >= i, j > i, i == j
    return j <= i, j < i, i == j


def _inv_unit_tri(mats, eye_f, L, dot):
    pows = [[-a for a in mats]]
    for _ in range(int(math.log2(L)) - 1):
        pows.append([dot(p, p) for p in pows[-1]])
    terms = [[eye_f + p for p in level] for level in pows]
    while len(terms) > 1:
        nxt = [[dot(x, y) for x, y in zip(terms[i], terms[i + 1])] for i in range(0, len(terms) - 1, 2)]
        if len(terms) % 2:
            nxt.append(terms[-1])
        terms = nxt
    return terms[0]


def _shifted(f, prev_ref, next_ref, tt, T):
    i = pl.program_id(0)
    first = (i * tt) % T == 0
    last = ((i + 1) * tt) % T == 0
    prev = jnp.where(first, 0.0, prev_ref[7:8, :])
    nxt = jnp.where(last, 0.0, next_ref[0:1, :])
    rows = lax.broadcasted_iota(jnp.int32, f.shape, 0)
    fp = jnp.where(rows == 0, prev, pltpu.roll(f, 1, 0))
    fn = jnp.where(rows == tt - 1, nxt, pltpu.roll(f, tt - 1, 0))
    return fp, fn


def _halo_specs(tt, width, M, col_fn):
    r8 = tt // 8
    last8 = M // 8 - 1
    return [
        pl.BlockSpec((tt, width), lambda *g: (g[0], col_fn(*g))),
        pl.BlockSpec((8, width), lambda *g: (jnp.maximum(g[0] * r8 - 1, 0), col_fn(*g))),
        pl.BlockSpec((8, width), lambda *g: (jnp.minimum((g[0] + 1) * r8, last8), col_fn(*g))),
    ]


NPAIR = B_HEADS // 2


def _pair_sum_matrix():
    r = lax.broadcasted_iota(jnp.int32, (LANES, LANES), 0) // B_HEAD_DIM
    c = lax.broadcasted_iota(jnp.int32, (LANES, LANES), 1) // B_HEAD_DIM
    return (r == c).astype(F32)


def _rwkv_prep_kernel(x_ref, prev_ref, next_ref, mu_ref, w0_ref, w2_ref, a0_ref, a2_ref, g2_ref, kk_ref_in,
                      r_o, k_o, v_o, kk_o, g_o, lw0_o, lw1_o, a0_o, a1_o, *, tt, T):
    f = x_ref[...]
    fp, fn = _shifted(f, prev_ref, next_ref, tt, T)
    f = f + mu_ref[0:1, :] * (fp - f) + mu_ref[1:2, :] * (fn - f)
    r = f[:, :B_W]
    k = f[:, B_W:2 * B_W]
    v = f[:, 2 * B_W:3 * B_W]
    off = 3 * B_W
    w_lo = f[:, off:off + 2 * B_DECAY_LORA]
    off += 2 * B_DECAY_LORA
    a_lo = f[:, off:off + 2 * B_AAA_LORA]
    off += 2 * B_AAA_LORA
    g_lo = f[:, off:]
    th = jnp.tanh(w_lo)
    logw, aa = [], []
    for d in range(2):
        wpre = w0_ref[d:d + 1, :] + _dot_hp(th[:, d * B_DECAY_LORA:(d + 1) * B_DECAY_LORA], w2_ref[d])
        wv = -_softplus(-wpre) - 0.5
        logw.append(-jnp.exp(wv))
        aa.append(_sigmoid(a0_ref[d:d + 1, :] + _dot_hp(a_lo[:, d * B_AAA_LORA:(d + 1) * B_AAA_LORA], a2_ref[d])))
    g = _dot_hp(_sigmoid(g_lo), g2_ref[...])
    kk = k * kk_ref_in[...]
    gsum = _pair_sum_matrix()
    for p in range(NPAIR):
        sl = slice(p * LANES, (p + 1) * LANES)
        kp = kk[:, sl]
        ssq = _dot_hp(kp * kp, gsum)
        kk_o[p] = kp / jnp.maximum(jnp.sqrt(ssq), 1e-12)
        r_o[p] = r[:, sl]
        k_o[p] = k[:, sl]
        v_o[p] = v[:, sl]
        g_o[p] = g[:, sl]
        lw0_o[p] = logw[0][:, sl]
        lw1_o[p] = logw[1][:, sl]
        a0_o[p] = aa[0][:, sl]
        a1_o[p] = aa[1][:, sl]


def _rwkv_prep(pb, B, T, mu, w0, w2, a0, a2, g2, k_k, tt=128):
    M = pb.shape[0]
    tt = min(tt, T)
    nt = T // tt
    full = lambda shape: pl.BlockSpec(shape, lambda i: tuple(0 for _ in shape))
    in_specs = _halo_specs(tt, B_IN, M, lambda i: 0) + [
        full((2, B_IN)), full((2, B_W)), full((2, B_DECAY_LORA, B_W)), full((2, B_W)),
        full((2, B_AAA_LORA, B_W)), full((B_GATE_LORA, B_W)), full((1, B_W))]
    ospec = pl.BlockSpec((None, NPAIR, tt, LANES), lambda i: (i // nt, 0, i % nt, 0))
    oshape = jax.ShapeDtypeStruct((B, NPAIR, T, LANES), F32)
    return pl.pallas_call(
        functools.partial(_rwkv_prep_kernel, tt=tt, T=T),
        grid=(M // tt,), in_specs=in_specs, out_specs=[ospec] * 9, out_shape=[oshape] * 9,
        compiler_params=_cparams(("parallel",)))(pb, pb, pb, mu, w0, w2, a0, a2, g2, k_k.reshape(1, B_W))


def _rwkv_chunk(r2, lw2, kd2, v2, kk2, a2, states, masks, L, rev):
    incl, strict, eye = masks
    incl_f = incl.astype(F32)
    eye_f = eye.astype(F32)
    i2 = lax.broadcasted_iota(jnp.int32, (L, 2 * L), 0)
    j2 = lax.broadcasted_iota(jnp.int32, (L, 2 * L), 1) % L
    incl2 = (j2 >= i2) if rev else (j2 <= i2)
    last = 0 if rev else L - 1
    halves = (slice(0, B_HEAD_DIM), slice(B_HEAD_DIM, 2 * B_HEAD_DIM))
    logp2 = [_dot_hp(incl_f, lw) for lw in lw2]
    lhs, rhs, rem_rhs, dec, vs = [], [], [], [], []
    for lp, lw, r, kd, v, kk, a in zip(logp2, lw2, r2, kd2, v2, kk2, a2):
        ninv = jnp.exp(-lp)
        b = kk * a
        kt = kk * jnp.exp(lp - lw)
        rt = r * jnp.exp(lp)
        bn, kdn = b * ninv, kd * ninv
        pl_row = lp[last:last + 1, :]
        rem = jnp.exp(pl_row - lp)
        bh, kh = b * rem, kd * rem
        dpl = jnp.exp(pl_row)
        for sl in halves:
            lhs.append(jnp.concatenate([kt[:, sl], rt[:, sl]], axis=0))
            rhs.append(jnp.concatenate([bn[:, sl], kdn[:, sl]], axis=0))
            rem_rhs.append(jnp.concatenate([bh[:, sl], kh[:, sl]], axis=0))
            dec.append(dpl[:, sl])
            vs.append(v[:, sl])
    gm = [_dot_hp(x, y, NT) for x, y in zip(lhs, rhs)]
    ls = [_dot_hp(x, s, NT) for x, s in zip(lhs, states)]
    akv = [_dot_hp(jnp.where(strict, g[:L, L:], 0.0), v) for g, v in zip(gm, vs)]
    tinv = _inv_unit_tri([jnp.where(strict, g[:L, :L], 0.0) for g in gm], eye_f, L, _dot_hp)
    u = [-_dot_hp(t, l[:L] + x) for t, l, x in zip(tinv, ls, akv)]
    uv = [jnp.concatenate([x, v], axis=0) for x, v in zip(u, vs)]
    y = [l[L:] + _dot_hp(jnp.where(incl2, g[L:, :], 0.0), x) for l, g, x in zip(ls, gm, uv)]
    s_new = [s * d + _dot_hp(x, m, TN) for s, d, x, m in zip(states, dec, uv, rem_rhs)]
    y2 = [jnp.concatenate([y[2 * p], y[2 * p + 1]], axis=1) for p in range(len(r2))]
    return y2, s_new


def _rwkv_scan_kernel(r_ref, v_ref, kk_ref, k_ref, lw_ref, a_ref, ka_ref, s0_ref, y_ref, sf_ref, s_ref, *, L, rev):
    c = pl.program_id(1)

    @pl.when(c == 0)
    def _():
        s_ref[...] = s0_ref[...]

    masks = _tri_masks(L, rev)
    pairs = range(NPAIR)
    a2 = [a_ref[p] for p in pairs]
    kd2 = [k_ref[p] * (1.0 + (a - 1.0) * ka_ref[p]) for p, a in zip(pairs, a2)]
    y2, s_new = _rwkv_chunk([r_ref[p] for p in pairs], [lw_ref[p] for p in pairs], kd2, [v_ref[p] for p in pairs],
                            [kk_ref[p] for p in pairs], a2, [s_ref[h] for h in range(B_HEADS)], masks, L, rev)
    for p in pairs:
        y_ref[p] = y2[p]
    for h in range(B_HEADS):
        s_ref[h] = s_new[h]

    @pl.when(c == pl.num_programs(1) - 1)
    def _():
        sf_ref[...] = s_ref[...]


def _rwkv_scan(r, v, kk, k, lw, a, k_a, s0, rev, L=CHUNK):
    B, _, T, _ = r.shape
    nc = T // L
    cmap = (lambda b, c: (b, 0, nc - 1 - c, 0)) if rev else (lambda b, c: (b, 0, c, 0))
    seq = pl.BlockSpec((None, NPAIR, L, LANES), cmap)
    st = pl.BlockSpec((None, B_HEADS, B_HEAD_DIM, B_HEAD_DIM), lambda b, c: (b, 0, 0, 0))
    return pl.pallas_call(
        functools.partial(_rwkv_scan_kernel, L=L, rev=rev),
        grid=(B, nc),
        in_specs=[seq] * 6 + [pl.BlockSpec((NPAIR, 1, LANES), lambda b, c: (0, 0, 0)), st],
        out_specs=[seq, st],
        out_shape=[jax.ShapeDtypeStruct((B, NPAIR, T, LANES), F32),
                   jax.ShapeDtypeStruct((B, B_HEADS, B_HEAD_DIM, B_HEAD_DIM), F32)],
        scratch_shapes=[pltpu.VMEM((B_HEADS, B_HEAD_DIM, B_HEAD_DIM), F32)],
        compiler_params=_cparams(("parallel", "arbitrary")))(r, v, kk, k, lw, a, k_a.reshape(NPAIR, 1, LANES), s0)


def _rwkv_post_kernel(y0_ref, y1_ref, r_ref, k_ref, v_ref, g_ref, rk_ref, lnw_ref, lnb_ref, o_ref):
    gsum = _pair_sum_matrix()
    inv_n = 1.0 / B_HEAD_DIM
    for p in range(NPAIR):
        y = y0_ref[p] + y1_ref[p]
        mean = _dot_hp(y, gsum) * inv_n
        yc = y - mean
        var = _dot_hp(yc * yc, gsum) * inv_n
        yn = yc * lax.rsqrt(var + B_GN_EPS) * lnw_ref[p] + lnb_ref[p]
        bonus = _dot_hp(r_ref[p] * k_ref[p] * rk_ref[p], gsum) * v_ref[p]
        o_ref[:, p * LANES:(p + 1) * LANES] = (yn + bonus) * g_ref[p]


def _rwkv_post(y0, y1, r, k, v, g, r_k, ln_w, ln_b, tt=256):
    B, _, T, _ = r.shape
    tt = min(tt, T)
    nt = T // tt
    seq = pl.BlockSpec((None, NPAIR, tt, LANES), lambda i: (i // nt, 0, i % nt, 0))
    par = pl.BlockSpec((NPAIR, 1, LANES), lambda i: (0, 0, 0))
    return pl.pallas_call(
        _rwkv_post_kernel, grid=(B * nt,),
        in_specs=[seq] * 6 + [par] * 3,
        out_specs=pl.BlockSpec((tt, B_W), lambda i: (i, 0)),
        out_shape=jax.ShapeDtypeStruct((B * T, B_W), F32),
        compiler_params=_cparams(("parallel",)))(
            y0, y1, r, k, v, g, r_k.reshape(NPAIR, 1, LANES), ln_w.reshape(NPAIR, 1, LANES),
            ln_b.reshape(NPAIR, 1, LANES))


def _rwkv_mixer(pb, B, T, prm, s0):
    r, k, v, kk, g, lw0, lw1, a0, a1 = _rwkv_prep(pb, B, T, prm['mu'], prm['w0'], prm['w2'], prm['a0'],
                                                   prm['a2'], prm['g2'], prm['k_k'])
    if s0 is None:
        s0 = jnp.zeros((B, 2, B_HEADS, B_HEAD_DIM, B_HEAD_DIM), F32)
    y0, sf0 = _rwkv_scan(r, v, kk, k, lw0, a0, prm['k_a'], s0[:, 0], rev=False)
    y1, sf1 = _rwkv_scan(r, v, kk, k, lw1, a1, prm['k_a'], s0[:, 1], rev=True)
    out = _rwkv_post(y0, y1, r, k, v, g, prm['r_k'], prm['ln_w'], prm['ln_b'])
    return out, jnp.stack([sf0, sf1], axis=1)


def _gdn_prep_kernel(x_ref, prev_ref, next_ref, cw_ref, o_ref, *, tt, T):
    c = pl.program_id(1)
    x = x_ref[...]
    xp, xn = _shifted(x, prev_ref, next_ref, tt, T)
    y = cw_ref[0:1, :] * xp + cw_ref[1:2, :] * x + cw_ref[2:3, :] * xn
    y = y * _sigmoid(y)

    @pl.when(c == 2)
    def _():
        o_ref[...] = y

    @pl.when(c < 2)
    def _():
        scale = jnp.where(c == 0, C_HEAD_DIM ** -0.5, 1.0).astype(F32)
        for h in range(C_HEADS):
            sl = slice(h * C_HEAD_DIM, (h + 1) * C_HEAD_DIM)
            yh = y[:, sl]
            o_ref[:, sl] = yh * lax.rsqrt(jnp.sum(yh * yh, axis=-1, keepdims=True) + 1e-6) * scale


def _gdn_prep(pc, T, conv_w, tt=256):
    M = pc.shape[0]
    tt = min(tt, T)
    in_specs = _halo_specs(tt, C_W, M, lambda i, c: c) + [pl.BlockSpec((3, C_W), lambda i, c: (0, c))]
    return pl.pallas_call(
        functools.partial(_gdn_prep_kernel, tt=tt, T=T),
        grid=(M // tt, 3), in_specs=in_specs,
        out_specs=pl.BlockSpec((tt, C_W), lambda i, c: (i, c)),
        out_shape=jax.ShapeDtypeStruct((M, 3 * C_W), F32),
        compiler_params=_cparams(("parallel", "arbitrary")))(pc, pc, pc, conv_w)


def _gdn_bg_kernel(x_ref, alog_ref, dtb_ref, o_ref):
    x = x_ref[...]
    lane = lax.broadcasted_iota(jnp.int32, x.shape, 1)
    beta = _sigmoid(x)
    g = -jnp.exp(alog_ref[...]) * _softplus(x + dtb_ref[...])
    o_ref[...] = jnp.where(lane < 2 * C_HEADS, beta, g)


def _gdn_bg(pc, a_log, dt_bias, tt=512):
    M = pc.shape[0]
    tt = min(tt, M)
    nb = 2 * C_HEADS
    pad = lambda p: jnp.concatenate([jnp.zeros((nb,), F32), p.reshape(nb), jnp.zeros((LANES - 2 * nb,), F32)]).reshape(1, LANES)
    return pl.pallas_call(
        _gdn_bg_kernel, grid=(M // tt,),
        in_specs=[pl.BlockSpec((tt, LANES), lambda i: (i, 4 * C_W // LANES)),
                  pl.BlockSpec((1, LANES), lambda i: (0, 0)), pl.BlockSpec((1, LANES), lambda i: (0, 0))],
        out_specs=pl.BlockSpec((tt, LANES), lambda i: (i, 0)),
        out_shape=jax.ShapeDtypeStruct((M, LANES), F32),
        compiler_params=_cparams(("parallel",)))(pc, pad(a_log), pad(dt_bias))


def _gdn_chunk(qs, ks, vs, bcol, gcol, grow, states, masks, L, rev):
    incl, strict, eye = masks
    incl_f = incl.astype(F32)
    eye_f = eye.astype(F32)
    nh = len(qs)
    i = lax.broadcasted_iota(jnp.int32, (L, L), 0)
    j = lax.broadcasted_iota(jnp.int32, (L, L), 1)
    cum_t = ((i >= j) if rev else (i <= j)).astype(F32)
    gc_cols = _dot_hp(incl_f, gcol)
    gc_rows = _dot_hp(grow, cum_t)
    last = 0 if rev else L - 1
    gcs, decays, kbs, rhss, kends, g_ends = [], [], [], [], [], []
    for h in range(nh):
        gc = jnp.broadcast_to(gc_cols[:, h:h + 1], (L, LANES))
        decays.append(jnp.exp(jnp.where(incl, gc[:, :L] - gc_rows[h:h + 1, :], -1e30)))
        beta = bcol[:, h:h + 1]
        kb = ks[h] * beta
        kbs.append(kb)
        rhss.append(jnp.concatenate([vs[h] * beta, kb * jnp.exp(gc)], axis=1))
        g_end = gc[last:last + 1, :]
        kends.append(ks[h] * jnp.exp(g_end - gc))
        g_ends.append(jnp.exp(g_end))
        gcs.append(jnp.exp(gc))
    kk = [_dot_hp(kb, k, NT) for kb, k in zip(kbs, ks)]
    qk = [_dot_hp(q, k, NT) for q, k in zip(qs, ks)]
    qs_s = [_dot_hp(q * e, s) for q, e, s in zip(qs, gcs, states)]
    tinv = _inv_unit_tri([jnp.where(strict, x * d, 0.0) for x, d in zip(kk, decays)], eye_f, L, _dot_hp)
    sol = [_dot_hp(t, r) for t, r in zip(tinv, rhss)]
    v_new = [x[:, :C_HEAD_DIM] - _dot_hp(x[:, C_HEAD_DIM:], s) for x, s in zip(sol, states)]
    o = [a + _dot_hp(jnp.where(incl, x * d, 0.0), vn) for a, x, d, vn in zip(qs_s, qk, decays, v_new)]
    s_new = [s * ge + _dot_hp(ke, vn, TN) for s, ge, ke, vn in zip(states, g_ends, kends, v_new)]
    return o, s_new


def _gdn_scan_kernel(q_ref, k_ref, v_ref, bcol_ref, gcol_ref, grow_ref, s0_ref, o_ref, sf_ref, s_ref, *, L, rev):
    c = pl.program_id(1)

    @pl.when(c == 0)
    def _():
        s_ref[...] = s0_ref[...]

    masks = _tri_masks(L, rev)
    heads = range(C_HEADS)
    hs = [slice(h * C_HEAD_DIM, (h + 1) * C_HEAD_DIM) for h in heads]
    o, s_new = _gdn_chunk([q_ref[:, sl] for sl in hs], [k_ref[:, sl] for sl in hs], [v_ref[:, sl] for sl in hs],
                          bcol_ref[...], gcol_ref[...], grow_ref[...], [s_ref[h] for h in heads], masks, L, rev)
    for h in heads:
        o_ref[:, hs[h]] = o[h]
        s_ref[h] = s_new[h]

    @pl.when(c == pl.num_programs(1) - 1)
    def _():
        sf_ref[...] = s_ref[...]


def _gdn_scan(qkv, bcol, gcol, grow, s0, B, T, rev, L=CHUNK):
    nc = T // L
    cidx = (lambda c: nc - 1 - c) if rev else (lambda c: c)
    seq = lambda col: pl.BlockSpec((L, C_W), lambda b, c: (b * nc + cidx(c), col))
    small = pl.BlockSpec((L, C_HEADS), lambda b, c: (b * nc + cidx(c), 0))
    st = pl.BlockSpec((None, C_HEADS, C_HEAD_DIM, C_HEAD_DIM), lambda b, c: (b, 0, 0, 0))
    return pl.pallas_call(
        functools.partial(_gdn_scan_kernel, L=L, rev=rev),
        grid=(B, nc),
        in_specs=[seq(0), seq(1), seq(2), small, small,
                  pl.BlockSpec((None, C_HEADS, L), lambda b, c: (b * nc + cidx(c), 0, 0)), st],
        out_specs=[pl.BlockSpec((L, C_W), lambda b, c: (b * nc + cidx(c), 0)), st],
        out_shape=[jax.ShapeDtypeStruct((B * T, C_W), F32),
                   jax.ShapeDtypeStruct((B, C_HEADS, C_HEAD_DIM, C_HEAD_DIM), F32)],
        scratch_shapes=[pltpu.VMEM((C_HEADS, C_HEAD_DIM, C_HEAD_DIM), F32)],
        compiler_params=_cparams(("parallel", "arbitrary")))(qkv, qkv, qkv, bcol, gcol, grow, s0)


def _gdn_post_kernel(o0_ref, o1_ref, z_ref, g_ref, out_ref):
    for h in range(C_HEADS):
        sl = slice(h * C_HEAD_DIM, (h + 1) * C_HEAD_DIM)
        y = o0_ref[:, sl] + o1_ref[:, sl]
        y = y * lax.rsqrt(jnp.mean(y * y, axis=-1, keepdims=True) + NORM_EPS) * g_ref[...]
        z = z_ref[:, sl]
        out_ref[:, sl] = y * (z * _sigmoid(z))


def _gdn_post(o0, o1, pc, norm_g, tt=256):
    M = o0.shape[0]
    tt = min(tt, M)
    return pl.pallas_call(
        _gdn_post_kernel, grid=(M // tt,),
        in_specs=[pl.BlockSpec((tt, C_W), lambda i: (i, 0)), pl.BlockSpec((tt, C_W), lambda i: (i, 0)),
                  pl.BlockSpec((tt, C_W), lambda i: (i, 3)), pl.BlockSpec((1, C_HEAD_DIM), lambda i: (0, 0))],
        out_specs=pl.BlockSpec((tt, C_W), lambda i: (i, 0)),
        out_shape=jax.ShapeDtypeStruct((M, C_W), F32),
        compiler_params=_cparams(("parallel",)))(o0, o1, pc, norm_g.reshape(1, C_HEAD_DIM))


def _gdn_mixer(pc, B, T, prm, s0, L=CHUNK):
    M = pc.shape[0]
    nc = T // L
    qkv = _gdn_prep(pc, T, prm['conv'])
    bg = _gdn_bg(pc, prm['a_log'], prm['dt_bias'])
    if s0 is None:
        s0 = jnp.zeros((B, 2, C_HEADS, C_HEAD_DIM, C_HEAD_DIM), F32)
    outs, finals = [], []
    for d in range(2):
        bcol = bg[:, d * C_HEADS:(d + 1) * C_HEADS]
        gcol = bg[:, 2 * C_HEADS + d * C_HEADS:2 * C_HEADS + (d + 1) * C_HEADS]
        grow = gcol.reshape(B * nc, L, C_HEADS).transpose(0, 2, 1)
        o, sf = _gdn_scan(qkv, bcol, gcol, grow, s0[:, d], B, T, rev=(d == 1))
        outs.append(o)
        finals.append(sf)
    y = _gdn_post(outs[0], outs[1], pc, prm['norm_g'])
    return y, jnp.stack(finals, axis=1)


def _top_values(tile, count, need_mask):
    rows = lax.broadcasted_iota(jnp.int32, tile.shape, 0).astype(F32)
    big = float(tile.shape[0])
    cur = tile
    vals = []
    for _ in range(count):
        m = jnp.max(cur, axis=0, keepdims=True)
        vals.append(m)
        idx = jnp.min(jnp.where(cur == m, rows, big), axis=0, keepdims=True)
        cur = jnp.where(rows == idx, -jnp.inf, cur)
    return vals, (cur == -jnp.inf) if need_mask else None


PEER_EB = 1024
PEER_GRP = MXU_K // PEER_HALF
PEER_NGRP = 2 * PEER_HEADS // PEER_GRP


def _peer_topk_kernel(q_ref, kh_ref, kl_ref, thr_ref, e0_ref, e1_ref, s1_ref, sc_ref):
    rows_g = PEER_GRP * N_KEYS
    for g in range(PEER_NGRP):
        qh, ql = _split(q_ref[:, g * MXU_K:(g + 1) * MXU_K])
        kh = kh_ref[g * rows_g:(g + 1) * rows_g, :]
        kl = kl_ref[g * rows_g:(g + 1) * rows_g, :]
        sc_ref[g * rows_g:(g + 1) * rows_g, :] = _dg(kh, qh, NT) + (_dg(kh, ql, NT) + _dg(kl, qh, NT))

    def head(h, carry):
        t0 = sc_ref[pl.ds(pl.multiple_of(h * 2 * N_KEYS, N_KEYS), N_KEYS), :]
        t1 = sc_ref[pl.ds(pl.multiple_of(h * 2 * N_KEYS + N_KEYS, N_KEYS), N_KEYS), :]
        a, ra = _top_values(t0, PEER_TOPK, True)
        b, rb = _top_values(t1, PEER_TOPK, True)
        s0 = jnp.where(ra, t0, -jnp.inf)
        s1 = jnp.where(rb, t1, -jnp.inf)
        amat = jnp.concatenate(a, axis=0)
        bmat = jnp.concatenate(b, axis=0)
        b_lo = bmat[:8]
        row8 = lax.broadcasted_iota(jnp.int32, b_lo.shape, 0)
        pieces = [a[0] + b_lo, a[0] + bmat[8:], a[1] + b_lo]
        for r in range(2, 8):
            pieces.append(jnp.where(row8 < PEER_TOPK // (r + 1), a[r] + b_lo, -jnp.inf))
        pieces.append(amat[8:] + b[0])
        cand = jnp.concatenate(pieces, axis=0)
        best, _ = _top_values(cand, PEER_TOPK + 1, False)
        z = jnp.zeros_like(best[0])
        for n in range(PEER_TOPK):
            z = z + jnp.exp(best[n] - best[0])
        tau = 0.5 * (best[PEER_TOPK - 1] + best[PEER_TOPK])
        thr = tau - s0
        e0 = jnp.exp(s0 - a[0]) * (1.0 / z)
        e1 = jnp.exp(s1 - b[0])
        for tc in range(t0.shape[1] // LANES):
            cols = slice(tc * LANES, (tc + 1) * LANES)
            thr_ref[h, tc] = thr[:, cols]
            e0_ref[h, tc] = e0[:, cols]
            e1_ref[h, tc] = e1[:, cols]
            s1_ref[h, tc] = s1[:, cols]
        return carry

    lax.fori_loop(0, PEER_HEADS, head, 0)


def _peer_topk(q, kdt_hi, kdt_lo, tt=256):
    M = q.shape[0]
    tt = min(tt, M)
    R = PEER_HEADS * 2 * N_KEYS
    big = pl.BlockSpec((PEER_HEADS, tt // LANES, N_KEYS, LANES), lambda i: (0, i, 0, 0))
    return pl.pallas_call(
        _peer_topk_kernel, grid=(M // tt,),
        in_specs=[pl.BlockSpec((tt, PEER_HEADS * PEER_QDIM), lambda i: (i, 0)),
                  pl.BlockSpec((R, MXU_K), lambda i: (0, 0)),
                  pl.BlockSpec((R, MXU_K), lambda i: (0, 0))],
        out_specs=[big] * 4,
        out_shape=[jax.ShapeDtypeStruct((PEER_HEADS, M // LANES, N_KEYS, LANES), F32)] * 4,
        scratch_shapes=[pltpu.VMEM((R, tt), F32)],
        compiler_params=_cparams(("parallel",)))(q, kdt_hi, kdt_lo)


def _peer_expert_kernel(xt_ref, u_ref, vt_ref, thr_ref, e0_ref, e1_ref, s1_ref, res_ref, m_ref, o_ref,
                        acc_ref, act0_ref, act1_ref, am0_ref, am1_ref, *, ni, gate_slot):
    e = pl.program_id(1)
    n_blocks = pl.num_programs(1) - 2
    tt = xt_ref.shape[1]

    @pl.when(e == 0)
    def _():
        acc_ref[...] = jnp.zeros_like(acc_ref)
        act0_ref[...] = jnp.zeros_like(act0_ref)
        act1_ref[...] = jnp.zeros_like(act1_ref)
        am0_ref[...] = jnp.zeros_like(am0_ref)
        am1_ref[...] = jnp.zeros_like(am1_ref)

    def stages(act_w, act_r, am_w, am_r):
        act_w[...] = _dg(u_ref[...], xt_ref[...])
        acc_ref[...] += _dg(vt_ref[...], am_r[...])
        first = jnp.clip(e - 1, 0, n_blocks - 1) * ni
        for ii in range(ni):
            i = first + ii
            rows = slice(ii * N_KEYS, (ii + 1) * N_KEYS)
            for tc in range(tt // LANES):
                cols = slice(tc * LANES, (tc + 1) * LANES)
                w = None
                for h in range(PEER_HEADS):
                    keep = s1_ref[h, tc] >= thr_ref[h, tc, pl.ds(i, 1), :]
                    contrib = jnp.where(keep, e1_ref[h, tc], 0.0) * e0_ref[h, tc, pl.ds(i, 1), :]
                    w = contrib if w is None else w + contrib
                a = act_r[rows, cols]
                a = 0.5 * a * (1.0 + lax.erf(a * (2.0 ** -0.5)))
                am_w[rows, cols] = (a * w).astype(BF16)

    @pl.when(e % 2 == 0)
    def _():
        stages(act0_ref, act1_ref, am1_ref, am0_ref)

    @pl.when(e % 2 == 1)
    def _():
        stages(act1_ref, act0_ref, am0_ref, am1_ref)

    @pl.when(e == pl.num_programs(1) - 1)
    def _():
        o_ref[...] = res_ref[...] + m_ref[gate_slot:gate_slot + 1, :] * acc_ref[...].T


def _peer_experts(xn, u_bf, vt_bf, thr, e0, e1, s1m, res, mods, gate_slot, rows_per_mod, tt=512):
    M, D = res.shape
    tt = min(tt, M, rows_per_mod) if mods.shape[0] > 1 else min(tt, M)
    nb, _, eb = vt_bf.shape
    ni = eb // N_KEYS
    once = pl.Buffered(1)
    if mods.shape[0] > 1:
        mspec = pl.BlockSpec((None, 6, D), lambda i, e: ((i * tt) // rows_per_mod, 0, 0), pipeline_mode=once)
    else:
        mspec = pl.BlockSpec((None, 6, D), lambda i, e: (0, 0, 0), pipeline_mode=once)
    big = pl.BlockSpec((PEER_HEADS, tt // LANES, N_KEYS, LANES), lambda i, e: (0, i, 0, 0), pipeline_mode=once)
    return pl.pallas_call(
        functools.partial(_peer_expert_kernel, ni=ni, gate_slot=gate_slot),
        grid=(M // tt, nb + 2),
        in_specs=[pl.BlockSpec((D, tt), lambda i, e: (0, i), pipeline_mode=once),
                  pl.BlockSpec((eb, D), lambda i, e: (jnp.minimum(e, nb - 1), 0)),
                  pl.BlockSpec((None, D, eb), lambda i, e: (jnp.clip(e - 2, 0, nb - 1), 0, 0)),
                  big, big, big, big,
                  pl.BlockSpec((tt, D), lambda i, e: (i, 0), pipeline_mode=once), mspec],
        out_specs=pl.BlockSpec((tt, D), lambda i, e: (i, 0)),
        out_shape=jax.ShapeDtypeStruct((M, D), F32),
        scratch_shapes=[pltpu.VMEM((D, tt), F32), pltpu.VMEM((eb, tt), F32), pltpu.VMEM((eb, tt), F32),
                        pltpu.VMEM((eb, tt), BF16), pltpu.VMEM((eb, tt), BF16)],
        compiler_params=_cparams(("parallel", "arbitrary"), VMEM_LIMIT_BIG))(
            xn, u_bf, vt_bf, thr, e0, e1, s1m, res, mods)


def _peer(x, mods, rows_per_mod, norm_g, prm):
    q, xn = _mm_norm(x, prm['wq'], norm_g=norm_g, mods=mods, slots=(3, 4), rows_per_mod=rows_per_mod,
                     hp=True, emit_xn=BF16)
    thr, e0, e1, s1m = _peer_topk(q, prm['kdt_hi'], prm['kdt_lo'])
    return _peer_experts(xn.T, prm['u'], prm['vt'], thr, e0, e1, s1m, x, mods, 5, rows_per_mod)


def _peer_params(wq, keys, u_tab, v_tab):
    nset = 2 * PEER_HEADS
    kd = jnp.zeros((nset, N_KEYS, PEER_GRP, PEER_HALF), F32)
    sets = jnp.arange(nset)
    kd = kd.at[sets, :, sets % PEER_GRP, :].set(keys.reshape(nset, N_KEYS, PEER_HALF))
    kd_hi, kd_lo = _split(kd.reshape(nset * N_KEYS, MXU_K))
    vt = v_tab.astype(BF16).reshape(N_EXPERTS // PEER_EB, PEER_EB, -1).transpose(0, 2, 1)
    return dict(wq=_split(wq), kdt_hi=kd_hi, kdt_lo=kd_lo, u=u_tab.astype(BF16), vt=vt)


def _layer_even(x, mods, B, T, norm1, w_a, w_b, w_out, qk_gains, rwkv_prm, ctx):
    rpm = T
    pa = _mm_norm(x, w_a, norm_g=norm1, mods=mods, slots=(0, 1), rows_per_mod=rpm)
    pb = _mm_norm(x, w_b, norm_g=norm1, mods=mods, slots=(0, 1), rows_per_mod=rpm)
    nqk = A_HEADS + A_KV_HEADS
    grp = A_HEADS // A_KV_HEADS
    scale = A_HEAD_DIM ** -0.5
    if ctx is None:
        qk = _norm_rope(pa, nqk, T, gains=qk_gains)
        oa = _attention(B, T, T, A_HEADS, scale, (qk, lambda h: h), (qk, lambda h: A_HEADS + h // grp),
                        (pa, lambda h: nqk + h // grp))
        ctx_out = (qk[:, A_Q:A_Q + A_KV], pa[:, A_Q + A_KV:])
        s0 = None
    else:
        cache_k, cache_v, s0 = ctx
        P = cache_k.shape[1]
        qk = _norm_rope(pa, nqk, T, gains=qk_gains, tables=_rope_tables(T, A_HEAD_DIM))
        k_all = jnp.concatenate([qk[:, A_Q:].reshape(B, T, A_KV), cache_k.reshape(B, P, A_KV)], axis=1)
        v_all = jnp.concatenate([pa[:, A_Q + A_KV:].reshape(B, T, A_KV), cache_v.reshape(B, P, A_KV)], axis=1)
        Tk = T + P
        oa = _attention(B, T, Tk, A_HEADS, scale, (qk, lambda h: h),
                        (k_all.reshape(B * Tk, A_KV).astype(BF16), lambda h: h // grp),
                        (v_all.reshape(B * Tk, A_KV).astype(BF16), lambda h: h // grp))
        ctx_out = None
    ob, s_fin = _rwkv_mixer(pb, B, T, rwkv_prm, s0)
    x = _mm_res([oa, ob], w_out, x, mods, 2, rpm)
    return x, ctx_out, s_fin


D_QROPE_BLK = 0
D_KROPE_BLK = D_HEADS
D_NROPE_BLK = D_HEADS + 1
D_CKV_BLK = 12
D_QNOPE_BLK = 16
D_PROJ = 24 * LANES


def _mla_in_weight(w_d):
    K = w_d.shape[0]
    wq = w_d[:, :D_Q].reshape(K, D_HEADS, D_QK)
    z64 = jnp.zeros((K, D_HEADS, LANES - D_ROPE), w_d.dtype)
    q_rope = jnp.concatenate([wq[:, :, D_NOPE:], z64], axis=-1).reshape(K, D_HEADS * LANES)
    q_nope = wq[:, :, :D_NOPE].reshape(K, D_HEADS * LANES)
    k_rope = jnp.concatenate([w_d[:, D_Q + D_KV_RANK:], jnp.zeros((K, LANES - D_ROPE), w_d.dtype)], axis=-1)
    padz = jnp.zeros((K, (D_CKV_BLK - D_NROPE_BLK) * LANES), w_d.dtype)
    return jnp.concatenate([q_rope, k_rope, padz, w_d[:, D_Q:D_Q + D_KV_RANK], q_nope], axis=-1)


def _layer_odd(x, mods, B, T, norm1, w_c, w_d, w_out, gdn_prm, kv_norm_g, w_kv_b, ctx):
    rpm = T
    pc = _mm_norm(x, w_c, norm_g=norm1, mods=mods, slots=(0, 1), rows_per_mod=rpm)
    pd = _mm_norm(x, w_d, norm_g=norm1, mods=mods, slots=(0, 1), rows_per_mod=rpm)
    scale = D_QK ** -0.5
    ckv_col = D_CKV_BLK * LANES // D_KV_RANK
    kv, ckv_n = _mm_norm(pd, w_kv_b, k=D_KV_RANK, xcol=ckv_col, norm_g=kv_norm_g, emit_xn=F32)
    kcol = lambda h: 2 * h
    vcol = lambda h: 2 * h + 1
    if ctx is None:
        od = _attention(B, T, T, D_HEADS, scale, (pd, lambda h: D_QNOPE_BLK + h), (kv, kcol), (kv, vcol),
                        q2=(pd, lambda h: D_QROPE_BLK + h), k2=(pd, lambda h: D_KROPE_BLK))
        ctx_out = (ckv_n, pd[:, D_KROPE_BLK * LANES:D_KROPE_BLK * LANES + D_ROPE])
        s0 = None
    else:
        s0, c_ckv, c_krope = ctx
        P = c_ckv.shape[1]
        Tk = T + P
        roped = _norm_rope(pd, D_NROPE_BLK, T, tables=_rope_tables(T, D_ROPE))
        kv_ctx = _mm_norm(c_ckv.reshape(B * P, D_KV_RANK), w_kv_b)
        kv_all = jnp.concatenate([kv.reshape(B, T, -1), kv_ctx.reshape(B, P, -1)], axis=1).reshape(B * Tk, -1)
        kv_all = kv_all.astype(BF16)
        kr_lat = roped[:, D_KROPE_BLK * LANES:].reshape(B, T, LANES)
        kr_ctx = jnp.concatenate([c_krope, jnp.zeros((B, P, LANES - D_ROPE), F32)], axis=-1)
        kr_all = jnp.concatenate([kr_lat, kr_ctx], axis=1).reshape(B * Tk, LANES).astype(BF16)
        od = _attention(B, T, Tk, D_HEADS, scale, (pd, lambda h: D_QNOPE_BLK + h), (kv_all, kcol), (kv_all, vcol),
                        q2=(roped, lambda h: D_QROPE_BLK + h), k2=(kr_all, lambda h: 0))
        ctx_out = None
    oc, s_fin = _gdn_mixer(pc, B, T, gdn_prm, s0)
    x = _mm_res([oc, od], w_out, x, mods, 2, rpm)
    return x, ctx_out, s_fin


def _modulation(cond, w_ada, b_ada):
    n = cond.shape[0]
    rows = ((n + 7) // 8) * 8
    a = cond * _sigmoid(cond)
    a = jnp.concatenate([a, jnp.zeros((rows - n, cond.shape[1]), F32)], axis=0)
    m = _mm_norm(a, _split(w_ada), hp=True, tn_cap=1024)[:n] + b_ada
    return m.reshape(n, 6, D_MODEL)


def kernel(x_prompt, x_sample, cache_attn_k, cache_attn_v, state_rwkv, state_gdn, cache_mla_ckv, cache_mla_krope,
           c, c_ctx, norm1_g, norm2_g, ada_w, ada_b, w_in_ab, w_out_ab, attn_q_norm, attn_k_norm,
           rwkv_mu, rwkv_w0, rwkv_w2, rwkv_a0, rwkv_a2, rwkv_g2, rwkv_k_k, rwkv_k_a, rwkv_r_k, rwkv_ln_w, rwkv_ln_b,
           w_in_cd, w_out_cd, gdn_conv, gdn_a_log, gdn_dt_bias, gdn_norm_g, mla_kv_norm_g, mla_w_kv_b,
           peer_wq, peer_keys, peer_u, peer_v, final_norm_g):
    Bp, Tp, D = x_prompt.shape
    Bs, Ts, _ = x_sample.shape
    depth = norm1_g.shape[0]
    xp = x_prompt.reshape(Bp * Tp, D)
    xs = x_sample.reshape(Bs * Ts, D)
    cond = jnp.concatenate([c_ctx[None, :], c], axis=0)
    outs = dict(k=[], v=[], rwkv=[], gdn=[], ckv=[], krope=[])
    for li in range(depth):
        j = li // 2
        mods = _modulation(cond, ada_w[li], ada_b[li])
        mods_p, mods_s = mods[:1], mods[1:]
        peer_prm = _peer_params(peer_wq[li], peer_keys[li], peer_u[li], peer_v[li])
        if li % 2 == 0:
            w_in = w_in_ab[j].astype(BF16)
            w_a, w_b = w_in[:, :A_IN], w_in[:, A_IN:]
            w_o = w_out_ab[j].astype(BF16)
            w_out = [w_o[:A_Q], w_o[A_Q:]]
            gains = jnp.concatenate([jnp.tile(attn_q_norm[j], A_HEADS), jnp.tile(attn_k_norm[j], A_KV_HEADS)])
            rp = dict(mu=rwkv_mu[j], w0=rwkv_w0[j], w2=rwkv_w2[j], a0=rwkv_a0[j], a2=rwkv_a2[j], g2=rwkv_g2[j],
                      k_k=rwkv_k_k[j], k_a=rwkv_k_a[j], r_k=rwkv_r_k[j], ln_w=rwkv_ln_w[j], ln_b=rwkv_ln_b[j])
            xp, (k_c, v_c), s_c = _layer_even(xp, mods_p, Bp, Tp, norm1_g[li], w_a, w_b, w_out, gains, rp, None)
            xs, _, _ = _layer_even(xs, mods_s, Bs, Ts, norm1_g[li], w_a, w_b, w_out, gains, rp,
                                   (cache_attn_k[:, j], cache_attn_v[:, j], state_rwkv[:, j]))
            outs['k'].append(k_c.reshape(Bp, Tp, A_KV_HEADS, A_HEAD_DIM))
            outs['v'].append(v_c.reshape(Bp, Tp, A_KV_HEADS, A_HEAD_DIM))
            outs['rwkv'].append(s_c)
        else:
            w_in = w_in_cd[j]
            w_c = jnp.concatenate([w_in[:, :C_IN], jnp.zeros((D, LANES - 4 * C_HEADS), F32)], axis=-1).astype(BF16)
            w_d = _mla_in_weight(w_in[:, C_IN:]).astype(BF16)
            w_o = w_out_cd[j].astype(BF16)
            w_out = [w_o[:C_W], w_o[C_W:]]
            gp = dict(conv=gdn_conv[j], a_log=gdn_a_log[j], dt_bias=gdn_dt_bias[j], norm_g=gdn_norm_g[j])
            w_kv_b = mla_w_kv_b[j].astype(BF16)
            xp, (ckv_c, kr_c), s_c = _layer_odd(xp, mods_p, Bp, Tp, norm1_g[li], w_c, w_d, w_out, gp,
                                                mla_kv_norm_g[j], w_kv_b, None)
            xs, _, _ = _layer_odd(xs, mods_s, Bs, Ts, norm1_g[li], w_c, w_d, w_out, gp, mla_kv_norm_g[j], w_kv_b,
                                  (state_gdn[:, j], cache_mla_ckv[:, j], cache_mla_krope[:, j]))
            outs['gdn'].append(s_c)
            outs['ckv'].append(ckv_c.reshape(Bp, Tp, D_KV_RANK))
            outs['krope'].append(kr_c.reshape(Bp, Tp, D_ROPE))
        xp = _peer(xp, mods_p, Tp, norm2_g[li], peer_prm)
        xs = _peer(xs, mods_s, Ts, norm2_g[li], peer_prm)
    y_prompt = _rmsnorm(xp, final_norm_g).reshape(Bp, Tp, D)
    y_sample = _rmsnorm(xs, final_norm_g).reshape(Bs, Ts, D)
    st = lambda name: jnp.stack(outs[name], axis=1)
    return (y_prompt, y_sample, st('k'), st('v'), st('rwkv'), st('gdn'), st('ckv'), st('krope'))
```

```python
import functools
import math

import numpy as np
import jax
import jax.numpy as jnp
from jax import lax
from jax.experimental import pallas as pl
from jax.experimental.pallas import tpu as pltpu

F32 = jnp.float32
BF16 = jnp.bfloat16

D_MODEL = 2048
GRID_W = 64
ROPE_THETA = 10000.0
NORM_EPS = 1e-6
A_HEADS, A_KV_HEADS, A_HEAD_DIM = 8, 2, 128
A_Q = A_HEADS * A_HEAD_DIM
A_KV = A_KV_HEADS * A_HEAD_DIM
A_IN = A_Q + 2 * A_KV
B_HEADS, B_HEAD_DIM = 16, 64
B_W = B_HEADS * B_HEAD_DIM
B_DECAY_LORA, B_AAA_LORA, B_GATE_LORA = 64, 64, 128
B_IN = 3 * B_W + 2 * B_DECAY_LORA + 2 * B_AAA_LORA + B_GATE_LORA
B_GN_EPS = 64e-5
C_HEADS, C_HEAD_DIM = 8, 128
C_W = C_HEADS * C_HEAD_DIM
C_IN = 4 * C_W + 4 * C_HEADS
D_HEADS, D_NOPE, D_ROPE, D_V = 8, 128, 64, 128
D_QK = D_NOPE + D_ROPE
D_KV_RANK = 512
D_Q = D_HEADS * D_QK
PEER_HEADS, N_KEYS, PEER_QDIM, PEER_TOPK = 8, 128, 128, 16
PEER_HALF = PEER_QDIM // 2
N_EXPERTS = N_KEYS * N_KEYS

LANES = 128
MXU_K = 256
VMEM_LIMIT = 48 * 1024 * 1024
VMEM_LIMIT_BIG = 56 * 1024 * 1024

CHUNK = 64

NN = ((1,), (0,))
NT = ((1,), (1,))
TN = ((0,), (0,))


def _dg(a, b, dims=NN):
    return lax.dot_general(a, b, (dims, ((), ())), preferred_element_type=F32)


def _split(x):
    hi = x.astype(BF16)
    lo = (x - hi.astype(F32)).astype(BF16)
    return hi, lo


def _dot_hp(a, b, dims=NN):
    ah, al = _split(a)
    bh, bl = _split(b)
    return _dg(ah, bh, dims) + (_dg(ah, bl, dims) + _dg(al, bh, dims))


def _dot_bf(a, b, dims=NN):
    return _dg(a.astype(BF16), b.astype(BF16), dims)


def _sigmoid(x):
    return 1.0 / (1.0 + jnp.exp(-x))


def _softplus(x):
    return jnp.maximum(x, 0.0) + jnp.log(1.0 + jnp.exp(-jnp.abs(x)))


def _cparams(sem, vmem=VMEM_LIMIT):
    return pltpu.CompilerParams(dimension_semantics=sem, vmem_limit_bytes=vmem)


def _pick_tile(n, cap):
    best = None
    t = LANES
    while t <= min(n, cap):
        if n % t == 0:
            best = t
        t += LANES
    return best if best is not None else n


def _mm_norm_kernel(*refs, has_norm, has_mod, shift_slot, scale_slot, hp, emit_xn):
    it = iter(refs)
    x_ref = next(it)
    g_ref = next(it) if has_norm else None
    m_ref = next(it) if has_mod else None
    wh_ref = next(it)
    wl_ref = next(it) if hp else None
    o_ref = next(it)
    xn_ref = next(it) if emit_xn else None
    xh_ref = next(it)
    xl_ref = next(it) if hp else None

    @pl.when(pl.program_id(1) == 0)
    def _():
        xv = x_ref[...].astype(F32)
        if has_norm:
            xv = xv * lax.rsqrt(jnp.mean(xv * xv, axis=-1, keepdims=True) + NORM_EPS) * g_ref[...]
        if has_mod:
            xv = xv * (1.0 + m_ref[scale_slot:scale_slot + 1, :]) + m_ref[shift_slot:shift_slot + 1, :]
        hi = xv.astype(BF16)
        xh_ref[...] = hi
        if hp:
            xl_ref[...] = (xv - hi.astype(F32)).astype(BF16)
        if emit_xn:
            xn_ref[...] = xv.astype(xn_ref.dtype)

    acc = _dg(xh_ref[...], wh_ref[...])
    if hp:
        acc = acc + (_dg(xh_ref[...], wl_ref[...]) + _dg(xl_ref[...], wh_ref[...]))
    o_ref[...] = acc


def _mm_norm(x, w, *, k=None, xcol=0, norm_g=None, mods=None, slots=(0, 1), rows_per_mod=None,
             hp=False, emit_xn=None, tm=512, tn_cap=1664):
    M = x.shape[0]
    K = k if k is not None else x.shape[1]
    ws = tuple(w) if isinstance(w, (tuple, list)) else (w,)
    N = ws[0].shape[1]
    tm = min(tm, M)
    if mods is not None and mods.shape[0] > 1:
        tm = min(tm, rows_per_mod)
    tn = _pick_tile(N, tn_cap)
    has_norm = norm_g is not None
    has_mod = mods is not None
    rpm = rows_per_mod if rows_per_mod is not None else M

    in_specs = [pl.BlockSpec((tm, K), lambda i, j: (i, xcol))]
    args = [x]
    if has_norm:
        in_specs.append(pl.BlockSpec((1, K), lambda i, j: (0, 0)))
        args.append(norm_g.reshape(1, K).astype(F32))
    if has_mod:
        if mods.shape[0] > 1:
            in_specs.append(pl.BlockSpec((None, 6, K), lambda i, j: ((i * tm) // rpm, 0, 0)))
        else:
            in_specs.append(pl.BlockSpec((None, 6, K), lambda i, j: (0, 0, 0)))
        args.append(mods)
    for wi in ws:
        in_specs.append(pl.BlockSpec((K, tn), lambda i, j: (0, j)))
        args.append(wi)
    out_shape = [jax.ShapeDtypeStruct((M, N), F32)]
    out_specs = [pl.BlockSpec((tm, tn), lambda i, j: (i, j))]
    if emit_xn is not None:
        out_shape.append(jax.ShapeDtypeStruct((M, K), emit_xn))
        out_specs.append(pl.BlockSpec((tm, K), lambda i, j: (i, 0)))
    scratch = [pltpu.VMEM((tm, K), BF16)]
    if hp:
        scratch.append(pltpu.VMEM((tm, K), BF16))
    kern = functools.partial(_mm_norm_kernel, has_norm=has_norm, has_mod=has_mod, shift_slot=slots[0],
                             scale_slot=slots[1], hp=hp, emit_xn=emit_xn is not None)
    outs = pl.pallas_call(
        kern, grid=(M // tm, N // tn), in_specs=in_specs, out_specs=out_specs, out_shape=out_shape,
        scratch_shapes=scratch, compiler_params=_cparams(("parallel", "arbitrary")))(*args)
    return outs if emit_xn is not None else outs[0]


def _mm_res_kernel(*refs, n_in, gate_slot):
    a_refs = refs[:n_in]
    w_refs = refs[n_in:2 * n_in]
    res_ref, m_ref, o_ref = refs[2 * n_in:]
    acc = None
    for a_ref, w_ref in zip(a_refs, w_refs):
        part = _dg(a_ref[...].astype(BF16), w_ref[...])
        acc = part if acc is None else acc + part
    o_ref[...] = res_ref[...] + m_ref[gate_slot:gate_slot + 1, :] * acc


def _mm_res(a_list, w_list, res, mods, gate_slot, rows_per_mod, tm=512, tn=1024):
    M, N = res.shape
    tm = min(tm, M, rows_per_mod) if mods.shape[0] > 1 else min(tm, M)
    tn = min(tn, N)
    n_in = len(a_list)
    in_specs, args = [], []
    for a in a_list:
        in_specs.append(pl.BlockSpec((tm, a.shape[1]), lambda i, j: (i, 0)))
        args.append(a)
    for w in w_list:
        in_specs.append(pl.BlockSpec((w.shape[0], tn), lambda i, j: (0, j)))
        args.append(w)
    in_specs.append(pl.BlockSpec((tm, tn), lambda i, j: (i, j)))
    args.append(res)
    if mods.shape[0] > 1:
        in_specs.append(pl.BlockSpec((None, 6, tn), lambda i, j: ((i * tm) // rows_per_mod, 0, j)))
    else:
        in_specs.append(pl.BlockSpec((None, 6, tn), lambda i, j: (0, 0, j)))
    args.append(mods)
    return pl.pallas_call(
        functools.partial(_mm_res_kernel, n_in=n_in, gate_slot=gate_slot),
        grid=(M // tm, N // tn), in_specs=in_specs,
        out_specs=pl.BlockSpec((tm, tn), lambda i, j: (i, j)),
        out_shape=jax.ShapeDtypeStruct((M, N), F32),
        compiler_params=_cparams(("parallel", "arbitrary")))(*args)


def _rmsnorm_kernel(x_ref, g_ref, o_ref):
    xv = x_ref[...]
    o_ref[...] = xv * lax.rsqrt(jnp.mean(xv * xv, axis=-1, keepdims=True) + NORM_EPS) * g_ref[...]


def _rmsnorm(x, g, tm=512):
    M, K = x.shape
    tm = min(tm, M)
    return pl.pallas_call(
        _rmsnorm_kernel, grid=(M // tm,),
        in_specs=[pl.BlockSpec((tm, K), lambda i: (i, 0)), pl.BlockSpec((1, K), lambda i: (0, 0))],
        out_specs=pl.BlockSpec((tm, K), lambda i: (i, 0)),
        out_shape=jax.ShapeDtypeStruct((M, K), F32),
        compiler_params=_cparams(("parallel",)))(x, g.reshape(1, K))


def _rope_tables(T, rot_dim):
    rows = T // GRID_W
    row = jnp.repeat(jnp.arange(rows, dtype=F32), GRID_W)
    col = jnp.tile(jnp.arange(GRID_W, dtype=F32), rows)
    n_freq = rot_dim // 4
    inv = ROPE_THETA ** (-jnp.arange(n_freq, dtype=F32) / n_freq)
    ang = jnp.concatenate([row[:, None] * inv, col[:, None] * inv], axis=-1)
    cos = jnp.repeat(jnp.cos(ang), 2, axis=-1)
    sin = jnp.repeat(jnp.sin(ang), 2, axis=-1) * jnp.tile(jnp.array([-1.0, 1.0], F32), rot_dim // 2)
    pad = LANES - rot_dim
    if pad:
        cos = jnp.concatenate([cos, jnp.ones((T, pad), F32)], axis=-1)
        sin = jnp.concatenate([sin, jnp.zeros((T, pad), F32)], axis=-1)
    return cos, sin


def _rope(y, cosf, sinf):
    lane = lax.broadcasted_iota(jnp.int32, y.shape, 1)
    even = (lane & 1) == 0
    width = y.shape[1]
    swap = jnp.where(even, pltpu.roll(y, width - 1, 1), pltpu.roll(y, 1, 1))
    return y * cosf + swap * sinf


def _norm_rope_kernel(*refs, nblk, norm, rope):
    it = iter(refs)
    x_ref = next(it)
    g_ref = next(it) if norm else None
    cos_ref = next(it) if rope else None
    sin_ref = next(it) if rope else None
    o_ref = next(it)
    for c in range(nblk):
        sl = slice(c * LANES, (c + 1) * LANES)
        y = x_ref[:, sl]
        if norm:
            y = y * lax.rsqrt(jnp.mean(y * y, axis=-1, keepdims=True) + NORM_EPS) * g_ref[:, sl]
        if rope:
            y = _rope(y, cos_ref[...], sin_ref[...])
        o_ref[:, sl] = y


def _norm_rope(x, nblk, T, gains=None, tables=None, tt=256):
    M = x.shape[0]
    tt = min(tt, T)
    W = nblk * LANES
    in_specs = [pl.BlockSpec((tt, W), lambda i: (i, 0))]
    args = [x]
    if gains is not None:
        in_specs.append(pl.BlockSpec((1, W), lambda i: (0, 0)))
        args.append(gains.reshape(1, W))
    if tables is not None:
        nt = T // tt
        for t in tables:
            in_specs.append(pl.BlockSpec((tt, LANES), lambda i: (i % nt, 0)))
            args.append(t)
    return pl.pallas_call(
        functools.partial(_norm_rope_kernel, nblk=nblk, norm=gains is not None, rope=tables is not None),
        grid=(M // tt,), in_specs=in_specs,
        out_specs=pl.BlockSpec((tt, W), lambda i: (i, 0)),
        out_shape=jax.ShapeDtypeStruct((M, W), F32),
        compiler_params=_cparams(("parallel",)))(*args)


def _attn_kernel(*refs, scale, has2):
    it = iter(refs)
    q_ref = next(it)
    q2_ref = next(it) if has2 else None
    k_ref = next(it)
    k2_ref = next(it) if has2 else None
    v_ref = next(it)
    o_ref = next(it)
    if has2:
        qq = jnp.concatenate([q_ref[...], q2_ref[...]], axis=1) * scale
        s = _dot_bf(qq, jnp.concatenate([k_ref[...], k2_ref[...]], axis=1), NT)
    else:
        s = _dot_bf(q_ref[...] * scale, k_ref[...], NT)
    m = jnp.max(s, axis=-1, keepdims=True)
    p = jnp.exp(s - m)
    l = jnp.sum(p, axis=-1, keepdims=True)
    o_ref[...] = _dot_bf(p, v_ref[...]) / l


def _attention(B, T, Tk, H, scale, q, k, v, q2=None, k2=None, tq=256):
    tq = min(tq, T)
    nq = T // tq
    has2 = q2 is not None

    def qspec(cf):
        return pl.BlockSpec((tq, LANES), lambda b, h, i: (b * nq + i, cf(h)))

    def kspec(cf):
        return pl.BlockSpec((Tk, LANES), lambda b, h, i: (b, cf(h)))

    in_specs, args = [qspec(q[1])], [q[0]]
    if has2:
        in_specs.append(qspec(q2[1]))
        args.append(q2[0])
    in_specs.append(kspec(k[1]))
    args.append(k[0])
    if has2:
        in_specs.append(kspec(k2[1]))
        args.append(k2[0])
    in_specs.append(kspec(v[1]))
    args.append(v[0])
    return pl.pallas_call(
        functools.partial(_attn_kernel, scale=scale, has2=has2),
        grid=(B, H, nq), in_specs=in_specs,
        out_specs=pl.BlockSpec((tq, LANES), lambda b, h, i: (b * nq + i, h)),
        out_shape=jax.ShapeDtypeStruct((B * T, H * LANES), F32),
        compiler_params=_cparams(("parallel", "arbitrary", "arbitrary")))(*args)


def _tri_masks(L, rev):
    i = lax.broadcasted_iota(jnp.int32, (L, L), 0)
    j = lax.broadcasted_iota(jnp.int32, (L, L), 1)
    if rev:
        return j >= i, j > i, i == j
    return j <= i, j < i, i == j


def _inv_unit_tri(mats, eye_f, L, dot):
    pows = [[-a for a in mats]]
    for _ in range(int(math.log2(L)) - 1):
        pows.append([dot(p, p) for p in pows[-1]])
    terms = [[eye_f + p for p in level] for level in pows]
    while len(terms) > 1:
        nxt = [[dot(x, y) for x, y in zip(terms[i], terms[i + 1])] for i in range(0, len(terms) - 1, 2)]
        if len(terms) % 2:
            nxt.append(terms[-1])
        terms = nxt
    return terms[0]


def _shifted(f, prev_ref, next_ref, tt, T):
    i = pl.program_id(0)
    first = (i * tt) % T == 0
    last = ((i + 1) * tt) % T == 0
    prev = jnp.where(first, 0.0, prev_ref[7:8, :])
    nxt = jnp.where(last, 0.0, next_ref[0:1, :])
    rows = lax.broadcasted_iota(jnp.int32, f.shape, 0)
    fp = jnp.where(rows == 0, prev, pltpu.roll(f, 1, 0))
    fn = jnp.where(rows == tt - 1, nxt, pltpu.roll(f, tt - 1, 0))
    return fp, fn


def _halo_specs(tt, width, M, col_fn):
    r8 = tt // 8
    last8 = M // 8 - 1
    return [
        pl.BlockSpec((tt, width), lambda *g: (g[0], col_fn(*g))),
        pl.BlockSpec((8, width), lambda *g: (jnp.maximum(g[0] * r8 - 1, 0), col_fn(*g))),
        pl.BlockSpec((8, width), lambda *g: (jnp.minimum((g[0] + 1) * r8, last8), col_fn(*g))),
    ]


NPAIR = B_HEADS // 2


def _pair_sum_matrix():
    r = lax.broadcasted_iota(jnp.int32, (LANES, LANES), 0) // B_HEAD_DIM
    c = lax.broadcasted_iota(jnp.int32, (LANES, LANES), 1) // B_HEAD_DIM
    return (r == c).astype(F32)


def _rwkv_prep_kernel(x_ref, prev_ref, next_ref, mu_ref, w0_ref, w2_ref, a0_ref, a2_ref, g2_ref, kk_ref_in,
                      r_o, k_o, v_o, kk_o, g_o, lw0_o, lw1_o, a0_o, a1_o, *, tt, T):
    f = x_ref[...]
    fp, fn = _shifted(f, prev_ref, next_ref, tt, T)
    f = f + mu_ref[0:1, :] * (fp - f) + mu_ref[1:2, :] * (fn - f)
    r = f[:, :B_W]
    k = f[:, B_W:2 * B_W]
    v = f[:, 2 * B_W:3 * B_W]
    off = 3 * B_W
    w_lo = f[:, off:off + 2 * B_DECAY_LORA]
    off += 2 * B_DECAY_LORA
    a_lo = f[:, off:off + 2 * B_AAA_LORA]
    off += 2 * B_AAA_LORA
    g_lo = f[:, off:]
    th = jnp.tanh(w_lo)
    logw, aa = [], []
    for d in range(2):
        wpre = w0_ref[d:d + 1, :] + _dot_hp(th[:, d * B_DECAY_LORA:(d + 1) * B_DECAY_LORA], w2_ref[d])
        wv = -_softplus(-wpre) - 0.5
        logw.append(-jnp.exp(wv))
        aa.append(_sigmoid(a0_ref[d:d + 1, :] + _dot_hp(a_lo[:, d * B_AAA_LORA:(d + 1) * B_AAA_LORA], a2_ref[d])))
    g = _dot_hp(_sigmoid(g_lo), g2_ref[...])
    kk = k * kk_ref_in[...]
    gsum = _pair_sum_matrix()
    for p in range(NPAIR):
        sl = slice(p * LANES, (p + 1) * LANES)
        kp = kk[:, sl]
        ssq = _dot_hp(kp * kp, gsum)
        kk_o[p] = kp / jnp.maximum(jnp.sqrt(ssq), 1e-12)
        r_o[p] = r[:, sl]
        k_o[p] = k[:, sl]
        v_o[p] = v[:, sl]
        g_o[p] = g[:, sl]
        lw0_o[p] = logw[0][:, sl]
        lw1_o[p] = logw[1][:, sl]
        a0_o[p] = aa[0][:, sl]
        a1_o[p] = aa[1][:, sl]


def _rwkv_prep(pb, B, T, mu, w0, w2, a0, a2, g2, k_k, tt=128):
    M = pb.shape[0]
    tt = min(tt, T)
    nt = T // tt
    full = lambda shape: pl.BlockSpec(shape, lambda i: tuple(0 for _ in shape))
    in_specs = _halo_specs(tt, B_IN, M, lambda i: 0) + [
        full((2, B_IN)), full((2, B_W)), full((2, B_DECAY_LORA, B_W)), full((2, B_W)),
        full((2, B_AAA_LORA, B_W)), full((B_GATE_LORA, B_W)), full((1, B_W))]
    ospec = pl.BlockSpec((None, NPAIR, tt, LANES), lambda i: (i // nt, 0, i % nt, 0))
    oshape = jax.ShapeDtypeStruct((B, NPAIR, T, LANES), F32)
    return pl.pallas_call(
        functools.partial(_rwkv_prep_kernel, tt=tt, T=T),
        grid=(M // tt,), in_specs=in_specs, out_specs=[ospec] * 9, out_shape=[oshape] * 9,
        compiler_params=_cparams(("parallel",)))(pb, pb, pb, mu, w0, w2, a0, a2, g2, k_k.reshape(1, B_W))


def _rwkv_chunk(r2, lw2, kd2, v2, kk2, a2, states, masks, L, rev):
    incl, strict, eye = masks
    incl_f = incl.astype(F32)
    eye_f = eye.astype(F32)
    i2 = lax.broadcasted_iota(jnp.int32, (L, 2 * L), 0)
    j2 = lax.broadcasted_iota(jnp.int32, (L, 2 * L), 1) % L
    incl2 = (j2 >= i2) if rev else (j2 <= i2)
    last = 0 if rev else L - 1
    halves = (slice(0, B_HEAD_DIM), slice(B_HEAD_DIM, 2 * B_HEAD_DIM))
    logp2 = [_dot_hp(incl_f, lw) for lw in lw2]
    lhs, rhs, rem_rhs, dec, vs = [], [], [], [], []
    for lp, lw, r, kd, v, kk, a in zip(logp2, lw2, r2, kd2, v2, kk2, a2):
        ninv = jnp.exp(-lp)
        b = kk * a
        kt = kk * jnp.exp(lp - lw)
        rt = r * jnp.exp(lp)
        bn, kdn = b * ninv, kd * ninv
        pl_row = lp[last:last + 1, :]
        rem = jnp.exp(pl_row - lp)
        bh, kh = b * rem, kd * rem
        dpl = jnp.exp(pl_row)
        for sl in halves:
            lhs.append(jnp.concatenate([kt[:, sl], rt[:, sl]], axis=0))
            rhs.append(jnp.concatenate([bn[:, sl], kdn[:, sl]], axis=0))
            rem_rhs.append(jnp.concatenate([bh[:, sl], kh[:, sl]], axis=0))
            dec.append(dpl[:, sl])
            vs.append(v[:, sl])
    gm = [_dot_hp(x, y, NT) for x, y in zip(lhs, rhs)]
    ls = [_dot_hp(x, s, NT) for x, s in zip(lhs, states)]
    akv = [_dot_hp(jnp.where(strict, g[:L, L:], 0.0), v) for g, v in zip(gm, vs)]
    tinv = _inv_unit_tri([jnp.where(strict, g[:L, :L], 0.0) for g in gm], eye_f, L, _dot_hp)
    u = [-_dot_hp(t, l[:L] + x) for t, l, x in zip(tinv, ls, akv)]
    uv = [jnp.concatenate([x, v], axis=0) for x, v in zip(u, vs)]
    y = [l[L:] + _dot_hp(jnp.where(incl2, g[L:, :], 0.0), x) for l, g, x in zip(ls, gm, uv)]
    s_new = [s * d + _dot_hp(x, m, TN) for s, d, x, m in zip(states, dec, uv, rem_rhs)]
    y2 = [jnp.concatenate([y[2 * p], y[2 * p + 1]], axis=1) for p in range(len(r2))]
    return y2, s_new


def _rwkv_scan_kernel(r_ref, v_ref, kk_ref, k_ref, lw_ref, a_ref, ka_ref, s0_ref, y_ref, sf_ref, s_ref, *, L, rev):
    c = pl.program_id(1)

    @pl.when(c == 0)
    def _():
        s_ref[...] = s0_ref[...]

    masks = _tri_masks(L, rev)
    pairs = range(NPAIR)
    a2 = [a_ref[p] for p in pairs]
    kd2 = [k_ref[p] * (1.0 + (a - 1.0) * ka_ref[p]) for p, a in zip(pairs, a2)]
    y2, s_new = _rwkv_chunk([r_ref[p] for p in pairs], [lw_ref[p] for p in pairs], kd2, [v_ref[p] for p in pairs],
                            [kk_ref[p] for p in pairs], a2, [s_ref[h] for h in range(B_HEADS)], masks, L, rev)
    for p in pairs:
        y_ref[p] = y2[p]
    for h in range(B_HEADS):
        s_ref[h] = s_new[h]

    @pl.when(c == pl.num_programs(1) - 1)
    def _():
        sf_ref[...] = s_ref[...]


def _rwkv_scan(r, v, kk, k, lw, a, k_a, s0, rev, L=CHUNK):
    B, _, T, _ = r.shape
    nc = T // L
    cmap = (lambda b, c: (b, 0, nc - 1 - c, 0)) if rev else (lambda b, c: (b, 0, c, 0))
    seq = pl.BlockSpec((None, NPAIR, L, LANES), cmap)
    st = pl.BlockSpec((None, B_HEADS, B_HEAD_DIM, B_HEAD_DIM), lambda b, c: (b, 0, 0, 0))
    return pl.pallas_call(
        functools.partial(_rwkv_scan_kernel, L=L, rev=rev),
        grid=(B, nc),
        in_specs=[seq] * 6 + [pl.BlockSpec((NPAIR, 1, LANES), lambda b, c: (0, 0, 0)), st],
        out_specs=[seq, st],
        out_shape=[jax.ShapeDtypeStruct((B, NPAIR, T, LANES), F32),
                   jax.ShapeDtypeStruct((B, B_HEADS, B_HEAD_DIM, B_HEAD_DIM), F32)],
        scratch_shapes=[pltpu.VMEM((B_HEADS, B_HEAD_DIM, B_HEAD_DIM), F32)],
        compiler_params=_cparams(("parallel", "arbitrary")))(r, v, kk, k, lw, a, k_a.reshape(NPAIR, 1, LANES), s0)


def _rwkv_post_kernel(y0_ref, y1_ref, r_ref, k_ref, v_ref, g_ref, rk_ref, lnw_ref, lnb_ref, o_ref):
    gsum = _pair_sum_matrix()
    inv_n = 1.0 / B_HEAD_DIM
    for p in range(NPAIR):
        y = y0_ref[p] + y1_ref[p]
        mean = _dot_hp(y, gsum) * inv_n
        yc = y - mean
        var = _dot_hp(yc * yc, gsum) * inv_n
        yn = yc * lax.rsqrt(var + B_GN_EPS) * lnw_ref[p] + lnb_ref[p]
        bonus = _dot_hp(r_ref[p] * k_ref[p] * rk_ref[p], gsum) * v_ref[p]
        o_ref[:, p * LANES:(p + 1) * LANES] = (yn + bonus) * g_ref[p]


def _rwkv_post(y0, y1, r, k, v, g, r_k, ln_w, ln_b, tt=256):
    B, _, T, _ = r.shape
    tt = min(tt, T)
    nt = T // tt
    seq = pl.BlockSpec((None, NPAIR, tt, LANES), lambda i: (i // nt, 0, i % nt, 0))
    par = pl.BlockSpec((NPAIR, 1, LANES), lambda i: (0, 0, 0))
    return pl.pallas_call(
        _rwkv_post_kernel, grid=(B * nt,),
        in_specs=[seq] * 6 + [par] * 3,
        out_specs=pl.BlockSpec((tt, B_W), lambda i: (i, 0)),
        out_shape=jax.ShapeDtypeStruct((B * T, B_W), F32),
        compiler_params=_cparams(("parallel",)))(
            y0, y1, r, k, v, g, r_k.reshape(NPAIR, 1, LANES), ln_w.reshape(NPAIR, 1, LANES),
            ln_b.reshape(NPAIR, 1, LANES))


def _rwkv_mixer(pb, B, T, prm, s0):
    r, k, v, kk, g, lw0, lw1, a0, a1 = _rwkv_prep(pb, B, T, prm['mu'], prm['w0'], prm['w2'], prm['a0'],
                                                   prm['a2'], prm['g2'], prm['k_k'])
    if s0 is None:
        s0 = jnp.zeros((B, 2, B_HEADS, B_HEAD_DIM, B_HEAD_DIM), F32)
    y0, sf0 = _rwkv_scan(r, v, kk, k, lw0, a0, prm['k_a'], s0[:, 0], rev=False)
    y1, sf1 = _rwkv_scan(r, v, kk, k, lw1, a1, prm['k_a'], s0[:, 1], rev=True)
    out = _rwkv_post(y0, y1, r, k, v, g, prm['r_k'], prm['ln_w'], prm['ln_b'])
    return out, jnp.stack([sf0, sf1], axis=1)


def _gdn_prep_kernel(x_ref, prev_ref, next_ref, cw_ref, o_ref, *, tt, T):
    c = pl.program_id(1)
    x = x_ref[...]
    xp, xn = _shifted(x, prev_ref, next_ref, tt, T)
    y = cw_ref[0:1, :] * xp + cw_ref[1:2, :] * x + cw_ref[2:3, :] * xn
    y = y * _sigmoid(y)

    @pl.when(c == 2)
    def _():
        o_ref[...] = y

    @pl.when(c < 2)
    def _():
        scale = jnp.where(c == 0, C_HEAD_DIM ** -0.5, 1.0).astype(F32)
        for h in range(C_HEADS):
            sl = slice(h * C_HEAD_DIM, (h + 1) * C_HEAD_DIM)
            yh = y[:, sl]
            o_ref[:, sl] = yh * lax.rsqrt(jnp.sum(yh * yh, axis=-1, keepdims=True) + 1e-6) * scale


def _gdn_prep(pc, T, conv_w, tt=256):
    M = pc.shape[0]
    tt = min(tt, T)
    in_specs = _halo_specs(tt, C_W, M, lambda i, c: c) + [pl.BlockSpec((3, C_W), lambda i, c: (0, c))]
    return pl.pallas_call(
        functools.partial(_gdn_prep_kernel, tt=tt, T=T),
        grid=(M // tt, 3), in_specs=in_specs,
        out_specs=pl.BlockSpec((tt, C_W), lambda i, c: (i, c)),
        out_shape=jax.ShapeDtypeStruct((M, 3 * C_W), F32),
        compiler_params=_cparams(("parallel", "arbitrary")))(pc, pc, pc, conv_w)


def _gdn_bg_kernel(x_ref, alog_ref, dtb_ref, o_ref):
    x = x_ref[...]
    lane = lax.broadcasted_iota(jnp.int32, x.shape, 1)
    beta = _sigmoid(x)
    g = -jnp.exp(alog_ref[...]) * _softplus(x + dtb_ref[...])
    o_ref[...] = jnp.where(lane < 2 * C_HEADS, beta, g)


def _gdn_bg(pc, a_log, dt_bias, tt=512):
    M = pc.shape[0]
    tt = min(tt, M)
    nb = 2 * C_HEADS
    pad = lambda p: jnp.concatenate([jnp.zeros((nb,), F32), p.reshape(nb), jnp.zeros((LANES - 2 * nb,), F32)]).reshape(1, LANES)
    return pl.pallas_call(
        _gdn_bg_kernel, grid=(M // tt,),
        in_specs=[pl.BlockSpec((tt, LANES), lambda i: (i, 4 * C_W // LANES)),
                  pl.BlockSpec((1, LANES), lambda i: (0, 0)), pl.BlockSpec((1, LANES), lambda i: (0, 0))],
        out_specs=pl.BlockSpec((tt, LANES), lambda i: (i, 0)),
        out_shape=jax.ShapeDtypeStruct((M, LANES), F32),
        compiler_params=_cparams(("parallel",)))(pc, pad(a_log), pad(dt_bias))


def _gdn_chunk(qs, ks, vs, bcol, gcol, grow, states, masks, L, rev):
    incl, strict, eye = masks
    incl_f = incl.astype(F32)
    eye_f = eye.astype(F32)
    nh = len(qs)
    i = lax.broadcasted_iota(jnp.int32, (L, L), 0)
    j = lax.broadcasted_iota(jnp.int32, (L, L), 1)
    cum_t = ((i >= j) if rev else (i <= j)).astype(F32)
    gc_cols = _dot_hp(incl_f, gcol)
    gc_rows = _dot_hp(grow, cum_t)
    last = 0 if rev else L - 1
    gcs, decays, kbs, rhss, kends, g_ends = [], [], [], [], [], []
    for h in range(nh):
        gc = jnp.broadcast_to(gc_cols[:, h:h + 1], (L, LANES))
        decays.append(jnp.exp(jnp.where(incl, gc[:, :L] - gc_rows[h:h + 1, :], -1e30)))
        beta = bcol[:, h:h + 1]
        kb = ks[h] * beta
        kbs.append(kb)
        rhss.append(jnp.concatenate([vs[h] * beta, kb * jnp.exp(gc)], axis=1))
        g_end = gc[last:last + 1, :]
        kends.append(ks[h] * jnp.exp(g_end - gc))
        g_ends.append(jnp.exp(g_end))
        gcs.append(jnp.exp(gc))
    kk = [_dot_hp(kb, k, NT) for kb, k in zip(kbs, ks)]
    qk = [_dot_hp(q, k, NT) for q, k in zip(qs, ks)]
    qs_s = [_dot_hp(q * e, s) for q, e, s in zip(qs, gcs, states)]
    tinv = _inv_unit_tri([jnp.where(strict, x * d, 0.0) for x, d in zip(kk, decays)], eye_f, L, _dot_hp)
    sol = [_dot_hp(t, r) for t, r in zip(tinv, rhss)]
    v_new = [x[:, :C_HEAD_DIM] - _dot_hp(x[:, C_HEAD_DIM:], s) for x, s in zip(sol, states)]
    o = [a + _dot_hp(jnp.where(incl, x * d, 0.0), vn) for a, x, d, vn in zip(qs_s, qk, decays, v_new)]
    s_new = [s * ge + _dot_hp(ke, vn, TN) for s, ge, ke, vn in zip(states, g_ends, kends, v_new)]
    return o, s_new


def _gdn_scan_kernel(q_ref, k_ref, v_ref, bcol_ref, gcol_ref, grow_ref, s0_ref, o_ref, sf_ref, s_ref, *, L, rev):
    c = pl.program_id(1)

    @pl.when(c == 0)
    def _():
        s_ref[...] = s0_ref[...]

    masks = _tri_masks(L, rev)
    heads = range(C_HEADS)
    hs = [slice(h * C_HEAD_DIM, (h + 1) * C_HEAD_DIM) for h in heads]
    o, s_new = _gdn_chunk([q_ref[:, sl] for sl in hs], [k_ref[:, sl] for sl in hs], [v_ref[:, sl] for sl in hs],
                          bcol_ref[...], gcol_ref[...], grow_ref[...], [s_ref[h] for h in heads], masks, L, rev)
    for h in heads:
        o_ref[:, hs[h]] = o[h]
        s_ref[h] = s_new[h]

    @pl.when(c == pl.num_programs(1) - 1)
    def _():
        sf_ref[...] = s_ref[...]


def _gdn_scan(qkv, bcol, gcol, grow, s0, B, T, rev, L=CHUNK):
    nc = T // L
    cidx = (lambda c: nc - 1 - c) if rev else (lambda c: c)
    seq = lambda col: pl.BlockSpec((L, C_W), lambda b, c: (b * nc + cidx(c), col))
    small = pl.BlockSpec((L, C_HEADS), lambda b, c: (b * nc + cidx(c), 0))
    st = pl.BlockSpec((None, C_HEADS, C_HEAD_DIM, C_HEAD_DIM), lambda b, c: (b, 0, 0, 0))
    return pl.pallas_call(
        functools.partial(_gdn_scan_kernel, L=L, rev=rev),
        grid=(B, nc),
        in_specs=[seq(0), seq(1), seq(2), small, small,
                  pl.BlockSpec((None, C_HEADS, L), lambda b, c: (b * nc + cidx(c), 0, 0)), st],
        out_specs=[pl.BlockSpec((L, C_W), lambda b, c: (b * nc + cidx(c), 0)), st],
        out_shape=[jax.ShapeDtypeStruct((B * T, C_W), F32),
                   jax.ShapeDtypeStruct((B, C_HEADS, C_HEAD_DIM, C_HEAD_DIM), F32)],
        scratch_shapes=[pltpu.VMEM((C_HEADS, C_HEAD_DIM, C_HEAD_DIM), F32)],
        compiler_params=_cparams(("parallel", "arbitrary")))(qkv, qkv, qkv, bcol, gcol, grow, s0)


def _gdn_post_kernel(o0_ref, o1_ref, z_ref, g_ref, out_ref):
    for h in range(C_HEADS):
        sl = slice(h * C_HEAD_DIM, (h + 1) * C_HEAD_DIM)
        y = o0_ref[:, sl] + o1_ref[:, sl]
        y = y * lax.rsqrt(jnp.mean(y * y, axis=-1, keepdims=True) + NORM_EPS) * g_ref[...]
        z = z_ref[:, sl]
        out_ref[:, sl] = y * (z * _sigmoid(z))


def _gdn_post(o0, o1, pc, norm_g, tt=256):
    M = o0.shape[0]
    tt = min(tt, M)
    return pl.pallas_call(
        _gdn_post_kernel, grid=(M // tt,),
        in_specs=[pl.BlockSpec((tt, C_W), lambda i: (i, 0)), pl.BlockSpec((tt, C_W), lambda i: (i, 0)),
                  pl.BlockSpec((tt, C_W), lambda i: (i, 3)), pl.BlockSpec((1, C_HEAD_DIM), lambda i: (0, 0))],
        out_specs=pl.BlockSpec((tt, C_W), lambda i: (i, 0)),
        out_shape=jax.ShapeDtypeStruct((M, C_W), F32),
        compiler_params=_cparams(("parallel",)))(o0, o1, pc, norm_g.reshape(1, C_HEAD_DIM))


def _gdn_mixer(pc, B, T, prm, s0, L=CHUNK):
    M = pc.shape[0]
    nc = T // L
    qkv = _gdn_prep(pc, T, prm['conv'])
    bg = _gdn_bg(pc, prm['a_log'], prm['dt_bias'])
    if s0 is None:
        s0 = jnp.zeros((B, 2, C_HEADS, C_HEAD_DIM, C_HEAD_DIM), F32)
    outs, finals = [], []
    for d in range(2):
        bcol = bg[:, d * C_HEADS:(d + 1) * C_HEADS]
        gcol = bg[:, 2 * C_HEADS + d * C_HEADS:2 * C_HEADS + (d + 1) * C_HEADS]
        grow = gcol.reshape(B * nc, L, C_HEADS).transpose(0, 2, 1)
        o, sf = _gdn_scan(qkv, bcol, gcol, grow, s0[:, d], B, T, rev=(d == 1))
        outs.append(o)
        finals.append(sf)
    y = _gdn_post(outs[0], outs[1], pc, prm['norm_g'])
    return y, jnp.stack(finals, axis=1)


def _top_values(tile, count, need_mask):
    rows = lax.broadcasted_iota(jnp.int32, tile.shape, 0).astype(F32)
    big = float(tile.shape[0])
    cur = tile
    vals = []
    for _ in range(count):
        m = jnp.max(cur, axis=0, keepdims=True)
        vals.append(m)
        idx = jnp.min(jnp.where(cur == m, rows, big), axis=0, keepdims=True)
        cur = jnp.where(rows == idx, -jnp.inf, cur)
    return vals, (cur == -jnp.inf) if need_mask else None


PEER_EB = 1024
PEER_GRP = MXU_K // PEER_HALF
PEER_NGRP = 2 * PEER_HEADS // PEER_GRP


def _peer_topk_kernel(q_ref, kh_ref, kl_ref, thr_ref, e0_ref, e1_ref, s1_ref, sc_ref):
    rows_g = PEER_GRP * N_KEYS
    for g in range(PEER_NGRP):
        qh, ql = _split(q_ref[:, g * MXU_K:(g + 1) * MXU_K])
        kh = kh_ref[g * rows_g:(g + 1) * rows_g, :]
        kl = kl_ref[g * rows_g:(g + 1) * rows_g, :]
        sc_ref[g * rows_g:(g + 1) * rows_g, :] = _dg(kh, qh, NT) + (_dg(kh, ql, NT) + _dg(kl, qh, NT))

    def head(h, carry):
        t0 = sc_ref[pl.ds(pl.multiple_of(h * 2 * N_KEYS, N_KEYS), N_KEYS), :]
        t1 = sc_ref[pl.ds(pl.multiple_of(h * 2 * N_KEYS + N_KEYS, N_KEYS), N_KEYS), :]
        a, ra = _top_values(t0, PEER_TOPK, True)
        b, rb = _top_values(t1, PEER_TOPK, True)
        s0 = jnp.where(ra, t0, -jnp.inf)
        s1 = jnp.where(rb, t1, -jnp.inf)
        amat = jnp.concatenate(a, axis=0)
        bmat = jnp.concatenate(b, axis=0)
        b_lo = bmat[:8]
        row8 = lax.broadcasted_iota(jnp.int32, b_lo.shape, 0)
        pieces = [a[0] + b_lo, a[0] + bmat[8:], a[1] + b_lo]
        for r in range(2, 8):
            pieces.append(jnp.where(row8 < PEER_TOPK // (r + 1), a[r] + b_lo, -jnp.inf))
        pieces.append(amat[8:] + b[0])
        cand = jnp.concatenate(pieces, axis=0)
        best, _ = _top_values(cand, PEER_TOPK + 1, False)
        z = jnp.zeros_like(best[0])
        for n in range(PEER_TOPK):
            z = z + jnp.exp(best[n] - best[0])
        tau = 0.5 * (best[PEER_TOPK - 1] + best[PEER_TOPK])
        thr = tau - s0
        e0 = jnp.exp(s0 - a[0]) * (1.0 / z)
        e1 = jnp.exp(s1 - b[0])
        for tc in range(t0.shape[1] // LANES):
            cols = slice(tc * LANES, (tc + 1) * LANES)
            thr_ref[h, tc] = thr[:, cols]
            e0_ref[h, tc] = e0[:, cols]
            e1_ref[h, tc] = e1[:, cols]
            s1_ref[h, tc] = s1[:, cols]
        return carry

    lax.fori_loop(0, PEER_HEADS, head, 0)


def _peer_topk(q, kdt_hi, kdt_lo, tt=256):
    M = q.shape[0]
    tt = min(tt, M)
    R = PEER_HEADS * 2 * N_KEYS
    big = pl.BlockSpec((PEER_HEADS, tt // LANES, N_KEYS, LANES), lambda i: (0, i, 0, 0))
    return pl.pallas_call(
        _peer_topk_kernel, grid=(M // tt,),
        in_specs=[pl.BlockSpec((tt, PEER_HEADS * PEER_QDIM), lambda i: (i, 0)),
                  pl.BlockSpec((R, MXU_K), lambda i: (0, 0)),
                  pl.BlockSpec((R, MXU_K), lambda i: (0, 0))],
        out_specs=[big] * 4,
        out_shape=[jax.ShapeDtypeStruct((PEER_HEADS, M // LANES, N_KEYS, LANES), F32)] * 4,
        scratch_shapes=[pltpu.VMEM((R, tt), F32)],
        compiler_params=_cparams(("parallel",)))(q, kdt_hi, kdt_lo)


def _peer_expert_kernel(xt_ref, u_ref, vt_ref, thr_ref, e0_ref, e1_ref, s1_ref, res_ref, m_ref, o_ref,
                        acc_ref, act0_ref, act1_ref, am0_ref, am1_ref, *, ni, gate_slot):
    e = pl.program_id(1)
    n_blocks = pl.num_programs(1) - 2
    tt = xt_ref.shape[1]
    even = e % 2 == 0
    odd = e % 2 == 1

    @pl.when(e == 0)
    def _():
        acc_ref[...] = jnp.zeros_like(acc_ref)

    def project(act_w):
        act_w[...] = _dg(u_ref[...], xt_ref[...])

    def weigh(act_r, am_w):
        first = (e - 1) * ni
        for ii in range(ni):
            i = first + ii
            rows = slice(ii * N_KEYS, (ii + 1) * N_KEYS)
            for tc in range(tt // LANES):
                cols = slice(tc * LANES, (tc + 1) * LANES)
                w = None
                for h in range(PEER_HEADS):
                    keep = s1_ref[h, tc] >= thr_ref[h, tc, pl.ds(i, 1), :]
                    contrib = jnp.where(keep, e1_ref[h, tc], 0.0) * e0_ref[h, tc, pl.ds(i, 1), :]
                    w = contrib if w is None else w + contrib
                a = act_r[rows, cols]
                a = 0.5 * a * (1.0 + lax.erf(a * (2.0 ** -0.5)))
                am_w[rows, cols] = (a * w).astype(BF16)

    def accumulate(am_r):
        acc_ref[...] += _dg(vt_ref[...], am_r[...])

    has_project = e < n_blocks
    has_weigh = jnp.logical_and(e >= 1, e <= n_blocks)
    has_accumulate = e >= 2
    pl.when(jnp.logical_and(has_project, even))(lambda: project(act0_ref))
    pl.when(jnp.logical_and(has_project, odd))(lambda: project(act1_ref))
    pl.when(jnp.logical_and(has_weigh, even))(lambda: weigh(act1_ref, am1_ref))
    pl.when(jnp.logical_and(has_weigh, odd))(lambda: weigh(act0_ref, am0_ref))
    pl.when(jnp.logical_and(has_accumulate, even))(lambda: accumulate(am0_ref))
    pl.when(jnp.logical_and(has_accumulate, odd))(lambda: accumulate(am1_ref))

    @pl.when(e == pl.num_programs(1) - 1)
    def _():
        o_ref[...] = res_ref[...] + m_ref[gate_slot:gate_slot + 1, :] * acc_ref[...].T


def _peer_experts(xn, u_bf, vt_bf, thr, e0, e1, s1m, res, mods, gate_slot, rows_per_mod, tt=512):
    M, D = res.shape
    tt = min(tt, M, rows_per_mod) if mods.shape[0] > 1 else min(tt, M)
    nb, _, eb = vt_bf.shape
    ni = eb // N_KEYS
    once = pl.Buffered(1)
    if mods.shape[0] > 1:
        mspec = pl.BlockSpec((None, 6, D), lambda i, e: ((i * tt) // rows_per_mod, 0, 0), pipeline_mode=once)
    else:
        mspec = pl.BlockSpec((None, 6, D), lambda i, e: (0, 0, 0), pipeline_mode=once)
    big = pl.BlockSpec((PEER_HEADS, tt // LANES, N_KEYS, LANES), lambda i, e: (0, i, 0, 0), pipeline_mode=once)
    return pl.pallas_call(
        functools.partial(_peer_expert_kernel, ni=ni, gate_slot=gate_slot),
        grid=(M // tt, nb + 2),
        in_specs=[pl.BlockSpec((D, tt), lambda i, e: (0, i), pipeline_mode=once),
                  pl.BlockSpec((eb, D), lambda i, e: (jnp.minimum(e, nb - 1), 0)),
                  pl.BlockSpec((None, D, eb), lambda i, e: (jnp.clip(e - 2, 0, nb - 1), 0, 0)),
                  big, big, big, big,
                  pl.BlockSpec((tt, D), lambda i, e: (i, 0), pipeline_mode=once), mspec],
        out_specs=pl.BlockSpec((tt, D), lambda i, e: (i, 0)),
        out_shape=jax.ShapeDtypeStruct((M, D), F32),
        scratch_shapes=[pltpu.VMEM((D, tt), F32), pltpu.VMEM((eb, tt), F32), pltpu.VMEM((eb, tt), F32),
                        pltpu.VMEM((eb, tt), BF16), pltpu.VMEM((eb, tt), BF16)],
        compiler_params=_cparams(("parallel", "arbitrary"), VMEM_LIMIT_BIG))(
            xn, u_bf, vt_bf, thr, e0, e1, s1m, res, mods)


def _peer(x, mods, rows_per_mod, norm_g, prm):
    q, xn = _mm_norm(x, prm['wq'], norm_g=norm_g, mods=mods, slots=(3, 4), rows_per_mod=rows_per_mod,
                     hp=True, emit_xn=BF16)
    thr, e0, e1, s1m = _peer_topk(q, prm['kdt_hi'], prm['kdt_lo'])
    return _peer_experts(xn.T, prm['u'], prm['vt'], thr, e0, e1, s1m, x, mods, 5, rows_per_mod)


def _peer_params(wq, keys, u_tab, v_tab):
    nset = 2 * PEER_HEADS
    kd = jnp.zeros((nset, N_KEYS, PEER_GRP, PEER_HALF), F32)
    sets = jnp.arange(nset)
    kd = kd.at[sets, :, sets % PEER_GRP, :].set(keys.reshape(nset, N_KEYS, PEER_HALF))
    kd_hi, kd_lo = _split(kd.reshape(nset * N_KEYS, MXU_K))
    vt = v_tab.astype(BF16).reshape(N_EXPERTS // PEER_EB, PEER_EB, -1).transpose(0, 2, 1)
    return dict(wq=_split(wq), kdt_hi=kd_hi, kdt_lo=kd_lo, u=u_tab.astype(BF16), vt=vt)


def _layer_even(x, mods, B, T, norm1, w_a, w_b, w_out, qk_gains, rwkv_prm, ctx):
    rpm = T
    pa = _mm_norm(x, w_a, norm_g=norm1, mods=mods, slots=(0, 1), rows_per_mod=rpm)
    pb = _mm_norm(x, w_b, norm_g=norm1, mods=mods, slots=(0, 1), rows_per_mod=rpm)
    nqk = A_HEADS + A_KV_HEADS
    grp = A_HEADS // A_KV_HEADS
    scale = A_HEAD_DIM ** -0.5
    if ctx is None:
        qk = _norm_rope(pa, nqk, T, gains=qk_gains)
        oa = _attention(B, T, T, A_HEADS, scale, (qk, lambda h: h), (qk, lambda h: A_HEADS + h // grp),
                        (pa, lambda h: nqk + h // grp))
        ctx_out = (qk[:, A_Q:A_Q + A_KV], pa[:, A_Q + A_KV:])
        s0 = None
    else:
        cache_k, cache_v, s0 = ctx
        P = cache_k.shape[1]
        qk = _norm_rope(pa, nqk, T, gains=qk_gains, tables=_rope_tables(T, A_HEAD_DIM))
        k_all = jnp.concatenate([qk[:, A_Q:].reshape(B, T, A_KV), cache_k.reshape(B, P, A_KV)], axis=1)
        v_all = jnp.concatenate([pa[:, A_Q + A_KV:].reshape(B, T, A_KV), cache_v.reshape(B, P, A_KV)], axis=1)
        Tk = T + P
        oa = _attention(B, T, Tk, A_HEADS, scale, (qk, lambda h: h),
                        (k_all.reshape(B * Tk, A_KV).astype(BF16), lambda h: h // grp),
                        (v_all.reshape(B * Tk, A_KV).astype(BF16), lambda h: h // grp))
        ctx_out = None
    ob, s_fin = _rwkv_mixer(pb, B, T, rwkv_prm, s0)
    x = _mm_res([oa, ob], w_out, x, mods, 2, rpm)
    return x, ctx_out, s_fin


D_QROPE_BLK = 0
D_KROPE_BLK = D_HEADS
D_NROPE_BLK = D_HEADS + 1
D_CKV_BLK = 12
D_QNOPE_BLK = 16
D_PROJ = 24 * LANES


def _mla_in_weight(w_d):
    K = w_d.shape[0]
    wq = w_d[:, :D_Q].reshape(K, D_HEADS, D_QK)
    z64 = jnp.zeros((K, D_HEADS, LANES - D_ROPE), w_d.dtype)
    q_rope = jnp.concatenate([wq[:, :, D_NOPE:], z64], axis=-1).reshape(K, D_HEADS * LANES)
    q_nope = wq[:, :, :D_NOPE].reshape(K, D_HEADS * LANES)
    k_rope = jnp.concatenate([w_d[:, D_Q + D_KV_RANK:], jnp.zeros((K, LANES - D_ROPE), w_d.dtype)], axis=-1)
    padz = jnp.zeros((K, (D_CKV_BLK - D_NROPE_BLK) * LANES), w_d.dtype)
    return jnp.concatenate([q_rope, k_rope, padz, w_d[:, D_Q:D_Q + D_KV_RANK], q_nope], axis=-1)


def _layer_odd(x, mods, B, T, norm1, w_c, w_d, w_out, gdn_prm, kv_norm_g, w_kv_b, ctx):
    rpm = T
    pc = _mm_norm(x, w_c, norm_g=norm1, mods=mods, slots=(0, 1), rows_per_mod=rpm)
    pd = _mm_norm(x, w_d, norm_g=norm1, mods=mods, slots=(0, 1), rows_per_mod=rpm)
    scale = D_QK ** -0.5
    ckv_col = D_CKV_BLK * LANES // D_KV_RANK
    kv, ckv_n = _mm_norm(pd, w_kv_b, k=D_KV_RANK, xcol=ckv_col, norm_g=kv_norm_g, emit_xn=F32)
    kcol = lambda h: 2 * h
    vcol = lambda h: 2 * h + 1
    if ctx is None:
        od = _attention(B, T, T, D_HEADS, scale, (pd, lambda h: D_QNOPE_BLK + h), (kv, kcol), (kv, vcol),
                        q2=(pd, lambda h: D_QROPE_BLK + h), k2=(pd, lambda h: D_KROPE_BLK))
        ctx_out = (ckv_n, pd[:, D_KROPE_BLK * LANES:D_KROPE_BLK * LANES + D_ROPE])
        s0 = None
    else:
        s0, c_ckv, c_krope = ctx
        P = c_ckv.shape[1]
        Tk = T + P
        roped = _norm_rope(pd, D_NROPE_BLK, T, tables=_rope_tables(T, D_ROPE))
        kv_ctx = _mm_norm(c_ckv.reshape(B * P, D_KV_RANK), w_kv_b)
        kv_all = jnp.concatenate([kv.reshape(B, T, -1), kv_ctx.reshape(B, P, -1)], axis=1).reshape(B * Tk, -1)
        kv_all = kv_all.astype(BF16)
        kr_lat = roped[:, D_KROPE_BLK * LANES:].reshape(B, T, LANES)
        kr_ctx = jnp.concatenate([c_krope, jnp.zeros((B, P, LANES - D_ROPE), F32)], axis=-1)
        kr_all = jnp.concatenate([kr_lat, kr_ctx], axis=1).reshape(B * Tk, LANES).astype(BF16)
        od = _attention(B, T, Tk, D_HEADS, scale, (pd, lambda h: D_QNOPE_BLK + h), (kv_all, kcol), (kv_all, vcol),
                        q2=(roped, lambda h: D_QROPE_BLK + h), k2=(kr_all, lambda h: 0))
        ctx_out = None
    oc, s_fin = _gdn_mixer(pc, B, T, gdn_prm, s0)
    x = _mm_res([oc, od], w_out, x, mods, 2, rpm)
    return x, ctx_out, s_fin


def _modulation(cond, w_ada, b_ada):
    n = cond.shape[0]
    rows = ((n + 7) // 8) * 8
    a = cond * _sigmoid(cond)
    a = jnp.concatenate([a, jnp.zeros((rows - n, cond.shape[1]), F32)], axis=0)
    m = _mm_norm(a, _split(w_ada), hp=True, tn_cap=1024)[:n] + b_ada
    return m.reshape(n, 6, D_MODEL)


def kernel(x_prompt, x_sample, cache_attn_k, cache_attn_v, state_rwkv, state_gdn, cache_mla_ckv, cache_mla_krope,
           c, c_ctx, norm1_g, norm2_g, ada_w, ada_b, w_in_ab, w_out_ab, attn_q_norm, attn_k_norm,
           rwkv_mu, rwkv_w0, rwkv_w2, rwkv_a0, rwkv_a2, rwkv_g2, rwkv_k_k, rwkv_k_a, rwkv_r_k, rwkv_ln_w, rwkv_ln_b,
           w_in_cd, w_out_cd, gdn_conv, gdn_a_log, gdn_dt_bias, gdn_norm_g, mla_kv_norm_g, mla_w_kv_b,
           peer_wq, peer_keys, peer_u, peer_v, final_norm_g):
    Bp, Tp, D = x_prompt.shape
    Bs, Ts, _ = x_sample.shape
    depth = norm1_g.shape[0]
    xp = x_prompt.reshape(Bp * Tp, D)
    xs = x_sample.reshape(Bs * Ts, D)
    cond = jnp.concatenate([c_ctx[None, :], c], axis=0)
    outs = dict(k=[], v=[], rwkv=[], gdn=[], ckv=[], krope=[])
    for li in range(depth):
        j = li // 2
        mods = _modulation(cond, ada_w[li], ada_b[li])
        mods_p, mods_s = mods[:1], mods[1:]
        peer_prm = _peer_params(peer_wq[li], peer_keys[li], peer_u[li], peer_v[li])
        if li % 2 == 0:
            w_in = w_in_ab[j].astype(BF16)
            w_a, w_b = w_in[:, :A_IN], w_in[:, A_IN:]
            w_o = w_out_ab[j].astype(BF16)
            w_out = [w_o[:A_Q], w_o[A_Q:]]
            gains = jnp.concatenate([jnp.tile(attn_q_norm[j], A_HEADS), jnp.tile(attn_k_norm[j], A_KV_HEADS)])
            rp = dict(mu=rwkv_mu[j], w0=rwkv_w0[j], w2=rwkv_w2[j], a0=rwkv_a0[j], a2=rwkv_a2[j], g2=rwkv_g2[j],
                      k_k=rwkv_k_k[j], k_a=rwkv_k_a[j], r_k=rwkv_r_k[j], ln_w=rwkv_ln_w[j], ln_b=rwkv_ln_b[j])
            xp, (k_c, v_c), s_c = _layer_even(xp, mods_p, Bp, Tp, norm1_g[li], w_a, w_b, w_out, gains, rp, None)
            xs, _, _ = _layer_even(xs, mods_s, Bs, Ts, norm1_g[li], w_a, w_b, w_out, gains, rp,
                                   (cache_attn_k[:, j], cache_attn_v[:, j], state_rwkv[:, j]))
            outs['k'].append(k_c.reshape(Bp, Tp, A_KV_HEADS, A_HEAD_DIM))
            outs['v'].append(v_c.reshape(Bp, Tp, A_KV_HEADS, A_HEAD_DIM))
            outs['rwkv'].append(s_c)
        else:
            w_in = w_in_cd[j]
            w_c = jnp.concatenate([w_in[:, :C_IN], jnp.zeros((D, LANES - 4 * C_HEADS), F32)], axis=-1).astype(BF16)
            w_d = _mla_in_weight(w_in[:, C_IN:]).astype(BF16)
            w_o = w_out_cd[j].astype(BF16)
            w_out = [w_o[:C_W], w_o[C_W:]]
            gp = dict(conv=gdn_conv[j], a_log=gdn_a_log[j], dt_bias=gdn_dt_bias[j], norm_g=gdn_norm_g[j])
            w_kv_b = mla_w_kv_b[j].astype(BF16)
            xp, (ckv_c, kr_c), s_c = _layer_odd(xp, mods_p, Bp, Tp, norm1_g[li], w_c, w_d, w_out, gp,
                                                mla_kv_norm_g[j], w_kv_b, None)
            xs, _, _ = _layer_odd(xs, mods_s, Bs, Ts, norm1_g[li], w_c, w_d, w_out, gp, mla_kv_norm_g[j], w_kv_b,
                                  (state_gdn[:, j], cache_mla_ckv[:, j], cache_mla_krope[:, j]))
            outs['gdn'].append(s_c)
            outs['ckv'].append(ckv_c.reshape(Bp, Tp, D_KV_RANK))
            outs['krope'].append(kr_c.reshape(Bp, Tp, D_ROPE))
        xp = _peer(xp, mods_p, Tp, norm2_g[li], peer_prm)
        xs = _peer(xs, mods_s, Ts, norm2_g[li], peer_prm)
    y_prompt = _rmsnorm(xp, final_norm_g).reshape(Bp, Tp, D)
    y_sample = _rmsnorm(xs, final_norm_g).reshape(Bs, Ts, D)
    st = lambda name: jnp.stack(outs[name], axis=1)
    return (y_prompt, y_sample, st('k'), st('v'), st('rwkv'), st('gdn'), st('ckv'), st('krope'))
```

```python
import functools
import math

import numpy as np
import jax
import jax.numpy as jnp
from jax import lax
from jax.experimental import pallas as pl
from jax.experimental.pallas import tpu as pltpu

F32 = jnp.float32
BF16 = jnp.bfloat16

D_MODEL = 2048
GRID_W = 64
ROPE_THETA = 10000.0
NORM_EPS = 1e-6
A_HEADS, A_KV_HEADS, A_HEAD_DIM = 8, 2, 128
A_Q = A_HEADS * A_HEAD_DIM
A_KV = A_KV_HEADS * A_HEAD_DIM
A_IN = A_Q + 2 * A_KV
B_HEADS, B_HEAD_DIM = 16, 64
B_W = B_HEADS * B_HEAD_DIM
B_DECAY_LORA, B_AAA_LORA, B_GATE_LORA = 64, 64, 128
B_IN = 3 * B_W + 2 * B_DECAY_LORA + 2 * B_AAA_LORA + B_GATE_LORA
B_GN_EPS = 64e-5
C_HEADS, C_HEAD_DIM = 8, 128
C_W = C_HEADS * C_HEAD_DIM
C_IN = 4 * C_W + 4 * C_HEADS
D_HEADS, D_NOPE, D_ROPE, D_V = 8, 128, 64, 128
D_QK = D_NOPE + D_ROPE
D_KV_RANK = 512
D_Q = D_HEADS * D_QK
PEER_HEADS, N_KEYS, PEER_QDIM, PEER_TOPK = 8, 128, 128, 16
PEER_HALF = PEER_QDIM // 2
N_EXPERTS = N_KEYS * N_KEYS

LANES = 128
MXU_K = 256
VMEM_LIMIT = 48 * 1024 * 1024
VMEM_LIMIT_BIG = 56 * 1024 * 1024

CHUNK = 64

NN = ((1,), (0,))
NT = ((1,), (1,))
TN = ((0,), (0,))


def _dg(a, b, dims=NN):
    return lax.dot_general(a, b, (dims, ((), ())), preferred_element_type=F32)


def _split(x):
    hi = x.astype(BF16)
    lo = (x - hi.astype(F32)).astype(BF16)
    return hi, lo


def _dot_hp(a, b, dims=NN):
    ah, al = _split(a)
    bh, bl = _split(b)
    return _dg(ah, bh, dims) + (_dg(ah, bl, dims) + _dg(al, bh, dims))


def _dot_bf(a, b, dims=NN):
    return _dg(a.astype(BF16), b.astype(BF16), dims)


def _sigmoid(x):
    return 1.0 / (1.0 + jnp.exp(-x))


def _softplus(x):
    return jnp.maximum(x, 0.0) + jnp.log(1.0 + jnp.exp(-jnp.abs(x)))


def _cparams(sem, vmem=VMEM_LIMIT):
    return pltpu.CompilerParams(dimension_semantics=sem, vmem_limit_bytes=vmem)


def _pick_tile(n, cap):
    best = None
    t = LANES
    while t <= min(n, cap):
        if n % t == 0:
            best = t
        t += LANES
    return best if best is not None else n


def _mm_norm_kernel(*refs, has_norm, has_mod, shift_slot, scale_slot, hp, emit_xn):
    it = iter(refs)
    x_ref = next(it)
    g_ref = next(it) if has_norm else None
    m_ref = next(it) if has_mod else None
    wh_ref = next(it)
    wl_ref = next(it) if hp else None
    o_ref = next(it)
    xn_ref = next(it) if emit_xn else None
    xh_ref = next(it)
    xl_ref = next(it) if hp else None

    @pl.when(pl.program_id(1) == 0)
    def _():
        xv = x_ref[...].astype(F32)
        if has_norm:
            xv = xv * lax.rsqrt(jnp.mean(xv * xv, axis=-1, keepdims=True) + NORM_EPS) * g_ref[...]
        if has_mod:
            xv = xv * (1.0 + m_ref[scale_slot:scale_slot + 1, :]) + m_ref[shift_slot:shift_slot + 1, :]
        hi = xv.astype(BF16)
        xh_ref[...] = hi
        if hp:
            xl_ref[...] = (xv - hi.astype(F32)).astype(BF16)
        if emit_xn:
            xn_ref[...] = xv.astype(xn_ref.dtype)

    acc = _dg(xh_ref[...], wh_ref[...])
    if hp:
        acc = acc + (_dg(xh_ref[...], wl_ref[...]) + _dg(xl_ref[...], wh_ref[...]))
    o_ref[...] = acc


def _mm_norm(x, w, *, k=None, xcol=0, norm_g=None, mods=None, slots=(0, 1), rows_per_mod=None,
             hp=False, emit_xn=None, tm=512, tn_cap=1664):
    M = x.shape[0]
    K = k if k is not None else x.shape[1]
    ws = tuple(w) if isinstance(w, (tuple, list)) else (w,)
    N = ws[0].shape[1]
    tm = min(tm, M)
    if mods is not None and mods.shape[0] > 1:
        tm = min(tm, rows_per_mod)
    tn = _pick_tile(N, tn_cap)
    has_norm = norm_g is not None
    has_mod = mods is not None
    rpm = rows_per_mod if rows_per_mod is not None else M

    in_specs = [pl.BlockSpec((tm, K), lambda i, j: (i, xcol))]
    args = [x]
    if has_norm:
        in_specs.append(pl.BlockSpec((1, K), lambda i, j: (0, 0)))
        args.append(norm_g.reshape(1, K).astype(F32))
    if has_mod:
        if mods.shape[0] > 1:
            in_specs.append(pl.BlockSpec((None, 6, K), lambda i, j: ((i * tm) // rpm, 0, 0)))
        else:
            in_specs.append(pl.BlockSpec((None, 6, K), lambda i, j: (0, 0, 0)))
        args.append(mods)
    for wi in ws:
        in_specs.append(pl.BlockSpec((K, tn), lambda i, j: (0, j)))
        args.append(wi)
    out_shape = [jax.ShapeDtypeStruct((M, N), F32)]
    out_specs = [pl.BlockSpec((tm, tn), lambda i, j: (i, j))]
    if emit_xn is not None:
        out_shape.append(jax.ShapeDtypeStruct((M, K), emit_xn))
        out_specs.append(pl.BlockSpec((tm, K), lambda i, j: (i, 0)))
    scratch = [pltpu.VMEM((tm, K), BF16)]
    if hp:
        scratch.append(pltpu.VMEM((tm, K), BF16))
    kern = functools.partial(_mm_norm_kernel, has_norm=has_norm, has_mod=has_mod, shift_slot=slots[0],
                             scale_slot=slots[1], hp=hp, emit_xn=emit_xn is not None)
    outs = pl.pallas_call(
        kern, grid=(M // tm, N // tn), in_specs=in_specs, out_specs=out_specs, out_shape=out_shape,
        scratch_shapes=scratch, compiler_params=_cparams(("parallel", "arbitrary")))(*args)
    return outs if emit_xn is not None else outs[0]


def _mm_res_kernel(*refs, n_in, gate_slot):
    a_refs = refs[:n_in]
    w_refs = refs[n_in:2 * n_in]
    res_ref, m_ref, o_ref = refs[2 * n_in:]
    acc = None
    for a_ref, w_ref in zip(a_refs, w_refs):
        part = _dg(a_ref[...].astype(BF16), w_ref[...])
        acc = part if acc is None else acc + part
    o_ref[...] = res_ref[...] + m_ref[gate_slot:gate_slot + 1, :] * acc


def _mm_res(a_list, w_list, res, mods, gate_slot, rows_per_mod, tm=512, tn=1024):
    M, N = res.shape
    tm = min(tm, M, rows_per_mod) if mods.shape[0] > 1 else min(tm, M)
    tn = min(tn, N)
    n_in = len(a_list)
    in_specs, args = [], []
    for a in a_list:
        in_specs.append(pl.BlockSpec((tm, a.shape[1]), lambda i, j: (i, 0)))
        args.append(a)
    for w in w_list:
        in_specs.append(pl.BlockSpec((w.shape[0], tn), lambda i, j: (0, j)))
        args.append(w)
    in_specs.append(pl.BlockSpec((tm, tn), lambda i, j: (i, j)))
    args.append(res)
    if mods.shape[0] > 1:
        in_specs.append(pl.BlockSpec((None, 6, tn), lambda i, j: ((i * tm) // rows_per_mod, 0, j)))
    else:
        in_specs.append(pl.BlockSpec((None, 6, tn), lambda i, j: (0, 0, j)))
    args.append(mods)
    return pl.pallas_call(
        functools.partial(_mm_res_kernel, n_in=n_in, gate_slot=gate_slot),
        grid=(M // tm, N // tn), in_specs=in_specs,
        out_specs=pl.BlockSpec((tm, tn), lambda i, j: (i, j)),
        out_shape=jax.ShapeDtypeStruct((M, N), F32),
        compiler_params=_cparams(("parallel", "arbitrary")))(*args)


def _rmsnorm_kernel(x_ref, g_ref, o_ref):
    xv = x_ref[...]
    o_ref[...] = xv * lax.rsqrt(jnp.mean(xv * xv, axis=-1, keepdims=True) + NORM_EPS) * g_ref[...]


def _rmsnorm(x, g, tm=512):
    M, K = x.shape
    tm = min(tm, M)
    return pl.pallas_call(
        _rmsnorm_kernel, grid=(M // tm,),
        in_specs=[pl.BlockSpec((tm, K), lambda i: (i, 0)), pl.BlockSpec((1, K), lambda i: (0, 0))],
        out_specs=pl.BlockSpec((tm, K), lambda i: (i, 0)),
        out_shape=jax.ShapeDtypeStruct((M, K), F32),
        compiler_params=_cparams(("parallel",)))(x, g.reshape(1, K))


def _rope_tables(T, rot_dim):
    rows = T // GRID_W
    row = jnp.repeat(jnp.arange(rows, dtype=F32), GRID_W)
    col = jnp.tile(jnp.arange(GRID_W, dtype=F32), rows)
    n_freq = rot_dim // 4
    inv = ROPE_THETA ** (-jnp.arange(n_freq, dtype=F32) / n_freq)
    ang = jnp.concatenate([row[:, None] * inv, col[:, None] * inv], axis=-1)
    cos = jnp.repeat(jnp.cos(ang), 2, axis=-1)
    sin = jnp.repeat(jnp.sin(ang), 2, axis=-1) * jnp.tile(jnp.array([-1.0, 1.0], F32), rot_dim // 2)
    pad = LANES - rot_dim
    if pad:
        cos = jnp.concatenate([cos, jnp.ones((T, pad), F32)], axis=-1)
        sin = jnp.concatenate([sin, jnp.zeros((T, pad), F32)], axis=-1)
    return cos, sin


def _rope(y, cosf, sinf):
    lane = lax.broadcasted_iota(jnp.int32, y.shape, 1)
    even = (lane & 1) == 0
    width = y.shape[1]
    swap = jnp.where(even, pltpu.roll(y, width - 1, 1), pltpu.roll(y, 1, 1))
    return y * cosf + swap * sinf


def _norm_rope_kernel(*refs, nblk, norm, rope):
    it = iter(refs)
    x_ref = next(it)
    g_ref = next(it) if norm else None
    cos_ref = next(it) if rope else None
    sin_ref = next(it) if rope else None
    o_ref = next(it)
    for c in range(nblk):
        sl = slice(c * LANES, (c + 1) * LANES)
        y = x_ref[:, sl]
        if norm:
            y = y * lax.rsqrt(jnp.mean(y * y, axis=-1, keepdims=True) + NORM_EPS) * g_ref[:, sl]
        if rope:
            y = _rope(y, cos_ref[...], sin_ref[...])
        o_ref[:, sl] = y


def _norm_rope(x, nblk, T, gains=None, tables=None, tt=256):
    M = x.shape[0]
    tt = min(tt, T)
    W = nblk * LANES
    in_specs = [pl.BlockSpec((tt, W), lambda i: (i, 0))]
    args = [x]
    if gains is not None:
        in_specs.append(pl.BlockSpec((1, W), lambda i: (0, 0)))
        args.append(gains.reshape(1, W))
    if tables is not None:
        nt = T // tt
        for t in tables:
            in_specs.append(pl.BlockSpec((tt, LANES), lambda i: (i % nt, 0)))
            args.append(t)
    return pl.pallas_call(
        functools.partial(_norm_rope_kernel, nblk=nblk, norm=gains is not None, rope=tables is not None),
        grid=(M // tt,), in_specs=in_specs,
        out_specs=pl.BlockSpec((tt, W), lambda i: (i, 0)),
        out_shape=jax.ShapeDtypeStruct((M, W), F32),
        compiler_params=_cparams(("parallel",)))(*args)


def _attn_kernel(*refs, scale, has2):
    it = iter(refs)
    q_ref = next(it)
    q2_ref = next(it) if has2 else None
    k_ref = next(it)
    k2_ref = next(it) if has2 else None
    v_ref = next(it)
    o_ref = next(it)
    if has2:
        qq = jnp.concatenate([q_ref[...], q2_ref[...]], axis=1) * scale
        s = _dot_bf(qq, jnp.concatenate([k_ref[...], k2_ref[...]], axis=1), NT)
    else:
        s = _dot_bf(q_ref[...] * scale, k_ref[...], NT)
    m = jnp.max(s, axis=-1, keepdims=True)
    p = jnp.exp(s - m)
    l = jnp.sum(p, axis=-1, keepdims=True)
    o_ref[...] = _dot_bf(p, v_ref[...]) / l


def _attention(B, T, Tk, H, scale, q, k, v, q2=None, k2=None, tq=256):
    tq = min(tq, T)
    nq = T // tq
    has2 = q2 is not None

    def qspec(cf):
        return pl.BlockSpec((tq, LANES), lambda b, h, i: (b * nq + i, cf(h)))

    def kspec(cf):
        return pl.BlockSpec((Tk, LANES), lambda b, h, i: (b, cf(h)))

    in_specs, args = [qspec(q[1])], [q[0]]
    if has2:
        in_specs.append(qspec(q2[1]))
        args.append(q2[0])
    in_specs.append(kspec(k[1]))
    args.append(k[0])
    if has2:
        in_specs.append(kspec(k2[1]))
        args.append(k2[0])
    in_specs.append(kspec(v[1]))
    args.append(v[0])
    return pl.pallas_call(
        functools.partial(_attn_kernel, scale=scale, has2=has2),
        grid=(B, H, nq), in_specs=in_specs,
        out_specs=pl.BlockSpec((tq, LANES), lambda b, h, i: (b * nq + i, h)),
        out_shape=jax.ShapeDtypeStruct((B * T, H * LANES), F32),
        compiler_params=_cparams(("parallel", "arbitrary", "arbitrary")))(*args)


def _tri_masks(L, rev):
    i = lax.broadcasted_iota(jnp.int32, (L, L), 0)
    j = lax.broadcasted_iota(jnp.int32, (L, L), 1)
    if rev:
        return j >= i, j > i, i == j
    return j <= i, j < i, i == j


def _inv_unit_tri(mats, eye_f, L):
    pows = [[-a for a in mats]]
    for _ in range(int(math.log2(L)) - 1):
        pows.append([_dot_bf(p, p) for p in pows[-1]])
    terms = [[eye_f + p for p in level] for level in pows]
    while len(terms) > 1:
        nxt = [[_dot_bf(x, y) for x, y in zip(terms[i], terms[i + 1])] for i in range(0, len(terms) - 1, 2)]
        if len(terms) % 2:
            nxt.append(terms[-1])
        terms = nxt
    x0 = terms[0]
    resid = [(eye_f - x) - _dot_hp(a, x) for a, x in zip(mats, x0)]
    return [x + _dot_bf(x, r) for x, r in zip(x0, resid)]


def _shifted(f, prev_ref, next_ref, tt, T):
    i = pl.program_id(0)
    first = (i * tt) % T == 0
    last = ((i + 1) * tt) % T == 0
    prev = jnp.where(first, 0.0, prev_ref[7:8, :])
    nxt = jnp.where(last, 0.0, next_ref[0:1, :])
    rows = lax.broadcasted_iota(jnp.int32, f.shape, 0)
    fp = jnp.where(rows == 0, prev, pltpu.roll(f, 1, 0))
    fn = jnp.where(rows == tt - 1, nxt, pltpu.roll(f, tt - 1, 0))
    return fp, fn


def _halo_specs(tt, width, M, col_fn):
    r8 = tt // 8
    last8 = M // 8 - 1
    return [
        pl.BlockSpec((tt, width), lambda *g: (g[0], col_fn(*g))),
        pl.BlockSpec((8, width), lambda *g: (jnp.maximum(g[0] * r8 - 1, 0), col_fn(*g))),
        pl.BlockSpec((8, width), lambda *g: (jnp.minimum((g[0] + 1) * r8, last8), col_fn(*g))),
    ]


NPAIR = B_HEADS // 2


def _pair_sum_matrix():
    r = lax.broadcasted_iota(jnp.int32, (LANES, LANES), 0) // B_HEAD_DIM
    c = lax.broadcasted_iota(jnp.int32, (LANES, LANES), 1) // B_HEAD_DIM
    return (r == c).astype(F32)


def _rwkv_prep_kernel(x_ref, prev_ref, next_ref, mu_ref, w0_ref, w2_ref, a0_ref, a2_ref, g2_ref, kk_ref_in,
                      r_o, k_o, v_o, kk_o, g_o, lw0_o, lw1_o, a0_o, a1_o, *, tt, T):
    f = x_ref[...]
    fp, fn = _shifted(f, prev_ref, next_ref, tt, T)
    f = f + mu_ref[0:1, :] * (fp - f) + mu_ref[1:2, :] * (fn - f)
    r = f[:, :B_W]
    k = f[:, B_W:2 * B_W]
    v = f[:, 2 * B_W:3 * B_W]
    off = 3 * B_W
    w_lo = f[:, off:off + 2 * B_DECAY_LORA]
    off += 2 * B_DECAY_LORA
    a_lo = f[:, off:off + 2 * B_AAA_LORA]
    off += 2 * B_AAA_LORA
    g_lo = f[:, off:]
    th = jnp.tanh(w_lo)
    logw, aa = [], []
    for d in range(2):
        wpre = w0_ref[d:d + 1, :] + _dot_hp(th[:, d * B_DECAY_LORA:(d + 1) * B_DECAY_LORA], w2_ref[d])
        wv = -_softplus(-wpre) - 0.5
        logw.append(-jnp.exp(wv))
        aa.append(_sigmoid(a0_ref[d:d + 1, :] + _dot_hp(a_lo[:, d * B_AAA_LORA:(d + 1) * B_AAA_LORA], a2_ref[d])))
    g = _dot_hp(_sigmoid(g_lo), g2_ref[...])
    kk = k * kk_ref_in[...]
    gsum = _pair_sum_matrix()
    for p in range(NPAIR):
        sl = slice(p * LANES, (p + 1) * LANES)
        kp = kk[:, sl]
        ssq = _dot_hp(kp * kp, gsum)
        kk_o[p] = kp / jnp.maximum(jnp.sqrt(ssq), 1e-12)
        r_o[p] = r[:, sl]
        k_o[p] = k[:, sl]
        v_o[p] = v[:, sl]
        g_o[p] = g[:, sl]
        lw0_o[p] = logw[0][:, sl]
        lw1_o[p] = logw[1][:, sl]
        a0_o[p] = aa[0][:, sl]
        a1_o[p] = aa[1][:, sl]


def _rwkv_prep(pb, B, T, mu, w0, w2, a0, a2, g2, k_k, tt=128):
    M = pb.shape[0]
    tt = min(tt, T)
    nt = T // tt
    full = lambda shape: pl.BlockSpec(shape, lambda i: tuple(0 for _ in shape))
    in_specs = _halo_specs(tt, B_IN, M, lambda i: 0) + [
        full((2, B_IN)), full((2, B_W)), full((2, B_DECAY_LORA, B_W)), full((2, B_W)),
        full((2, B_AAA_LORA, B_W)), full((B_GATE_LORA, B_W)), full((1, B_W))]
    ospec = pl.BlockSpec((None, NPAIR, tt, LANES), lambda i: (i // nt, 0, i % nt, 0))
    oshape = jax.ShapeDtypeStruct((B, NPAIR, T, LANES), F32)
    return pl.pallas_call(
        functools.partial(_rwkv_prep_kernel, tt=tt, T=T),
        grid=(M // tt,), in_specs=in_specs, out_specs=[ospec] * 9, out_shape=[oshape] * 9,
        compiler_params=_cparams(("parallel",)))(pb, pb, pb, mu, w0, w2, a0, a2, g2, k_k.reshape(1, B_W))


def _rwkv_chunk(r2, lw2, kd2, v2, kk2, a2, states, masks, L, rev):
    incl, strict, eye = masks
    incl_f = incl.astype(F32)
    eye_f = eye.astype(F32)
    i2 = lax.broadcasted_iota(jnp.int32, (L, 2 * L), 0)
    j2 = lax.broadcasted_iota(jnp.int32, (L, 2 * L), 1) % L
    incl2 = (j2 >= i2) if rev else (j2 <= i2)
    last = 0 if rev else L - 1
    halves = (slice(0, B_HEAD_DIM), slice(B_HEAD_DIM, 2 * B_HEAD_DIM))
    logp2 = [_dot_hp(incl_f, lw) for lw in lw2]
    lhs, rhs, rem_rhs, dec, vs = [], [], [], [], []
    for lp, lw, r, kd, v, kk, a in zip(logp2, lw2, r2, kd2, v2, kk2, a2):
        ninv = jnp.exp(-lp)
        b = kk * a
        kt = kk * jnp.exp(lp - lw)
        rt = r * jnp.exp(lp)
        bn, kdn = b * ninv, kd * ninv
        pl_row = lp[last:last + 1, :]
        rem = jnp.exp(pl_row - lp)
        bh, kh = b * rem, kd * rem
        dpl = jnp.exp(pl_row)
        for sl in halves:
            lhs.append(jnp.concatenate([kt[:, sl], rt[:, sl]], axis=0))
            rhs.append(jnp.concatenate([bn[:, sl], kdn[:, sl]], axis=0))
            rem_rhs.append(jnp.concatenate([bh[:, sl], kh[:, sl]], axis=0))
            dec.append(dpl[:, sl])
            vs.append(v[:, sl])
    gm = [_dot_hp(x, y, NT) for x, y in zip(lhs, rhs)]
    ls = [_dot_hp(x, s, NT) for x, s in zip(lhs, states)]
    akv = [_dot_hp(jnp.where(strict, g[:L, L:], 0.0), v) for g, v in zip(gm, vs)]
    tinv = _inv_unit_tri([jnp.where(strict, g[:L, :L], 0.0) for g in gm], eye_f, L)
    u = [-_dot_hp(t, l[:L] + x) for t, l, x in zip(tinv, ls, akv)]
    uv = [jnp.concatenate([x, v], axis=0) for x, v in zip(u, vs)]
    y = [l[L:] + _dot_hp(jnp.where(incl2, g[L:, :], 0.0), x) for l, g, x in zip(ls, gm, uv)]
    s_new = [s * d + _dot_hp(x, m, TN) for s, d, x, m in zip(states, dec, uv, rem_rhs)]
    y2 = [jnp.concatenate([y[2 * p], y[2 * p + 1]], axis=1) for p in range(len(r2))]
    return y2, s_new


def _rwkv_scan_kernel(r_ref, v_ref, kk_ref, k_ref, lw_ref, a_ref, ka_ref, s0_ref, y_ref, sf_ref, s_ref, *, L, rev):
    c = pl.program_id(1)

    @pl.when(c == 0)
    def _():
        s_ref[...] = s0_ref[...]

    masks = _tri_masks(L, rev)
    pairs = range(NPAIR)
    a2 = [a_ref[p] for p in pairs]
    kd2 = [k_ref[p] * (1.0 + (a - 1.0) * ka_ref[p]) for p, a in zip(pairs, a2)]
    y2, s_new = _rwkv_chunk([r_ref[p] for p in pairs], [lw_ref[p] for p in pairs], kd2, [v_ref[p] for p in pairs],
                            [kk_ref[p] for p in pairs], a2, [s_ref[h] for h in range(B_HEADS)], masks, L, rev)
    for p in pairs:
        y_ref[p] = y2[p]
    for h in range(B_HEADS):
        s_ref[h] = s_new[h]

    @pl.when(c == pl.num_programs(1) - 1)
    def _():
        sf_ref[...] = s_ref[...]


def _rwkv_scan(r, v, kk, k, lw, a, k_a, s0, rev, L=CHUNK):
    B, _, T, _ = r.shape
    nc = T // L
    cmap = (lambda b, c: (b, 0, nc - 1 - c, 0)) if rev else (lambda b, c: (b, 0, c, 0))
    seq = pl.BlockSpec((None, NPAIR, L, LANES), cmap)
    st = pl.BlockSpec((None, B_HEADS, B_HEAD_DIM, B_HEAD_DIM), lambda b, c: (b, 0, 0, 0))
    return pl.pallas_call(
        functools.partial(_rwkv_scan_kernel, L=L, rev=rev),
        grid=(B, nc),
        in_specs=[seq] * 6 + [pl.BlockSpec((NPAIR, 1, LANES), lambda b, c: (0, 0, 0)), st],
        out_specs=[seq, st],
        out_shape=[jax.ShapeDtypeStruct((B, NPAIR, T, LANES), F32),
                   jax.ShapeDtypeStruct((B, B_HEADS, B_HEAD_DIM, B_HEAD_DIM), F32)],
        scratch_shapes=[pltpu.VMEM((B_HEADS, B_HEAD_DIM, B_HEAD_DIM), F32)],
        compiler_params=_cparams(("parallel", "arbitrary")))(r, v, kk, k, lw, a, k_a.reshape(NPAIR, 1, LANES), s0)


def _rwkv_post_kernel(y0_ref, y1_ref, r_ref, k_ref, v_ref, g_ref, rk_ref, lnw_ref, lnb_ref, o_ref):
    gsum = _pair_sum_matrix()
    inv_n = 1.0 / B_HEAD_DIM
    for p in range(NPAIR):
        y = y0_ref[p] + y1_ref[p]
        mean = _dot_hp(y, gsum) * inv_n
        yc = y - mean
        var = _dot_hp(yc * yc, gsum) * inv_n
        yn = yc * lax.rsqrt(var + B_GN_EPS) * lnw_ref[p] + lnb_ref[p]
        bonus = _dot_hp(r_ref[p] * k_ref[p] * rk_ref[p], gsum) * v_ref[p]
        o_ref[:, p * LANES:(p + 1) * LANES] = (yn + bonus) * g_ref[p]


def _rwkv_post(y0, y1, r, k, v, g, r_k, ln_w, ln_b, tt=256):
    B, _, T, _ = r.shape
    tt = min(tt, T)
    nt = T // tt
    seq = pl.BlockSpec((None, NPAIR, tt, LANES), lambda i: (i // nt, 0, i % nt, 0))
    par = pl.BlockSpec((NPAIR, 1, LANES), lambda i: (0, 0, 0))
    return pl.pallas_call(
        _rwkv_post_kernel, grid=(B * nt,),
        in_specs=[seq] * 6 + [par] * 3,
        out_specs=pl.BlockSpec((tt, B_W), lambda i: (i, 0)),
        out_shape=jax.ShapeDtypeStruct((B * T, B_W), F32),
        compiler_params=_cparams(("parallel",)))(
            y0, y1, r, k, v, g, r_k.reshape(NPAIR, 1, LANES), ln_w.reshape(NPAIR, 1, LANES),
            ln_b.reshape(NPAIR, 1, LANES))


def _rwkv_mixer(pb, B, T, prm, s0):
    r, k, v, kk, g, lw0, lw1, a0, a1 = _rwkv_prep(pb, B, T, prm['mu'], prm['w0'], prm['w2'], prm['a0'],
                                                   prm['a2'], prm['g2'], prm['k_k'])
    if s0 is None:
        s0 = jnp.zeros((B, 2, B_HEADS, B_HEAD_DIM, B_HEAD_DIM), F32)
    y0, sf0 = _rwkv_scan(r, v, kk, k, lw0, a0, prm['k_a'], s0[:, 0], rev=False)
    y1, sf1 = _rwkv_scan(r, v, kk, k, lw1, a1, prm['k_a'], s0[:, 1], rev=True)
    out = _rwkv_post(y0, y1, r, k, v, g, prm['r_k'], prm['ln_w'], prm['ln_b'])
    return out, jnp.stack([sf0, sf1], axis=1)


def _gdn_prep_kernel(x_ref, prev_ref, next_ref, cw_ref, o_ref, *, tt, T):
    c = pl.program_id(1)
    x = x_ref[...]
    xp, xn = _shifted(x, prev_ref, next_ref, tt, T)
    y = cw_ref[0:1, :] * xp + cw_ref[1:2, :] * x + cw_ref[2:3, :] * xn
    y = y * _sigmoid(y)

    @pl.when(c == 2)
    def _():
        o_ref[...] = y

    @pl.when(c < 2)
    def _():
        scale = jnp.where(c == 0, C_HEAD_DIM ** -0.5, 1.0).astype(F32)
        for h in range(C_HEADS):
            sl = slice(h * C_HEAD_DIM, (h + 1) * C_HEAD_DIM)
            yh = y[:, sl]
            o_ref[:, sl] = yh * lax.rsqrt(jnp.sum(yh * yh, axis=-1, keepdims=True) + 1e-6) * scale


def _gdn_prep(pc, T, conv_w, tt=256):
    M = pc.shape[0]
    tt = min(tt, T)
    in_specs = _halo_specs(tt, C_W, M, lambda i, c: c) + [pl.BlockSpec((3, C_W), lambda i, c: (0, c))]
    return pl.pallas_call(
        functools.partial(_gdn_prep_kernel, tt=tt, T=T),
        grid=(M // tt, 3), in_specs=in_specs,
        out_specs=pl.BlockSpec((tt, C_W), lambda i, c: (i, c)),
        out_shape=jax.ShapeDtypeStruct((M, 3 * C_W), F32),
        compiler_params=_cparams(("parallel", "arbitrary")))(pc, pc, pc, conv_w)


def _gdn_bg_kernel(x_ref, alog_ref, dtb_ref, o_ref):
    x = x_ref[...]
    lane = lax.broadcasted_iota(jnp.int32, x.shape, 1)
    beta = _sigmoid(x)
    g = -jnp.exp(alog_ref[...]) * _softplus(x + dtb_ref[...])
    o_ref[...] = jnp.where(lane < 2 * C_HEADS, beta, g)


def _gdn_bg(pc, a_log, dt_bias, tt=512):
    M = pc.shape[0]
    tt = min(tt, M)
    nb = 2 * C_HEADS
    pad = lambda p: jnp.concatenate([jnp.zeros((nb,), F32), p.reshape(nb), jnp.zeros((LANES - 2 * nb,), F32)]).reshape(1, LANES)
    return pl.pallas_call(
        _gdn_bg_kernel, grid=(M // tt,),
        in_specs=[pl.BlockSpec((tt, LANES), lambda i: (i, 4 * C_W // LANES)),
                  pl.BlockSpec((1, LANES), lambda i: (0, 0)), pl.BlockSpec((1, LANES), lambda i: (0, 0))],
        out_specs=pl.BlockSpec((tt, LANES), lambda i: (i, 0)),
        out_shape=jax.ShapeDtypeStruct((M, LANES), F32),
        compiler_params=_cparams(("parallel",)))(pc, pad(a_log), pad(dt_bias))


def _gdn_chunk(qs, ks, vs, bcol, gcol, grow, states, masks, L, rev):
    incl, strict, eye = masks
    incl_f = incl.astype(F32)
    eye_f = eye.astype(F32)
    nh = len(qs)
    i = lax.broadcasted_iota(jnp.int32, (L, L), 0)
    j = lax.broadcasted_iota(jnp.int32, (L, L), 1)
    cum_t = ((i >= j) if rev else (i <= j)).astype(F32)
    gc_cols = _dot_hp(incl_f, gcol)
    gc_rows = _dot_hp(grow, cum_t)
    last = 0 if rev else L - 1
    gcs, decays, kbs, rhss, kends, g_ends = [], [], [], [], [], []
    for h in range(nh):
        gc = jnp.broadcast_to(gc_cols[:, h:h + 1], (L, LANES))
        decays.append(jnp.exp(jnp.where(incl, gc[:, :L] - gc_rows[h:h + 1, :], -1e30)))
        beta = bcol[:, h:h + 1]
        kb = ks[h] * beta
        kbs.append(kb)
        rhss.append(jnp.concatenate([vs[h] * beta, kb * jnp.exp(gc)], axis=1))
        g_end = gc[last:last + 1, :]
        kends.append(ks[h] * jnp.exp(g_end - gc))
        g_ends.append(jnp.exp(g_end))
        gcs.append(jnp.exp(gc))
    kk = [_dot_hp(kb, k, NT) for kb, k in zip(kbs, ks)]
    qk = [_dot_hp(q, k, NT) for q, k in zip(qs, ks)]
    qs_s = [_dot_hp(q * e, s) for q, e, s in zip(qs, gcs, states)]
    tinv = _inv_unit_tri([jnp.where(strict, x * d, 0.0) for x, d in zip(kk, decays)], eye_f, L)
    sol = [_dot_hp(t, r) for t, r in zip(tinv, rhss)]
    v_new = [x[:, :C_HEAD_DIM] - _dot_hp(x[:, C_HEAD_DIM:], s) for x, s in zip(sol, states)]
    o = [a + _dot_hp(jnp.where(incl, x * d, 0.0), vn) for a, x, d, vn in zip(qs_s, qk, decays, v_new)]
    s_new = [s * ge + _dot_hp(ke, vn, TN) for s, ge, ke, vn in zip(states, g_ends, kends, v_new)]
    return o, s_new


def _gdn_scan_kernel(q_ref, k_ref, v_ref, bcol_ref, gcol_ref, grow_ref, s0_ref, o_ref, sf_ref, s_ref, *, L, rev):
    c = pl.program_id(1)

    @pl.when(c == 0)
    def _():
        s_ref[...] = s0_ref[...]

    masks = _tri_masks(L, rev)
    heads = range(C_HEADS)
    hs = [slice(h * C_HEAD_DIM, (h + 1) * C_HEAD_DIM) for h in heads]
    o, s_new = _gdn_chunk([q_ref[:, sl] for sl in hs], [k_ref[:, sl] for sl in hs], [v_ref[:, sl] for sl in hs],
                          bcol_ref[...], gcol_ref[...], grow_ref[...], [s_ref[h] for h in heads], masks, L, rev)
    for h in heads:
        o_ref[:, hs[h]] = o[h]
        s_ref[h] = s_new[h]

    @pl.when(c == pl.num_programs(1) - 1)
    def _():
        sf_ref[...] = s_ref[...]


def _gdn_scan(qkv, bcol, gcol, grow, s0, B, T, rev, L=CHUNK):
    nc = T // L
    cidx = (lambda c: nc - 1 - c) if rev else (lambda c: c)
    seq = lambda col: pl.BlockSpec((L, C_W), lambda b, c: (b * nc + cidx(c), col))
    small = pl.BlockSpec((L, C_HEADS), lambda b, c: (b * nc + cidx(c), 0))
    st = pl.BlockSpec((None, C_HEADS, C_HEAD_DIM, C_HEAD_DIM), lambda b, c: (b, 0, 0, 0))
    return pl.pallas_call(
        functools.partial(_gdn_scan_kernel, L=L, rev=rev),
        grid=(B, nc),
        in_specs=[seq(0), seq(1), seq(2), small, small,
                  pl.BlockSpec((None, C_HEADS, L), lambda b, c: (b * nc + cidx(c), 0, 0)), st],
        out_specs=[pl.BlockSpec((L, C_W), lambda b, c: (b * nc + cidx(c), 0)), st],
        out_shape=[jax.ShapeDtypeStruct((B * T, C_W), F32),
                   jax.ShapeDtypeStruct((B, C_HEADS, C_HEAD_DIM, C_HEAD_DIM), F32)],
        scratch_shapes=[pltpu.VMEM((C_HEADS, C_HEAD_DIM, C_HEAD_DIM), F32)],
        compiler_params=_cparams(("parallel", "arbitrary")))(qkv, qkv, qkv, bcol, gcol, grow, s0)


def _gdn_post_kernel(o0_ref, o1_ref, z_ref, g_ref, out_ref):
    for h in range(C_HEADS):
        sl = slice(h * C_HEAD_DIM, (h + 1) * C_HEAD_DIM)
        y = o0_ref[:, sl] + o1_ref[:, sl]
        y = y * lax.rsqrt(jnp.mean(y * y, axis=-1, keepdims=True) + NORM_EPS) * g_ref[...]
        z = z_ref[:, sl]
        out_ref[:, sl] = y * (z * _sigmoid(z))


def _gdn_post(o0, o1, pc, norm_g, tt=256):
    M = o0.shape[0]
    tt = min(tt, M)
    return pl.pallas_call(
        _gdn_post_kernel, grid=(M // tt,),
        in_specs=[pl.BlockSpec((tt, C_W), lambda i: (i, 0)), pl.BlockSpec((tt, C_W), lambda i: (i, 0)),
                  pl.BlockSpec((tt, C_W), lambda i: (i, 3)), pl.BlockSpec((1, C_HEAD_DIM), lambda i: (0, 0))],
        out_specs=pl.BlockSpec((tt, C_W), lambda i: (i, 0)),
        out_shape=jax.ShapeDtypeStruct((M, C_W), F32),
        compiler_params=_cparams(("parallel",)))(o0, o1, pc, norm_g.reshape(1, C_HEAD_DIM))


def _gdn_mixer(pc, B, T, prm, s0, L=CHUNK):
    M = pc.shape[0]
    nc = T // L
    qkv = _gdn_prep(pc, T, prm['conv'])
    bg = _gdn_bg(pc, prm['a_log'], prm['dt_bias'])
    if s0 is None:
        s0 = jnp.zeros((B, 2, C_HEADS, C_HEAD_DIM, C_HEAD_DIM), F32)
    outs, finals = [], []
    for d in range(2):
        bcol = bg[:, d * C_HEADS:(d + 1) * C_HEADS]
        gcol = bg[:, 2 * C_HEADS + d * C_HEADS:2 * C_HEADS + (d + 1) * C_HEADS]
        grow = gcol.reshape(B * nc, L, C_HEADS).transpose(0, 2, 1)
        o, sf = _gdn_scan(qkv, bcol, gcol, grow, s0[:, d], B, T, rev=(d == 1))
        outs.append(o)
        finals.append(sf)
    y = _gdn_post(outs[0], outs[1], pc, prm['norm_g'])
    return y, jnp.stack(finals, axis=1)


def _top_values(tile, count, need_mask):
    rows = lax.broadcasted_iota(jnp.int32, tile.shape, 0).astype(F32)
    big = float(tile.shape[0])
    cur = tile
    vals = []
    for _ in range(count):
        m = jnp.max(cur, axis=0, keepdims=True)
        vals.append(m)
        idx = jnp.min(jnp.where(cur == m, rows, big), axis=0, keepdims=True)
        cur = jnp.where(rows == idx, -jnp.inf, cur)
    return vals, (cur == -jnp.inf) if need_mask else None


PEER_EB = 1024
PEER_GRP = MXU_K // PEER_HALF
PEER_NGRP = 2 * PEER_HEADS // PEER_GRP


def _peer_topk_kernel(q_ref, kh_ref, kl_ref, thr_ref, e0_ref, e1_ref, s1_ref, sc_ref):
    rows_g = PEER_GRP * N_KEYS
    for g in range(PEER_NGRP):
        qh, ql = _split(q_ref[:, g * MXU_K:(g + 1) * MXU_K])
        kh = kh_ref[g * rows_g:(g + 1) * rows_g, :]
        kl = kl_ref[g * rows_g:(g + 1) * rows_g, :]
        sc_ref[g * rows_g:(g + 1) * rows_g, :] = _dg(kh, qh, NT) + (_dg(kh, ql, NT) + _dg(kl, qh, NT))

    def head(h, carry):
        t0 = sc_ref[pl.ds(pl.multiple_of(h * 2 * N_KEYS, N_KEYS), N_KEYS), :]
        t1 = sc_ref[pl.ds(pl.multiple_of(h * 2 * N_KEYS + N_KEYS, N_KEYS), N_KEYS), :]
        a, ra = _top_values(t0, PEER_TOPK, True)
        b, rb = _top_values(t1, PEER_TOPK, True)
        s0 = jnp.where(ra, t0, -jnp.inf)
        s1 = jnp.where(rb, t1, -jnp.inf)
        amat = jnp.concatenate(a, axis=0)
        bmat = jnp.concatenate(b, axis=0)
        b_lo = bmat[:8]
        row8 = lax.broadcasted_iota(jnp.int32, b_lo.shape, 0)
        pieces = [a[0] + b_lo, a[0] + bmat[8:], a[1] + b_lo]
        for r in range(2, 8):
            pieces.append(jnp.where(row8 < PEER_TOPK // (r + 1), a[r] + b_lo, -jnp.inf))
        pieces.append(amat[8:] + b[0])
        cand = jnp.concatenate(pieces, axis=0)
        best, _ = _top_values(cand, PEER_TOPK + 1, False)
        z = jnp.zeros_like(best[0])
        for n in range(PEER_TOPK):
            z = z + jnp.exp(best[n] - best[0])
        tau = 0.5 * (best[PEER_TOPK - 1] + best[PEER_TOPK])
        thr = tau - s0
        e0 = jnp.exp(s0 - a[0]) * (1.0 / z)
        e1 = jnp.exp(s1 - b[0])
        for tc in range(t0.shape[1] // LANES):
            cols = slice(tc * LANES, (tc + 1) * LANES)
            thr_ref[h, tc] = thr[:, cols]
            e0_ref[h, tc] = e0[:, cols]
            e1_ref[h, tc] = e1[:, cols]
            s1_ref[h, tc] = s1[:, cols]
        return carry

    lax.fori_loop(0, PEER_HEADS, head, 0)


def _peer_topk(q, kdt_hi, kdt_lo, tt=256):
    M = q.shape[0]
    tt = min(tt, M)
    R = PEER_HEADS * 2 * N_KEYS
    big = pl.BlockSpec((PEER_HEADS, tt // LANES, N_KEYS, LANES), lambda i: (0, i, 0, 0))
    return pl.pallas_call(
        _peer_topk_kernel, grid=(M // tt,),
        in_specs=[pl.BlockSpec((tt, PEER_HEADS * PEER_QDIM), lambda i: (i, 0)),
                  pl.BlockSpec((R, MXU_K), lambda i: (0, 0)),
                  pl.BlockSpec((R, MXU_K), lambda i: (0, 0))],
        out_specs=[big] * 4,
        out_shape=[jax.ShapeDtypeStruct((PEER_HEADS, M // LANES, N_KEYS, LANES), F32)] * 4,
        scratch_shapes=[pltpu.VMEM((R, tt), F32)],
        compiler_params=_cparams(("parallel",)))(q, kdt_hi, kdt_lo)


def _peer_expert_kernel(xt_ref, u_ref, vt_ref, thr_ref, e0_ref, e1_ref, s1_ref, res_ref, m_ref, o_ref,
                        acc_ref, act0_ref, act1_ref, am0_ref, am1_ref, *, ni, gate_slot):
    e = pl.program_id(1)
    n_blocks = pl.num_programs(1) - 2
    tt = xt_ref.shape[1]
    even = e % 2 == 0
    odd = e % 2 == 1

    @pl.when(e == 0)
    def _():
        acc_ref[...] = jnp.zeros_like(acc_ref)

    def project(act_w):
        act_w[...] = _dg(u_ref[...], xt_ref[...])

    def weigh(act_r, am_w):
        first = (e - 1) * ni
        for ii in range(ni):
            i = first + ii
            rows = slice(ii * N_KEYS, (ii + 1) * N_KEYS)
            for tc in range(tt // LANES):
                cols = slice(tc * LANES, (tc + 1) * LANES)
                w = None
                for h in range(PEER_HEADS):
                    keep = s1_ref[h, tc] >= thr_ref[h, tc, pl.ds(i, 1), :]
                    contrib = jnp.where(keep, e1_ref[h, tc], 0.0) * e0_ref[h, tc, pl.ds(i, 1), :]
                    w = contrib if w is None else w + contrib
                a = act_r[rows, cols]
                a = 0.5 * a * (1.0 + lax.erf(a * (2.0 ** -0.5)))
                am_w[rows, cols] = (a * w).astype(BF16)

    def accumulate(am_r):
        acc_ref[...] += _dg(vt_ref[...], am_r[...])

    has_project = e < n_blocks
    has_weigh = jnp.logical_and(e >= 1, e <= n_blocks)
    has_accumulate = e >= 2
    pl.when(jnp.logical_and(has_project, even))(lambda: project(act0_ref))
    pl.when(jnp.logical_and(has_project, odd))(lambda: project(act1_ref))
    pl.when(jnp.logical_and(has_weigh, even))(lambda: weigh(act1_ref, am1_ref))
    pl.when(jnp.logical_and(has_weigh, odd))(lambda: weigh(act0_ref, am0_ref))
    pl.when(jnp.logical_and(has_accumulate, even))(lambda: accumulate(am0_ref))
    pl.when(jnp.logical_and(has_accumulate, odd))(lambda: accumulate(am1_ref))

    @pl.when(e == pl.num_programs(1) - 1)
    def _():
        o_ref[...] = res_ref[...] + m_ref[gate_slot:gate_slot + 1, :] * acc_ref[...].T


def _peer_experts(xn, u_bf, vt_bf, thr, e0, e1, s1m, res, mods, gate_slot, rows_per_mod, tt=512):
    M, D = res.shape
    tt = min(tt, M, rows_per_mod) if mods.shape[0] > 1 else min(tt, M)
    nb, _, eb = vt_bf.shape
    ni = eb // N_KEYS
    once = pl.Buffered(1)
    if mods.shape[0] > 1:
        mspec = pl.BlockSpec((None, 6, D), lambda i, e: ((i * tt) // rows_per_mod, 0, 0), pipeline_mode=once)
    else:
        mspec = pl.BlockSpec((None, 6, D), lambda i, e: (0, 0, 0), pipeline_mode=once)
    big = pl.BlockSpec((PEER_HEADS, tt // LANES, N_KEYS, LANES), lambda i, e: (0, i, 0, 0), pipeline_mode=once)
    return pl.pallas_call(
        functools.partial(_peer_expert_kernel, ni=ni, gate_slot=gate_slot),
        grid=(M // tt, nb + 2),
        in_specs=[pl.BlockSpec((D, tt), lambda i, e: (0, i), pipeline_mode=once),
                  pl.BlockSpec((eb, D), lambda i, e: (jnp.minimum(e, nb - 1), 0)),
                  pl.BlockSpec((None, D, eb), lambda i, e: (jnp.clip(e - 2, 0, nb - 1), 0, 0)),
                  big, big, big, big,
                  pl.BlockSpec((tt, D), lambda i, e: (i, 0), pipeline_mode=once), mspec],
        out_specs=pl.BlockSpec((tt, D), lambda i, e: (i, 0)),
        out_shape=jax.ShapeDtypeStruct((M, D), F32),
        scratch_shapes=[pltpu.VMEM((D, tt), F32), pltpu.VMEM((eb, tt), F32), pltpu.VMEM((eb, tt), F32),
                        pltpu.VMEM((eb, tt), BF16), pltpu.VMEM((eb, tt), BF16)],
        compiler_params=_cparams(("parallel", "arbitrary"), VMEM_LIMIT_BIG))(
            xn, u_bf, vt_bf, thr, e0, e1, s1m, res, mods)


def _peer(x, mods, rows_per_mod, norm_g, prm):
    q, xn = _mm_norm(x, prm['wq'], norm_g=norm_g, mods=mods, slots=(3, 4), rows_per_mod=rows_per_mod,
                     hp=True, emit_xn=BF16)
    thr, e0, e1, s1m = _peer_topk(q, prm['kdt_hi'], prm['kdt_lo'])
    return _peer_experts(xn.T, prm['u'], prm['vt'], thr, e0, e1, s1m, x, mods, 5, rows_per_mod)


def _peer_params(wq, keys, u_tab, v_tab):
    nset = 2 * PEER_HEADS
    kd = jnp.zeros((nset, N_KEYS, PEER_GRP, PEER_HALF), F32)
    sets = jnp.arange(nset)
    kd = kd.at[sets, :, sets % PEER_GRP, :].set(keys.reshape(nset, N_KEYS, PEER_HALF))
    kd_hi, kd_lo = _split(kd.reshape(nset * N_KEYS, MXU_K))
    vt = v_tab.astype(BF16).reshape(N_EXPERTS // PEER_EB, PEER_EB, -1).transpose(0, 2, 1)
    return dict(wq=_split(wq), kdt_hi=kd_hi, kdt_lo=kd_lo, u=u_tab.astype(BF16), vt=vt)


def _layer_even(x, mods, B, T, norm1, w_a, w_b, w_out, qk_gains, rwkv_prm, ctx):
    rpm = T
    pa = _mm_norm(x, w_a, norm_g=norm1, mods=mods, slots=(0, 1), rows_per_mod=rpm)
    pb = _mm_norm(x, w_b, norm_g=norm1, mods=mods, slots=(0, 1), rows_per_mod=rpm)
    nqk = A_HEADS + A_KV_HEADS
    grp = A_HEADS // A_KV_HEADS
    scale = A_HEAD_DIM ** -0.5
    if ctx is None:
        qk = _norm_rope(pa, nqk, T, gains=qk_gains)
        oa = _attention(B, T, T, A_HEADS, scale, (qk, lambda h: h), (qk, lambda h: A_HEADS + h // grp),
                        (pa, lambda h: nqk + h // grp))
        ctx_out = (qk[:, A_Q:A_Q + A_KV], pa[:, A_Q + A_KV:])
        s0 = None
    else:
        cache_k, cache_v, s0 = ctx
        P = cache_k.shape[1]
        qk = _norm_rope(pa, nqk, T, gains=qk_gains, tables=_rope_tables(T, A_HEAD_DIM))
        k_all = jnp.concatenate([qk[:, A_Q:].reshape(B, T, A_KV), cache_k.reshape(B, P, A_KV)], axis=1)
        v_all = jnp.concatenate([pa[:, A_Q + A_KV:].reshape(B, T, A_KV), cache_v.reshape(B, P, A_KV)], axis=1)
        Tk = T + P
        oa = _attention(B, T, Tk, A_HEADS, scale, (qk, lambda h: h),
                        (k_all.reshape(B * Tk, A_KV).astype(BF16), lambda h: h // grp),
                        (v_all.reshape(B * Tk, A_KV).astype(BF16), lambda h: h // grp))
        ctx_out = None
    ob, s_fin = _rwkv_mixer(pb, B, T, rwkv_prm, s0)
    x = _mm_res([oa, ob], w_out, x, mods, 2, rpm)
    return x, ctx_out, s_fin


D_QROPE_BLK = 0
D_KROPE_BLK = D_HEADS
D_NROPE_BLK = D_HEADS + 1
D_CKV_BLK = 12
D_QNOPE_BLK = 16
D_PROJ = 24 * LANES


def _mla_in_weight(w_d):
    K = w_d.shape[0]
    wq = w_d[:, :D_Q].reshape(K, D_HEADS, D_QK)
    z64 = jnp.zeros((K, D_HEADS, LANES - D_ROPE), w_d.dtype)
    q_rope = jnp.concatenate([wq[:, :, D_NOPE:], z64], axis=-1).reshape(K, D_HEADS * LANES)
    q_nope = wq[:, :, :D_NOPE].reshape(K, D_HEADS * LANES)
    k_rope = jnp.concatenate([w_d[:, D_Q + D_KV_RANK:], jnp.zeros((K, LANES - D_ROPE), w_d.dtype)], axis=-1)
    padz = jnp.zeros((K, (D_CKV_BLK - D_NROPE_BLK) * LANES), w_d.dtype)
    return jnp.concatenate([q_rope, k_rope, padz, w_d[:, D_Q:D_Q + D_KV_RANK], q_nope], axis=-1)


def _layer_odd(x, mods, B, T, norm1, w_c, w_d, w_out, gdn_prm, kv_norm_g, w_kv_b, ctx):
    rpm = T
    pc = _mm_norm(x, w_c, norm_g=norm1, mods=mods, slots=(0, 1), rows_per_mod=rpm)
    pd = _mm_norm(x, w_d, norm_g=norm1, mods=mods, slots=(0, 1), rows_per_mod=rpm)
    scale = D_QK ** -0.5
    ckv_col = D_CKV_BLK * LANES // D_KV_RANK
    kv, ckv_n = _mm_norm(pd, w_kv_b, k=D_KV_RANK, xcol=ckv_col, norm_g=kv_norm_g, emit_xn=F32)
    kcol = lambda h: 2 * h
    vcol = lambda h: 2 * h + 1
    if ctx is None:
        od = _attention(B, T, T, D_HEADS, scale, (pd, lambda h: D_QNOPE_BLK + h), (kv, kcol), (kv, vcol),
                        q2=(pd, lambda h: D_QROPE_BLK + h), k2=(pd, lambda h: D_KROPE_BLK))
        ctx_out = (ckv_n, pd[:, D_KROPE_BLK * LANES:D_KROPE_BLK * LANES + D_ROPE])
        s0 = None
    else:
        s0, c_ckv, c_krope = ctx
        P = c_ckv.shape[1]
        Tk = T + P
        roped = _norm_rope(pd, D_NROPE_BLK, T, tables=_rope_tables(T, D_ROPE))
        kv_ctx = _mm_norm(c_ckv.reshape(B * P, D_KV_RANK), w_kv_b)
        kv_all = jnp.concatenate([kv.reshape(B, T, -1), kv_ctx.reshape(B, P, -1)], axis=1).reshape(B * Tk, -1)
        kv_all = kv_all.astype(BF16)
        kr_lat = roped[:, D_KROPE_BLK * LANES:].reshape(B, T, LANES)
        kr_ctx = jnp.concatenate([c_krope, jnp.zeros((B, P, LANES - D_ROPE), F32)], axis=-1)
        kr_all = jnp.concatenate([kr_lat, kr_ctx], axis=1).reshape(B * Tk, LANES).astype(BF16)
        od = _attention(B, T, Tk, D_HEADS, scale, (pd, lambda h: D_QNOPE_BLK + h), (kv_all, kcol), (kv_all, vcol),
                        q2=(roped, lambda h: D_QROPE_BLK + h), k2=(kr_all, lambda h: 0))
        ctx_out = None
    oc, s_fin = _gdn_mixer(pc, B, T, gdn_prm, s0)
    x = _mm_res([oc, od], w_out, x, mods, 2, rpm)
    return x, ctx_out, s_fin


def _modulation(cond, w_ada, b_ada):
    n = cond.shape[0]
    rows = ((n + 7) // 8) * 8
    a = cond * _sigmoid(cond)
    a = jnp.concatenate([a, jnp.zeros((rows - n, cond.shape[1]), F32)], axis=0)
    m = _mm_norm(a, _split(w_ada), hp=True, tn_cap=1024)[:n] + b_ada
    return m.reshape(n, 6, D_MODEL)


def kernel(x_prompt, x_sample, cache_attn_k, cache_attn_v, state_rwkv, state_gdn, cache_mla_ckv, cache_mla_krope,
           c, c_ctx, norm1_g, norm2_g, ada_w, ada_b, w_in_ab, w_out_ab, attn_q_norm, attn_k_norm,
           rwkv_mu, rwkv_w0, rwkv_w2, rwkv_a0, rwkv_a2, rwkv_g2, rwkv_k_k, rwkv_k_a, rwkv_r_k, rwkv_ln_w, rwkv_ln_b,
           w_in_cd, w_out_cd, gdn_conv, gdn_a_log, gdn_dt_bias, gdn_norm_g, mla_kv_norm_g, mla_w_kv_b,
           peer_wq, peer_keys, peer_u, peer_v, final_norm_g):
    Bp, Tp, D = x_prompt.shape
    Bs, Ts, _ = x_sample.shape
    depth = norm1_g.shape[0]
    xp = x_prompt.reshape(Bp * Tp, D)
    xs = x_sample.reshape(Bs * Ts, D)
    cond = jnp.concatenate([c_ctx[None, :], c], axis=0)
    outs = dict(k=[], v=[], rwkv=[], gdn=[], ckv=[], krope=[])
    for li in range(depth):
        j = li // 2
        mods = _modulation(cond, ada_w[li], ada_b[li])
        mods_p, mods_s = mods[:1], mods[1:]
        peer_prm = _peer_params(peer_wq[li], peer_keys[li], peer_u[li], peer_v[li])
        if li % 2 == 0:
            w_in = w_in_ab[j].astype(BF16)
            w_a, w_b = w_in[:, :A_IN], w_in[:, A_IN:]
            w_o = w_out_ab[j].astype(BF16)
            w_out = [w_o[:A_Q], w_o[A_Q:]]
            gains = jnp.concatenate([jnp.tile(attn_q_norm[j], A_HEADS), jnp.tile(attn_k_norm[j], A_KV_HEADS)])
            rp = dict(mu=rwkv_mu[j], w0=rwkv_w0[j], w2=rwkv_w2[j], a0=rwkv_a0[j], a2=rwkv_a2[j], g2=rwkv_g2[j],
                      k_k=rwkv_k_k[j], k_a=rwkv_k_a[j], r_k=rwkv_r_k[j], ln_w=rwkv_ln_w[j], ln_b=rwkv_ln_b[j])
            xp, (k_c, v_c), s_c = _layer_even(xp, mods_p, Bp, Tp, norm1_g[li], w_a, w_b, w_out, gains, rp, None)
            xs, _, _ = _layer_even(xs, mods_s, Bs, Ts, norm1_g[li], w_a, w_b, w_out, gains, rp,
                                   (cache_attn_k[:, j], cache_attn_v[:, j], state_rwkv[:, j]))
            outs['k'].append(k_c.reshape(Bp, Tp, A_KV_HEADS, A_HEAD_DIM))
            outs['v'].append(v_c.reshape(Bp, Tp, A_KV_HEADS, A_HEAD_DIM))
            outs['rwkv'].append(s_c)
        else:
            w_in = w_in_cd[j]
            w_c = jnp.concatenate([w_in[:, :C_IN], jnp.zeros((D, LANES - 4 * C_HEADS), F32)], axis=-1).astype(BF16)
            w_d = _mla_in_weight(w_in[:, C_IN:]).astype(BF16)
            w_o = w_out_cd[j].astype(BF16)
            w_out = [w_o[:C_W], w_o[C_W:]]
            gp = dict(conv=gdn_conv[j], a_log=gdn_a_log[j], dt_bias=gdn_dt_bias[j], norm_g=gdn_norm_g[j])
            w_kv_b = mla_w_kv_b[j].astype(BF16)
            xp, (ckv_c, kr_c), s_c = _layer_odd(xp, mods_p, Bp, Tp, norm1_g[li], w_c, w_d, w_out, gp,
                                                mla_kv_norm_g[j], w_kv_b, None)
            xs, _, _ = _layer_odd(xs, mods_s, Bs, Ts, norm1_g[li], w_c, w_d, w_out, gp, mla_kv_norm_g[j], w_kv_b,
                                  (state_gdn[:, j], cache_mla_ckv[:, j], cache_mla_krope[:, j]))
            outs['gdn'].append(s_c)
            outs['ckv'].append(ckv_c.reshape(Bp, Tp, D_KV_RANK))
            outs['krope'].append(kr_c.reshape(Bp, Tp, D_ROPE))
        xp = _peer(xp, mods_p, Tp, norm2_g[li], peer_prm)
        xs = _peer(xs, mods_s, Ts, norm2_g[li], peer_prm)
    y_prompt = _rmsnorm(xp, final_norm_g).reshape(Bp, Tp, D)
    y_sample = _rmsnorm(xs, final_norm_g).reshape(Bs, Ts, D)
    st = lambda name: jnp.stack(outs[name], axis=1)
    return (y_prompt, y_sample, st('k'), st('v'), st('rwkv'), st('gdn'), st('ckv'), st('krope'))
```

```python
import functools
import math

import numpy as np
import jax
import jax.numpy as jnp
from jax import lax
from jax.experimental import pallas as pl
from jax.experimental.pallas import tpu as pltpu

F32 = jnp.float32
BF16 = jnp.bfloat16

D_MODEL = 2048
GRID_W = 64
ROPE_THETA = 10000.0
NORM_EPS = 1e-6
A_HEADS, A_KV_HEADS, A_HEAD_DIM = 8, 2, 128
A_Q = A_HEADS * A_HEAD_DIM
A_KV = A_KV_HEADS * A_HEAD_DIM
A_IN = A_Q + 2 * A_KV
B_HEADS, B_HEAD_DIM = 16, 64
B_W = B_HEADS * B_HEAD_DIM
B_DECAY_LORA, B_AAA_LORA, B_GATE_LORA = 64, 64, 128
B_IN = 3 * B_W + 2 * B_DECAY_LORA + 2 * B_AAA_LORA + B_GATE_LORA
B_GN_EPS = 64e-5
C_HEADS, C_HEAD_DIM = 8, 128
C_W = C_HEADS * C_HEAD_DIM
C_IN = 4 * C_W + 4 * C_HEADS
D_HEADS, D_NOPE, D_ROPE, D_V = 8, 128, 64, 128
D_QK = D_NOPE + D_ROPE
D_KV_RANK = 512
D_Q = D_HEADS * D_QK
PEER_HEADS, N_KEYS, PEER_QDIM, PEER_TOPK = 8, 128, 128, 16
PEER_HALF = PEER_QDIM // 2
N_EXPERTS = N_KEYS * N_KEYS

LANES = 128
MXU_K = 256
VMEM_LIMIT = 48 * 1024 * 1024
VMEM_LIMIT_BIG = 56 * 1024 * 1024

CHUNK = 64

NN = ((1,), (0,))
NT = ((1,), (1,))
TN = ((0,), (0,))


def _dg(a, b, dims=NN):
    return lax.dot_general(a, b, (dims, ((), ())), preferred_element_type=F32)


def _split(x):
    hi = x.astype(BF16)
    lo = (x - hi.astype(F32)).astype(BF16)
    return hi, lo


def _dot_hp(a, b, dims=NN):
    ah, al = _split(a)
    bh, bl = _split(b)
    return _dg(ah, bh, dims) + (_dg(ah, bl, dims) + _dg(al, bh, dims))


def _dot_bf(a, b, dims=NN):
    return _dg(a.astype(BF16), b.astype(BF16), dims)


def _sigmoid(x):
    return 1.0 / (1.0 + jnp.exp(-x))


def _softplus(x):
    return jnp.maximum(x, 0.0) + jnp.log(1.0 + jnp.exp(-jnp.abs(x)))


def _cparams(sem, vmem=VMEM_LIMIT):
    return pltpu.CompilerParams(dimension_semantics=sem, vmem_limit_bytes=vmem)


def _pick_tile(n, cap):
    best = None
    t = LANES
    while t <= min(n, cap):
        if n % t == 0:
            best = t
        t += LANES
    return best if best is not None else n


def _mm_norm_kernel(*refs, has_norm, has_mod, shift_slot, scale_slot, hp, emit_xn):
    it = iter(refs)
    x_ref = next(it)
    g_ref = next(it) if has_norm else None
    m_ref = next(it) if has_mod else None
    wh_ref = next(it)
    wl_ref = next(it) if hp else None
    o_ref = next(it)
    xn_ref = next(it) if emit_xn else None
    xh_ref = next(it)
    xl_ref = next(it) if hp else None

    @pl.when(pl.program_id(1) == 0)
    def _():
        xv = x_ref[...].astype(F32)
        if has_norm:
            xv = xv * lax.rsqrt(jnp.mean(xv * xv, axis=-1, keepdims=True) + NORM_EPS) * g_ref[...]
        if has_mod:
            xv = xv * (1.0 + m_ref[scale_slot:scale_slot + 1, :]) + m_ref[shift_slot:shift_slot + 1, :]
        hi = xv.astype(BF16)
        xh_ref[...] = hi
        if hp:
            xl_ref[...] = (xv - hi.astype(F32)).astype(BF16)
        if emit_xn:
            xn_ref[...] = xv.astype(xn_ref.dtype)

    acc = _dg(xh_ref[...], wh_ref[...])
    if hp:
        acc = acc + (_dg(xh_ref[...], wl_ref[...]) + _dg(xl_ref[...], wh_ref[...]))
    o_ref[...] = acc


def _mm_norm(x, w, *, k=None, xcol=0, norm_g=None, mods=None, slots=(0, 1), rows_per_mod=None,
             hp=False, emit_xn=None, tm=512, tn_cap=1664):
    M = x.shape[0]
    K = k if k is not None else x.shape[1]
    ws = tuple(w) if isinstance(w, (tuple, list)) else (w,)
    N = ws[0].shape[1]
    tm = min(tm, M)
    if mods is not None and mods.shape[0] > 1:
        tm = min(tm, rows_per_mod)
    tn = _pick_tile(N, tn_cap)
    has_norm = norm_g is not None
    has_mod = mods is not None
    rpm = rows_per_mod if rows_per_mod is not None else M

    in_specs = [pl.BlockSpec((tm, K), lambda i, j: (i, xcol))]
    args = [x]
    if has_norm:
        in_specs.append(pl.BlockSpec((1, K), lambda i, j: (0, 0)))
        args.append(norm_g.reshape(1, K).astype(F32))
    if has_mod:
        if mods.shape[0] > 1:
            in_specs.append(pl.BlockSpec((None, 6, K), lambda i, j: ((i * tm) // rpm, 0, 0)))
        else:
            in_specs.append(pl.BlockSpec((None, 6, K), lambda i, j: (0, 0, 0)))
        args.append(mods)
    for wi in ws:
        in_specs.append(pl.BlockSpec((K, tn), lambda i, j: (0, j)))
        args.append(wi)
    out_shape = [jax.ShapeDtypeStruct((M, N), F32)]
    out_specs = [pl.BlockSpec((tm, tn), lambda i, j: (i, j))]
    if emit_xn is not None:
        out_shape.append(jax.ShapeDtypeStruct((M, K), emit_xn))
        out_specs.append(pl.BlockSpec((tm, K), lambda i, j: (i, 0)))
    scratch = [pltpu.VMEM((tm, K), BF16)]
    if hp:
        scratch.append(pltpu.VMEM((tm, K), BF16))
    kern = functools.partial(_mm_norm_kernel, has_norm=has_norm, has_mod=has_mod, shift_slot=slots[0],
                             scale_slot=slots[1], hp=hp, emit_xn=emit_xn is not None)
    outs = pl.pallas_call(
        kern, grid=(M // tm, N // tn), in_specs=in_specs, out_specs=out_specs, out_shape=out_shape,
        scratch_shapes=scratch, compiler_params=_cparams(("parallel", "arbitrary")))(*args)
    return outs if emit_xn is not None else outs[0]


def _mm_res_kernel(*refs, n_in, gate_slot):
    a_refs = refs[:n_in]
    w_refs = refs[n_in:2 * n_in]
    res_ref, m_ref, o_ref = refs[2 * n_in:]
    acc = None
    for a_ref, w_ref in zip(a_refs, w_refs):
        part = _dg(a_ref[...].astype(BF16), w_ref[...])
        acc = part if acc is None else acc + part
    o_ref[...] = res_ref[...] + m_ref[gate_slot:gate_slot + 1, :] * acc


def _mm_res(a_list, w_list, res, mods, gate_slot, rows_per_mod, tm=512, tn=1024):
    M, N = res.shape
    tm = min(tm, M, rows_per_mod) if mods.shape[0] > 1 else min(tm, M)
    tn = min(tn, N)
    n_in = len(a_list)
    in_specs, args = [], []
    for a in a_list:
        in_specs.append(pl.BlockSpec((tm, a.shape[1]), lambda i, j: (i, 0)))
        args.append(a)
    for w in w_list:
        in_specs.append(pl.BlockSpec((w.shape[0], tn), lambda i, j: (0, j)))
        args.append(w)
    in_specs.append(pl.BlockSpec((tm, tn), lambda i, j: (i, j)))
    args.append(res)
    if mods.shape[0] > 1:
        in_specs.append(pl.BlockSpec((None, 6, tn), lambda i, j: ((i * tm) // rows_per_mod, 0, j)))
    else:
        in_specs.append(pl.BlockSpec((None, 6, tn), lambda i, j: (0, 0, j)))
    args.append(mods)
    return pl.pallas_call(
        functools.partial(_mm_res_kernel, n_in=n_in, gate_slot=gate_slot),
        grid=(M // tm, N // tn), in_specs=in_specs,
        out_specs=pl.BlockSpec((tm, tn), lambda i, j: (i, j)),
        out_shape=jax.ShapeDtypeStruct((M, N), F32),
        compiler_params=_cparams(("parallel", "arbitrary")))(*args)


def _rmsnorm_kernel(x_ref, g_ref, o_ref):
    xv = x_ref[...]
    o_ref[...] = xv * lax.rsqrt(jnp.mean(xv * xv, axis=-1, keepdims=True) + NORM_EPS) * g_ref[...]


def _rmsnorm(x, g, tm=512):
    M, K = x.shape
    tm = min(tm, M)
    return pl.pallas_call(
        _rmsnorm_kernel, grid=(M // tm,),
        in_specs=[pl.BlockSpec((tm, K), lambda i: (i, 0)), pl.BlockSpec((1, K), lambda i: (0, 0))],
        out_specs=pl.BlockSpec((tm, K), lambda i: (i, 0)),
        out_shape=jax.ShapeDtypeStruct((M, K), F32),
        compiler_params=_cparams(("parallel",)))(x, g.reshape(1, K))


def _rope_tables(T, rot_dim):
    rows = T // GRID_W
    row = jnp.repeat(jnp.arange(rows, dtype=F32), GRID_W)
    col = jnp.tile(jnp.arange(GRID_W, dtype=F32), rows)
    n_freq = rot_dim // 4
    inv = ROPE_THETA ** (-jnp.arange(n_freq, dtype=F32) / n_freq)
    ang = jnp.concatenate([row[:, None] * inv, col[:, None] * inv], axis=-1)
    cos = jnp.repeat(jnp.cos(ang), 2, axis=-1)
    sin = jnp.repeat(jnp.sin(ang), 2, axis=-1) * jnp.tile(jnp.array([-1.0, 1.0], F32), rot_dim // 2)
    pad = LANES - rot_dim
    if pad:
        cos = jnp.concatenate([cos, jnp.ones((T, pad), F32)], axis=-1)
        sin = jnp.concatenate([sin, jnp.zeros((T, pad), F32)], axis=-1)
    return cos, sin


def _rope(y, cosf, sinf):
    lane = lax.broadcasted_iota(jnp.int32, y.shape, 1)
    even = (lane & 1) == 0
    width = y.shape[1]
    swap = jnp.where(even, pltpu.roll(y, width - 1, 1), pltpu.roll(y, 1, 1))
    return y * cosf + swap * sinf


def _norm_rope_kernel(*refs, nblk, norm, rope):
    it = iter(refs)
    x_ref = next(it)
    g_ref = next(it) if norm else None
    cos_ref = next(it) if rope else None
    sin_ref = next(it) if rope else None
    o_ref = next(it)
    for c in range(nblk):
        sl = slice(c * LANES, (c + 1) * LANES)
        y = x_ref[:, sl]
        if norm:
            y = y * lax.rsqrt(jnp.mean(y * y, axis=-1, keepdims=True) + NORM_EPS) * g_ref[:, sl]
        if rope:
            y = _rope(y, cos_ref[...], sin_ref[...])
        o_ref[:, sl] = y


def _norm_rope(x, nblk, T, gains=None, tables=None, tt=256):
    M = x.shape[0]
    tt = min(tt, T)
    W = nblk * LANES
    in_specs = [pl.BlockSpec((tt, W), lambda i: (i, 0))]
    args = [x]
    if gains is not None:
        in_specs.append(pl.BlockSpec((1, W), lambda i: (0, 0)))
        args.append(gains.reshape(1, W))
    if tables is not None:
        nt = T // tt
        for t in tables:
            in_specs.append(pl.BlockSpec((tt, LANES), lambda i: (i % nt, 0)))
            args.append(t)
    return pl.pallas_call(
        functools.partial(_norm_rope_kernel, nblk=nblk, norm=gains is not None, rope=tables is not None),
        grid=(M // tt,), in_specs=in_specs,
        out_specs=pl.BlockSpec((tt, W), lambda i: (i, 0)),
        out_shape=jax.ShapeDtypeStruct((M, W), F32),
        compiler_params=_cparams(("parallel",)))(*args)


def _attn_kernel(*refs, scale, has2):
    it = iter(refs)
    q_ref = next(it)
    q2_ref = next(it) if has2 else None
    k_ref = next(it)
    k2_ref = next(it) if has2 else None
    v_ref = next(it)
    o_ref = next(it)
    if has2:
        qq = jnp.concatenate([q_ref[...], q2_ref[...]], axis=1) * scale
        s = _dot_bf(qq, jnp.concatenate([k_ref[...], k2_ref[...]], axis=1), NT)
    else:
        s = _dot_bf(q_ref[...] * scale, k_ref[...], NT)
    m = jnp.max(s, axis=-1, keepdims=True)
    p = jnp.exp(s - m)
    l = jnp.sum(p, axis=-1, keepdims=True)
    o_ref[...] = _dot_bf(p, v_ref[...]) / l


def _attention(B, T, Tk, H, scale, q, k, v, q2=None, k2=None, tq=256):
    tq = min(tq, T)
    nq = T // tq
    has2 = q2 is not None

    def qspec(cf):
        return pl.BlockSpec((tq, LANES), lambda b, h, i: (b * nq + i, cf(h)))

    def kspec(cf):
        return pl.BlockSpec((Tk, LANES), lambda b, h, i: (b, cf(h)))

    in_specs, args = [qspec(q[1])], [q[0]]
    if has2:
        in_specs.append(qspec(q2[1]))
        args.append(q2[0])
    in_specs.append(kspec(k[1]))
    args.append(k[0])
    if has2:
        in_specs.append(kspec(k2[1]))
        args.append(k2[0])
    in_specs.append(kspec(v[1]))
    args.append(v[0])
    return pl.pallas_call(
        functools.partial(_attn_kernel, scale=scale, has2=has2),
        grid=(B, H, nq), in_specs=in_specs,
        out_specs=pl.BlockSpec((tq, LANES), lambda b, h, i: (b * nq + i, h)),
        out_shape=jax.ShapeDtypeStruct((B * T, H * LANES), F32),
        compiler_params=_cparams(("parallel", "arbitrary", "arbitrary")))(*args)


def _tri_masks(L, rev):
    i = lax.broadcasted_iota(jnp.int32, (L, L), 0)
    j = lax.broadcasted_iota(jnp.int32, (L, L), 1)
    if rev:
        return j >= i, j > i, i == j
    return j <= i, j < i, i == j


def _inv_unit_tri(mats, eye_f, L):
    pows = [[-a for a in mats]]
    for _ in range(int(math.log2(L)) - 1):
        pows.append([_dot_bf(p, p) for p in pows[-1]])
    terms = [[eye_f + p for p in level] for level in pows]
    while len(terms) > 1:
        nxt = [[_dot_bf(x, y) for x, y in zip(terms[i], terms[i + 1])] for i in range(0, len(terms) - 1, 2)]
        if len(terms) % 2:
            nxt.append(terms[-1])
        terms = nxt
    x0 = terms[0]
    resid = [(eye_f - x) - _dot_hp(a, x) for a, x in zip(mats, x0)]
    return [x + _dot_bf(x, r) for x, r in zip(x0, resid)]


def _shifted(f, prev_ref, next_ref, tt, T):
    i = pl.program_id(0)
    first = (i * tt) % T == 0
    last = ((i + 1) * tt) % T == 0
    prev = jnp.where(first, 0.0, prev_ref[7:8, :])
    nxt = jnp.where(last, 0.0, next_ref[0:1, :])
    rows = lax.broadcasted_iota(jnp.int32, f.shape, 0)
    fp = jnp.where(rows == 0, prev, pltpu.roll(f, 1, 0))
    fn = jnp.where(rows == tt - 1, nxt, pltpu.roll(f, tt - 1, 0))
    return fp, fn


def _halo_specs(tt, width, M, col_fn):
    r8 = tt // 8
    last8 = M // 8 - 1
    return [
        pl.BlockSpec((tt, width), lambda *g: (g[0], col_fn(*g))),
        pl.BlockSpec((8, width), lambda *g: (jnp.maximum(g[0] * r8 - 1, 0), col_fn(*g))),
        pl.BlockSpec((8, width), lambda *g: (jnp.minimum((g[0] + 1) * r8, last8), col_fn(*g))),
    ]


NPAIR = B_HEADS // 2


def _pair_sum_matrix():
    r = lax.broadcasted_iota(jnp.int32, (LANES, LANES), 0) // B_HEAD_DIM
    c = lax.broadcasted_iota(jnp.int32, (LANES, LANES), 1) // B_HEAD_DIM
    return (r == c).astype(F32)


def _rwkv_prep_kernel(x_ref, prev_ref, next_ref, mu_ref, w0_ref, w2_ref, a0_ref, a2_ref, g2_ref, kk_ref_in,
                      r_o, k_o, v_o, kk_o, g_o, lw0_o, lw1_o, a0_o, a1_o, *, tt, T):
    f = x_ref[...]
    fp, fn = _shifted(f, prev_ref, next_ref, tt, T)
    f = f + mu_ref[0:1, :] * (fp - f) + mu_ref[1:2, :] * (fn - f)
    r = f[:, :B_W]
    k = f[:, B_W:2 * B_W]
    v = f[:, 2 * B_W:3 * B_W]
    off = 3 * B_W
    w_lo = f[:, off:off + 2 * B_DECAY_LORA]
    off += 2 * B_DECAY_LORA
    a_lo = f[:, off:off + 2 * B_AAA_LORA]
    off += 2 * B_AAA_LORA
    g_lo = f[:, off:]
    th = jnp.tanh(w_lo)
    logw, aa = [], []
    for d in range(2):
        wpre = w0_ref[d:d + 1, :] + _dot_hp(th[:, d * B_DECAY_LORA:(d + 1) * B_DECAY_LORA], w2_ref[d])
        wv = -_softplus(-wpre) - 0.5
        logw.append(-jnp.exp(wv))
        aa.append(_sigmoid(a0_ref[d:d + 1, :] + _dot_hp(a_lo[:, d * B_AAA_LORA:(d + 1) * B_AAA_LORA], a2_ref[d])))
    g = _dot_hp(_sigmoid(g_lo), g2_ref[...])
    kk = k * kk_ref_in[...]
    gsum = _pair_sum_matrix()
    for p in range(NPAIR):
        sl = slice(p * LANES, (p + 1) * LANES)
        kp = kk[:, sl]
        ssq = _dot_hp(kp * kp, gsum)
        kk_o[p] = kp / jnp.maximum(jnp.sqrt(ssq), 1e-12)
        r_o[p] = r[:, sl]
        k_o[p] = k[:, sl]
        v_o[p] = v[:, sl]
        g_o[p] = g[:, sl]
        lw0_o[p] = logw[0][:, sl]
        lw1_o[p] = logw[1][:, sl]
        a0_o[p] = aa[0][:, sl]
        a1_o[p] = aa[1][:, sl]


def _rwkv_prep(pb, B, T, mu, w0, w2, a0, a2, g2, k_k, tt=128):
    M = pb.shape[0]
    tt = min(tt, T)
    nt = T // tt
    full = lambda shape: pl.BlockSpec(shape, lambda i: tuple(0 for _ in shape))
    in_specs = _halo_specs(tt, B_IN, M, lambda i: 0) + [
        full((2, B_IN)), full((2, B_W)), full((2, B_DECAY_LORA, B_W)), full((2, B_W)),
        full((2, B_AAA_LORA, B_W)), full((B_GATE_LORA, B_W)), full((1, B_W))]
    ospec = pl.BlockSpec((None, NPAIR, tt, LANES), lambda i: (i // nt, 0, i % nt, 0))
    oshape = jax.ShapeDtypeStruct((B, NPAIR, T, LANES), F32)
    return pl.pallas_call(
        functools.partial(_rwkv_prep_kernel, tt=tt, T=T),
        grid=(M // tt,), in_specs=in_specs, out_specs=[ospec] * 9, out_shape=[oshape] * 9,
        compiler_params=_cparams(("parallel",)))(pb, pb, pb, mu, w0, w2, a0, a2, g2, k_k.reshape(1, B_W))


def _rwkv_chunk(r2, lw2, kd2, v2, kk2, a2, states, masks, L, rev):
    incl, strict, eye = masks
    incl_f = incl.astype(F32)
    eye_f = eye.astype(F32)
    i2 = lax.broadcasted_iota(jnp.int32, (L, 2 * L), 0)
    j2 = lax.broadcasted_iota(jnp.int32, (L, 2 * L), 1) % L
    incl2 = (j2 >= i2) if rev else (j2 <= i2)
    last = 0 if rev else L - 1
    halves = (slice(0, B_HEAD_DIM), slice(B_HEAD_DIM, 2 * B_HEAD_DIM))
    logp2 = [_dot_hp(incl_f, lw) for lw in lw2]
    lhs, rhs, rem_rhs, dec, vs = [], [], [], [], []
    for lp, lw, r, kd, v, kk, a in zip(logp2, lw2, r2, kd2, v2, kk2, a2):
        ninv = jnp.exp(-lp)
        b = kk * a
        kt = kk * jnp.exp(lp - lw)
        rt = r * jnp.exp(lp)
        bn, kdn = b * ninv, kd * ninv
        pl_row = lp[last:last + 1, :]
        rem = jnp.exp(pl_row - lp)
        bh, kh = b * rem, kd * rem
        dpl = jnp.exp(pl_row)
        for sl in halves:
            lhs.append(jnp.concatenate([kt[:, sl], rt[:, sl]], axis=0))
            rhs.append(jnp.concatenate([bn[:, sl], kdn[:, sl]], axis=0))
            rem_rhs.append(jnp.concatenate([bh[:, sl], kh[:, sl]], axis=0))
            dec.append(dpl[:, sl])
            vs.append(v[:, sl])
    gm = [_dot_hp(x, y, NT) for x, y in zip(lhs, rhs)]
    ls = [_dot_hp(x, s, NT) for x, s in zip(lhs, states)]
    akv = [_dot_hp(jnp.where(strict, g[:L, L:], 0.0), v) for g, v in zip(gm, vs)]
    tinv = _inv_unit_tri([jnp.where(strict, g[:L, :L], 0.0) for g in gm], eye_f, L)
    u = [-_dot_hp(t, l[:L] + x) for t, l, x in zip(tinv, ls, akv)]
    uv = [jnp.concatenate([x, v], axis=0) for x, v in zip(u, vs)]
    y = [l[L:] + _dot_hp(jnp.where(incl2, g[L:, :], 0.0), x) for l, g, x in zip(ls, gm, uv)]
    s_new = [s * d + _dot_hp(x, m, TN) for s, d, x, m in zip(states, dec, uv, rem_rhs)]
    y2 = [jnp.concatenate([y[2 * p], y[2 * p + 1]], axis=1) for p in range(len(r2))]
    return y2, s_new


def _rwkv_scan_kernel(r_ref, v_ref, kk_ref, k_ref, lw_ref, a_ref, ka_ref, s0_ref, y_ref, sf_ref, s_ref, *, L, rev):
    c = pl.program_id(1)

    @pl.when(c == 0)
    def _():
        s_ref[...] = s0_ref[...]

    masks = _tri_masks(L, rev)
    pairs = range(NPAIR)
    a2 = [a_ref[p] for p in pairs]
    kd2 = [k_ref[p] * (1.0 + (a - 1.0) * ka_ref[p]) for p, a in zip(pairs, a2)]
    y2, s_new = _rwkv_chunk([r_ref[p] for p in pairs], [lw_ref[p] for p in pairs], kd2, [v_ref[p] for p in pairs],
                            [kk_ref[p] for p in pairs], a2, [s_ref[h] for h in range(B_HEADS)], masks, L, rev)
    for p in pairs:
        y_ref[p] = y2[p]
    for h in range(B_HEADS):
        s_ref[h] = s_new[h]

    @pl.when(c == pl.num_programs(1) - 1)
    def _():
        sf_ref[...] = s_ref[...]


def _rwkv_scan(r, v, kk, k, lw, a, k_a, s0, rev, L=CHUNK):
    B, _, T, _ = r.shape
    nc = T // L
    cmap = (lambda b, c: (b, 0, nc - 1 - c, 0)) if rev else (lambda b, c: (b, 0, c, 0))
    seq = pl.BlockSpec((None, NPAIR, L, LANES), cmap)
    st = pl.BlockSpec((None, B_HEADS, B_HEAD_DIM, B_HEAD_DIM), lambda b, c: (b, 0, 0, 0))
    return pl.pallas_call(
        functools.partial(_rwkv_scan_kernel, L=L, rev=rev),
        grid=(B, nc),
        in_specs=[seq] * 6 + [pl.BlockSpec((NPAIR, 1, LANES), lambda b, c: (0, 0, 0)), st],
        out_specs=[seq, st],
        out_shape=[jax.ShapeDtypeStruct((B, NPAIR, T, LANES), F32),
                   jax.ShapeDtypeStruct((B, B_HEADS, B_HEAD_DIM, B_HEAD_DIM), F32)],
        scratch_shapes=[pltpu.VMEM((B_HEADS, B_HEAD_DIM, B_HEAD_DIM), F32)],
        compiler_params=_cparams(("parallel", "arbitrary")))(r, v, kk, k, lw, a, k_a.reshape(NPAIR, 1, LANES), s0)


def _rwkv_post_kernel(y0_ref, y1_ref, r_ref, k_ref, v_ref, g_ref, rk_ref, lnw_ref, lnb_ref, o_ref):
    gsum = _pair_sum_matrix()
    inv_n = 1.0 / B_HEAD_DIM
    for p in range(NPAIR):
        y = y0_ref[p] + y1_ref[p]
        mean = _dot_hp(y, gsum) * inv_n
        yc = y - mean
        var = _dot_hp(yc * yc, gsum) * inv_n
        yn = yc * lax.rsqrt(var + B_GN_EPS) * lnw_ref[p] + lnb_ref[p]
        bonus = _dot_hp(r_ref[p] * k_ref[p] * rk_ref[p], gsum) * v_ref[p]
        o_ref[:, p * LANES:(p + 1) * LANES] = (yn + bonus) * g_ref[p]


def _rwkv_post(y0, y1, r, k, v, g, r_k, ln_w, ln_b, tt=256):
    B, _, T, _ = r.shape
    tt = min(tt, T)
    nt = T // tt
    seq = pl.BlockSpec((None, NPAIR, tt, LANES), lambda i: (i // nt, 0, i % nt, 0))
    par = pl.BlockSpec((NPAIR, 1, LANES), lambda i: (0, 0, 0))
    return pl.pallas_call(
        _rwkv_post_kernel, grid=(B * nt,),
        in_specs=[seq] * 6 + [par] * 3,
        out_specs=pl.BlockSpec((tt, B_W), lambda i: (i, 0)),
        out_shape=jax.ShapeDtypeStruct((B * T, B_W), F32),
        compiler_params=_cparams(("parallel",)))(
            y0, y1, r, k, v, g, r_k.reshape(NPAIR, 1, LANES), ln_w.reshape(NPAIR, 1, LANES),
            ln_b.reshape(NPAIR, 1, LANES))


def _rwkv_mixer(pb, B, T, prm, s0):
    r, k, v, kk, g, lw0, lw1, a0, a1 = _rwkv_prep(pb, B, T, prm['mu'], prm['w0'], prm['w2'], prm['a0'],
                                                   prm['a2'], prm['g2'], prm['k_k'])
    if s0 is None:
        s0 = jnp.zeros((B, 2, B_HEADS, B_HEAD_DIM, B_HEAD_DIM), F32)
    y0, sf0 = _rwkv_scan(r, v, kk, k, lw0, a0, prm['k_a'], s0[:, 0], rev=False)
    y1, sf1 = _rwkv_scan(r, v, kk, k, lw1, a1, prm['k_a'], s0[:, 1], rev=True)
    out = _rwkv_post(y0, y1, r, k, v, g, prm['r_k'], prm['ln_w'], prm['ln_b'])
    return out, jnp.stack([sf0, sf1], axis=1)


def _gdn_prep_kernel(x_ref, prev_ref, next_ref, cw_ref, o_ref, *, tt, T):
    c = pl.program_id(1)
    x = x_ref[...]
    xp, xn = _shifted(x, prev_ref, next_ref, tt, T)
    y = cw_ref[0:1, :] * xp + cw_ref[1:2, :] * x + cw_ref[2:3, :] * xn
    y = y * _sigmoid(y)

    @pl.when(c == 2)
    def _():
        o_ref[...] = y

    @pl.when(c < 2)
    def _():
        scale = jnp.where(c == 0, C_HEAD_DIM ** -0.5, 1.0).astype(F32)
        for h in range(C_HEADS):
            sl = slice(h * C_HEAD_DIM, (h + 1) * C_HEAD_DIM)
            yh = y[:, sl]
            o_ref[:, sl] = yh * lax.rsqrt(jnp.sum(yh * yh, axis=-1, keepdims=True) + 1e-6) * scale


def _gdn_prep(pc, T, conv_w, tt=256):
    M = pc.shape[0]
    tt = min(tt, T)
    in_specs = _halo_specs(tt, C_W, M, lambda i, c: c) + [pl.BlockSpec((3, C_W), lambda i, c: (0, c))]
    return pl.pallas_call(
        functools.partial(_gdn_prep_kernel, tt=tt, T=T),
        grid=(M // tt, 3), in_specs=in_specs,
        out_specs=pl.BlockSpec((tt, C_W), lambda i, c: (i, c)),
        out_shape=jax.ShapeDtypeStruct((M, 3 * C_W), F32),
        compiler_params=_cparams(("parallel", "arbitrary")))(pc, pc, pc, conv_w)


def _gdn_bg_kernel(x_ref, alog_ref, dtb_ref, o_ref):
    x = x_ref[...]
    lane = lax.broadcasted_iota(jnp.int32, x.shape, 1)
    beta = _sigmoid(x)
    g = -jnp.exp(alog_ref[...]) * _softplus(x + dtb_ref[...])
    o_ref[...] = jnp.where(lane < 2 * C_HEADS, beta, g)


def _gdn_bg(pc, a_log, dt_bias, tt=512):
    M = pc.shape[0]
    tt = min(tt, M)
    nb = 2 * C_HEADS
    pad = lambda p: jnp.concatenate([jnp.zeros((nb,), F32), p.reshape(nb), jnp.zeros((LANES - 2 * nb,), F32)]).reshape(1, LANES)
    return pl.pallas_call(
        _gdn_bg_kernel, grid=(M // tt,),
        in_specs=[pl.BlockSpec((tt, LANES), lambda i: (i, 4 * C_W // LANES)),
                  pl.BlockSpec((1, LANES), lambda i: (0, 0)), pl.BlockSpec((1, LANES), lambda i: (0, 0))],
        out_specs=pl.BlockSpec((tt, LANES), lambda i: (i, 0)),
        out_shape=jax.ShapeDtypeStruct((M, LANES), F32),
        compiler_params=_cparams(("parallel",)))(pc, pad(a_log), pad(dt_bias))


def _gdn_chunk(qs, ks, vs, bcol, gcol, grow, states, masks, L, rev):
    incl, strict, eye = masks
    incl_f = incl.astype(F32)
    eye_f = eye.astype(F32)
    nh = len(qs)
    i = lax.broadcasted_iota(jnp.int32, (L, L), 0)
    j = lax.broadcasted_iota(jnp.int32, (L, L), 1)
    cum_t = ((i >= j) if rev else (i <= j)).astype(F32)
    gc_cols = _dot_hp(incl_f, gcol)
    gc_rows = _dot_hp(grow, cum_t)
    last = 0 if rev else L - 1
    gcs, decays, kbs, rhss, kends, g_ends = [], [], [], [], [], []
    for h in range(nh):
        gc = jnp.broadcast_to(gc_cols[:, h:h + 1], (L, LANES))
        decays.append(jnp.exp(jnp.where(incl, gc[:, :L] - gc_rows[h:h + 1, :], -1e30)))
        beta = bcol[:, h:h + 1]
        kb = ks[h] * beta
        kbs.append(kb)
        rhss.append(jnp.concatenate([vs[h] * beta, kb * jnp.exp(gc)], axis=1))
        g_end = gc[last:last + 1, :]
        kends.append(ks[h] * jnp.exp(g_end - gc))
        g_ends.append(jnp.exp(g_end))
        gcs.append(jnp.exp(gc))
    kk = [_dot_hp(kb, k, NT) for kb, k in zip(kbs, ks)]
    qk = [_dot_hp(q, k, NT) for q, k in zip(qs, ks)]
    qs_s = [_dot_hp(q * e, s) for q, e, s in zip(qs, gcs, states)]
    tinv = _inv_unit_tri([jnp.where(strict, x * d, 0.0) for x, d in zip(kk, decays)], eye_f, L)
    sol = [_dot_hp(t, r) for t, r in zip(tinv, rhss)]
    v_new = [x[:, :C_HEAD_DIM] - _dot_hp(x[:, C_HEAD_DIM:], s) for x, s in zip(sol, states)]
    o = [a + _dot_hp(jnp.where(incl, x * d, 0.0), vn) for a, x, d, vn in zip(qs_s, qk, decays, v_new)]
    s_new = [s * ge + _dot_hp(ke, vn, TN) for s, ge, ke, vn in zip(states, g_ends, kends, v_new)]
    return o, s_new


def _gdn_scan_kernel(q_ref, k_ref, v_ref, bcol_ref, gcol_ref, grow_ref, s0_ref, o_ref, sf_ref, s_ref, *, L, rev):
    c = pl.program_id(1)

    @pl.when(c == 0)
    def _():
        s_ref[...] = s0_ref[...]

    masks = _tri_masks(L, rev)
    heads = range(C_HEADS)
    hs = [slice(h * C_HEAD_DIM, (h + 1) * C_HEAD_DIM) for h in heads]
    o, s_new = _gdn_chunk([q_ref[:, sl] for sl in hs], [k_ref[:, sl] for sl in hs], [v_ref[:, sl] for sl in hs],
                          bcol_ref[...], gcol_ref[...], grow_ref[...], [s_ref[h] for h in heads], masks, L, rev)
    for h in heads:
        o_ref[:, hs[h]] = o[h]
        s_ref[h] = s_new[h]

    @pl.when(c == pl.num_programs(1) - 1)
    def _():
        sf_ref[...] = s_ref[...]


def _gdn_scan(qkv, bcol, gcol, grow, s0, B, T, rev, L=CHUNK):
    nc = T // L
    cidx = (lambda c: nc - 1 - c) if rev else (lambda c: c)
    seq = lambda col: pl.BlockSpec((L, C_W), lambda b, c: (b * nc + cidx(c), col))
    small = pl.BlockSpec((L, C_HEADS), lambda b, c: (b * nc + cidx(c), 0))
    st = pl.BlockSpec((None, C_HEADS, C_HEAD_DIM, C_HEAD_DIM), lambda b, c: (b, 0, 0, 0))
    return pl.pallas_call(
        functools.partial(_gdn_scan_kernel, L=L, rev=rev),
        grid=(B, nc),
        in_specs=[seq(0), seq(1), seq(2), small, small,
                  pl.BlockSpec((None, C_HEADS, L), lambda b, c: (b * nc + cidx(c), 0, 0)), st],
        out_specs=[pl.BlockSpec((L, C_W), lambda b, c: (b * nc + cidx(c), 0)), st],
        out_shape=[jax.ShapeDtypeStruct((B * T, C_W), F32),
                   jax.ShapeDtypeStruct((B, C_HEADS, C_HEAD_DIM, C_HEAD_DIM), F32)],
        scratch_shapes=[pltpu.VMEM((C_HEADS, C_HEAD_DIM, C_HEAD_DIM), F32)],
        compiler_params=_cparams(("parallel", "arbitrary")))(qkv, qkv, qkv, bcol, gcol, grow, s0)


def _gdn_post_kernel(o0_ref, o1_ref, z_ref, g_ref, out_ref):
    for h in range(C_HEADS):
        sl = slice(h * C_HEAD_DIM, (h + 1) * C_HEAD_DIM)
        y = o0_ref[:, sl] + o1_ref[:, sl]
        y = y * lax.rsqrt(jnp.mean(y * y, axis=-1, keepdims=True) + NORM_EPS) * g_ref[...]
        z = z_ref[:, sl]
        out_ref[:, sl] = y * (z * _sigmoid(z))


def _gdn_post(o0, o1, pc, norm_g, tt=256):
    M = o0.shape[0]
    tt = min(tt, M)
    return pl.pallas_call(
        _gdn_post_kernel, grid=(M // tt,),
        in_specs=[pl.BlockSpec((tt, C_W), lambda i: (i, 0)), pl.BlockSpec((tt, C_W), lambda i: (i, 0)),
                  pl.BlockSpec((tt, C_W), lambda i: (i, 3)), pl.BlockSpec((1, C_HEAD_DIM), lambda i: (0, 0))],
        out_specs=pl.BlockSpec((tt, C_W), lambda i: (i, 0)),
        out_shape=jax.ShapeDtypeStruct((M, C_W), F32),
        compiler_params=_cparams(("parallel",)))(o0, o1, pc, norm_g.reshape(1, C_HEAD_DIM))


def _gdn_mixer(pc, B, T, prm, s0, L=CHUNK):
    M = pc.shape[0]
    nc = T // L
    qkv = _gdn_prep(pc, T, prm['conv'])
    bg = _gdn_bg(pc, prm['a_log'], prm['dt_bias'])
    if s0 is None:
        s0 = jnp.zeros((B, 2, C_HEADS, C_HEAD_DIM, C_HEAD_DIM), F32)
    outs, finals = [], []
    for d in range(2):
        bcol = bg[:, d * C_HEADS:(d + 1) * C_HEADS]
        gcol = bg[:, 2 * C_HEADS + d * C_HEADS:2 * C_HEADS + (d + 1) * C_HEADS]
        grow = gcol.reshape(B * nc, L, C_HEADS).transpose(0, 2, 1)
        o, sf = _gdn_scan(qkv, bcol, gcol, grow, s0[:, d], B, T, rev=(d == 1))
        outs.append(o)
        finals.append(sf)
    y = _gdn_post(outs[0], outs[1], pc, prm['norm_g'])
    return y, jnp.stack(finals, axis=1)


def _top_values(tile, count, need_mask):
    rows = lax.broadcasted_iota(jnp.int32, tile.shape, 0).astype(F32)
    big = float(tile.shape[0])
    cur = tile
    vals = []
    for _ in range(count):
        m = jnp.max(cur, axis=0, keepdims=True)
        vals.append(m)
        idx = jnp.min(jnp.where(cur == m, rows, big), axis=0, keepdims=True)
        cur = jnp.where(rows == idx, -jnp.inf, cur)
    return vals, (cur == -jnp.inf) if need_mask else None


def _top_values_untied(tile, count):
    cur = tile
    vals = []
    for _ in range(count):
        m = jnp.max(cur, axis=0, keepdims=True)
        vals.append(m)
        cur = jnp.where(cur == m, -jnp.inf, cur)
    return vals, cur == -jnp.inf


PEER_EB = 1024
PEER_GRP = MXU_K // PEER_HALF
PEER_NGRP = 2 * PEER_HEADS // PEER_GRP


def _peer_topk_kernel(q_ref, kh_ref, kl_ref, thr_ref, e0_ref, e1_ref, s1_ref, sc_ref):
    rows_g = PEER_GRP * N_KEYS
    for g in range(PEER_NGRP):
        qh, ql = _split(q_ref[:, g * MXU_K:(g + 1) * MXU_K])
        kh = kh_ref[g * rows_g:(g + 1) * rows_g, :]
        kl = kl_ref[g * rows_g:(g + 1) * rows_g, :]
        sc_ref[g * rows_g:(g + 1) * rows_g, :] = _dg(kh, qh, NT) + (_dg(kh, ql, NT) + _dg(kl, qh, NT))

    def route(h, t0, t1, exact):
        if exact:
            a, ra = _top_values(t0, PEER_TOPK, True)
            b, rb = _top_values(t1, PEER_TOPK, True)
        else:
            a, ra = _top_values_untied(t0, PEER_TOPK)
            b, rb = _top_values_untied(t1, PEER_TOPK)
        s0 = jnp.where(ra, t0, -jnp.inf)
        s1 = jnp.where(rb, t1, -jnp.inf)
        amat = jnp.concatenate(a, axis=0)
        bmat = jnp.concatenate(b, axis=0)
        b_lo = bmat[:8]
        row8 = lax.broadcasted_iota(jnp.int32, b_lo.shape, 0)
        pieces = [a[0] + b_lo, a[0] + bmat[8:], a[1] + b_lo]
        n_pad = 0
        for r in range(2, 8):
            pieces.append(jnp.where(row8 < PEER_TOPK // (r + 1), a[r] + b_lo, -jnp.inf))
            n_pad += 8 - PEER_TOPK // (r + 1)
        pieces.append(amat[8:] + b[0])
        cand = jnp.concatenate(pieces, axis=0)
        tied = None
        if exact:
            best, _ = _top_values(cand, PEER_TOPK + 1, False)
        else:
            best, rc = _top_values_untied(cand, PEER_TOPK + 1)
            count = lambda mask: jnp.sum(jnp.where(mask, 1.0, 0.0), axis=0, keepdims=True)
            wrong = jnp.logical_or(jnp.logical_or(count(ra) != PEER_TOPK, count(rb) != PEER_TOPK),
                                   count(rc) != PEER_TOPK + 1 + n_pad)
            tied = jnp.max(jnp.where(wrong, 1.0, 0.0)) > 0.0
        z = jnp.zeros_like(best[0])
        for n in range(PEER_TOPK):
            z = z + jnp.exp(best[n] - best[0])
        tau = 0.5 * (best[PEER_TOPK - 1] + best[PEER_TOPK])
        thr = tau - s0
        e0 = jnp.exp(s0 - a[0]) * (1.0 / z)
        e1 = jnp.exp(s1 - b[0])
        for tc in range(t0.shape[1] // LANES):
            cols = slice(tc * LANES, (tc + 1) * LANES)
            thr_ref[h, tc] = thr[:, cols]
            e0_ref[h, tc] = e0[:, cols]
            e1_ref[h, tc] = e1[:, cols]
            s1_ref[h, tc] = s1[:, cols]
        return tied

    def head(h, carry):
        t0 = sc_ref[pl.ds(pl.multiple_of(h * 2 * N_KEYS, N_KEYS), N_KEYS), :]
        t1 = sc_ref[pl.ds(pl.multiple_of(h * 2 * N_KEYS + N_KEYS, N_KEYS), N_KEYS), :]
        tied = route(h, t0, t1, exact=False)
        pl.when(tied)(lambda: route(h, t0, t1, exact=True))
        return carry

    lax.fori_loop(0, PEER_HEADS, head, 0)


def _peer_topk(q, kdt_hi, kdt_lo, tt=256):
    M = q.shape[0]
    tt = min(tt, M)
    R = PEER_HEADS * 2 * N_KEYS
    big = pl.BlockSpec((PEER_HEADS, tt // LANES, N_KEYS, LANES), lambda i: (0, i, 0, 0))
    return pl.pallas_call(
        _peer_topk_kernel, grid=(M // tt,),
        in_specs=[pl.BlockSpec((tt, PEER_HEADS * PEER_QDIM), lambda i: (i, 0)),
                  pl.BlockSpec((R, MXU_K), lambda i: (0, 0)),
                  pl.BlockSpec((R, MXU_K), lambda i: (0, 0))],
        out_specs=[big] * 4,
        out_shape=[jax.ShapeDtypeStruct((PEER_HEADS, M // LANES, N_KEYS, LANES), F32)] * 4,
        scratch_shapes=[pltpu.VMEM((R, tt), F32)],
        compiler_params=_cparams(("parallel",)))(q, kdt_hi, kdt_lo)


def _peer_expert_kernel(xt_ref, u_ref, vt_ref, thr_ref, e0_ref, e1_ref, s1_ref, res_ref, m_ref, o_ref,
                        acc_ref, act0_ref, act1_ref, am0_ref, am1_ref, *, ni, gate_slot):
    e = pl.program_id(1)
    n_blocks = pl.num_programs(1) - 2
    tt = xt_ref.shape[1]
    even = e % 2 == 0
    odd = e % 2 == 1

    @pl.when(e == 0)
    def _():
        acc_ref[...] = jnp.zeros_like(acc_ref)

    def project(act_w):
        act_w[...] = _dg(u_ref[...], xt_ref[...])

    def weigh(act_r, am_w):
        first = (e - 1) * ni
        for ii in range(ni):
            i = first + ii
            rows = slice(ii * N_KEYS, (ii + 1) * N_KEYS)
            for tc in range(tt // LANES):
                cols = slice(tc * LANES, (tc + 1) * LANES)
                w = None
                for h in range(PEER_HEADS):
                    keep = s1_ref[h, tc] >= thr_ref[h, tc, pl.ds(i, 1), :]
                    contrib = jnp.where(keep, e1_ref[h, tc], 0.0) * e0_ref[h, tc, pl.ds(i, 1), :]
                    w = contrib if w is None else w + contrib
                a = act_r[rows, cols]
                a = 0.5 * a * (1.0 + lax.erf(a * (2.0 ** -0.5)))
                am_w[rows, cols] = (a * w).astype(BF16)

    def accumulate(am_r):
        acc_ref[...] += _dg(vt_ref[...], am_r[...])

    has_project = e < n_blocks
    has_weigh = jnp.logical_and(e >= 1, e <= n_blocks)
    has_accumulate = e >= 2
    pl.when(jnp.logical_and(has_project, even))(lambda: project(act0_ref))
    pl.when(jnp.logical_and(has_project, odd))(lambda: project(act1_ref))
    pl.when(jnp.logical_and(has_weigh, even))(lambda: weigh(act1_ref, am1_ref))
    pl.when(jnp.logical_and(has_weigh, odd))(lambda: weigh(act0_ref, am0_ref))
    pl.when(jnp.logical_and(has_accumulate, even))(lambda: accumulate(am0_ref))
    pl.when(jnp.logical_and(has_accumulate, odd))(lambda: accumulate(am1_ref))

    @pl.when(e == pl.num_programs(1) - 1)
    def _():
        o_ref[...] = res_ref[...] + m_ref[gate_slot:gate_slot + 1, :] * acc_ref[...].T


def _peer_experts(xn, u_bf, vt_bf, thr, e0, e1, s1m, res, mods, gate_slot, rows_per_mod, tt=512):
    M, D = res.shape
    tt = min(tt, M, rows_per_mod) if mods.shape[0] > 1 else min(tt, M)
    nb, _, eb = vt_bf.shape
    ni = eb // N_KEYS
    once = pl.Buffered(1)
    if mods.shape[0] > 1:
        mspec = pl.BlockSpec((None, 6, D), lambda i, e: ((i * tt) // rows_per_mod, 0, 0), pipeline_mode=once)
    else:
        mspec = pl.BlockSpec((None, 6, D), lambda i, e: (0, 0, 0), pipeline_mode=once)
    big = pl.BlockSpec((PEER_HEADS, tt // LANES, N_KEYS, LANES), lambda i, e: (0, i, 0, 0), pipeline_mode=once)
    return pl.pallas_call(
        functools.partial(_peer_expert_kernel, ni=ni, gate_slot=gate_slot),
        grid=(M // tt, nb + 2),
        in_specs=[pl.BlockSpec((D, tt), lambda i, e: (0, i), pipeline_mode=once),
                  pl.BlockSpec((eb, D), lambda i, e: (jnp.minimum(e, nb - 1), 0)),
                  pl.BlockSpec((None, D, eb), lambda i, e: (jnp.clip(e - 2, 0, nb - 1), 0, 0)),
                  big, big, big, big,
                  pl.BlockSpec((tt, D), lambda i, e: (i, 0), pipeline_mode=once), mspec],
        out_specs=pl.BlockSpec((tt, D), lambda i, e: (i, 0)),
        out_shape=jax.ShapeDtypeStruct((M, D), F32),
        scratch_shapes=[pltpu.VMEM((D, tt), F32), pltpu.VMEM((eb, tt), F32), pltpu.VMEM((eb, tt), F32),
                        pltpu.VMEM((eb, tt), BF16), pltpu.VMEM((eb, tt), BF16)],
        compiler_params=_cparams(("parallel", "arbitrary"), VMEM_LIMIT_BIG))(
            xn, u_bf, vt_bf, thr, e0, e1, s1m, res, mods)


def _peer(x, mods, rows_per_mod, norm_g, prm):
    q, xn = _mm_norm(x, prm['wq'], norm_g=norm_g, mods=mods, slots=(3, 4), rows_per_mod=rows_per_mod,
                     hp=True, emit_xn=BF16)
    thr, e0, e1, s1m = _peer_topk(q, prm['kdt_hi'], prm['kdt_lo'])
    return _peer_experts(xn.T, prm['u'], prm['vt'], thr, e0, e1, s1m, x, mods, 5, rows_per_mod)


def _peer_params(wq, keys, u_tab, v_tab):
    nset = 2 * PEER_HEADS
    kd = jnp.zeros((nset, N_KEYS, PEER_GRP, PEER_HALF), F32)
    sets = jnp.arange(nset)
    kd = kd.at[sets, :, sets % PEER_GRP, :].set(keys.reshape(nset, N_KEYS, PEER_HALF))
    kd_hi, kd_lo = _split(kd.reshape(nset * N_KEYS, MXU_K))
    vt = v_tab.astype(BF16).reshape(N_EXPERTS // PEER_EB, PEER_EB, -1).transpose(0, 2, 1)
    return dict(wq=_split(wq), kdt_hi=kd_hi, kdt_lo=kd_lo, u=u_tab.astype(BF16), vt=vt)


def _layer_even(x, mods, B, T, norm1, w_a, w_b, w_out, qk_gains, rwkv_prm, ctx):
    rpm = T
    pa = _mm_norm(x, w_a, norm_g=norm1, mods=mods, slots=(0, 1), rows_per_mod=rpm)
    pb = _mm_norm(x, w_b, norm_g=norm1, mods=mods, slots=(0, 1), rows_per_mod=rpm)
    nqk = A_HEADS + A_KV_HEADS
    grp = A_HEADS // A_KV_HEADS
    scale = A_HEAD_DIM ** -0.5
    if ctx is None:
        qk = _norm_rope(pa, nqk, T, gains=qk_gains)
        oa = _attention(B, T, T, A_HEADS, scale, (qk, lambda h: h), (qk, lambda h: A_HEADS + h // grp),
                        (pa, lambda h: nqk + h // grp))
        ctx_out = (qk[:, A_Q:A_Q + A_KV], pa[:, A_Q + A_KV:])
        s0 = None
    else:
        cache_k, cache_v, s0 = ctx
        P = cache_k.shape[1]
        qk = _norm_rope(pa, nqk, T, gains=qk_gains, tables=_rope_tables(T, A_HEAD_DIM))
        k_all = jnp.concatenate([qk[:, A_Q:].reshape(B, T, A_KV), cache_k.reshape(B, P, A_KV)], axis=1)
        v_all = jnp.concatenate([pa[:, A_Q + A_KV:].reshape(B, T, A_KV), cache_v.reshape(B, P, A_KV)], axis=1)
        Tk = T + P
        oa = _attention(B, T, Tk, A_HEADS, scale, (qk, lambda h: h),
                        (k_all.reshape(B * Tk, A_KV).astype(BF16), lambda h: h // grp),
                        (v_all.reshape(B * Tk, A_KV).astype(BF16), lambda h: h // grp))
        ctx_out = None
    ob, s_fin = _rwkv_mixer(pb, B, T, rwkv_prm, s0)
    x = _mm_res([oa, ob], w_out, x, mods, 2, rpm)
    return x, ctx_out, s_fin


D_QROPE_BLK = 0
D_KROPE_BLK = D_HEADS
D_NROPE_BLK = D_HEADS + 1
D_CKV_BLK = 12
D_QNOPE_BLK = 16
D_PROJ = 24 * LANES


def _mla_in_weight(w_d):
    K = w_d.shape[0]
    wq = w_d[:, :D_Q].reshape(K, D_HEADS, D_QK)
    z64 = jnp.zeros((K, D_HEADS, LANES - D_ROPE), w_d.dtype)
    q_rope = jnp.concatenate([wq[:, :, D_NOPE:], z64], axis=-1).reshape(K, D_HEADS * LANES)
    q_nope = wq[:, :, :D_NOPE].reshape(K, D_HEADS * LANES)
    k_rope = jnp.concatenate([w_d[:, D_Q + D_KV_RANK:], jnp.zeros((K, LANES - D_ROPE), w_d.dtype)], axis=-1)
    padz = jnp.zeros((K, (D_CKV_BLK - D_NROPE_BLK) * LANES), w_d.dtype)
    return jnp.concatenate([q_rope, k_rope, padz, w_d[:, D_Q:D_Q + D_KV_RANK], q_nope], axis=-1)


def _layer_odd(x, mods, B, T, norm1, w_c, w_d, w_out, gdn_prm, kv_norm_g, w_kv_b, ctx):
    rpm = T
    pc = _mm_norm(x, w_c, norm_g=norm1, mods=mods, slots=(0, 1), rows_per_mod=rpm)
    pd = _mm_norm(x, w_d, norm_g=norm1, mods=mods, slots=(0, 1), rows_per_mod=rpm)
    scale = D_QK ** -0.5
    ckv_col = D_CKV_BLK * LANES // D_KV_RANK
    kv, ckv_n = _mm_norm(pd, w_kv_b, k=D_KV_RANK, xcol=ckv_col, norm_g=kv_norm_g, emit_xn=F32)
    kcol = lambda h: 2 * h
    vcol = lambda h: 2 * h + 1
    if ctx is None:
        od = _attention(B, T, T, D_HEADS, scale, (pd, lambda h: D_QNOPE_BLK + h), (kv, kcol), (kv, vcol),
                        q2=(pd, lambda h: D_QROPE_BLK + h), k2=(pd, lambda h: D_KROPE_BLK))
        ctx_out = (ckv_n, pd[:, D_KROPE_BLK * LANES:D_KROPE_BLK * LANES + D_ROPE])
        s0 = None
    else:
        s0, c_ckv, c_krope = ctx
        P = c_ckv.shape[1]
        Tk = T + P
        roped = _norm_rope(pd, D_NROPE_BLK, T, tables=_rope_tables(T, D_ROPE))
        kv_ctx = _mm_norm(c_ckv.reshape(B * P, D_KV_RANK), w_kv_b)
        kv_all = jnp.concatenate([kv.reshape(B, T, -1), kv_ctx.reshape(B, P, -1)], axis=1).reshape(B * Tk, -1)
        kv_all = kv_all.astype(BF16)
        kr_lat = roped[:, D_KROPE_BLK * LANES:].reshape(B, T, LANES)
        kr_ctx = jnp.concatenate([c_krope, jnp.zeros((B, P, LANES - D_ROPE), F32)], axis=-1)
        kr_all = jnp.concatenate([kr_lat, kr_ctx], axis=1).reshape(B * Tk, LANES).astype(BF16)
        od = _attention(B, T, Tk, D_HEADS, scale, (pd, lambda h: D_QNOPE_BLK + h), (kv_all, kcol), (kv_all, vcol),
                        q2=(roped, lambda h: D_QROPE_BLK + h), k2=(kr_all, lambda h: 0))
        ctx_out = None
    oc, s_fin = _gdn_mixer(pc, B, T, gdn_prm, s0)
    x = _mm_res([oc, od], w_out, x, mods, 2, rpm)
    return x, ctx_out, s_fin


def _modulation(cond, w_ada, b_ada):
    n = cond.shape[0]
    rows = ((n + 7) // 8) * 8
    a = cond * _sigmoid(cond)
    a = jnp.concatenate([a, jnp.zeros((rows - n, cond.shape[1]), F32)], axis=0)
    m = _mm_norm(a, _split(w_ada), hp=True, tn_cap=1024)[:n] + b_ada
    return m.reshape(n, 6, D_MODEL)


def kernel(x_prompt, x_sample, cache_attn_k, cache_attn_v, state_rwkv, state_gdn, cache_mla_ckv, cache_mla_krope,
           c, c_ctx, norm1_g, norm2_g, ada_w, ada_b, w_in_ab, w_out_ab, attn_q_norm, attn_k_norm,
           rwkv_mu, rwkv_w0, rwkv_w2, rwkv_a0, rwkv_a2, rwkv_g2, rwkv_k_k, rwkv_k_a, rwkv_r_k, rwkv_ln_w, rwkv_ln_b,
           w_in_cd, w_out_cd, gdn_conv, gdn_a_log, gdn_dt_bias, gdn_norm_g, mla_kv_norm_g, mla_w_kv_b,
           peer_wq, peer_keys, peer_u, peer_v, final_norm_g):
    Bp, Tp, D = x_prompt.shape
    Bs, Ts, _ = x_sample.shape
    depth = norm1_g.shape[0]
    xp = x_prompt.reshape(Bp * Tp, D)
    xs = x_sample.reshape(Bs * Ts, D)
    cond = jnp.concatenate([c_ctx[None, :], c], axis=0)
    outs = dict(k=[], v=[], rwkv=[], gdn=[], ckv=[], krope=[])
    for li in range(depth):
        j = li // 2
        mods = _modulation(cond, ada_w[li], ada_b[li])
        mods_p, mods_s = mods[:1], mods[1:]
        peer_prm = _peer_params(peer_wq[li], peer_keys[li], peer_u[li], peer_v[li])
        if li % 2 == 0:
            w_in = w_in_ab[j].astype(BF16)
            w_a, w_b = w_in[:, :A_IN], w_in[:, A_IN:]
            w_o = w_out_ab[j].astype(BF16)
            w_out = [w_o[:A_Q], w_o[A_Q:]]
            gains = jnp.concatenate([jnp.tile(attn_q_norm[j], A_HEADS), jnp.tile(attn_k_norm[j], A_KV_HEADS)])
            rp = dict(mu=rwkv_mu[j], w0=rwkv_w0[j], w2=rwkv_w2[j], a0=rwkv_a0[j], a2=rwkv_a2[j], g2=rwkv_g2[j],
                      k_k=rwkv_k_k[j], k_a=rwkv_k_a[j], r_k=rwkv_r_k[j], ln_w=rwkv_ln_w[j], ln_b=rwkv_ln_b[j])
            xp, (k_c, v_c), s_c = _layer_even(xp, mods_p, Bp, Tp, norm1_g[li], w_a, w_b, w_out, gains, rp, None)
            xs, _, _ = _layer_even(xs, mods_s, Bs, Ts, norm1_g[li], w_a, w_b, w_out, gains, rp,
                                   (cache_attn_k[:, j], cache_attn_v[:, j], state_rwkv[:, j]))
            outs['k'].append(k_c.reshape(Bp, Tp, A_KV_HEADS, A_HEAD_DIM))
            outs['v'].append(v_c.reshape(Bp, Tp, A_KV_HEADS, A_HEAD_DIM))
            outs['rwkv'].append(s_c)
        else:
            w_in = w_in_cd[j]
            w_c = jnp.concatenate([w_in[:, :C_IN], jnp.zeros((D, LANES - 4 * C_HEADS), F32)], axis=-1).astype(BF16)
            w_d = _mla_in_weight(w_in[:, C_IN:]).astype(BF16)
            w_o = w_out_cd[j].astype(BF16)
            w_out = [w_o[:C_W], w_o[C_W:]]
            gp = dict(conv=gdn_conv[j], a_log=gdn_a_log[j], dt_bias=gdn_dt_bias[j], norm_g=gdn_norm_g[j])
            w_kv_b = mla_w_kv_b[j].astype(BF16)
            xp, (ckv_c, kr_c), s_c = _layer_odd(xp, mods_p, Bp, Tp, norm1_g[li], w_c, w_d, w_out, gp,
                                                mla_kv_norm_g[j], w_kv_b, None)
            xs, _, _ = _layer_odd(xs, mods_s, Bs, Ts, norm1_g[li], w_c, w_d, w_out, gp, mla_kv_norm_g[j], w_kv_b,
                                  (state_gdn[:, j], cache_mla_ckv[:, j], cache_mla_krope[:, j]))
            outs['gdn'].append(s_c)
            outs['ckv'].append(ckv_c.reshape(Bp, Tp, D_KV_RANK))
            outs['krope'].append(kr_c.reshape(Bp, Tp, D_ROPE))
        xp = _peer(xp, mods_p, Tp, norm2_g[li], peer_prm)
        xs = _peer(xs, mods_s, Ts, norm2_g[li], peer_prm)
    y_prompt = _rmsnorm(xp, final_norm_g).reshape(Bp, Tp, D)
    y_sample = _rmsnorm(xs, final_norm_g).reshape(Bs, Ts, D)
    st = lambda name: jnp.stack(outs[name], axis=1)
    return (y_prompt, y_sample, st('k'), st('v'), st('rwkv'), st('gdn'), st('ckv'), st('krope'))
```

```python
import functools
import math

import numpy as np
import jax
import jax.numpy as jnp
from jax import lax
from jax.experimental import pallas as pl
from jax.experimental.pallas import tpu as pltpu

F32 = jnp.float32
BF16 = jnp.bfloat16

D_MODEL = 2048
GRID_W = 64
ROPE_THETA = 10000.0
NORM_EPS = 1e-6
A_HEADS, A_KV_HEADS, A_HEAD_DIM = 8, 2, 128
A_Q = A_HEADS * A_HEAD_DIM
A_KV = A_KV_HEADS * A_HEAD_DIM
A_IN = A_Q + 2 * A_KV
B_HEADS, B_HEAD_DIM = 16, 64
B_W = B_HEADS * B_HEAD_DIM
B_DECAY_LORA, B_AAA_LORA, B_GATE_LORA = 64, 64, 128
B_IN = 3 * B_W + 2 * B_DECAY_LORA + 2 * B_AAA_LORA + B_GATE_LORA
B_GN_EPS = 64e-5
C_HEADS, C_HEAD_DIM = 8, 128
C_W = C_HEADS * C_HEAD_DIM
C_IN = 4 * C_W + 4 * C_HEADS
D_HEADS, D_NOPE, D_ROPE, D_V = 8, 128, 64, 128
D_QK = D_NOPE + D_ROPE
D_KV_RANK = 512
D_Q = D_HEADS * D_QK
PEER_HEADS, N_KEYS, PEER_QDIM, PEER_TOPK = 8, 128, 128, 16
PEER_HALF = PEER_QDIM // 2
N_EXPERTS = N_KEYS * N_KEYS

LANES = 128
MXU_K = 256
VMEM_LIMIT = 48 * 1024 * 1024
VMEM_LIMIT_BIG = 56 * 1024 * 1024

CHUNK = 64

NN = ((1,), (0,))
NT = ((1,), (1,))
TN = ((0,), (0,))


def _dg(a, b, dims=NN):
    return lax.dot_general(a, b, (dims, ((), ())), preferred_element_type=F32)


def _split(x):
    hi = x.astype(BF16)
    lo = (x - hi.astype(F32)).astype(BF16)
    return hi, lo


def _dot_hp(a, b, dims=NN):
    ah, al = _split(a)
    bh, bl = _split(b)
    return _dg(ah, bh, dims) + (_dg(ah, bl, dims) + _dg(al, bh, dims))


def _dot_bf(a, b, dims=NN):
    return _dg(a.astype(BF16), b.astype(BF16), dims)


def _sigmoid(x):
    return 1.0 / (1.0 + jnp.exp(-x))


def _softplus(x):
    return jnp.maximum(x, 0.0) + jnp.log(1.0 + jnp.exp(-jnp.abs(x)))


def _cparams(sem, vmem=VMEM_LIMIT):
    return pltpu.CompilerParams(dimension_semantics=sem, vmem_limit_bytes=vmem)


def _pick_tile(n, cap):
    best = None
    t = LANES
    while t <= min(n, cap):
        if n % t == 0:
            best = t
        t += LANES
    return best if best is not None else n


def _mm_norm_kernel(*refs, has_norm, has_mod, shift_slot, scale_slot, hp, emit_xn):
    it = iter(refs)
    x_ref = next(it)
    g_ref = next(it) if has_norm else None
    m_ref = next(it) if has_mod else None
    wh_ref = next(it)
    wl_ref = next(it) if hp else None
    o_ref = next(it)
    xn_ref = next(it) if emit_xn else None
    xh_ref = next(it)
    xl_ref = next(it) if hp else None

    @pl.when(pl.program_id(1) == 0)
    def _():
        xv = x_ref[...].astype(F32)
        if has_norm:
            xv = xv * lax.rsqrt(jnp.mean(xv * xv, axis=-1, keepdims=True) + NORM_EPS) * g_ref[...]
        if has_mod:
            xv = xv * (1.0 + m_ref[scale_slot:scale_slot + 1, :]) + m_ref[shift_slot:shift_slot + 1, :]
        hi = xv.astype(BF16)
        xh_ref[...] = hi
        if hp:
            xl_ref[...] = (xv - hi.astype(F32)).astype(BF16)
        if emit_xn:
            xn_ref[...] = xv.astype(xn_ref.dtype)

    acc = _dg(xh_ref[...], wh_ref[...])
    if hp:
        acc = acc + (_dg(xh_ref[...], wl_ref[...]) + _dg(xl_ref[...], wh_ref[...]))
    o_ref[...] = acc


def _mm_norm(x, w, *, k=None, xcol=0, norm_g=None, mods=None, slots=(0, 1), rows_per_mod=None,
             hp=False, emit_xn=None, tm=512, tn_cap=1664):
    M = x.shape[0]
    K = k if k is not None else x.shape[1]
    ws = tuple(w) if isinstance(w, (tuple, list)) else (w,)
    N = ws[0].shape[1]
    tm = min(tm, M)
    if mods is not None and mods.shape[0] > 1:
        tm = min(tm, rows_per_mod)
    tn = _pick_tile(N, tn_cap)
    has_norm = norm_g is not None
    has_mod = mods is not None
    rpm = rows_per_mod if rows_per_mod is not None else M

    in_specs = [pl.BlockSpec((tm, K), lambda i, j: (i, xcol))]
    args = [x]
    if has_norm:
        in_specs.append(pl.BlockSpec((1, K), lambda i, j: (0, 0)))
        args.append(norm_g.reshape(1, K).astype(F32))
    if has_mod:
        if mods.shape[0] > 1:
            in_specs.append(pl.BlockSpec((None, 6, K), lambda i, j: ((i * tm) // rpm, 0, 0)))
        else:
            in_specs.append(pl.BlockSpec((None, 6, K), lambda i, j: (0, 0, 0)))
        args.append(mods)
    for wi in ws:
        in_specs.append(pl.BlockSpec((K, tn), lambda i, j: (0, j)))
        args.append(wi)
    out_shape = [jax.ShapeDtypeStruct((M, N), F32)]
    out_specs = [pl.BlockSpec((tm, tn), lambda i, j: (i, j))]
    if emit_xn is not None:
        out_shape.append(jax.ShapeDtypeStruct((M, K), emit_xn))
        out_specs.append(pl.BlockSpec((tm, K), lambda i, j: (i, 0)))
    scratch = [pltpu.VMEM((tm, K), BF16)]
    if hp:
        scratch.append(pltpu.VMEM((tm, K), BF16))
    kern = functools.partial(_mm_norm_kernel, has_norm=has_norm, has_mod=has_mod, shift_slot=slots[0],
                             scale_slot=slots[1], hp=hp, emit_xn=emit_xn is not None)
    outs = pl.pallas_call(
        kern, grid=(M // tm, N // tn), in_specs=in_specs, out_specs=out_specs, out_shape=out_shape,
        scratch_shapes=scratch, compiler_params=_cparams(("parallel", "arbitrary")))(*args)
    return outs if emit_xn is not None else outs[0]


def _mm_res_kernel(*refs, n_in, gate_slot):
    a_refs = refs[:n_in]
    w_refs = refs[n_in:2 * n_in]
    res_ref, m_ref, o_ref = refs[2 * n_in:]
    acc = None
    for a_ref, w_ref in zip(a_refs, w_refs):
        part = _dg(a_ref[...].astype(BF16), w_ref[...])
        acc = part if acc is None else acc + part
    o_ref[...] = res_ref[...] + m_ref[gate_slot:gate_slot + 1, :] * acc


def _mm_res(a_list, w_list, res, mods, gate_slot, rows_per_mod, tm=512, tn=1024):
    M, N = res.shape
    tm = min(tm, M, rows_per_mod) if mods.shape[0] > 1 else min(tm, M)
    tn = min(tn, N)
    n_in = len(a_list)
    in_specs, args = [], []
    for a in a_list:
        in_specs.append(pl.BlockSpec((tm, a.shape[1]), lambda i, j: (i, 0)))
        args.append(a)
    for w in w_list:
        in_specs.append(pl.BlockSpec((w.shape[0], tn), lambda i, j: (0, j)))
        args.append(w)
    in_specs.append(pl.BlockSpec((tm, tn), lambda i, j: (i, j)))
    args.append(res)
    if mods.shape[0] > 1:
        in_specs.append(pl.BlockSpec((None, 6, tn), lambda i, j: ((i * tm) // rows_per_mod, 0, j)))
    else:
        in_specs.append(pl.BlockSpec((None, 6, tn), lambda i, j: (0, 0, j)))
    args.append(mods)
    return pl.pallas_call(
        functools.partial(_mm_res_kernel, n_in=n_in, gate_slot=gate_slot),
        grid=(M // tm, N // tn), in_specs=in_specs,
        out_specs=pl.BlockSpec((tm, tn), lambda i, j: (i, j)),
        out_shape=jax.ShapeDtypeStruct((M, N), F32),
        compiler_params=_cparams(("parallel", "arbitrary")))(*args)


def _rmsnorm_kernel(x_ref, g_ref, o_ref):
    xv = x_ref[...]
    o_ref[...] = xv * lax.rsqrt(jnp.mean(xv * xv, axis=-1, keepdims=True) + NORM_EPS) * g_ref[...]


def _rmsnorm(x, g, tm=512):
    M, K = x.shape
    tm = min(tm, M)
    return pl.pallas_call(
        _rmsnorm_kernel, grid=(M // tm,),
        in_specs=[pl.BlockSpec((tm, K), lambda i: (i, 0)), pl.BlockSpec((1, K), lambda i: (0, 0))],
        out_specs=pl.BlockSpec((tm, K), lambda i: (i, 0)),
        out_shape=jax.ShapeDtypeStruct((M, K), F32),
        compiler_params=_cparams(("parallel",)))(x, g.reshape(1, K))


def _rope_tables(T, rot_dim):
    rows = T // GRID_W
    row = jnp.repeat(jnp.arange(rows, dtype=F32), GRID_W)
    col = jnp.tile(jnp.arange(GRID_W, dtype=F32), rows)
    n_freq = rot_dim // 4
    inv = ROPE_THETA ** (-jnp.arange(n_freq, dtype=F32) / n_freq)
    ang = jnp.concatenate([row[:, None] * inv, col[:, None] * inv], axis=-1)
    cos = jnp.repeat(jnp.cos(ang), 2, axis=-1)
    sin = jnp.repeat(jnp.sin(ang), 2, axis=-1) * jnp.tile(jnp.array([-1.0, 1.0], F32), rot_dim // 2)
    pad = LANES - rot_dim
    if pad:
        cos = jnp.concatenate([cos, jnp.ones((T, pad), F32)], axis=-1)
        sin = jnp.concatenate([sin, jnp.zeros((T, pad), F32)], axis=-1)
    return cos, sin


def _rope(y, cosf, sinf):
    lane = lax.broadcasted_iota(jnp.int32, y.shape, 1)
    even = (lane & 1) == 0
    width = y.shape[1]
    swap = jnp.where(even, pltpu.roll(y, width - 1, 1), pltpu.roll(y, 1, 1))
    return y * cosf + swap * sinf


def _norm_rope_kernel(*refs, nblk, norm, rope):
    it = iter(refs)
    x_ref = next(it)
    g_ref = next(it) if norm else None
    cos_ref = next(it) if rope else None
    sin_ref = next(it) if rope else None
    o_ref = next(it)
    for c in range(nblk):
        sl = slice(c * LANES, (c + 1) * LANES)
        y = x_ref[:, sl]
        if norm:
            y = y * lax.rsqrt(jnp.mean(y * y, axis=-1, keepdims=True) + NORM_EPS) * g_ref[:, sl]
        if rope:
            y = _rope(y, cos_ref[...], sin_ref[...])
        o_ref[:, sl] = y


def _norm_rope(x, nblk, T, gains=None, tables=None, tt=256):
    M = x.shape[0]
    tt = min(tt, T)
    W = nblk * LANES
    in_specs = [pl.BlockSpec((tt, W), lambda i: (i, 0))]
    args = [x]
    if gains is not None:
        in_specs.append(pl.BlockSpec((1, W), lambda i: (0, 0)))
        args.append(gains.reshape(1, W))
    if tables is not None:
        nt = T // tt
        for t in tables:
            in_specs.append(pl.BlockSpec((tt, LANES), lambda i: (i % nt, 0)))
            args.append(t)
    return pl.pallas_call(
        functools.partial(_norm_rope_kernel, nblk=nblk, norm=gains is not None, rope=tables is not None),
        grid=(M // tt,), in_specs=in_specs,
        out_specs=pl.BlockSpec((tt, W), lambda i: (i, 0)),
        out_shape=jax.ShapeDtypeStruct((M, W), F32),
        compiler_params=_cparams(("parallel",)))(*args)


def _attn_kernel(*refs, scale, has2):
    it = iter(refs)
    q_ref = next(it)
    q2_ref = next(it) if has2 else None
    k_ref = next(it)
    k2_ref = next(it) if has2 else None
    v_ref = next(it)
    o_ref = next(it)
    if has2:
        qq = jnp.concatenate([q_ref[...], q2_ref[...]], axis=1) * scale
        s = _dot_bf(qq, jnp.concatenate([k_ref[...], k2_ref[...]], axis=1), NT)
    else:
        s = _dot_bf(q_ref[...] * scale, k_ref[...], NT)
    m = jnp.max(s, axis=-1, keepdims=True)
    p = jnp.exp(s - m)
    l = jnp.sum(p, axis=-1, keepdims=True)
    o_ref[...] = _dot_bf(p, v_ref[...]) / l


def _attention(B, T, Tk, H, scale, q, k, v, q2=None, k2=None, tq=256):
    tq = min(tq, T)
    nq = T // tq
    has2 = q2 is not None

    def qspec(cf):
        return pl.BlockSpec((tq, LANES), lambda b, h, i: (b * nq + i, cf(h)))

    def kspec(cf):
        return pl.BlockSpec((Tk, LANES), lambda b, h, i: (b, cf(h)))

    in_specs, args = [qspec(q[1])], [q[0]]
    if has2:
        in_specs.append(qspec(q2[1]))
        args.append(q2[0])
    in_specs.append(kspec(k[1]))
    args.append(k[0])
    if has2:
        in_specs.append(kspec(k2[1]))
        args.append(k2[0])
    in_specs.append(kspec(v[1]))
    args.append(v[0])
    return pl.pallas_call(
        functools.partial(_attn_kernel, scale=scale, has2=has2),
        grid=(B, H, nq), in_specs=in_specs,
        out_specs=pl.BlockSpec((tq, LANES), lambda b, h, i: (b * nq + i, h)),
        out_shape=jax.ShapeDtypeStruct((B * T, H * LANES), F32),
        compiler_params=_cparams(("parallel", "arbitrary", "arbitrary")))(*args)


def _tri_masks(L, rev):
    i = lax.broadcasted_iota(jnp.int32, (L, L), 0)
    j = lax.broadcasted_iota(jnp.int32, (L, L), 1)
    if rev:
        return j >= i, j > i, i == j
    return j <= i, j < i, i == j


def _inv_unit_tri(mats, eye_f, L):
    pows = [[-a for a in mats]]
    for _ in range(int(math.log2(L)) - 1):
        pows.append([_dot_bf(p, p) for p in pows[-1]])
    terms = [[eye_f + p for p in level] for level in pows]
    while len(terms) > 1:
        nxt = [[_dot_bf(x, y) for x, y in zip(terms[i], terms[i + 1])] for i in range(0, len(terms) - 1, 2)]
        if len(terms) % 2:
            nxt.append(terms[-1])
        terms = nxt
    x0 = terms[0]
    resid = [(eye_f - x) - _dot_hp(a, x) for a, x in zip(mats, x0)]
    return [x + _dot_bf(x, r) for x, r in zip(x0, resid)]


def _shifted(f, prev_ref, next_ref, tt, T):
    i = pl.program_id(0)
    first = (i * tt) % T == 0
    last = ((i + 1) * tt) % T == 0
    prev = jnp.where(first, 0.0, prev_ref[7:8, :])
    nxt = jnp.where(last, 0.0, next_ref[0:1, :])
    rows = lax.broadcasted_iota(jnp.int32, f.shape, 0)
    fp = jnp.where(rows == 0, prev, pltpu.roll(f, 1, 0))
    fn = jnp.where(rows == tt - 1, nxt, pltpu.roll(f, tt - 1, 0))
    return fp, fn


def _halo_specs(tt, width, M, col_fn):
    r8 = tt // 8
    last8 = M // 8 - 1
    return [
        pl.BlockSpec((tt, width), lambda *g: (g[0], col_fn(*g))),
        pl.BlockSpec((8, width), lambda *g: (jnp.maximum(g[0] * r8 - 1, 0), col_fn(*g))),
        pl.BlockSpec((8, width), lambda *g: (jnp.minimum((g[0] + 1) * r8, last8), col_fn(*g))),
    ]


NPAIR = B_HEADS // 2


def _pair_sum_matrix():
    r = lax.broadcasted_iota(jnp.int32, (LANES, LANES), 0) // B_HEAD_DIM
    c = lax.broadcasted_iota(jnp.int32, (LANES, LANES), 1) // B_HEAD_DIM
    return (r == c).astype(F32)


def _rwkv_prep_kernel(x_ref, prev_ref, next_ref, mu_ref, w0_ref, w2_ref, a0_ref, a2_ref, g2_ref, kk_ref_in,
                      r_o, k_o, v_o, kk_o, g_o, lw0_o, lw1_o, a0_o, a1_o, *, tt, T):
    f = x_ref[...]
    fp, fn = _shifted(f, prev_ref, next_ref, tt, T)
    f = f + mu_ref[0:1, :] * (fp - f) + mu_ref[1:2, :] * (fn - f)
    r = f[:, :B_W]
    k = f[:, B_W:2 * B_W]
    v = f[:, 2 * B_W:3 * B_W]
    off = 3 * B_W
    w_lo = f[:, off:off + 2 * B_DECAY_LORA]
    off += 2 * B_DECAY_LORA
    a_lo = f[:, off:off + 2 * B_AAA_LORA]
    off += 2 * B_AAA_LORA
    g_lo = f[:, off:]
    th = jnp.tanh(w_lo)
    logw, aa = [], []
    for d in range(2):
        wpre = w0_ref[d:d + 1, :] + _dot_hp(th[:, d * B_DECAY_LORA:(d + 1) * B_DECAY_LORA], w2_ref[d])
        wv = -_softplus(-wpre) - 0.5
        logw.append(-jnp.exp(wv))
        aa.append(_sigmoid(a0_ref[d:d + 1, :] + _dot_hp(a_lo[:, d * B_AAA_LORA:(d + 1) * B_AAA_LORA], a2_ref[d])))
    g = _dot_hp(_sigmoid(g_lo), g2_ref[...])
    kk = k * kk_ref_in[...]
    gsum = _pair_sum_matrix()
    for p in range(NPAIR):
        sl = slice(p * LANES, (p + 1) * LANES)
        kp = kk[:, sl]
        ssq = _dot_hp(kp * kp, gsum)
        kk_o[p] = kp / jnp.maximum(jnp.sqrt(ssq), 1e-12)
        r_o[p] = r[:, sl]
        k_o[p] = k[:, sl]
        v_o[p] = v[:, sl]
        g_o[p] = g[:, sl]
        lw0_o[p] = logw[0][:, sl]
        lw1_o[p] = logw[1][:, sl]
        a0_o[p] = aa[0][:, sl]
        a1_o[p] = aa[1][:, sl]


def _rwkv_prep(pb, B, T, mu, w0, w2, a0, a2, g2, k_k, tt=128):
    M = pb.shape[0]
    tt = min(tt, T)
    nt = T // tt
    full = lambda shape: pl.BlockSpec(shape, lambda i: tuple(0 for _ in shape))
    in_specs = _halo_specs(tt, B_IN, M, lambda i: 0) + [
        full((2, B_IN)), full((2, B_W)), full((2, B_DECAY_LORA, B_W)), full((2, B_W)),
        full((2, B_AAA_LORA, B_W)), full((B_GATE_LORA, B_W)), full((1, B_W))]
    ospec = pl.BlockSpec((None, NPAIR, tt, LANES), lambda i: (i // nt, 0, i % nt, 0))
    oshape = jax.ShapeDtypeStruct((B, NPAIR, T, LANES), F32)
    return pl.pallas_call(
        functools.partial(_rwkv_prep_kernel, tt=tt, T=T),
        grid=(M // tt,), in_specs=in_specs, out_specs=[ospec] * 9, out_shape=[oshape] * 9,
        compiler_params=_cparams(("parallel",)))(pb, pb, pb, mu, w0, w2, a0, a2, g2, k_k.reshape(1, B_W))


def _rwkv_chains(r2, lw2, kd2, v2, kk2, a2, L, rev):
    incl, strict, _ = _tri_masks(L, rev)
    incl_f = incl.astype(F32)
    i2 = lax.broadcasted_iota(jnp.int32, (L, 2 * L), 0)
    j2 = lax.broadcasted_iota(jnp.int32, (L, 2 * L), 1) % L
    incl2 = (j2 >= i2) if rev else (j2 <= i2)
    last = 0 if rev else L - 1
    halves = (slice(0, B_HEAD_DIM), slice(B_HEAD_DIM, 2 * B_HEAD_DIM))
    logp2 = [_dot_hp(incl_f, lw) for lw in lw2]
    chains = []
    for lp, lw, r, kd, v, kk, a in zip(logp2, lw2, r2, kd2, v2, kk2, a2):
        ninv = jnp.exp(-lp)
        b = kk * a
        kt = kk * jnp.exp(lp - lw)
        rt = r * jnp.exp(lp)
        bn, kdn = b * ninv, kd * ninv
        pl_row = lp[last:last + 1, :]
        rem = jnp.exp(pl_row - lp)
        bh, kh = b * rem, kd * rem
        dpl = jnp.exp(pl_row)
        for sl in halves:
            chains.append(dict(
                strict=strict, incl2=incl2, dec=dpl[:, sl], v=v[:, sl],
                lhs=jnp.concatenate([kt[:, sl], rt[:, sl]], axis=0),
                rhs=jnp.concatenate([bn[:, sl], kdn[:, sl]], axis=0),
                rem_rhs=jnp.concatenate([bh[:, sl], kh[:, sl]], axis=0)))
    return chains


def _rwkv_chunk(chains, states, L):
    row = lax.broadcasted_iota(jnp.int32, (L, L), 0)
    col = lax.broadcasted_iota(jnp.int32, (L, L), 1)
    eye_f = (row == col).astype(F32)
    gm = [_dot_hp(ch['lhs'], ch['rhs'], NT) for ch in chains]
    ls = [_dot_hp(ch['lhs'], s, NT) for ch, s in zip(chains, states)]
    akv = [_dot_hp(jnp.where(ch['strict'], g[:L, L:], 0.0), ch['v']) for g, ch in zip(gm, chains)]
    tinv = _inv_unit_tri([jnp.where(ch['strict'], g[:L, :L], 0.0) for g, ch in zip(gm, chains)], eye_f, L)
    u = [-_dot_hp(t, l[:L] + x) for t, l, x in zip(tinv, ls, akv)]
    uv = [jnp.concatenate([x, ch['v']], axis=0) for x, ch in zip(u, chains)]
    y = [l[L:] + _dot_hp(jnp.where(ch['incl2'], g[L:, :], 0.0), x) for l, g, x, ch in zip(ls, gm, uv, chains)]
    s_new = [s * ch['dec'] + _dot_hp(x, ch['rem_rhs'], TN) for s, ch, x in zip(states, chains, uv)]
    return y, s_new


def _rwkv_scan_kernel(rf_ref, vf_ref, kkf_ref, kf_ref, lwf_ref, af_ref, rr_ref, vr_ref, kkr_ref, kr_ref, lwr_ref,
                      ar_ref, ka_ref, s0_ref, yf_ref, yr_ref, sf_ref, s_ref, *, L):
    c = pl.program_id(1)

    @pl.when(c == 0)
    def _():
        s_ref[...] = s0_ref[...]

    pairs = range(NPAIR)
    chains = []
    for rev, (r_ref, v_ref, kk_ref, k_ref, lw_ref, a_ref) in enumerate(
            ((rf_ref, vf_ref, kkf_ref, kf_ref, lwf_ref, af_ref), (rr_ref, vr_ref, kkr_ref, kr_ref, lwr_ref, ar_ref))):
        a2 = [a_ref[p] for p in pairs]
        kd2 = [k_ref[p] * (1.0 + (a - 1.0) * ka_ref[p]) for p, a in zip(pairs, a2)]
        chains += _rwkv_chains([r_ref[p] for p in pairs], [lw_ref[p] for p in pairs], kd2,
                               [v_ref[p] for p in pairs], [kk_ref[p] for p in pairs], a2, L, bool(rev))
    y, s_new = _rwkv_chunk(chains, [s_ref[d, h] for d in range(2) for h in range(B_HEADS)], L)
    for p in pairs:
        yf_ref[p] = jnp.concatenate([y[2 * p], y[2 * p + 1]], axis=1)
        yr_ref[p] = jnp.concatenate([y[B_HEADS + 2 * p], y[B_HEADS + 2 * p + 1]], axis=1)
    for h in range(B_HEADS):
        s_ref[0, h] = s_new[h]
        s_ref[1, h] = s_new[B_HEADS + h]

    @pl.when(c == pl.num_programs(1) - 1)
    def _():
        sf_ref[...] = s_ref[...]


def _rwkv_scan(r, v, kk, k, lws, aas, k_a, s0, L=CHUNK):
    B, _, T, _ = r.shape
    nc = T // L
    fwd = pl.BlockSpec((None, NPAIR, L, LANES), lambda b, c: (b, 0, c, 0))
    bwd = pl.BlockSpec((None, NPAIR, L, LANES), lambda b, c: (b, 0, nc - 1 - c, 0))
    st = pl.BlockSpec((None, 2, B_HEADS, B_HEAD_DIM, B_HEAD_DIM), lambda b, c: (b, 0, 0, 0, 0))
    seq_shape = jax.ShapeDtypeStruct((B, NPAIR, T, LANES), F32)
    return pl.pallas_call(
        functools.partial(_rwkv_scan_kernel, L=L),
        grid=(B, nc),
        in_specs=[fwd] * 6 + [bwd] * 6 + [pl.BlockSpec((NPAIR, 1, LANES), lambda b, c: (0, 0, 0)), st],
        out_specs=[fwd, bwd, st],
        out_shape=[seq_shape, seq_shape, jax.ShapeDtypeStruct((B, 2, B_HEADS, B_HEAD_DIM, B_HEAD_DIM), F32)],
        scratch_shapes=[pltpu.VMEM((2, B_HEADS, B_HEAD_DIM, B_HEAD_DIM), F32)],
        compiler_params=_cparams(("parallel", "arbitrary")))(
            r, v, kk, k, lws[0], aas[0], r, v, kk, k, lws[1], aas[1], k_a.reshape(NPAIR, 1, LANES), s0)


def _rwkv_post_kernel(y0_ref, y1_ref, r_ref, k_ref, v_ref, g_ref, rk_ref, lnw_ref, lnb_ref, o_ref):
    gsum = _pair_sum_matrix()
    inv_n = 1.0 / B_HEAD_DIM
    for p in range(NPAIR):
        y = y0_ref[p] + y1_ref[p]
        mean = _dot_hp(y, gsum) * inv_n
        yc = y - mean
        var = _dot_hp(yc * yc, gsum) * inv_n
        yn = yc * lax.rsqrt(var + B_GN_EPS) * lnw_ref[p] + lnb_ref[p]
        bonus = _dot_hp(r_ref[p] * k_ref[p] * rk_ref[p], gsum) * v_ref[p]
        o_ref[:, p * LANES:(p + 1) * LANES] = (yn + bonus) * g_ref[p]


def _rwkv_post(y0, y1, r, k, v, g, r_k, ln_w, ln_b, tt=256):
    B, _, T, _ = r.shape
    tt = min(tt, T)
    nt = T // tt
    seq = pl.BlockSpec((None, NPAIR, tt, LANES), lambda i: (i // nt, 0, i % nt, 0))
    par = pl.BlockSpec((NPAIR, 1, LANES), lambda i: (0, 0, 0))
    return pl.pallas_call(
        _rwkv_post_kernel, grid=(B * nt,),
        in_specs=[seq] * 6 + [par] * 3,
        out_specs=pl.BlockSpec((tt, B_W), lambda i: (i, 0)),
        out_shape=jax.ShapeDtypeStruct((B * T, B_W), F32),
        compiler_params=_cparams(("parallel",)))(
            y0, y1, r, k, v, g, r_k.reshape(NPAIR, 1, LANES), ln_w.reshape(NPAIR, 1, LANES),
            ln_b.reshape(NPAIR, 1, LANES))


def _rwkv_mixer(pb, B, T, prm, s0):
    r, k, v, kk, g, lw0, lw1, a0, a1 = _rwkv_prep(pb, B, T, prm['mu'], prm['w0'], prm['w2'], prm['a0'],
                                                   prm['a2'], prm['g2'], prm['k_k'])
    if s0 is None:
        s0 = jnp.zeros((B, 2, B_HEADS, B_HEAD_DIM, B_HEAD_DIM), F32)
    y0, y1, s_fin = _rwkv_scan(r, v, kk, k, (lw0, lw1), (a0, a1), prm['k_a'], s0)
    out = _rwkv_post(y0, y1, r, k, v, g, prm['r_k'], prm['ln_w'], prm['ln_b'])
    return out, s_fin


def _gdn_prep_kernel(x_ref, prev_ref, next_ref, cw_ref, o_ref, *, tt, T):
    c = pl.program_id(1)
    x = x_ref[...]
    xp, xn = _shifted(x, prev_ref, next_ref, tt, T)
    y = cw_ref[0:1, :] * xp + cw_ref[1:2, :] * x + cw_ref[2:3, :] * xn
    y = y * _sigmoid(y)

    @pl.when(c == 2)
    def _():
        o_ref[...] = y

    @pl.when(c < 2)
    def _():
        scale = jnp.where(c == 0, C_HEAD_DIM ** -0.5, 1.0).astype(F32)
        for h in range(C_HEADS):
            sl = slice(h * C_HEAD_DIM, (h + 1) * C_HEAD_DIM)
            yh = y[:, sl]
            o_ref[:, sl] = yh * lax.rsqrt(jnp.sum(yh * yh, axis=-1, keepdims=True) + 1e-6) * scale


def _gdn_prep(pc, T, conv_w, tt=256):
    M = pc.shape[0]
    tt = min(tt, T)
    in_specs = _halo_specs(tt, C_W, M, lambda i, c: c) + [pl.BlockSpec((3, C_W), lambda i, c: (0, c))]
    return pl.pallas_call(
        functools.partial(_gdn_prep_kernel, tt=tt, T=T),
        grid=(M // tt, 3), in_specs=in_specs,
        out_specs=pl.BlockSpec((tt, C_W), lambda i, c: (i, c)),
        out_shape=jax.ShapeDtypeStruct((M, 3 * C_W), F32),
        compiler_params=_cparams(("parallel", "arbitrary")))(pc, pc, pc, conv_w)


def _gdn_bg_kernel(x_ref, alog_ref, dtb_ref, o_ref):
    x = x_ref[...]
    lane = lax.broadcasted_iota(jnp.int32, x.shape, 1)
    beta = _sigmoid(x)
    g = -jnp.exp(alog_ref[...]) * _softplus(x + dtb_ref[...])
    o_ref[...] = jnp.where(lane < 2 * C_HEADS, beta, g)


def _gdn_bg(pc, a_log, dt_bias, tt=512):
    M = pc.shape[0]
    tt = min(tt, M)
    nb = 2 * C_HEADS
    pad = lambda p: jnp.concatenate([jnp.zeros((nb,), F32), p.reshape(nb), jnp.zeros((LANES - 2 * nb,), F32)]).reshape(1, LANES)
    return pl.pallas_call(
        _gdn_bg_kernel, grid=(M // tt,),
        in_specs=[pl.BlockSpec((tt, LANES), lambda i: (i, 4 * C_W // LANES)),
                  pl.BlockSpec((1, LANES), lambda i: (0, 0)), pl.BlockSpec((1, LANES), lambda i: (0, 0))],
        out_specs=pl.BlockSpec((tt, LANES), lambda i: (i, 0)),
        out_shape=jax.ShapeDtypeStruct((M, LANES), F32),
        compiler_params=_cparams(("parallel",)))(pc, pad(a_log), pad(dt_bias))


def _gdn_chains(qs, ks, vs, bcol, gcol, grow, L, rev):
    incl, strict, _ = _tri_masks(L, rev)
    i = lax.broadcasted_iota(jnp.int32, (L, L), 0)
    j = lax.broadcasted_iota(jnp.int32, (L, L), 1)
    cum_t = ((i >= j) if rev else (i <= j)).astype(F32)
    gc_cols = _dot_hp(incl.astype(F32), gcol)
    gc_rows = _dot_hp(grow, cum_t)
    last = 0 if rev else L - 1
    chains = []
    for h in range(len(qs)):
        gc = jnp.broadcast_to(gc_cols[:, h:h + 1], (L, LANES))
        beta = bcol[:, h:h + 1]
        kb = ks[h] * beta
        egc = jnp.exp(gc)
        g_end = gc[last:last + 1, :]
        chains.append(dict(
            q=qs[h], k=ks[h], incl=incl, strict=strict, kb=kb, qe=qs[h] * egc,
            decay=jnp.exp(jnp.where(incl, gc[:, :L] - gc_rows[h:h + 1, :], -1e30)),
            rhs=jnp.concatenate([vs[h] * beta, kb * egc], axis=1),
            kend=ks[h] * jnp.exp(g_end - gc), g_end=jnp.exp(g_end)))
    return chains


def _gdn_chunk(chains, states, L):
    r = lax.broadcasted_iota(jnp.int32, (L, L), 0)
    c = lax.broadcasted_iota(jnp.int32, (L, L), 1)
    eye_f = (r == c).astype(F32)
    kk = [_dot_hp(ch['kb'], ch['k'], NT) for ch in chains]
    qk = [_dot_hp(ch['q'], ch['k'], NT) for ch in chains]
    qs_s = [_dot_hp(ch['qe'], s) for ch, s in zip(chains, states)]
    tinv = _inv_unit_tri([jnp.where(ch['strict'], x * ch['decay'], 0.0) for ch, x in zip(chains, kk)], eye_f, L)
    sol = [_dot_hp(t, ch['rhs']) for t, ch in zip(tinv, chains)]
    v_new = [x[:, :C_HEAD_DIM] - _dot_hp(x[:, C_HEAD_DIM:], s) for x, s in zip(sol, states)]
    o = [a + _dot_hp(jnp.where(ch['incl'], x * ch['decay'], 0.0), vn)
         for a, x, ch, vn in zip(qs_s, qk, chains, v_new)]
    s_new = [s * ch['g_end'] + _dot_hp(ch['kend'], vn, TN) for s, ch, vn in zip(states, chains, v_new)]
    return o, s_new


def _gdn_scan_kernel(qf_ref, kf_ref, vf_ref, qr_ref, kr_ref, vr_ref, bf_ref, gf_ref, rf_ref, br_ref, gr_ref, rr_ref,
                     s0_ref, of_ref, or_ref, sf_ref, s_ref, *, L):
    c = pl.program_id(1)

    @pl.when(c == 0)
    def _():
        s_ref[...] = s0_ref[...]

    heads = range(C_HEADS)
    hs = [slice(h * C_HEAD_DIM, (h + 1) * C_HEAD_DIM) for h in heads]
    chains = _gdn_chains([qf_ref[:, sl] for sl in hs], [kf_ref[:, sl] for sl in hs], [vf_ref[:, sl] for sl in hs],
                         bf_ref[...], gf_ref[...], rf_ref[...], L, False)
    chains += _gdn_chains([qr_ref[:, sl] for sl in hs], [kr_ref[:, sl] for sl in hs], [vr_ref[:, sl] for sl in hs],
                          br_ref[...], gr_ref[...], rr_ref[...], L, True)
    o, s_new = _gdn_chunk(chains, [s_ref[d, h] for d in range(2) for h in heads], L)
    for h in heads:
        of_ref[:, hs[h]] = o[h]
        or_ref[:, hs[h]] = o[C_HEADS + h]
        s_ref[0, h] = s_new[h]
        s_ref[1, h] = s_new[C_HEADS + h]

    @pl.when(c == pl.num_programs(1) - 1)
    def _():
        sf_ref[...] = s_ref[...]


def _gdn_scan(qkv, bcols, gcols, grows, s0, B, T, L=CHUNK):
    nc = T // L
    fwd = lambda c: c
    bwd = lambda c: nc - 1 - c
    seq = lambda col, pos: pl.BlockSpec((L, C_W), lambda b, c: (b * nc + pos(c), col))
    small = lambda pos: pl.BlockSpec((L, C_HEADS), lambda b, c: (b * nc + pos(c), 0))
    rows = lambda pos: pl.BlockSpec((None, C_HEADS, L), lambda b, c: (b * nc + pos(c), 0, 0))
    st = pl.BlockSpec((None, 2, C_HEADS, C_HEAD_DIM, C_HEAD_DIM), lambda b, c: (b, 0, 0, 0, 0))
    return pl.pallas_call(
        functools.partial(_gdn_scan_kernel, L=L),
        grid=(B, nc),
        in_specs=[seq(0, fwd), seq(1, fwd), seq(2, fwd), seq(0, bwd), seq(1, bwd), seq(2, bwd),
                  small(fwd), small(fwd), rows(fwd), small(bwd), small(bwd), rows(bwd), st],
        out_specs=[pl.BlockSpec((L, C_W), lambda b, c: (b * nc + c, 0)),
                   pl.BlockSpec((L, C_W), lambda b, c: (b * nc + nc - 1 - c, 0)), st],
        out_shape=[jax.ShapeDtypeStruct((B * T, C_W), F32), jax.ShapeDtypeStruct((B * T, C_W), F32),
                   jax.ShapeDtypeStruct((B, 2, C_HEADS, C_HEAD_DIM, C_HEAD_DIM), F32)],
        scratch_shapes=[pltpu.VMEM((2, C_HEADS, C_HEAD_DIM, C_HEAD_DIM), F32)],
        compiler_params=_cparams(("parallel", "arbitrary")))(
            qkv, qkv, qkv, qkv, qkv, qkv, bcols[0], gcols[0], grows[0], bcols[1], gcols[1], grows[1], s0)


def _gdn_post_kernel(o0_ref, o1_ref, z_ref, g_ref, out_ref):
    for h in range(C_HEADS):
        sl = slice(h * C_HEAD_DIM, (h + 1) * C_HEAD_DIM)
        y = o0_ref[:, sl] + o1_ref[:, sl]
        y = y * lax.rsqrt(jnp.mean(y * y, axis=-1, keepdims=True) + NORM_EPS) * g_ref[...]
        z = z_ref[:, sl]
        out_ref[:, sl] = y * (z * _sigmoid(z))


def _gdn_post(o0, o1, pc, norm_g, tt=256):
    M = o0.shape[0]
    tt = min(tt, M)
    return pl.pallas_call(
        _gdn_post_kernel, grid=(M // tt,),
        in_specs=[pl.BlockSpec((tt, C_W), lambda i: (i, 0)), pl.BlockSpec((tt, C_W), lambda i: (i, 0)),
                  pl.BlockSpec((tt, C_W), lambda i: (i, 3)), pl.BlockSpec((1, C_HEAD_DIM), lambda i: (0, 0))],
        out_specs=pl.BlockSpec((tt, C_W), lambda i: (i, 0)),
        out_shape=jax.ShapeDtypeStruct((M, C_W), F32),
        compiler_params=_cparams(("parallel",)))(o0, o1, pc, norm_g.reshape(1, C_HEAD_DIM))


def _gdn_mixer(pc, B, T, prm, s0, L=CHUNK):
    nc = T // L
    qkv = _gdn_prep(pc, T, prm['conv'])
    bg = _gdn_bg(pc, prm['a_log'], prm['dt_bias'])
    if s0 is None:
        s0 = jnp.zeros((B, 2, C_HEADS, C_HEAD_DIM, C_HEAD_DIM), F32)
    bcols = [bg[:, d * C_HEADS:(d + 1) * C_HEADS] for d in range(2)]
    gcols = [bg[:, (2 + d) * C_HEADS:(3 + d) * C_HEADS] for d in range(2)]
    grows = [g.reshape(B * nc, L, C_HEADS).transpose(0, 2, 1) for g in gcols]
    o_fwd, o_bwd, s_fin = _gdn_scan(qkv, bcols, gcols, grows, s0, B, T)
    return _gdn_post(o_fwd, o_bwd, pc, prm['norm_g']), s_fin


def _top_values(tile, count, need_mask):
    rows = lax.broadcasted_iota(jnp.int32, tile.shape, 0).astype(F32)
    big = float(tile.shape[0])
    cur = tile
    vals = []
    for _ in range(count):
        m = jnp.max(cur, axis=0, keepdims=True)
        vals.append(m)
        idx = jnp.min(jnp.where(cur == m, rows, big), axis=0, keepdims=True)
        cur = jnp.where(rows == idx, -jnp.inf, cur)
    return vals, (cur == -jnp.inf) if need_mask else None


def _top_values_untied(tile, count):
    cur = tile
    vals = []
    for _ in range(count):
        m = jnp.max(cur, axis=0, keepdims=True)
        vals.append(m)
        cur = jnp.where(cur == m, -jnp.inf, cur)
    return vals, cur == -jnp.inf


PEER_EB = 1024
PEER_GRP = MXU_K // PEER_HALF
PEER_NGRP = 2 * PEER_HEADS // PEER_GRP


def _peer_topk_kernel(q_ref, kh_ref, kl_ref, thr_ref, e0_ref, e1_ref, s1_ref, sc_ref):
    rows_g = PEER_GRP * N_KEYS
    for g in range(PEER_NGRP):
        qh, ql = _split(q_ref[:, g * MXU_K:(g + 1) * MXU_K])
        kh = kh_ref[g * rows_g:(g + 1) * rows_g, :]
        kl = kl_ref[g * rows_g:(g + 1) * rows_g, :]
        sc_ref[g * rows_g:(g + 1) * rows_g, :] = _dg(kh, qh, NT) + (_dg(kh, ql, NT) + _dg(kl, qh, NT))

    def route(h, t0, t1, exact):
        if exact:
            a, ra = _top_values(t0, PEER_TOPK, True)
            b, rb = _top_values(t1, PEER_TOPK, True)
        else:
            a, ra = _top_values_untied(t0, PEER_TOPK)
            b, rb = _top_values_untied(t1, PEER_TOPK)
        s0 = jnp.where(ra, t0, -jnp.inf)
        s1 = jnp.where(rb, t1, -jnp.inf)
        amat = jnp.concatenate(a, axis=0)
        bmat = jnp.concatenate(b, axis=0)
        b_lo = bmat[:8]
        row8 = lax.broadcasted_iota(jnp.int32, b_lo.shape, 0)
        pieces = [a[0] + b_lo, a[0] + bmat[8:], a[1] + b_lo]
        n_pad = 0
        for r in range(2, 8):
            pieces.append(jnp.where(row8 < PEER_TOPK // (r + 1), a[r] + b_lo, -jnp.inf))
            n_pad += 8 - PEER_TOPK // (r + 1)
        pieces.append(amat[8:] + b[0])
        cand = jnp.concatenate(pieces, axis=0)
        tied = None
        if exact:
            best, _ = _top_values(cand, PEER_TOPK + 1, False)
        else:
            best, rc = _top_values_untied(cand, PEER_TOPK + 1)
            count = lambda mask: jnp.sum(jnp.where(mask, 1.0, 0.0), axis=0, keepdims=True)
            wrong = jnp.logical_or(jnp.logical_or(count(ra) != PEER_TOPK, count(rb) != PEER_TOPK),
                                   count(rc) != PEER_TOPK + 1 + n_pad)
            tied = jnp.max(jnp.where(wrong, 1.0, 0.0)) > 0.0
        z = jnp.zeros_like(best[0])
        for n in range(PEER_TOPK):
            z = z + jnp.exp(best[n] - best[0])
        tau = 0.5 * (best[PEER_TOPK - 1] + best[PEER_TOPK])
        thr = tau - s0
        e0 = jnp.exp(s0 - a[0]) * (1.0 / z)
        e1 = jnp.exp(s1 - b[0])
        for tc in range(t0.shape[1] // LANES):
            cols = slice(tc * LANES, (tc + 1) * LANES)
            thr_ref[h, tc] = thr[:, cols]
            e0_ref[h, tc] = e0[:, cols]
            e1_ref[h, tc] = e1[:, cols]
            s1_ref[h, tc] = s1[:, cols]
        return tied

    def head(h, carry):
        t0 = sc_ref[pl.ds(pl.multiple_of(h * 2 * N_KEYS, N_KEYS), N_KEYS), :]
        t1 = sc_ref[pl.ds(pl.multiple_of(h * 2 * N_KEYS + N_KEYS, N_KEYS), N_KEYS), :]
        tied = route(h, t0, t1, exact=False)
        pl.when(tied)(lambda: route(h, t0, t1, exact=True))
        return carry

    lax.fori_loop(0, PEER_HEADS, head, 0)


def _peer_topk(q, kdt_hi, kdt_lo, tt=256):
    M = q.shape[0]
    tt = min(tt, M)
    R = PEER_HEADS * 2 * N_KEYS
    big = pl.BlockSpec((PEER_HEADS, tt // LANES, N_KEYS, LANES), lambda i: (0, i, 0, 0))
    return pl.pallas_call(
        _peer_topk_kernel, grid=(M // tt,),
        in_specs=[pl.BlockSpec((tt, PEER_HEADS * PEER_QDIM), lambda i: (i, 0)),
                  pl.BlockSpec((R, MXU_K), lambda i: (0, 0)),
                  pl.BlockSpec((R, MXU_K), lambda i: (0, 0))],
        out_specs=[big] * 4,
        out_shape=[jax.ShapeDtypeStruct((PEER_HEADS, M // LANES, N_KEYS, LANES), F32)] * 4,
        scratch_shapes=[pltpu.VMEM((R, tt), F32)],
        compiler_params=_cparams(("parallel",)))(q, kdt_hi, kdt_lo)


def _peer_expert_kernel(xt_ref, u_ref, vt_ref, thr_ref, e0_ref, e1_ref, s1_ref, res_ref, m_ref, o_ref,
                        acc_ref, act0_ref, act1_ref, am0_ref, am1_ref, *, ni, gate_slot):
    e = pl.program_id(1)
    n_blocks = pl.num_programs(1) - 2
    tt = xt_ref.shape[1]
    even = e % 2 == 0
    odd = e % 2 == 1

    @pl.when(e == 0)
    def _():
        acc_ref[...] = jnp.zeros_like(acc_ref)

    def project(act_w):
        act_w[...] = _dg(u_ref[...], xt_ref[...])

    def weigh(act_r, am_w):
        first = (e - 1) * ni
        for ii in range(ni):
            i = first + ii
            rows = slice(ii * N_KEYS, (ii + 1) * N_KEYS)
            for tc in range(tt // LANES):
                cols = slice(tc * LANES, (tc + 1) * LANES)
                w = None
                for h in range(PEER_HEADS):
                    keep = s1_ref[h, tc] >= thr_ref[h, tc, pl.ds(i, 1), :]
                    contrib = jnp.where(keep, e1_ref[h, tc], 0.0) * e0_ref[h, tc, pl.ds(i, 1), :]
                    w = contrib if w is None else w + contrib
                a = act_r[rows, cols]
                a = 0.5 * a * (1.0 + lax.erf(a * (2.0 ** -0.5)))
                am_w[rows, cols] = (a * w).astype(BF16)

    def accumulate(am_r):
        acc_ref[...] += _dg(vt_ref[...], am_r[...])

    has_project = e < n_blocks
    has_weigh = jnp.logical_and(e >= 1, e <= n_blocks)
    has_accumulate = e >= 2
    pl.when(jnp.logical_and(has_project, even))(lambda: project(act0_ref))
    pl.when(jnp.logical_and(has_project, odd))(lambda: project(act1_ref))
    pl.when(jnp.logical_and(has_weigh, even))(lambda: weigh(act1_ref, am1_ref))
    pl.when(jnp.logical_and(has_weigh, odd))(lambda: weigh(act0_ref, am0_ref))
    pl.when(jnp.logical_and(has_accumulate, even))(lambda: accumulate(am0_ref))
    pl.when(jnp.logical_and(has_accumulate, odd))(lambda: accumulate(am1_ref))

    @pl.when(e == pl.num_programs(1) - 1)
    def _():
        o_ref[...] = res_ref[...] + m_ref[gate_slot:gate_slot + 1, :] * acc_ref[...].T


def _peer_experts(xn, u_bf, vt_bf, thr, e0, e1, s1m, res, mods, gate_slot, rows_per_mod, tt=512):
    M, D = res.shape
    tt = min(tt, M, rows_per_mod) if mods.shape[0] > 1 else min(tt, M)
    nb, _, eb = vt_bf.shape
    ni = eb // N_KEYS
    once = pl.Buffered(1)
    if mods.shape[0] > 1:
        mspec = pl.BlockSpec((None, 6, D), lambda i, e: ((i * tt) // rows_per_mod, 0, 0), pipeline_mode=once)
    else:
        mspec = pl.BlockSpec((None, 6, D), lambda i, e: (0, 0, 0), pipeline_mode=once)
    big = pl.BlockSpec((PEER_HEADS, tt // LANES, N_KEYS, LANES), lambda i, e: (0, i, 0, 0), pipeline_mode=once)
    return pl.pallas_call(
        functools.partial(_peer_expert_kernel, ni=ni, gate_slot=gate_slot),
        grid=(M // tt, nb + 2),
        in_specs=[pl.BlockSpec((D, tt), lambda i, e: (0, i), pipeline_mode=once),
                  pl.BlockSpec((eb, D), lambda i, e: (jnp.minimum(e, nb - 1), 0)),
                  pl.BlockSpec((None, D, eb), lambda i, e: (jnp.clip(e - 2, 0, nb - 1), 0, 0)),
                  big, big, big, big,
                  pl.BlockSpec((tt, D), lambda i, e: (i, 0), pipeline_mode=once), mspec],
        out_specs=pl.BlockSpec((tt, D), lambda i, e: (i, 0)),
        out_shape=jax.ShapeDtypeStruct((M, D), F32),
        scratch_shapes=[pltpu.VMEM((D, tt), F32), pltpu.VMEM((eb, tt), F32), pltpu.VMEM((eb, tt), F32),
                        pltpu.VMEM((eb, tt), BF16), pltpu.VMEM((eb, tt), BF16)],
        compiler_params=_cparams(("parallel", "arbitrary"), VMEM_LIMIT_BIG))(
            xn, u_bf, vt_bf, thr, e0, e1, s1m, res, mods)


def _peer(x, mods, rows_per_mod, norm_g, prm):
    q, xn = _mm_norm(x, prm['wq'], norm_g=norm_g, mods=mods, slots=(3, 4), rows_per_mod=rows_per_mod,
                     hp=True, emit_xn=BF16)
    thr, e0, e1, s1m = _peer_topk(q, prm['kdt_hi'], prm['kdt_lo'])
    return _peer_experts(xn.T, prm['u'], prm['vt'], thr, e0, e1, s1m, x, mods, 5, rows_per_mod)


def _peer_params(wq, keys, u_tab, v_tab):
    nset = 2 * PEER_HEADS
    kd = jnp.zeros((nset, N_KEYS, PEER_GRP, PEER_HALF), F32)
    sets = jnp.arange(nset)
    kd = kd.at[sets, :, sets % PEER_GRP, :].set(keys.reshape(nset, N_KEYS, PEER_HALF))
    kd_hi, kd_lo = _split(kd.reshape(nset * N_KEYS, MXU_K))
    vt = v_tab.astype(BF16).reshape(N_EXPERTS // PEER_EB, PEER_EB, -1).transpose(0, 2, 1)
    return dict(wq=_split(wq), kdt_hi=kd_hi, kdt_lo=kd_lo, u=u_tab.astype(BF16), vt=vt)


def _layer_even(x, mods, B, T, norm1, w_a, w_b, w_out, qk_gains, rwkv_prm, ctx):
    rpm = T
    pa = _mm_norm(x, w_a, norm_g=norm1, mods=mods, slots=(0, 1), rows_per_mod=rpm)
    pb = _mm_norm(x, w_b, norm_g=norm1, mods=mods, slots=(0, 1), rows_per_mod=rpm)
    nqk = A_HEADS + A_KV_HEADS
    grp = A_HEADS // A_KV_HEADS
    scale = A_HEAD_DIM ** -0.5
    if ctx is None:
        qk = _norm_rope(pa, nqk, T, gains=qk_gains)
        oa = _attention(B, T, T, A_HEADS, scale, (qk, lambda h: h), (qk, lambda h: A_HEADS + h // grp),
                        (pa, lambda h: nqk + h // grp))
        ctx_out = (qk[:, A_Q:A_Q + A_KV], pa[:, A_Q + A_KV:])
        s0 = None
    else:
        cache_k, cache_v, s0 = ctx
        P = cache_k.shape[1]
        qk = _norm_rope(pa, nqk, T, gains=qk_gains, tables=_rope_tables(T, A_HEAD_DIM))
        k_all = jnp.concatenate([qk[:, A_Q:].reshape(B, T, A_KV), cache_k.reshape(B, P, A_KV)], axis=1)
        v_all = jnp.concatenate([pa[:, A_Q + A_KV:].reshape(B, T, A_KV), cache_v.reshape(B, P, A_KV)], axis=1)
        Tk = T + P
        oa = _attention(B, T, Tk, A_HEADS, scale, (qk, lambda h: h),
                        (k_all.reshape(B * Tk, A_KV).astype(BF16), lambda h: h // grp),
                        (v_all.reshape(B * Tk, A_KV).astype(BF16), lambda h: h // grp))
        ctx_out = None
    ob, s_fin = _rwkv_mixer(pb, B, T, rwkv_prm, s0)
    x = _mm_res([oa, ob], w_out, x, mods, 2, rpm)
    return x, ctx_out, s_fin


D_QROPE_BLK = 0
D_KROPE_BLK = D_HEADS
D_NROPE_BLK = D_HEADS + 1
D_CKV_BLK = 12
D_QNOPE_BLK = 16
D_PROJ = 24 * LANES


def _mla_in_weight(w_d):
    K = w_d.shape[0]
    wq = w_d[:, :D_Q].reshape(K, D_HEADS, D_QK)
    z64 = jnp.zeros((K, D_HEADS, LANES - D_ROPE), w_d.dtype)
    q_rope = jnp.concatenate([wq[:, :, D_NOPE:], z64], axis=-1).reshape(K, D_HEADS * LANES)
    q_nope = wq[:, :, :D_NOPE].reshape(K, D_HEADS * LANES)
    k_rope = jnp.concatenate([w_d[:, D_Q + D_KV_RANK:], jnp.zeros((K, LANES - D_ROPE), w_d.dtype)], axis=-1)
    padz = jnp.zeros((K, (D_CKV_BLK - D_NROPE_BLK) * LANES), w_d.dtype)
    return jnp.concatenate([q_rope, k_rope, padz, w_d[:, D_Q:D_Q + D_KV_RANK], q_nope], axis=-1)


def _layer_odd(x, mods, B, T, norm1, w_c, w_d, w_out, gdn_prm, kv_norm_g, w_kv_b, ctx):
    rpm = T
    pc = _mm_norm(x, w_c, norm_g=norm1, mods=mods, slots=(0, 1), rows_per_mod=rpm)
    pd = _mm_norm(x, w_d, norm_g=norm1, mods=mods, slots=(0, 1), rows_per_mod=rpm)
    scale = D_QK ** -0.5
    ckv_col = D_CKV_BLK * LANES // D_KV_RANK
    kv, ckv_n = _mm_norm(pd, w_kv_b, k=D_KV_RANK, xcol=ckv_col, norm_g=kv_norm_g, emit_xn=F32)
    kcol = lambda h: 2 * h
    vcol = lambda h: 2 * h + 1
    if ctx is None:
        od = _attention(B, T, T, D_HEADS, scale, (pd, lambda h: D_QNOPE_BLK + h), (kv, kcol), (kv, vcol),
                        q2=(pd, lambda h: D_QROPE_BLK + h), k2=(pd, lambda h: D_KROPE_BLK))
        ctx_out = (ckv_n, pd[:, D_KROPE_BLK * LANES:D_KROPE_BLK * LANES + D_ROPE])
        s0 = None
    else:
        s0, c_ckv, c_krope = ctx
        P = c_ckv.shape[1]
        Tk = T + P
        roped = _norm_rope(pd, D_NROPE_BLK, T, tables=_rope_tables(T, D_ROPE))
        kv_ctx = _mm_norm(c_ckv.reshape(B * P, D_KV_RANK), w_kv_b)
        kv_all = jnp.concatenate([kv.reshape(B, T, -1), kv_ctx.reshape(B, P, -1)], axis=1).reshape(B * Tk, -1)
        kv_all = kv_all.astype(BF16)
        kr_lat = roped[:, D_KROPE_BLK * LANES:].reshape(B, T, LANES)
        kr_ctx = jnp.concatenate([c_krope, jnp.zeros((B, P, LANES - D_ROPE), F32)], axis=-1)
        kr_all = jnp.concatenate([kr_lat, kr_ctx], axis=1).reshape(B * Tk, LANES).astype(BF16)
        od = _attention(B, T, Tk, D_HEADS, scale, (pd, lambda h: D_QNOPE_BLK + h), (kv_all, kcol), (kv_all, vcol),
                        q2=(roped, lambda h: D_QROPE_BLK + h), k2=(kr_all, lambda h: 0))
        ctx_out = None
    oc, s_fin = _gdn_mixer(pc, B, T, gdn_prm, s0)
    x = _mm_res([oc, od], w_out, x, mods, 2, rpm)
    return x, ctx_out, s_fin


def _modulation(cond, w_ada, b_ada):
    n = cond.shape[0]
    rows = ((n + 7) // 8) * 8
    a = cond * _sigmoid(cond)
    a = jnp.concatenate([a, jnp.zeros((rows - n, cond.shape[1]), F32)], axis=0)
    m = _mm_norm(a, _split(w_ada), hp=True, tn_cap=1024)[:n] + b_ada
    return m.reshape(n, 6, D_MODEL)


def kernel(x_prompt, x_sample, cache_attn_k, cache_attn_v, state_rwkv, state_gdn, cache_mla_ckv, cache_mla_krope,
           c, c_ctx, norm1_g, norm2_g, ada_w, ada_b, w_in_ab, w_out_ab, attn_q_norm, attn_k_norm,
           rwkv_mu, rwkv_w0, rwkv_w2, rwkv_a0, rwkv_a2, rwkv_g2, rwkv_k_k, rwkv_k_a, rwkv_r_k, rwkv_ln_w, rwkv_ln_b,
           w_in_cd, w_out_cd, gdn_conv, gdn_a_log, gdn_dt_bias, gdn_norm_g, mla_kv_norm_g, mla_w_kv_b,
           peer_wq, peer_keys, peer_u, peer_v, final_norm_g):
    Bp, Tp, D = x_prompt.shape
    Bs, Ts, _ = x_sample.shape
    depth = norm1_g.shape[0]
    xp = x_prompt.reshape(Bp * Tp, D)
    xs = x_sample.reshape(Bs * Ts, D)
    cond = jnp.concatenate([c_ctx[None, :], c], axis=0)
    outs = dict(k=[], v=[], rwkv=[], gdn=[], ckv=[], krope=[])
    for li in range(depth):
        j = li // 2
        mods = _modulation(cond, ada_w[li], ada_b[li])
        mods_p, mods_s = mods[:1], mods[1:]
        peer_prm = _peer_params(peer_wq[li], peer_keys[li], peer_u[li], peer_v[li])
        if li % 2 == 0:
            w_in = w_in_ab[j].astype(BF16)
            w_a, w_b = w_in[:, :A_IN], w_in[:, A_IN:]
            w_o = w_out_ab[j].astype(BF16)
            w_out = [w_o[:A_Q], w_o[A_Q:]]
            gains = jnp.concatenate([jnp.tile(attn_q_norm[j], A_HEADS), jnp.tile(attn_k_norm[j], A_KV_HEADS)])
            rp = dict(mu=rwkv_mu[j], w0=rwkv_w0[j], w2=rwkv_w2[j], a0=rwkv_a0[j], a2=rwkv_a2[j], g2=rwkv_g2[j],
                      k_k=rwkv_k_k[j], k_a=rwkv_k_a[j], r_k=rwkv_r_k[j], ln_w=rwkv_ln_w[j], ln_b=rwkv_ln_b[j])
            xp, (k_c, v_c), s_c = _layer_even(xp, mods_p, Bp, Tp, norm1_g[li], w_a, w_b, w_out, gains, rp, None)
            xs, _, _ = _layer_even(xs, mods_s, Bs, Ts, norm1_g[li], w_a, w_b, w_out, gains, rp,
                                   (cache_attn_k[:, j], cache_attn_v[:, j], state_rwkv[:, j]))
            outs['k'].append(k_c.reshape(Bp, Tp, A_KV_HEADS, A_HEAD_DIM))
            outs['v'].append(v_c.reshape(Bp, Tp, A_KV_HEADS, A_HEAD_DIM))
            outs['rwkv'].append(s_c)
        else:
            w_in = w_in_cd[j]
            w_c = jnp.concatenate([w_in[:, :C_IN], jnp.zeros((D, LANES - 4 * C_HEADS), F32)], axis=-1).astype(BF16)
            w_d = _mla_in_weight(w_in[:, C_IN:]).astype(BF16)
            w_o = w_out_cd[j].astype(BF16)
            w_out = [w_o[:C_W], w_o[C_W:]]
            gp = dict(conv=gdn_conv[j], a_log=gdn_a_log[j], dt_bias=gdn_dt_bias[j], norm_g=gdn_norm_g[j])
            w_kv_b = mla_w_kv_b[j].astype(BF16)
            xp, (ckv_c, kr_c), s_c = _layer_odd(xp, mods_p, Bp, Tp, norm1_g[li], w_c, w_d, w_out, gp,
                                                mla_kv_norm_g[j], w_kv_b, None)
            xs, _, _ = _layer_odd(xs, mods_s, Bs, Ts, norm1_g[li], w_c, w_d, w_out, gp, mla_kv_norm_g[j], w_kv_b,
                                  (state_gdn[:, j], cache_mla_ckv[:, j], cache_mla_krope[:, j]))
            outs['gdn'].append(s_c)
            outs['ckv'].append(ckv_c.reshape(Bp, Tp, D_KV_RANK))
            outs['krope'].append(kr_c.reshape(Bp, Tp, D_ROPE))
        xp = _peer(xp, mods_p, Tp, norm2_g[li], peer_prm)
        xs = _peer(xs, mods_s, Ts, norm2_g[li], peer_prm)
    y_prompt = _rmsnorm(xp, final_norm_g).reshape(Bp, Tp, D)
    y_sample = _rmsnorm(xs, final_norm_g).reshape(Bs, Ts, D)
    st = lambda name: jnp.stack(outs[name], axis=1)
    return (y_prompt, y_sample, st('k'), st('v'), st('rwkv'), st('gdn'), st('ckv'), st('krope'))
```

```python
import functools
import math

import numpy as np
import jax
import jax.numpy as jnp
from jax import lax
from jax.experimental import pallas as pl
from jax.experimental.pallas import tpu as pltpu

F32 = jnp.float32
BF16 = jnp.bfloat16

D_MODEL = 2048
GRID_W = 64
ROPE_THETA = 10000.0
NORM_EPS = 1e-6
A_HEADS, A_KV_HEADS, A_HEAD_DIM = 8, 2, 128
A_Q = A_HEADS * A_HEAD_DIM
A_KV = A_KV_HEADS * A_HEAD_DIM
A_IN = A_Q + 2 * A_KV
B_HEADS, B_HEAD_DIM = 16, 64
B_W = B_HEADS * B_HEAD_DIM
B_DECAY_LORA, B_AAA_LORA, B_GATE_LORA = 64, 64, 128
B_IN = 3 * B_W + 2 * B_DECAY_LORA + 2 * B_AAA_LORA + B_GATE_LORA
B_GN_EPS = 64e-5
C_HEADS, C_HEAD_DIM = 8, 128
C_W = C_HEADS * C_HEAD_DIM
C_IN = 4 * C_W + 4 * C_HEADS
D_HEADS, D_NOPE, D_ROPE, D_V = 8, 128, 64, 128
D_QK = D_NOPE + D_ROPE
D_KV_RANK = 512
D_Q = D_HEADS * D_QK
PEER_HEADS, N_KEYS, PEER_QDIM, PEER_TOPK = 8, 128, 128, 16
PEER_HALF = PEER_QDIM // 2
N_EXPERTS = N_KEYS * N_KEYS

LANES = 128
MXU_K = 256
VMEM_LIMIT = 48 * 1024 * 1024
VMEM_LIMIT_BIG = 56 * 1024 * 1024

CHUNK = 64

NN = ((1,), (0,))
NT = ((1,), (1,))
TN = ((0,), (0,))


def _dg(a, b, dims=NN):
    return lax.dot_general(a, b, (dims, ((), ())), preferred_element_type=F32)


def _split(x):
    hi = x.astype(BF16)
    lo = (x - hi.astype(F32)).astype(BF16)
    return hi, lo


def _dot_hp(a, b, dims=NN):
    ah, al = _split(a)
    bh, bl = _split(b)
    return _dg(ah, bh, dims) + (_dg(ah, bl, dims) + _dg(al, bh, dims))


def _dot_bf(a, b, dims=NN):
    return _dg(a.astype(BF16), b.astype(BF16), dims)


def _sigmoid(x):
    return 1.0 / (1.0 + jnp.exp(-x))


def _softplus(x):
    return jnp.maximum(x, 0.0) + jnp.log(1.0 + jnp.exp(-jnp.abs(x)))


def _cparams(sem, vmem=VMEM_LIMIT):
    return pltpu.CompilerParams(dimension_semantics=sem, vmem_limit_bytes=vmem)


def _pick_tile(n, cap):
    best = None
    t = LANES
    while t <= min(n, cap):
        if n % t == 0:
            best = t
        t += LANES
    return best if best is not None else n


def _mm_norm_kernel(*refs, has_norm, has_mod, shift_slot, scale_slot, hp, emit_xn):
    it = iter(refs)
    x_ref = next(it)
    g_ref = next(it) if has_norm else None
    m_ref = next(it) if has_mod else None
    wh_ref = next(it)
    wl_ref = next(it) if hp else None
    o_ref = next(it)
    xn_ref = next(it) if emit_xn else None
    xh_ref = next(it)
    xl_ref = next(it) if hp else None

    @pl.when(pl.program_id(1) == 0)
    def _():
        xv = x_ref[...].astype(F32)
        if has_norm:
            xv = xv * lax.rsqrt(jnp.mean(xv * xv, axis=-1, keepdims=True) + NORM_EPS) * g_ref[...]
        if has_mod:
            xv = xv * (1.0 + m_ref[scale_slot:scale_slot + 1, :]) + m_ref[shift_slot:shift_slot + 1, :]
        hi = xv.astype(BF16)
        xh_ref[...] = hi
        if hp:
            xl_ref[...] = (xv - hi.astype(F32)).astype(BF16)
        if emit_xn:
            xn_ref[...] = xv.astype(xn_ref.dtype)

    acc = _dg(xh_ref[...], wh_ref[...])
    if hp:
        acc = acc + (_dg(xh_ref[...], wl_ref[...]) + _dg(xl_ref[...], wh_ref[...]))
    o_ref[...] = acc


def _mm_norm(x, w, *, k=None, xcol=0, norm_g=None, mods=None, slots=(0, 1), rows_per_mod=None,
             hp=False, emit_xn=None, tm=512, tn_cap=1664):
    M = x.shape[0]
    K = k if k is not None else x.shape[1]
    ws = tuple(w) if isinstance(w, (tuple, list)) else (w,)
    N = ws[0].shape[1]
    tm = min(tm, M)
    if mods is not None and mods.shape[0] > 1:
        tm = min(tm, rows_per_mod)
    tn = _pick_tile(N, tn_cap)
    has_norm = norm_g is not None
    has_mod = mods is not None
    rpm = rows_per_mod if rows_per_mod is not None else M

    in_specs = [pl.BlockSpec((tm, K), lambda i, j: (i, xcol))]
    args = [x]
    if has_norm:
        in_specs.append(pl.BlockSpec((1, K), lambda i, j: (0, 0)))
        args.append(norm_g.reshape(1, K).astype(F32))
    if has_mod:
        if mods.shape[0] > 1:
            in_specs.append(pl.BlockSpec((None, 6, K), lambda i, j: ((i * tm) // rpm, 0, 0)))
        else:
            in_specs.append(pl.BlockSpec((None, 6, K), lambda i, j: (0, 0, 0)))
        args.append(mods)
    for wi in ws:
        in_specs.append(pl.BlockSpec((K, tn), lambda i, j: (0, j)))
        args.append(wi)
    out_shape = [jax.ShapeDtypeStruct((M, N), F32)]
    out_specs = [pl.BlockSpec((tm, tn), lambda i, j: (i, j))]
    if emit_xn is not None:
        out_shape.append(jax.ShapeDtypeStruct((M, K), emit_xn))
        out_specs.append(pl.BlockSpec((tm, K), lambda i, j: (i, 0)))
    scratch = [pltpu.VMEM((tm, K), BF16)]
    if hp:
        scratch.append(pltpu.VMEM((tm, K), BF16))
    kern = functools.partial(_mm_norm_kernel, has_norm=has_norm, has_mod=has_mod, shift_slot=slots[0],
                             scale_slot=slots[1], hp=hp, emit_xn=emit_xn is not None)
    outs = pl.pallas_call(
        kern, grid=(M // tm, N // tn), in_specs=in_specs, out_specs=out_specs, out_shape=out_shape,
        scratch_shapes=scratch, compiler_params=_cparams(("parallel", "arbitrary")))(*args)
    return outs if emit_xn is not None else outs[0]


def _mm_res_kernel(*refs, n_in, gate_slot):
    a_refs = refs[:n_in]
    w_refs = refs[n_in:2 * n_in]
    res_ref, m_ref, o_ref = refs[2 * n_in:]
    acc = None
    for a_ref, w_ref in zip(a_refs, w_refs):
        part = _dg(a_ref[...].astype(BF16), w_ref[...])
        acc = part if acc is None else acc + part
    o_ref[...] = res_ref[...] + m_ref[gate_slot:gate_slot + 1, :] * acc


def _mm_res(a_list, w_list, res, mods, gate_slot, rows_per_mod, tm=512, tn=1024):
    M, N = res.shape
    tm = min(tm, M, rows_per_mod) if mods.shape[0] > 1 else min(tm, M)
    tn = min(tn, N)
    n_in = len(a_list)
    in_specs, args = [], []
    for a in a_list:
        in_specs.append(pl.BlockSpec((tm, a.shape[1]), lambda i, j: (i, 0)))
        args.append(a)
    for w in w_list:
        in_specs.append(pl.BlockSpec((w.shape[0], tn), lambda i, j: (0, j)))
        args.append(w)
    in_specs.append(pl.BlockSpec((tm, tn), lambda i, j: (i, j)))
    args.append(res)
    if mods.shape[0] > 1:
        in_specs.append(pl.BlockSpec((None, 6, tn), lambda i, j: ((i * tm) // rows_per_mod, 0, j)))
    else:
        in_specs.append(pl.BlockSpec((None, 6, tn), lambda i, j: (0, 0, j)))
    args.append(mods)
    return pl.pallas_call(
        functools.partial(_mm_res_kernel, n_in=n_in, gate_slot=gate_slot),
        grid=(M // tm, N // tn), in_specs=in_specs,
        out_specs=pl.BlockSpec((tm, tn), lambda i, j: (i, j)),
        out_shape=jax.ShapeDtypeStruct((M, N), F32),
        compiler_params=_cparams(("parallel", "arbitrary")))(*args)


def _rmsnorm_kernel(x_ref, g_ref, o_ref):
    xv = x_ref[...]
    o_ref[...] = xv * lax.rsqrt(jnp.mean(xv * xv, axis=-1, keepdims=True) + NORM_EPS) * g_ref[...]


def _rmsnorm(x, g, tm=512):
    M, K = x.shape
    tm = min(tm, M)
    return pl.pallas_call(
        _rmsnorm_kernel, grid=(M // tm,),
        in_specs=[pl.BlockSpec((tm, K), lambda i: (i, 0)), pl.BlockSpec((1, K), lambda i: (0, 0))],
        out_specs=pl.BlockSpec((tm, K), lambda i: (i, 0)),
        out_shape=jax.ShapeDtypeStruct((M, K), F32),
        compiler_params=_cparams(("parallel",)))(x, g.reshape(1, K))


def _rope_tables(T, rot_dim):
    rows = T // GRID_W
    row = jnp.repeat(jnp.arange(rows, dtype=F32), GRID_W)
    col = jnp.tile(jnp.arange(GRID_W, dtype=F32), rows)
    n_freq = rot_dim // 4
    inv = ROPE_THETA ** (-jnp.arange(n_freq, dtype=F32) / n_freq)
    ang = jnp.concatenate([row[:, None] * inv, col[:, None] * inv], axis=-1)
    cos = jnp.repeat(jnp.cos(ang), 2, axis=-1)
    sin = jnp.repeat(jnp.sin(ang), 2, axis=-1) * jnp.tile(jnp.array([-1.0, 1.0], F32), rot_dim // 2)
    pad = LANES - rot_dim
    if pad:
        cos = jnp.concatenate([cos, jnp.ones((T, pad), F32)], axis=-1)
        sin = jnp.concatenate([sin, jnp.zeros((T, pad), F32)], axis=-1)
    return cos, sin


def _rope(y, cosf, sinf):
    lane = lax.broadcasted_iota(jnp.int32, y.shape, 1)
    even = (lane & 1) == 0
    width = y.shape[1]
    swap = jnp.where(even, pltpu.roll(y, width - 1, 1), pltpu.roll(y, 1, 1))
    return y * cosf + swap * sinf


def _norm_rope_kernel(*refs, nblk, norm, rope):
    it = iter(refs)
    x_ref = next(it)
    g_ref = next(it) if norm else None
    cos_ref = next(it) if rope else None
    sin_ref = next(it) if rope else None
    o_ref = next(it)
    for c in range(nblk):
        sl = slice(c * LANES, (c + 1) * LANES)
        y = x_ref[:, sl]
        if norm:
            y = y * lax.rsqrt(jnp.mean(y * y, axis=-1, keepdims=True) + NORM_EPS) * g_ref[:, sl]
        if rope:
            y = _rope(y, cos_ref[...], sin_ref[...])
        o_ref[:, sl] = y


def _norm_rope(x, nblk, T, gains=None, tables=None, tt=256):
    M = x.shape[0]
    tt = min(tt, T)
    W = nblk * LANES
    in_specs = [pl.BlockSpec((tt, W), lambda i: (i, 0))]
    args = [x]
    if gains is not None:
        in_specs.append(pl.BlockSpec((1, W), lambda i: (0, 0)))
        args.append(gains.reshape(1, W))
    if tables is not None:
        nt = T // tt
        for t in tables:
            in_specs.append(pl.BlockSpec((tt, LANES), lambda i: (i % nt, 0)))
            args.append(t)
    return pl.pallas_call(
        functools.partial(_norm_rope_kernel, nblk=nblk, norm=gains is not None, rope=tables is not None),
        grid=(M // tt,), in_specs=in_specs,
        out_specs=pl.BlockSpec((tt, W), lambda i: (i, 0)),
        out_shape=jax.ShapeDtypeStruct((M, W), F32),
        compiler_params=_cparams(("parallel",)))(*args)


ATT_GROUP = 4


def _attn_kernel(*refs, scale, latent):
    heads = range(ATT_GROUP)
    lane = lambda j: slice(j * LANES, (j + 1) * LANES)
    if latent:
        q_ref, q2_ref, kv_ref, k2_ref, o_ref = refs
        k2 = k2_ref[...]
        qs = [jnp.concatenate([q_ref[:, lane(j)], q2_ref[:, lane(j)]], axis=1) * scale for j in heads]
        ks = [jnp.concatenate([kv_ref[:, lane(2 * j)], k2], axis=1) for j in heads]
        vs = [kv_ref[:, lane(2 * j + 1)] for j in heads]
    else:
        q_ref, k_ref, v_ref, o_ref = refs
        qs = [q_ref[:, lane(j)] * scale for j in heads]
        ks = [k_ref[...]] * ATT_GROUP
        vs = [v_ref[...]] * ATT_GROUP
    s = [_dot_bf(q, k, NT) for q, k in zip(qs, ks)]
    p = [jnp.exp(x - jnp.max(x, axis=-1, keepdims=True)) for x in s]
    l = [jnp.sum(x, axis=-1, keepdims=True) for x in p]
    o = [_dot_bf(x, v) / n for x, v, n in zip(p, vs, l)]
    for j in heads:
        o_ref[:, lane(j)] = o[j]


def _attention(B, T, Tk, H, scale, q, k, v, q2=None, k2=None, tq=256):
    tq = min(tq, T)
    nq = T // tq
    latent = q2 is not None
    gw = ATT_GROUP * LANES

    def qspec(first):
        assert first % ATT_GROUP == 0
        return pl.BlockSpec((tq, gw), lambda b, g, i: (b * nq + i, first // ATT_GROUP + g))

    if latent:
        assert k[0] is v[0] and k[1] % (2 * ATT_GROUP) == 0
        in_specs = [qspec(q[1]), qspec(q2[1]),
                    pl.BlockSpec((Tk, 2 * gw), lambda b, g, i: (b, k[1] // (2 * ATT_GROUP) + g)),
                    pl.BlockSpec((Tk, LANES), lambda b, g, i: (b, k2[1]))]
        args = [q[0], q2[0], k[0], k2[0]]
    else:
        in_specs = [qspec(q[1]), pl.BlockSpec((Tk, LANES), lambda b, g, i: (b, k[1] + g)),
                    pl.BlockSpec((Tk, LANES), lambda b, g, i: (b, v[1] + g))]
        args = [q[0], k[0], v[0]]
    return pl.pallas_call(
        functools.partial(_attn_kernel, scale=scale, latent=latent),
        grid=(B, H // ATT_GROUP, nq), in_specs=in_specs,
        out_specs=pl.BlockSpec((tq, gw), lambda b, g, i: (b * nq + i, g)),
        out_shape=jax.ShapeDtypeStruct((B * T, H * LANES), F32),
        compiler_params=_cparams(("parallel", "arbitrary", "arbitrary")))(*args)


def _tri_masks(L, rev):
    i = lax.broadcasted_iota(jnp.int32, (L, L), 0)
    j = lax.broadcasted_iota(jnp.int32, (L, L), 1)
    if rev:
        return j >= i, j > i, i == j
    return j <= i, j < i, i == j


def _inv_unit_tri(mats, eye_f, L):
    pows = [[-a for a in mats]]
    for _ in range(int(math.log2(L)) - 1):
        pows.append([_dot_bf(p, p) for p in pows[-1]])
    terms = [[eye_f + p for p in level] for level in pows]
    while len(terms) > 1:
        nxt = [[_dot_bf(x, y) for x, y in zip(terms[i], terms[i + 1])] for i in range(0, len(terms) - 1, 2)]
        if len(terms) % 2:
            nxt.append(terms[-1])
        terms = nxt
    x0 = terms[0]
    resid = [(eye_f - x) - _dot_hp(a, x) for a, x in zip(mats, x0)]
    return [x + _dot_bf(x, r) for x, r in zip(x0, resid)]


def _shifted(f, prev_ref, next_ref, tt, T):
    i = pl.program_id(0)
    first = (i * tt) % T == 0
    last = ((i + 1) * tt) % T == 0
    prev = jnp.where(first, 0.0, prev_ref[7:8, :])
    nxt = jnp.where(last, 0.0, next_ref[0:1, :])
    rows = lax.broadcasted_iota(jnp.int32, f.shape, 0)
    fp = jnp.where(rows == 0, prev, pltpu.roll(f, 1, 0))
    fn = jnp.where(rows == tt - 1, nxt, pltpu.roll(f, tt - 1, 0))
    return fp, fn


def _halo_specs(tt, width, M, col_fn):
    r8 = tt // 8
    last8 = M // 8 - 1
    return [
        pl.BlockSpec((tt, width), lambda *g: (g[0], col_fn(*g))),
        pl.BlockSpec((8, width), lambda *g: (jnp.maximum(g[0] * r8 - 1, 0), col_fn(*g))),
        pl.BlockSpec((8, width), lambda *g: (jnp.minimum((g[0] + 1) * r8, last8), col_fn(*g))),
    ]


NPAIR = B_HEADS // 2


def _pair_sum_matrix():
    r = lax.broadcasted_iota(jnp.int32, (LANES, LANES), 0) // B_HEAD_DIM
    c = lax.broadcasted_iota(jnp.int32, (LANES, LANES), 1) // B_HEAD_DIM
    return (r == c).astype(F32)


def _rwkv_prep_kernel(x_ref, prev_ref, next_ref, mu_ref, w0_ref, w2_ref, a0_ref, a2_ref, g2_ref, kk_ref_in,
                      r_o, k_o, v_o, kk_o, g_o, lw0_o, lw1_o, a0_o, a1_o, *, tt, T):
    f = x_ref[...]
    fp, fn = _shifted(f, prev_ref, next_ref, tt, T)
    f = f + mu_ref[0:1, :] * (fp - f) + mu_ref[1:2, :] * (fn - f)
    r = f[:, :B_W]
    k = f[:, B_W:2 * B_W]
    v = f[:, 2 * B_W:3 * B_W]
    off = 3 * B_W
    w_lo = f[:, off:off + 2 * B_DECAY_LORA]
    off += 2 * B_DECAY_LORA
    a_lo = f[:, off:off + 2 * B_AAA_LORA]
    off += 2 * B_AAA_LORA
    g_lo = f[:, off:]
    th = jnp.tanh(w_lo)
    logw, aa = [], []
    for d in range(2):
        wpre = w0_ref[d:d + 1, :] + _dot_hp(th[:, d * B_DECAY_LORA:(d + 1) * B_DECAY_LORA], w2_ref[d])
        wv = -_softplus(-wpre) - 0.5
        logw.append(-jnp.exp(wv))
        aa.append(_sigmoid(a0_ref[d:d + 1, :] + _dot_hp(a_lo[:, d * B_AAA_LORA:(d + 1) * B_AAA_LORA], a2_ref[d])))
    g = _dot_hp(_sigmoid(g_lo), g2_ref[...])
    kk = k * kk_ref_in[...]
    gsum = _pair_sum_matrix()
    for p in range(NPAIR):
        sl = slice(p * LANES, (p + 1) * LANES)
        kp = kk[:, sl]
        ssq = _dot_hp(kp * kp, gsum)
        kk_o[p] = kp / jnp.maximum(jnp.sqrt(ssq), 1e-12)
        r_o[p] = r[:, sl]
        k_o[p] = k[:, sl]
        v_o[p] = v[:, sl]
        g_o[p] = g[:, sl]
        lw0_o[p] = logw[0][:, sl]
        lw1_o[p] = logw[1][:, sl]
        a0_o[p] = aa[0][:, sl]
        a1_o[p] = aa[1][:, sl]


def _rwkv_prep(pb, B, T, mu, w0, w2, a0, a2, g2, k_k, tt=128):
    M = pb.shape[0]
    tt = min(tt, T)
    nt = T // tt
    full = lambda shape: pl.BlockSpec(shape, lambda i: tuple(0 for _ in shape))
    in_specs = _halo_specs(tt, B_IN, M, lambda i: 0) + [
        full((2, B_IN)), full((2, B_W)), full((2, B_DECAY_LORA, B_W)), full((2, B_W)),
        full((2, B_AAA_LORA, B_W)), full((B_GATE_LORA, B_W)), full((1, B_W))]
    ospec = pl.BlockSpec((None, NPAIR, tt, LANES), lambda i: (i // nt, 0, i % nt, 0))
    oshape = jax.ShapeDtypeStruct((B, NPAIR, T, LANES), F32)
    return pl.pallas_call(
        functools.partial(_rwkv_prep_kernel, tt=tt, T=T),
        grid=(M // tt,), in_specs=in_specs, out_specs=[ospec] * 9, out_shape=[oshape] * 9,
        compiler_params=_cparams(("parallel",)))(pb, pb, pb, mu, w0, w2, a0, a2, g2, k_k.reshape(1, B_W))


def _rwkv_chains(r2, lw2, kd2, v2, kk2, a2, L, rev):
    incl, strict, _ = _tri_masks(L, rev)
    incl_f = incl.astype(F32)
    i2 = lax.broadcasted_iota(jnp.int32, (L, 2 * L), 0)
    j2 = lax.broadcasted_iota(jnp.int32, (L, 2 * L), 1) % L
    incl2 = (j2 >= i2) if rev else (j2 <= i2)
    last = 0 if rev else L - 1
    halves = (slice(0, B_HEAD_DIM), slice(B_HEAD_DIM, 2 * B_HEAD_DIM))
    logp2 = [_dot_hp(incl_f, lw) for lw in lw2]
    chains = []
    for lp, lw, r, kd, v, kk, a in zip(logp2, lw2, r2, kd2, v2, kk2, a2):
        ninv = jnp.exp(-lp)
        b = kk * a
        kt = kk * jnp.exp(lp - lw)
        rt = r * jnp.exp(lp)
        bn, kdn = b * ninv, kd * ninv
        pl_row = lp[last:last + 1, :]
        rem = jnp.exp(pl_row - lp)
        bh, kh = b * rem, kd * rem
        dpl = jnp.exp(pl_row)
        for sl in halves:
            chains.append(dict(
                strict=strict, incl2=incl2, dec=dpl[:, sl], v=v[:, sl],
                lhs=jnp.concatenate([kt[:, sl], rt[:, sl]], axis=0),
                rhs=jnp.concatenate([bn[:, sl], kdn[:, sl]], axis=0),
                rem_rhs=jnp.concatenate([bh[:, sl], kh[:, sl]], axis=0)))
    return chains


def _rwkv_chunk(chains, states, L):
    row = lax.broadcasted_iota(jnp.int32, (L, L), 0)
    col = lax.broadcasted_iota(jnp.int32, (L, L), 1)
    eye_f = (row == col).astype(F32)
    gm = [_dot_hp(ch['lhs'], ch['rhs'], NT) for ch in chains]
    ls = [_dot_hp(ch['lhs'], s, NT) for ch, s in zip(chains, states)]
    akv = [_dot_hp(jnp.where(ch['strict'], g[:L, L:], 0.0), ch['v']) for g, ch in zip(gm, chains)]
    tinv = _inv_unit_tri([jnp.where(ch['strict'], g[:L, :L], 0.0) for g, ch in zip(gm, chains)], eye_f, L)
    u = [-_dot_hp(t, l[:L] + x) for t, l, x in zip(tinv, ls, akv)]
    uv = [jnp.concatenate([x, ch['v']], axis=0) for x, ch in zip(u, chains)]
    y = [l[L:] + _dot_hp(jnp.where(ch['incl2'], g[L:, :], 0.0), x) for l, g, x, ch in zip(ls, gm, uv, chains)]
    s_new = [s * ch['dec'] + _dot_hp(x, ch['rem_rhs'], TN) for s, ch, x in zip(states, chains, uv)]
    return y, s_new


def _rwkv_scan_kernel(rf_ref, vf_ref, kkf_ref, kf_ref, lwf_ref, af_ref, rr_ref, vr_ref, kkr_ref, kr_ref, lwr_ref,
                      ar_ref, ka_ref, s0_ref, yf_ref, yr_ref, sf_ref, s_ref, *, L):
    c = pl.program_id(1)

    @pl.when(c == 0)
    def _():
        s_ref[...] = s0_ref[...]

    pairs = range(NPAIR)
    chains = []
    for rev, (r_ref, v_ref, kk_ref, k_ref, lw_ref, a_ref) in enumerate(
            ((rf_ref, vf_ref, kkf_ref, kf_ref, lwf_ref, af_ref), (rr_ref, vr_ref, kkr_ref, kr_ref, lwr_ref, ar_ref))):
        a2 = [a_ref[p] for p in pairs]
        kd2 = [k_ref[p] * (1.0 + (a - 1.0) * ka_ref[p]) for p, a in zip(pairs, a2)]
        chains += _rwkv_chains([r_ref[p] for p in pairs], [lw_ref[p] for p in pairs], kd2,
                               [v_ref[p] for p in pairs], [kk_ref[p] for p in pairs], a2, L, bool(rev))
    y, s_new = _rwkv_chunk(chains, [s_ref[d, h] for d in range(2) for h in range(B_HEADS)], L)
    for p in pairs:
        yf_ref[p] = jnp.concatenate([y[2 * p], y[2 * p + 1]], axis=1)
        yr_ref[p] = jnp.concatenate([y[B_HEADS + 2 * p], y[B_HEADS + 2 * p + 1]], axis=1)
    for h in range(B_HEADS):
        s_ref[0, h] = s_new[h]
        s_ref[1, h] = s_new[B_HEADS + h]

    @pl.when(c == pl.num_programs(1) - 1)
    def _():
        sf_ref[...] = s_ref[...]


def _rwkv_scan(r, v, kk, k, lws, aas, k_a, s0, L=CHUNK):
    B, _, T, _ = r.shape
    nc = T // L
    fwd = pl.BlockSpec((None, NPAIR, L, LANES), lambda b, c: (b, 0, c, 0))
    bwd = pl.BlockSpec((None, NPAIR, L, LANES), lambda b, c: (b, 0, nc - 1 - c, 0))
    st = pl.BlockSpec((None, 2, B_HEADS, B_HEAD_DIM, B_HEAD_DIM), lambda b, c: (b, 0, 0, 0, 0))
    seq_shape = jax.ShapeDtypeStruct((B, NPAIR, T, LANES), F32)
    return pl.pallas_call(
        functools.partial(_rwkv_scan_kernel, L=L),
        grid=(B, nc),
        in_specs=[fwd] * 6 + [bwd] * 6 + [pl.BlockSpec((NPAIR, 1, LANES), lambda b, c: (0, 0, 0)), st],
        out_specs=[fwd, bwd, st],
        out_shape=[seq_shape, seq_shape, jax.ShapeDtypeStruct((B, 2, B_HEADS, B_HEAD_DIM, B_HEAD_DIM), F32)],
        scratch_shapes=[pltpu.VMEM((2, B_HEADS, B_HEAD_DIM, B_HEAD_DIM), F32)],
        compiler_params=_cparams(("parallel", "arbitrary")))(
            r, v, kk, k, lws[0], aas[0], r, v, kk, k, lws[1], aas[1], k_a.reshape(NPAIR, 1, LANES), s0)


def _rwkv_post_kernel(y0_ref, y1_ref, r_ref, k_ref, v_ref, g_ref, rk_ref, lnw_ref, lnb_ref, o_ref):
    gsum = _pair_sum_matrix()
    inv_n = 1.0 / B_HEAD_DIM
    for p in range(NPAIR):
        y = y0_ref[p] + y1_ref[p]
        mean = _dot_hp(y, gsum) * inv_n
        yc = y - mean
        var = _dot_hp(yc * yc, gsum) * inv_n
        yn = yc * lax.rsqrt(var + B_GN_EPS) * lnw_ref[p] + lnb_ref[p]
        bonus = _dot_hp(r_ref[p] * k_ref[p] * rk_ref[p], gsum) * v_ref[p]
        o_ref[:, p * LANES:(p + 1) * LANES] = (yn + bonus) * g_ref[p]


def _rwkv_post(y0, y1, r, k, v, g, r_k, ln_w, ln_b, tt=256):
    B, _, T, _ = r.shape
    tt = min(tt, T)
    nt = T // tt
    seq = pl.BlockSpec((None, NPAIR, tt, LANES), lambda i: (i // nt, 0, i % nt, 0))
    par = pl.BlockSpec((NPAIR, 1, LANES), lambda i: (0, 0, 0))
    return pl.pallas_call(
        _rwkv_post_kernel, grid=(B * nt,),
        in_specs=[seq] * 6 + [par] * 3,
        out_specs=pl.BlockSpec((tt, B_W), lambda i: (i, 0)),
        out_shape=jax.ShapeDtypeStruct((B * T, B_W), F32),
        compiler_params=_cparams(("parallel",)))(
            y0, y1, r, k, v, g, r_k.reshape(NPAIR, 1, LANES), ln_w.reshape(NPAIR, 1, LANES),
            ln_b.reshape(NPAIR, 1, LANES))


def _rwkv_mixer(pb, B, T, prm, s0):
    r, k, v, kk, g, lw0, lw1, a0, a1 = _rwkv_prep(pb, B, T, prm['mu'], prm['w0'], prm['w2'], prm['a0'],
                                                   prm['a2'], prm['g2'], prm['k_k'])
    if s0 is None:
        s0 = jnp.zeros((B, 2, B_HEADS, B_HEAD_DIM, B_HEAD_DIM), F32)
    y0, y1, s_fin = _rwkv_scan(r, v, kk, k, (lw0, lw1), (a0, a1), prm['k_a'], s0)
    out = _rwkv_post(y0, y1, r, k, v, g, prm['r_k'], prm['ln_w'], prm['ln_b'])
    return out, s_fin


def _gdn_prep_kernel(x_ref, prev_ref, next_ref, cw_ref, o_ref, *, tt, T):
    c = pl.program_id(1)
    x = x_ref[...]
    xp, xn = _shifted(x, prev_ref, next_ref, tt, T)
    y = cw_ref[0:1, :] * xp + cw_ref[1:2, :] * x + cw_ref[2:3, :] * xn
    y = y * _sigmoid(y)

    @pl.when(c == 2)
    def _():
        o_ref[...] = y

    @pl.when(c < 2)
    def _():
        scale = jnp.where(c == 0, C_HEAD_DIM ** -0.5, 1.0).astype(F32)
        for h in range(C_HEADS):
            sl = slice(h * C_HEAD_DIM, (h + 1) * C_HEAD_DIM)
            yh = y[:, sl]
            o_ref[:, sl] = yh * lax.rsqrt(jnp.sum(yh * yh, axis=-1, keepdims=True) + 1e-6) * scale


def _gdn_prep(pc, T, conv_w, tt=256):
    M = pc.shape[0]
    tt = min(tt, T)
    in_specs = _halo_specs(tt, C_W, M, lambda i, c: c) + [pl.BlockSpec((3, C_W), lambda i, c: (0, c))]
    return pl.pallas_call(
        functools.partial(_gdn_prep_kernel, tt=tt, T=T),
        grid=(M // tt, 3), in_specs=in_specs,
        out_specs=pl.BlockSpec((tt, C_W), lambda i, c: (i, c)),
        out_shape=jax.ShapeDtypeStruct((M, 3 * C_W), F32),
        compiler_params=_cparams(("parallel", "arbitrary")))(pc, pc, pc, conv_w)


def _gdn_bg_kernel(x_ref, alog_ref, dtb_ref, o_ref):
    x = x_ref[...]
    lane = lax.broadcasted_iota(jnp.int32, x.shape, 1)
    beta = _sigmoid(x)
    g = -jnp.exp(alog_ref[...]) * _softplus(x + dtb_ref[...])
    o_ref[...] = jnp.where(lane < 2 * C_HEADS, beta, g)


def _gdn_bg(pc, a_log, dt_bias, tt=512):
    M = pc.shape[0]
    tt = min(tt, M)
    nb = 2 * C_HEADS
    pad = lambda p: jnp.concatenate([jnp.zeros((nb,), F32), p.reshape(nb), jnp.zeros((LANES - 2 * nb,), F32)]).reshape(1, LANES)
    return pl.pallas_call(
        _gdn_bg_kernel, grid=(M // tt,),
        in_specs=[pl.BlockSpec((tt, LANES), lambda i: (i, 4 * C_W // LANES)),
                  pl.BlockSpec((1, LANES), lambda i: (0, 0)), pl.BlockSpec((1, LANES), lambda i: (0, 0))],
        out_specs=pl.BlockSpec((tt, LANES), lambda i: (i, 0)),
        out_shape=jax.ShapeDtypeStruct((M, LANES), F32),
        compiler_params=_cparams(("parallel",)))(pc, pad(a_log), pad(dt_bias))


def _gdn_chains(qs, ks, vs, bcol, gcol, grow, L, rev):
    incl, strict, _ = _tri_masks(L, rev)
    i = lax.broadcasted_iota(jnp.int32, (L, L), 0)
    j = lax.broadcasted_iota(jnp.int32, (L, L), 1)
    cum_t = ((i >= j) if rev else (i <= j)).astype(F32)
    gc_cols = _dot_hp(incl.astype(F32), gcol)
    gc_rows = _dot_hp(grow, cum_t)
    last = 0 if rev else L - 1
    chains = []
    for h in range(len(qs)):
        gc = jnp.broadcast_to(gc_cols[:, h:h + 1], (L, LANES))
        beta = bcol[:, h:h + 1]
        kb = ks[h] * beta
        egc = jnp.exp(gc)
        g_end = gc[last:last + 1, :]
        chains.append(dict(
            q=qs[h], k=ks[h], incl=incl, strict=strict, kb=kb, qe=qs[h] * egc,
            decay=jnp.exp(jnp.where(incl, gc[:, :L] - gc_rows[h:h + 1, :], -1e30)),
            rhs=jnp.concatenate([vs[h] * beta, kb * egc], axis=1),
            kend=ks[h] * jnp.exp(g_end - gc), g_end=jnp.exp(g_end)))
    return chains


def _gdn_chunk(chains, states, L):
    r = lax.broadcasted_iota(jnp.int32, (L, L), 0)
    c = lax.broadcasted_iota(jnp.int32, (L, L), 1)
    eye_f = (r == c).astype(F32)
    kk = [_dot_hp(ch['kb'], ch['k'], NT) for ch in chains]
    qk = [_dot_hp(ch['q'], ch['k'], NT) for ch in chains]
    qs_s = [_dot_hp(ch['qe'], s) for ch, s in zip(chains, states)]
    tinv = _inv_unit_tri([jnp.where(ch['strict'], x * ch['decay'], 0.0) for ch, x in zip(chains, kk)], eye_f, L)
    sol = [_dot_hp(t, ch['rhs']) for t, ch in zip(tinv, chains)]
    v_new = [x[:, :C_HEAD_DIM] - _dot_hp(x[:, C_HEAD_DIM:], s) for x, s in zip(sol, states)]
    o = [a + _dot_hp(jnp.where(ch['incl'], x * ch['decay'], 0.0), vn)
         for a, x, ch, vn in zip(qs_s, qk, chains, v_new)]
    s_new = [s * ch['g_end'] + _dot_hp(ch['kend'], vn, TN) for s, ch, vn in zip(states, chains, v_new)]
    return o, s_new


def _gdn_scan_kernel(qf_ref, kf_ref, vf_ref, qr_ref, kr_ref, vr_ref, bf_ref, gf_ref, rf_ref, br_ref, gr_ref, rr_ref,
                     s0_ref, of_ref, or_ref, sf_ref, s_ref, *, L):
    c = pl.program_id(1)

    @pl.when(c == 0)
    def _():
        s_ref[...] = s0_ref[...]

    heads = range(C_HEADS)
    hs = [slice(h * C_HEAD_DIM, (h + 1) * C_HEAD_DIM) for h in heads]
    chains = _gdn_chains([qf_ref[:, sl] for sl in hs], [kf_ref[:, sl] for sl in hs], [vf_ref[:, sl] for sl in hs],
                         bf_ref[...], gf_ref[...], rf_ref[...], L, False)
    chains += _gdn_chains([qr_ref[:, sl] for sl in hs], [kr_ref[:, sl] for sl in hs], [vr_ref[:, sl] for sl in hs],
                          br_ref[...], gr_ref[...], rr_ref[...], L, True)
    o, s_new = _gdn_chunk(chains, [s_ref[d, h] for d in range(2) for h in heads], L)
    for h in heads:
        of_ref[:, hs[h]] = o[h]
        or_ref[:, hs[h]] = o[C_HEADS + h]
        s_ref[0, h] = s_new[h]
        s_ref[1, h] = s_new[C_HEADS + h]

    @pl.when(c == pl.num_programs(1) - 1)
    def _():
        sf_ref[...] = s_ref[...]


def _gdn_scan(qkv, bcols, gcols, grows, s0, B, T, L=CHUNK):
    nc = T // L
    fwd = lambda c: c
    bwd = lambda c: nc - 1 - c
    seq = lambda col, pos: pl.BlockSpec((L, C_W), lambda b, c: (b * nc + pos(c), col))
    small = lambda pos: pl.BlockSpec((L, C_HEADS), lambda b, c: (b * nc + pos(c), 0))
    rows = lambda pos: pl.BlockSpec((None, C_HEADS, L), lambda b, c: (b * nc + pos(c), 0, 0))
    st = pl.BlockSpec((None, 2, C_HEADS, C_HEAD_DIM, C_HEAD_DIM), lambda b, c: (b, 0, 0, 0, 0))
    return pl.pallas_call(
        functools.partial(_gdn_scan_kernel, L=L),
        grid=(B, nc),
        in_specs=[seq(0, fwd), seq(1, fwd), seq(2, fwd), seq(0, bwd), seq(1, bwd), seq(2, bwd),
                  small(fwd), small(fwd), rows(fwd), small(bwd), small(bwd), rows(bwd), st],
        out_specs=[pl.BlockSpec((L, C_W), lambda b, c: (b * nc + c, 0)),
                   pl.BlockSpec((L, C_W), lambda b, c: (b * nc + nc - 1 - c, 0)), st],
        out_shape=[jax.ShapeDtypeStruct((B * T, C_W), F32), jax.ShapeDtypeStruct((B * T, C_W), F32),
                   jax.ShapeDtypeStruct((B, 2, C_HEADS, C_HEAD_DIM, C_HEAD_DIM), F32)],
        scratch_shapes=[pltpu.VMEM((2, C_HEADS, C_HEAD_DIM, C_HEAD_DIM), F32)],
        compiler_params=_cparams(("parallel", "arbitrary")))(
            qkv, qkv, qkv, qkv, qkv, qkv, bcols[0], gcols[0], grows[0], bcols[1], gcols[1], grows[1], s0)


def _gdn_post_kernel(o0_ref, o1_ref, z_ref, g_ref, out_ref):
    for h in range(C_HEADS):
        sl = slice(h * C_HEAD_DIM, (h + 1) * C_HEAD_DIM)
        y = o0_ref[:, sl] + o1_ref[:, sl]
        y = y * lax.rsqrt(jnp.mean(y * y, axis=-1, keepdims=True) + NORM_EPS) * g_ref[...]
        z = z_ref[:, sl]
        out_ref[:, sl] = y * (z * _sigmoid(z))


def _gdn_post(o0, o1, pc, norm_g, tt=256):
    M = o0.shape[0]
    tt = min(tt, M)
    return pl.pallas_call(
        _gdn_post_kernel, grid=(M // tt,),
        in_specs=[pl.BlockSpec((tt, C_W), lambda i: (i, 0)), pl.BlockSpec((tt, C_W), lambda i: (i, 0)),
                  pl.BlockSpec((tt, C_W), lambda i: (i, 3)), pl.BlockSpec((1, C_HEAD_DIM), lambda i: (0, 0))],
        out_specs=pl.BlockSpec((tt, C_W), lambda i: (i, 0)),
        out_shape=jax.ShapeDtypeStruct((M, C_W), F32),
        compiler_params=_cparams(("parallel",)))(o0, o1, pc, norm_g.reshape(1, C_HEAD_DIM))


def _gdn_mixer(pc, B, T, prm, s0, L=CHUNK):
    nc = T // L
    qkv = _gdn_prep(pc, T, prm['conv'])
    bg = _gdn_bg(pc, prm['a_log'], prm['dt_bias'])
    if s0 is None:
        s0 = jnp.zeros((B, 2, C_HEADS, C_HEAD_DIM, C_HEAD_DIM), F32)
    bcols = [bg[:, d * C_HEADS:(d + 1) * C_HEADS] for d in range(2)]
    gcols = [bg[:, (2 + d) * C_HEADS:(3 + d) * C_HEADS] for d in range(2)]
    grows = [g.reshape(B * nc, L, C_HEADS).transpose(0, 2, 1) for g in gcols]
    o_fwd, o_bwd, s_fin = _gdn_scan(qkv, bcols, gcols, grows, s0, B, T)
    return _gdn_post(o_fwd, o_bwd, pc, prm['norm_g']), s_fin


def _top_values(tile, count, need_mask):
    rows = lax.broadcasted_iota(jnp.int32, tile.shape, 0).astype(F32)
    big = float(tile.shape[0])
    cur = tile
    vals = []
    for _ in range(count):
        m = jnp.max(cur, axis=0, keepdims=True)
        vals.append(m)
        idx = jnp.min(jnp.where(cur == m, rows, big), axis=0, keepdims=True)
        cur = jnp.where(rows == idx, -jnp.inf, cur)
    return vals, (cur == -jnp.inf) if need_mask else None


def _top_values_untied(tile, count):
    cur = tile
    vals = []
    for _ in range(count):
        m = jnp.max(cur, axis=0, keepdims=True)
        vals.append(m)
        cur = jnp.where(cur == m, -jnp.inf, cur)
    return vals, cur == -jnp.inf


PEER_EB = 1024
PEER_GRP = MXU_K // PEER_HALF
PEER_NGRP = 2 * PEER_HEADS // PEER_GRP


def _peer_topk_kernel(q_ref, kh_ref, kl_ref, thr_ref, e0_ref, e1_ref, s1_ref, sc_ref):
    rows_g = PEER_GRP * N_KEYS
    for g in range(PEER_NGRP):
        qh, ql = _split(q_ref[:, g * MXU_K:(g + 1) * MXU_K])
        kh = kh_ref[g * rows_g:(g + 1) * rows_g, :]
        kl = kl_ref[g * rows_g:(g + 1) * rows_g, :]
        sc_ref[g * rows_g:(g + 1) * rows_g, :] = _dg(kh, qh, NT) + (_dg(kh, ql, NT) + _dg(kl, qh, NT))

    def route(h, t0, t1, exact):
        if exact:
            a, ra = _top_values(t0, PEER_TOPK, True)
            b, rb = _top_values(t1, PEER_TOPK, True)
        else:
            a, ra = _top_values_untied(t0, PEER_TOPK)
            b, rb = _top_values_untied(t1, PEER_TOPK)
        s0 = jnp.where(ra, t0, -jnp.inf)
        s1 = jnp.where(rb, t1, -jnp.inf)
        amat = jnp.concatenate(a, axis=0)
        bmat = jnp.concatenate(b, axis=0)
        b_lo = bmat[:8]
        row8 = lax.broadcasted_iota(jnp.int32, b_lo.shape, 0)
        pieces = [a[0] + b_lo, a[0] + bmat[8:], a[1] + b_lo]
        n_pad = 0
        for r in range(2, 8):
            pieces.append(jnp.where(row8 < PEER_TOPK // (r + 1), a[r] + b_lo, -jnp.inf))
            n_pad += 8 - PEER_TOPK // (r + 1)
        pieces.append(amat[8:] + b[0])
        cand = jnp.concatenate(pieces, axis=0)
        tied = None
        if exact:
            best, _ = _top_values(cand, PEER_TOPK + 1, False)
        else:
            best, rc = _top_values_untied(cand, PEER_TOPK + 1)
            count = lambda mask: jnp.sum(jnp.where(mask, 1.0, 0.0), axis=0, keepdims=True)
            wrong = jnp.logical_or(jnp.logical_or(count(ra) != PEER_TOPK, count(rb) != PEER_TOPK),
                                   count(rc) != PEER_TOPK + 1 + n_pad)
            tied = jnp.max(jnp.where(wrong, 1.0, 0.0)) > 0.0
        z = jnp.zeros_like(best[0])
        for n in range(PEER_TOPK):
            z = z + jnp.exp(best[n] - best[0])
        tau = 0.5 * (best[PEER_TOPK - 1] + best[PEER_TOPK])
        thr = tau - s0
        e0 = jnp.exp(s0 - a[0]) * (1.0 / z)
        e1 = jnp.exp(s1 - b[0])
        for tc in range(t0.shape[1] // LANES):
            cols = slice(tc * LANES, (tc + 1) * LANES)
            thr_ref[h, tc] = thr[:, cols]
            e0_ref[h, tc] = e0[:, cols]
            e1_ref[h, tc] = e1[:, cols]
            s1_ref[h, tc] = s1[:, cols]
        return tied

    def head(h, carry):
        t0 = sc_ref[pl.ds(pl.multiple_of(h * 2 * N_KEYS, N_KEYS), N_KEYS), :]
        t1 = sc_ref[pl.ds(pl.multiple_of(h * 2 * N_KEYS + N_KEYS, N_KEYS), N_KEYS), :]
        tied = route(h, t0, t1, exact=False)
        pl.when(tied)(lambda: route(h, t0, t1, exact=True))
        return carry

    lax.fori_loop(0, PEER_HEADS, head, 0)


def _peer_topk(q, kdt_hi, kdt_lo, tt=256):
    M = q.shape[0]
    tt = min(tt, M)
    R = PEER_HEADS * 2 * N_KEYS
    big = pl.BlockSpec((PEER_HEADS, tt // LANES, N_KEYS, LANES), lambda i: (0, i, 0, 0))
    return pl.pallas_call(
        _peer_topk_kernel, grid=(M // tt,),
        in_specs=[pl.BlockSpec((tt, PEER_HEADS * PEER_QDIM), lambda i: (i, 0)),
                  pl.BlockSpec((R, MXU_K), lambda i: (0, 0)),
                  pl.BlockSpec((R, MXU_K), lambda i: (0, 0))],
        out_specs=[big] * 4,
        out_shape=[jax.ShapeDtypeStruct((PEER_HEADS, M // LANES, N_KEYS, LANES), F32)] * 4,
        scratch_shapes=[pltpu.VMEM((R, tt), F32)],
        compiler_params=_cparams(("parallel",)))(q, kdt_hi, kdt_lo)


def _peer_expert_kernel(xt_ref, u_ref, vt_ref, thr_ref, e0_ref, e1_ref, s1_ref, res_ref, m_ref, o_ref,
                        acc_ref, act0_ref, act1_ref, am0_ref, am1_ref, *, ni, gate_slot):
    e = pl.program_id(1)
    n_blocks = pl.num_programs(1) - 2
    tt = xt_ref.shape[1]
    even = e % 2 == 0
    odd = e % 2 == 1

    @pl.when(e == 0)
    def _():
        acc_ref[...] = jnp.zeros_like(acc_ref)

    def project(act_w):
        act_w[...] = _dg(u_ref[...], xt_ref[...])

    def weigh(act_r, am_w):
        first = (e - 1) * ni
        for ii in range(ni):
            i = first + ii
            rows = slice(ii * N_KEYS, (ii + 1) * N_KEYS)
            for tc in range(tt // LANES):
                cols = slice(tc * LANES, (tc + 1) * LANES)
                w = None
                for h in range(PEER_HEADS):
                    keep = s1_ref[h, tc] >= thr_ref[h, tc, pl.ds(i, 1), :]
                    contrib = jnp.where(keep, e1_ref[h, tc], 0.0) * e0_ref[h, tc, pl.ds(i, 1), :]
                    w = contrib if w is None else w + contrib
                a = act_r[rows, cols]
                a = 0.5 * a * (1.0 + lax.erf(a * (2.0 ** -0.5)))
                am_w[rows, cols] = (a * w).astype(BF16)

    def accumulate(am_r):
        acc_ref[...] += _dg(vt_ref[...], am_r[...])

    has_project = e < n_blocks
    has_weigh = jnp.logical_and(e >= 1, e <= n_blocks)
    has_accumulate = e >= 2
    pl.when(jnp.logical_and(has_project, even))(lambda: project(act0_ref))
    pl.when(jnp.logical_and(has_project, odd))(lambda: project(act1_ref))
    pl.when(jnp.logical_and(has_weigh, even))(lambda: weigh(act1_ref, am1_ref))
    pl.when(jnp.logical_and(has_weigh, odd))(lambda: weigh(act0_ref, am0_ref))
    pl.when(jnp.logical_and(has_accumulate, even))(lambda: accumulate(am0_ref))
    pl.when(jnp.logical_and(has_accumulate, odd))(lambda: accumulate(am1_ref))

    @pl.when(e == pl.num_programs(1) - 1)
    def _():
        o_ref[...] = res_ref[...] + m_ref[gate_slot:gate_slot + 1, :] * acc_ref[...].T


def _peer_experts(xn, u_bf, vt_bf, thr, e0, e1, s1m, res, mods, gate_slot, rows_per_mod, tt=512):
    M, D = res.shape
    tt = min(tt, M, rows_per_mod) if mods.shape[0] > 1 else min(tt, M)
    nb, _, eb = vt_bf.shape
    ni = eb // N_KEYS
    once = pl.Buffered(1)
    if mods.shape[0] > 1:
        mspec = pl.BlockSpec((None, 6, D), lambda i, e: ((i * tt) // rows_per_mod, 0, 0), pipeline_mode=once)
    else:
        mspec = pl.BlockSpec((None, 6, D), lambda i, e: (0, 0, 0), pipeline_mode=once)
    big = pl.BlockSpec((PEER_HEADS, tt // LANES, N_KEYS, LANES), lambda i, e: (0, i, 0, 0), pipeline_mode=once)
    return pl.pallas_call(
        functools.partial(_peer_expert_kernel, ni=ni, gate_slot=gate_slot),
        grid=(M // tt, nb + 2),
        in_specs=[pl.BlockSpec((D, tt), lambda i, e: (0, i), pipeline_mode=once),
                  pl.BlockSpec((eb, D), lambda i, e: (jnp.minimum(e, nb - 1), 0)),
                  pl.BlockSpec((None, D, eb), lambda i, e: (jnp.clip(e - 2, 0, nb - 1), 0, 0)),
                  big, big, big, big,
                  pl.BlockSpec((tt, D), lambda i, e: (i, 0), pipeline_mode=once), mspec],
        out_specs=pl.BlockSpec((tt, D), lambda i, e: (i, 0)),
        out_shape=jax.ShapeDtypeStruct((M, D), F32),
        scratch_shapes=[pltpu.VMEM((D, tt), F32), pltpu.VMEM((eb, tt), F32), pltpu.VMEM((eb, tt), F32),
                        pltpu.VMEM((eb, tt), BF16), pltpu.VMEM((eb, tt), BF16)],
        compiler_params=_cparams(("parallel", "arbitrary"), VMEM_LIMIT_BIG))(
            xn, u_bf, vt_bf, thr, e0, e1, s1m, res, mods)


def _peer(x, mods, rows_per_mod, norm_g, prm):
    q, xn = _mm_norm(x, prm['wq'], norm_g=norm_g, mods=mods, slots=(3, 4), rows_per_mod=rows_per_mod,
                     hp=True, emit_xn=BF16)
    thr, e0, e1, s1m = _peer_topk(q, prm['kdt_hi'], prm['kdt_lo'])
    return _peer_experts(xn.T, prm['u'], prm['vt'], thr, e0, e1, s1m, x, mods, 5, rows_per_mod)


def _peer_params(wq, keys, u_tab, v_tab):
    nset = 2 * PEER_HEADS
    kd = jnp.zeros((nset, N_KEYS, PEER_GRP, PEER_HALF), F32)
    sets = jnp.arange(nset)
    kd = kd.at[sets, :, sets % PEER_GRP, :].set(keys.reshape(nset, N_KEYS, PEER_HALF))
    kd_hi, kd_lo = _split(kd.reshape(nset * N_KEYS, MXU_K))
    vt = v_tab.astype(BF16).reshape(N_EXPERTS // PEER_EB, PEER_EB, -1).transpose(0, 2, 1)
    return dict(wq=_split(wq), kdt_hi=kd_hi, kdt_lo=kd_lo, u=u_tab.astype(BF16), vt=vt)


def _layer_even(x, mods, B, T, norm1, w_a, w_b, w_out, qk_gains, rwkv_prm, ctx):
    rpm = T
    pa = _mm_norm(x, w_a, norm_g=norm1, mods=mods, slots=(0, 1), rows_per_mod=rpm)
    pb = _mm_norm(x, w_b, norm_g=norm1, mods=mods, slots=(0, 1), rows_per_mod=rpm)
    nqk = A_HEADS + A_KV_HEADS
    assert A_HEADS // A_KV_HEADS == ATT_GROUP
    scale = A_HEAD_DIM ** -0.5
    if ctx is None:
        qk = _norm_rope(pa, nqk, T, gains=qk_gains)
        oa = _attention(B, T, T, A_HEADS, scale, (qk, 0), (qk, A_HEADS), (pa, nqk))
        ctx_out = (qk[:, A_Q:A_Q + A_KV], pa[:, A_Q + A_KV:])
        s0 = None
    else:
        cache_k, cache_v, s0 = ctx
        P = cache_k.shape[1]
        qk = _norm_rope(pa, nqk, T, gains=qk_gains, tables=_rope_tables(T, A_HEAD_DIM))
        k_all = jnp.concatenate([qk[:, A_Q:].reshape(B, T, A_KV), cache_k.reshape(B, P, A_KV)], axis=1)
        v_all = jnp.concatenate([pa[:, A_Q + A_KV:].reshape(B, T, A_KV), cache_v.reshape(B, P, A_KV)], axis=1)
        Tk = T + P
        oa = _attention(B, T, Tk, A_HEADS, scale, (qk, 0), (k_all.reshape(B * Tk, A_KV).astype(BF16), 0),
                        (v_all.reshape(B * Tk, A_KV).astype(BF16), 0))
        ctx_out = None
    ob, s_fin = _rwkv_mixer(pb, B, T, rwkv_prm, s0)
    x = _mm_res([oa, ob], w_out, x, mods, 2, rpm)
    return x, ctx_out, s_fin


D_QROPE_BLK = 0
D_KROPE_BLK = D_HEADS
D_NROPE_BLK = D_HEADS + 1
D_CKV_BLK = 12
D_QNOPE_BLK = 16
D_PROJ = 24 * LANES


def _mla_in_weight(w_d):
    K = w_d.shape[0]
    wq = w_d[:, :D_Q].reshape(K, D_HEADS, D_QK)
    z64 = jnp.zeros((K, D_HEADS, LANES - D_ROPE), w_d.dtype)
    q_rope = jnp.concatenate([wq[:, :, D_NOPE:], z64], axis=-1).reshape(K, D_HEADS * LANES)
    q_nope = wq[:, :, :D_NOPE].reshape(K, D_HEADS * LANES)
    k_rope = jnp.concatenate([w_d[:, D_Q + D_KV_RANK:], jnp.zeros((K, LANES - D_ROPE), w_d.dtype)], axis=-1)
    padz = jnp.zeros((K, (D_CKV_BLK - D_NROPE_BLK) * LANES), w_d.dtype)
    return jnp.concatenate([q_rope, k_rope, padz, w_d[:, D_Q:D_Q + D_KV_RANK], q_nope], axis=-1)


def _layer_odd(x, mods, B, T, norm1, w_c, w_d, w_out, gdn_prm, kv_norm_g, w_kv_b, ctx):
    rpm = T
    pc = _mm_norm(x, w_c, norm_g=norm1, mods=mods, slots=(0, 1), rows_per_mod=rpm)
    pd = _mm_norm(x, w_d, norm_g=norm1, mods=mods, slots=(0, 1), rows_per_mod=rpm)
    scale = D_QK ** -0.5
    ckv_col = D_CKV_BLK * LANES // D_KV_RANK
    kv, ckv_n = _mm_norm(pd, w_kv_b, k=D_KV_RANK, xcol=ckv_col, norm_g=kv_norm_g, emit_xn=F32)
    if ctx is None:
        od = _attention(B, T, T, D_HEADS, scale, (pd, D_QNOPE_BLK), (kv, 0), (kv, 0),
                        q2=(pd, D_QROPE_BLK), k2=(pd, D_KROPE_BLK))
        ctx_out = (ckv_n, pd[:, D_KROPE_BLK * LANES:D_KROPE_BLK * LANES + D_ROPE])
        s0 = None
    else:
        s0, c_ckv, c_krope = ctx
        P = c_ckv.shape[1]
        Tk = T + P
        roped = _norm_rope(pd, D_NROPE_BLK, T, tables=_rope_tables(T, D_ROPE))
        kv_ctx = _mm_norm(c_ckv.reshape(B * P, D_KV_RANK), w_kv_b)
        kv_all = jnp.concatenate([kv.reshape(B, T, -1), kv_ctx.reshape(B, P, -1)], axis=1).reshape(B * Tk, -1)
        kv_all = kv_all.astype(BF16)
        kr_lat = roped[:, D_KROPE_BLK * LANES:].reshape(B, T, LANES)
        kr_ctx = jnp.concatenate([c_krope, jnp.zeros((B, P, LANES - D_ROPE), F32)], axis=-1)
        kr_all = jnp.concatenate([kr_lat, kr_ctx], axis=1).reshape(B * Tk, LANES).astype(BF16)
        od = _attention(B, T, Tk, D_HEADS, scale, (pd, D_QNOPE_BLK), (kv_all, 0), (kv_all, 0),
                        q2=(roped, D_QROPE_BLK), k2=(kr_all, 0))
        ctx_out = None
    oc, s_fin = _gdn_mixer(pc, B, T, gdn_prm, s0)
    x = _mm_res([oc, od], w_out, x, mods, 2, rpm)
    return x, ctx_out, s_fin


def _modulation(cond, w_ada, b_ada):
    n = cond.shape[0]
    rows = ((n + 7) // 8) * 8
    a = cond * _sigmoid(cond)
    a = jnp.concatenate([a, jnp.zeros((rows - n, cond.shape[1]), F32)], axis=0)
    m = _mm_norm(a, _split(w_ada), hp=True, tn_cap=1024)[:n] + b_ada
    return m.reshape(n, 6, D_MODEL)


def kernel(x_prompt, x_sample, cache_attn_k, cache_attn_v, state_rwkv, state_gdn, cache_mla_ckv, cache_mla_krope,
           c, c_ctx, norm1_g, norm2_g, ada_w, ada_b, w_in_ab, w_out_ab, attn_q_norm, attn_k_norm,
           rwkv_mu, rwkv_w0, rwkv_w2, rwkv_a0, rwkv_a2, rwkv_g2, rwkv_k_k, rwkv_k_a, rwkv_r_k, rwkv_ln_w, rwkv_ln_b,
           w_in_cd, w_out_cd, gdn_conv, gdn_a_log, gdn_dt_bias, gdn_norm_g, mla_kv_norm_g, mla_w_kv_b,
           peer_wq, peer_keys, peer_u, peer_v, final_norm_g):
    Bp, Tp, D = x_prompt.shape
    Bs, Ts, _ = x_sample.shape
    depth = norm1_g.shape[0]
    xp = x_prompt.reshape(Bp * Tp, D)
    xs = x_sample.reshape(Bs * Ts, D)
    cond = jnp.concatenate([c_ctx[None, :], c], axis=0)
    outs = dict(k=[], v=[], rwkv=[], gdn=[], ckv=[], krope=[])
    for li in range(depth):
        j = li // 2
        mods = _modulation(cond, ada_w[li], ada_b[li])
        mods_p, mods_s = mods[:1], mods[1:]
        peer_prm = _peer_params(peer_wq[li], peer_keys[li], peer_u[li], peer_v[li])
        if li % 2 == 0:
            w_in = w_in_ab[j].astype(BF16)
            w_a, w_b = w_in[:, :A_IN], w_in[:, A_IN:]
            w_o = w_out_ab[j].astype(BF16)
            w_out = [w_o[:A_Q], w_o[A_Q:]]
            gains = jnp.concatenate([jnp.tile(attn_q_norm[j], A_HEADS), jnp.tile(attn_k_norm[j], A_KV_HEADS)])
            rp = dict(mu=rwkv_mu[j], w0=rwkv_w0[j], w2=rwkv_w2[j], a0=rwkv_a0[j], a2=rwkv_a2[j], g2=rwkv_g2[j],
                      k_k=rwkv_k_k[j], k_a=rwkv_k_a[j], r_k=rwkv_r_k[j], ln_w=rwkv_ln_w[j], ln_b=rwkv_ln_b[j])
            xp, (k_c, v_c), s_c = _layer_even(xp, mods_p, Bp, Tp, norm1_g[li], w_a, w_b, w_out, gains, rp, None)
            xs, _, _ = _layer_even(xs, mods_s, Bs, Ts, norm1_g[li], w_a, w_b, w_out, gains, rp,
                                   (cache_attn_k[:, j], cache_attn_v[:, j], state_rwkv[:, j]))
            outs['k'].append(k_c.reshape(Bp, Tp, A_KV_HEADS, A_HEAD_DIM))
            outs['v'].append(v_c.reshape(Bp, Tp, A_KV_HEADS, A_HEAD_DIM))
            outs['rwkv'].append(s_c)
        else:
            w_in = w_in_cd[j]
            w_c = jnp.concatenate([w_in[:, :C_IN], jnp.zeros((D, LANES - 4 * C_HEADS), F32)], axis=-1).astype(BF16)
            w_d = _mla_in_weight(w_in[:, C_IN:]).astype(BF16)
            w_o = w_out_cd[j].astype(BF16)
            w_out = [w_o[:C_W], w_o[C_W:]]
            gp = dict(conv=gdn_conv[j], a_log=gdn_a_log[j], dt_bias=gdn_dt_bias[j], norm_g=gdn_norm_g[j])
            w_kv_b = mla_w_kv_b[j].astype(BF16)
            xp, (ckv_c, kr_c), s_c = _layer_odd(xp, mods_p, Bp, Tp, norm1_g[li], w_c, w_d, w_out, gp,
                                                mla_kv_norm_g[j], w_kv_b, None)
            xs, _, _ = _layer_odd(xs, mods_s, Bs, Ts, norm1_g[li], w_c, w_d, w_out, gp, mla_kv_norm_g[j], w_kv_b,
                                  (state_gdn[:, j], cache_mla_ckv[:, j], cache_mla_krope[:, j]))
            outs['gdn'].append(s_c)
            outs['ckv'].append(ckv_c.reshape(Bp, Tp, D_KV_RANK))
            outs['krope'].append(kr_c.reshape(Bp, Tp, D_ROPE))
        xp = _peer(xp, mods_p, Tp, norm2_g[li], peer_prm)
        xs = _peer(xs, mods_s, Ts, norm2_g[li], peer_prm)
    y_prompt = _rmsnorm(xp, final_norm_g).reshape(Bp, Tp, D)
    y_sample = _rmsnorm(xs, final_norm_g).reshape(Bs, Ts, D)
    st = lambda name: jnp.stack(outs[name], axis=1)
    return (y_prompt, y_sample, st('k'), st('v'), st('rwkv'), st('gdn'), st('ckv'), st('krope'))
```

```python
import functools
import math

import numpy as np
import jax
import jax.numpy as jnp
from jax import lax
from jax.experimental import pallas as pl
from jax.experimental.pallas import tpu as pltpu

F32 = jnp.float32
BF16 = jnp.bfloat16

D_MODEL = 2048
GRID_W = 64
ROPE_THETA = 10000.0
NORM_EPS = 1e-6
A_HEADS, A_KV_HEADS, A_HEAD_DIM = 8, 2, 128
A_Q = A_HEADS * A_HEAD_DIM
A_KV = A_KV_HEADS * A_HEAD_DIM
A_IN = A_Q + 2 * A_KV
B_HEADS, B_HEAD_DIM = 16, 64
B_W = B_HEADS * B_HEAD_DIM
B_DECAY_LORA, B_AAA_LORA, B_GATE_LORA = 64, 64, 128
B_IN = 3 * B_W + 2 * B_DECAY_LORA + 2 * B_AAA_LORA + B_GATE_LORA
B_GN_EPS = 64e-5
C_HEADS, C_HEAD_DIM = 8, 128
C_W = C_HEADS * C_HEAD_DIM
C_IN = 4 * C_W + 4 * C_HEADS
D_HEADS, D_NOPE, D_ROPE, D_V = 8, 128, 64, 128
D_QK = D_NOPE + D_ROPE
D_KV_RANK = 512
D_Q = D_HEADS * D_QK
PEER_HEADS, N_KEYS, PEER_QDIM, PEER_TOPK = 8, 128, 128, 16
PEER_HALF = PEER_QDIM // 2
N_EXPERTS = N_KEYS * N_KEYS

LANES = 128
MXU_K = 256
VMEM_LIMIT = 48 * 1024 * 1024
VMEM_LIMIT_BIG = 56 * 1024 * 1024

CHUNK = 64

NN = ((1,), (0,))
NT = ((1,), (1,))
TN = ((0,), (0,))


def _dg(a, b, dims=NN):
    return lax.dot_general(a, b, (dims, ((), ())), preferred_element_type=F32)


def _split(x):
    hi = x.astype(BF16)
    lo = (x - hi.astype(F32)).astype(BF16)
    return hi, lo


def _dot_hp(a, b, dims=NN):
    ah, al = _split(a)
    bh, bl = _split(b)
    return _dg(ah, bh, dims) + (_dg(ah, bl, dims) + _dg(al, bh, dims))


def _dot_bf(a, b, dims=NN):
    return _dg(a.astype(BF16), b.astype(BF16), dims)


def _sigmoid(x):
    return 1.0 / (1.0 + jnp.exp(-x))


def _softplus(x):
    return jnp.maximum(x, 0.0) + jnp.log(1.0 + jnp.exp(-jnp.abs(x)))


def _cparams(sem, vmem=VMEM_LIMIT):
    return pltpu.CompilerParams(dimension_semantics=sem, vmem_limit_bytes=vmem)


def _pick_tile(n, cap):
    best = None
    t = LANES
    while t <= min(n, cap):
        if n % t == 0:
            best = t
        t += LANES
    return best if best is not None else n


def _mm_norm_kernel(*refs, has_norm, has_mod, shift_slot, scale_slot, hp, emit_xn):
    it = iter(refs)
    x_ref = next(it)
    g_ref = next(it) if has_norm else None
    m_ref = next(it) if has_mod else None
    wh_ref = next(it)
    wl_ref = next(it) if hp else None
    o_ref = next(it)
    xn_ref = next(it) if emit_xn else None
    xh_ref = next(it)
    xl_ref = next(it) if hp else None

    @pl.when(pl.program_id(1) == 0)
    def _():
        xv = x_ref[...].astype(F32)
        if has_norm:
            xv = xv * lax.rsqrt(jnp.mean(xv * xv, axis=-1, keepdims=True) + NORM_EPS) * g_ref[...]
        if has_mod:
            xv = xv * (1.0 + m_ref[scale_slot:scale_slot + 1, :]) + m_ref[shift_slot:shift_slot + 1, :]
        hi = xv.astype(BF16)
        xh_ref[...] = hi
        if hp:
            xl_ref[...] = (xv - hi.astype(F32)).astype(BF16)
        if emit_xn:
            xn_ref[...] = xv.astype(xn_ref.dtype)

    acc = _dg(xh_ref[...], wh_ref[...])
    if hp:
        acc = acc + (_dg(xh_ref[...], wl_ref[...]) + _dg(xl_ref[...], wh_ref[...]))
    o_ref[...] = acc


def _mm_norm(x, w, *, k=None, xcol=0, norm_g=None, mods=None, slots=(0, 1), rows_per_mod=None,
             hp=False, emit_xn=None, tm=512, tn_cap=1664):
    M = x.shape[0]
    K = k if k is not None else x.shape[1]
    ws = tuple(w) if isinstance(w, (tuple, list)) else (w,)
    N = ws[0].shape[1]
    tm = min(tm, M)
    if mods is not None and mods.shape[0] > 1:
        tm = min(tm, rows_per_mod)
    tn = _pick_tile(N, tn_cap)
    has_norm = norm_g is not None
    has_mod = mods is not None
    rpm = rows_per_mod if rows_per_mod is not None else M

    in_specs = [pl.BlockSpec((tm, K), lambda i, j: (i, xcol))]
    args = [x]
    if has_norm:
        in_specs.append(pl.BlockSpec((1, K), lambda i, j: (0, 0)))
        args.append(norm_g.reshape(1, K).astype(F32))
    if has_mod:
        if mods.shape[0] > 1:
            in_specs.append(pl.BlockSpec((None, 6, K), lambda i, j: ((i * tm) // rpm, 0, 0)))
        else:
            in_specs.append(pl.BlockSpec((None, 6, K), lambda i, j: (0, 0, 0)))
        args.append(mods)
    for wi in ws:
        in_specs.append(pl.BlockSpec((K, tn), lambda i, j: (0, j)))
        args.append(wi)
    out_shape = [jax.ShapeDtypeStruct((M, N), F32)]
    out_specs = [pl.BlockSpec((tm, tn), lambda i, j: (i, j))]
    if emit_xn is not None:
        out_shape.append(jax.ShapeDtypeStruct((M, K), emit_xn))
        out_specs.append(pl.BlockSpec((tm, K), lambda i, j: (i, 0)))
    scratch = [pltpu.VMEM((tm, K), BF16)]
    if hp:
        scratch.append(pltpu.VMEM((tm, K), BF16))
    kern = functools.partial(_mm_norm_kernel, has_norm=has_norm, has_mod=has_mod, shift_slot=slots[0],
                             scale_slot=slots[1], hp=hp, emit_xn=emit_xn is not None)
    outs = pl.pallas_call(
        kern, grid=(M // tm, N // tn), in_specs=in_specs, out_specs=out_specs, out_shape=out_shape,
        scratch_shapes=scratch, compiler_params=_cparams(("parallel", "arbitrary")))(*args)
    return outs if emit_xn is not None else outs[0]


def _mm_res_kernel(*refs, n_in, gate_slot):
    a_refs = refs[:n_in]
    w_refs = refs[n_in:2 * n_in]
    res_ref, m_ref, o_ref = refs[2 * n_in:]
    acc = None
    for a_ref, w_ref in zip(a_refs, w_refs):
        part = _dg(a_ref[...].astype(BF16), w_ref[...])
        acc = part if acc is None else acc + part
    o_ref[...] = res_ref[...] + m_ref[gate_slot:gate_slot + 1, :] * acc


def _mm_res(a_list, w_list, res, mods, gate_slot, rows_per_mod, tm=512, tn=1024):
    M, N = res.shape
    tm = min(tm, M, rows_per_mod) if mods.shape[0] > 1 else min(tm, M)
    tn = min(tn, N)
    n_in = len(a_list)
    in_specs, args = [], []
    for a in a_list:
        in_specs.append(pl.BlockSpec((tm, a.shape[1]), lambda i, j: (i, 0)))
        args.append(a)
    for w in w_list:
        in_specs.append(pl.BlockSpec((w.shape[0], tn), lambda i, j: (0, j)))
        args.append(w)
    in_specs.append(pl.BlockSpec((tm, tn), lambda i, j: (i, j)))
    args.append(res)
    if mods.shape[0] > 1:
        in_specs.append(pl.BlockSpec((None, 6, tn), lambda i, j: ((i * tm) // rows_per_mod, 0, j)))
    else:
        in_specs.append(pl.BlockSpec((None, 6, tn), lambda i, j: (0, 0, j)))
    args.append(mods)
    return pl.pallas_call(
        functools.partial(_mm_res_kernel, n_in=n_in, gate_slot=gate_slot),
        grid=(M // tm, N // tn), in_specs=in_specs,
        out_specs=pl.BlockSpec((tm, tn), lambda i, j: (i, j)),
        out_shape=jax.ShapeDtypeStruct((M, N), F32),
        compiler_params=_cparams(("parallel", "arbitrary")))(*args)


def _rmsnorm_kernel(x_ref, g_ref, o_ref):
    xv = x_ref[...]
    o_ref[...] = xv * lax.rsqrt(jnp.mean(xv * xv, axis=-1, keepdims=True) + NORM_EPS) * g_ref[...]


def _rmsnorm(x, g, tm=512):
    M, K = x.shape
    tm = min(tm, M)
    return pl.pallas_call(
        _rmsnorm_kernel, grid=(M // tm,),
        in_specs=[pl.BlockSpec((tm, K), lambda i: (i, 0)), pl.BlockSpec((1, K), lambda i: (0, 0))],
        out_specs=pl.BlockSpec((tm, K), lambda i: (i, 0)),
        out_shape=jax.ShapeDtypeStruct((M, K), F32),
        compiler_params=_cparams(("parallel",)))(x, g.reshape(1, K))


def _rope_tables(T, rot_dim):
    rows = T // GRID_W
    row = jnp.repeat(jnp.arange(rows, dtype=F32), GRID_W)
    col = jnp.tile(jnp.arange(GRID_W, dtype=F32), rows)
    n_freq = rot_dim // 4
    inv = ROPE_THETA ** (-jnp.arange(n_freq, dtype=F32) / n_freq)
    ang = jnp.concatenate([row[:, None] * inv, col[:, None] * inv], axis=-1)
    cos = jnp.repeat(jnp.cos(ang), 2, axis=-1)
    sin = jnp.repeat(jnp.sin(ang), 2, axis=-1) * jnp.tile(jnp.array([-1.0, 1.0], F32), rot_dim // 2)
    pad = LANES - rot_dim
    if pad:
        cos = jnp.concatenate([cos, jnp.ones((T, pad), F32)], axis=-1)
        sin = jnp.concatenate([sin, jnp.zeros((T, pad), F32)], axis=-1)
    return cos, sin


def _rope(y, cosf, sinf):
    lane = lax.broadcasted_iota(jnp.int32, y.shape, 1)
    even = (lane & 1) == 0
    width = y.shape[1]
    swap = jnp.where(even, pltpu.roll(y, width - 1, 1), pltpu.roll(y, 1, 1))
    return y * cosf + swap * sinf


def _norm_rope_kernel(*refs, nblk, norm, rope):
    it = iter(refs)
    x_ref = next(it)
    g_ref = next(it) if norm else None
    cos_ref = next(it) if rope else None
    sin_ref = next(it) if rope else None
    o_ref = next(it)
    for c in range(nblk):
        sl = slice(c * LANES, (c + 1) * LANES)
        y = x_ref[:, sl]
        if norm:
            y = y * lax.rsqrt(jnp.mean(y * y, axis=-1, keepdims=True) + NORM_EPS) * g_ref[:, sl]
        if rope:
            y = _rope(y, cos_ref[...], sin_ref[...])
        o_ref[:, sl] = y


def _norm_rope(x, nblk, T, gains=None, tables=None, tt=256):
    M = x.shape[0]
    tt = min(tt, T)
    W = nblk * LANES
    in_specs = [pl.BlockSpec((tt, W), lambda i: (i, 0))]
    args = [x]
    if gains is not None:
        in_specs.append(pl.BlockSpec((1, W), lambda i: (0, 0)))
        args.append(gains.reshape(1, W))
    if tables is not None:
        nt = T // tt
        for t in tables:
            in_specs.append(pl.BlockSpec((tt, LANES), lambda i: (i % nt, 0)))
            args.append(t)
    return pl.pallas_call(
        functools.partial(_norm_rope_kernel, nblk=nblk, norm=gains is not None, rope=tables is not None),
        grid=(M // tt,), in_specs=in_specs,
        out_specs=pl.BlockSpec((tt, W), lambda i: (i, 0)),
        out_shape=jax.ShapeDtypeStruct((M, W), F32),
        compiler_params=_cparams(("parallel",)))(*args)


ATT_GROUP = 4


def _attn_kernel(*refs, scale, latent):
    heads = range(ATT_GROUP)
    lane = lambda j: slice(j * LANES, (j + 1) * LANES)
    if latent:
        q_ref, q2_ref, kv_ref, k2_ref, o_ref = refs
        k2 = k2_ref[...]
        qs = [jnp.concatenate([q_ref[:, lane(j)], q2_ref[:, lane(j)]], axis=1) * scale for j in heads]
        ks = [jnp.concatenate([kv_ref[:, lane(2 * j)], k2], axis=1) for j in heads]
        vs = [kv_ref[:, lane(2 * j + 1)] for j in heads]
    else:
        q_ref, k_ref, v_ref, o_ref = refs
        qs = [q_ref[:, lane(j)] * scale for j in heads]
        ks = [k_ref[...]] * ATT_GROUP
        vs = [v_ref[...]] * ATT_GROUP
    s = [_dot_bf(q, k, NT) for q, k in zip(qs, ks)]
    p = [jnp.exp(x - jnp.max(x, axis=-1, keepdims=True)) for x in s]
    l = [jnp.sum(x, axis=-1, keepdims=True) for x in p]
    o = [_dot_bf(x, v) / n for x, v, n in zip(p, vs, l)]
    for j in heads:
        o_ref[:, lane(j)] = o[j]


def _attention(B, T, Tk, H, scale, q, k, v, q2=None, k2=None, tq=512):
    tq = min(tq, T)
    nq = T // tq
    latent = q2 is not None
    gw = ATT_GROUP * LANES

    def qspec(first):
        assert first % ATT_GROUP == 0
        return pl.BlockSpec((tq, gw), lambda b, g, i: (b * nq + i, first // ATT_GROUP + g))

    if latent:
        assert k[0] is v[0] and k[1] % (2 * ATT_GROUP) == 0
        in_specs = [qspec(q[1]), qspec(q2[1]),
                    pl.BlockSpec((Tk, 2 * gw), lambda b, g, i: (b, k[1] // (2 * ATT_GROUP) + g)),
                    pl.BlockSpec((Tk, LANES), lambda b, g, i: (b, k2[1]))]
        args = [q[0], q2[0], k[0], k2[0]]
    else:
        in_specs = [qspec(q[1]), pl.BlockSpec((Tk, LANES), lambda b, g, i: (b, k[1] + g)),
                    pl.BlockSpec((Tk, LANES), lambda b, g, i: (b, v[1] + g))]
        args = [q[0], k[0], v[0]]
    return pl.pallas_call(
        functools.partial(_attn_kernel, scale=scale, latent=latent),
        grid=(B, H // ATT_GROUP, nq), in_specs=in_specs,
        out_specs=pl.BlockSpec((tq, gw), lambda b, g, i: (b * nq + i, g)),
        out_shape=jax.ShapeDtypeStruct((B * T, H * LANES), F32),
        compiler_params=_cparams(("parallel", "arbitrary", "arbitrary")))(*args)


def _tri_masks(L, rev):
    i = lax.broadcasted_iota(jnp.int32, (L, L), 0)
    j = lax.broadcasted_iota(jnp.int32, (L, L), 1)
    if rev:
        return j >= i, j > i, i == j
    return j <= i, j < i, i == j


def _inv_unit_tri(mats, eye_f, L):
    pows = [[-a for a in mats]]
    for _ in range(int(math.log2(L)) - 1):
        pows.append([_dot_bf(p, p) for p in pows[-1]])
    terms = [[eye_f + p for p in level] for level in pows]
    while len(terms) > 1:
        nxt = [[_dot_bf(x, y) for x, y in zip(terms[i], terms[i + 1])] for i in range(0, len(terms) - 1, 2)]
        if len(terms) % 2:
            nxt.append(terms[-1])
        terms = nxt
    x0 = terms[0]
    resid = [(eye_f - x) - _dot_hp(a, x) for a, x in zip(mats, x0)]
    return [x + _dot_bf(x, r) for x, r in zip(x0, resid)]


def _shifted(f, prev_ref, next_ref, tt, T):
    i = pl.program_id(0)
    first = (i * tt) % T == 0
    last = ((i + 1) * tt) % T == 0
    prev = jnp.where(first, 0.0, prev_ref[7:8, :])
    nxt = jnp.where(last, 0.0, next_ref[0:1, :])
    rows = lax.broadcasted_iota(jnp.int32, f.shape, 0)
    fp = jnp.where(rows == 0, prev, pltpu.roll(f, 1, 0))
    fn = jnp.where(rows == tt - 1, nxt, pltpu.roll(f, tt - 1, 0))
    return fp, fn


def _halo_specs(tt, width, M, col_fn):
    r8 = tt // 8
    last8 = M // 8 - 1
    return [
        pl.BlockSpec((tt, width), lambda *g: (g[0], col_fn(*g))),
        pl.BlockSpec((8, width), lambda *g: (jnp.maximum(g[0] * r8 - 1, 0), col_fn(*g))),
        pl.BlockSpec((8, width), lambda *g: (jnp.minimum((g[0] + 1) * r8, last8), col_fn(*g))),
    ]


NPAIR = B_HEADS // 2


def _pair_sum_matrix():
    r = lax.broadcasted_iota(jnp.int32, (LANES, LANES), 0) // B_HEAD_DIM
    c = lax.broadcasted_iota(jnp.int32, (LANES, LANES), 1) // B_HEAD_DIM
    return (r == c).astype(F32)


def _rwkv_prep_kernel(x_ref, prev_ref, next_ref, mu_ref, w0_ref, w2_ref, a0_ref, a2_ref, g2_ref, kk_ref_in,
                      r_o, k_o, v_o, kk_o, g_o, lw0_o, lw1_o, a0_o, a1_o, *, tt, T):
    f = x_ref[...]
    fp, fn = _shifted(f, prev_ref, next_ref, tt, T)
    f = f + mu_ref[0:1, :] * (fp - f) + mu_ref[1:2, :] * (fn - f)
    r = f[:, :B_W]
    k = f[:, B_W:2 * B_W]
    v = f[:, 2 * B_W:3 * B_W]
    off = 3 * B_W
    w_lo = f[:, off:off + 2 * B_DECAY_LORA]
    off += 2 * B_DECAY_LORA
    a_lo = f[:, off:off + 2 * B_AAA_LORA]
    off += 2 * B_AAA_LORA
    g_lo = f[:, off:]
    th = jnp.tanh(w_lo)
    logw, aa = [], []
    for d in range(2):
        wpre = w0_ref[d:d + 1, :] + _dot_hp(th[:, d * B_DECAY_LORA:(d + 1) * B_DECAY_LORA], w2_ref[d])
        wv = -_softplus(-wpre) - 0.5
        logw.append(-jnp.exp(wv))
        aa.append(_sigmoid(a0_ref[d:d + 1, :] + _dot_hp(a_lo[:, d * B_AAA_LORA:(d + 1) * B_AAA_LORA], a2_ref[d])))
    g = _dot_hp(_sigmoid(g_lo), g2_ref[...])
    kk = k * kk_ref_in[...]
    gsum = _pair_sum_matrix()
    for p in range(NPAIR):
        sl = slice(p * LANES, (p + 1) * LANES)
        kp = kk[:, sl]
        ssq = _dot_hp(kp * kp, gsum)
        kk_o[p] = kp / jnp.maximum(jnp.sqrt(ssq), 1e-12)
        r_o[p] = r[:, sl]
        k_o[p] = k[:, sl]
        v_o[p] = v[:, sl]
        g_o[p] = g[:, sl]
        lw0_o[p] = logw[0][:, sl]
        lw1_o[p] = logw[1][:, sl]
        a0_o[p] = aa[0][:, sl]
        a1_o[p] = aa[1][:, sl]


def _rwkv_prep(pb, B, T, mu, w0, w2, a0, a2, g2, k_k, tt=128):
    M = pb.shape[0]
    tt = min(tt, T)
    nt = T // tt
    full = lambda shape: pl.BlockSpec(shape, lambda i: tuple(0 for _ in shape))
    in_specs = _halo_specs(tt, B_IN, M, lambda i: 0) + [
        full((2, B_IN)), full((2, B_W)), full((2, B_DECAY_LORA, B_W)), full((2, B_W)),
        full((2, B_AAA_LORA, B_W)), full((B_GATE_LORA, B_W)), full((1, B_W))]
    ospec = pl.BlockSpec((None, NPAIR, tt, LANES), lambda i: (i // nt, 0, i % nt, 0))
    oshape = jax.ShapeDtypeStruct((B, NPAIR, T, LANES), F32)
    return pl.pallas_call(
        functools.partial(_rwkv_prep_kernel, tt=tt, T=T),
        grid=(M // tt,), in_specs=in_specs, out_specs=[ospec] * 9, out_shape=[oshape] * 9,
        compiler_params=_cparams(("parallel",)))(pb, pb, pb, mu, w0, w2, a0, a2, g2, k_k.reshape(1, B_W))


def _rwkv_chains(r2, lw2, kd2, v2, kk2, a2, L, rev):
    incl, strict, _ = _tri_masks(L, rev)
    incl_f = incl.astype(F32)
    i2 = lax.broadcasted_iota(jnp.int32, (L, 2 * L), 0)
    j2 = lax.broadcasted_iota(jnp.int32, (L, 2 * L), 1) % L
    incl2 = (j2 >= i2) if rev else (j2 <= i2)
    last = 0 if rev else L - 1
    halves = (slice(0, B_HEAD_DIM), slice(B_HEAD_DIM, 2 * B_HEAD_DIM))
    logp2 = [_dot_hp(incl_f, lw) for lw in lw2]
    chains = []
    for lp, lw, r, kd, v, kk, a in zip(logp2, lw2, r2, kd2, v2, kk2, a2):
        ninv = jnp.exp(-lp)
        b = kk * a
        kt = kk * jnp.exp(lp - lw)
        rt = r * jnp.exp(lp)
        bn, kdn = b * ninv, kd * ninv
        pl_row = lp[last:last + 1, :]
        rem = jnp.exp(pl_row - lp)
        bh, kh = b * rem, kd * rem
        dpl = jnp.exp(pl_row)
        for sl in halves:
            chains.append(dict(
                strict=strict, incl2=incl2, dec=dpl[:, sl], v=v[:, sl],
                lhs=jnp.concatenate([kt[:, sl], rt[:, sl]], axis=0),
                rhs=jnp.concatenate([bn[:, sl], kdn[:, sl]], axis=0),
                rem_rhs=jnp.concatenate([bh[:, sl], kh[:, sl]], axis=0)))
    return chains


def _rwkv_chunk(chains, states, L):
    row = lax.broadcasted_iota(jnp.int32, (L, L), 0)
    col = lax.broadcasted_iota(jnp.int32, (L, L), 1)
    eye_f = (row == col).astype(F32)
    gm = [_dot_hp(ch['lhs'], ch['rhs'], NT) for ch in chains]
    ls = [_dot_hp(ch['lhs'], s, NT) for ch, s in zip(chains, states)]
    akv = [_dot_hp(jnp.where(ch['strict'], g[:L, L:], 0.0), ch['v']) for g, ch in zip(gm, chains)]
    tinv = _inv_unit_tri([jnp.where(ch['strict'], g[:L, :L], 0.0) for g, ch in zip(gm, chains)], eye_f, L)
    u = [-_dot_hp(t, l[:L] + x) for t, l, x in zip(tinv, ls, akv)]
    uv = [jnp.concatenate([x, ch['v']], axis=0) for x, ch in zip(u, chains)]
    y = [l[L:] + _dot_hp(jnp.where(ch['incl2'], g[L:, :], 0.0), x) for l, g, x, ch in zip(ls, gm, uv, chains)]
    s_new = [s * ch['dec'] + _dot_hp(x, ch['rem_rhs'], TN) for s, ch, x in zip(states, chains, uv)]
    return y, s_new


def _rwkv_scan_kernel(rf_ref, vf_ref, kkf_ref, kf_ref, lwf_ref, af_ref, rr_ref, vr_ref, kkr_ref, kr_ref, lwr_ref,
                      ar_ref, ka_ref, s0_ref, yf_ref, yr_ref, sf_ref, s_ref, *, L):
    c = pl.program_id(1)

    @pl.when(c == 0)
    def _():
        s_ref[...] = s0_ref[...]

    pairs = range(NPAIR)
    chains = []
    for rev, (r_ref, v_ref, kk_ref, k_ref, lw_ref, a_ref) in enumerate(
            ((rf_ref, vf_ref, kkf_ref, kf_ref, lwf_ref, af_ref), (rr_ref, vr_ref, kkr_ref, kr_ref, lwr_ref, ar_ref))):
        a2 = [a_ref[p] for p in pairs]
        kd2 = [k_ref[p] * (1.0 + (a - 1.0) * ka_ref[p]) for p, a in zip(pairs, a2)]
        chains += _rwkv_chains([r_ref[p] for p in pairs], [lw_ref[p] for p in pairs], kd2,
                               [v_ref[p] for p in pairs], [kk_ref[p] for p in pairs], a2, L, bool(rev))
    y, s_new = _rwkv_chunk(chains, [s_ref[d, h] for d in range(2) for h in range(B_HEADS)], L)
    for p in pairs:
        yf_ref[p] = jnp.concatenate([y[2 * p], y[2 * p + 1]], axis=1)
        yr_ref[p] = jnp.concatenate([y[B_HEADS + 2 * p], y[B_HEADS + 2 * p + 1]], axis=1)
    for h in range(B_HEADS):
        s_ref[0, h] = s_new[h]
        s_ref[1, h] = s_new[B_HEADS + h]

    @pl.when(c == pl.num_programs(1) - 1)
    def _():
        sf_ref[...] = s_ref[...]


def _rwkv_scan(r, v, kk, k, lws, aas, k_a, s0, L=CHUNK):
    B, _, T, _ = r.shape
    nc = T // L
    fwd = pl.BlockSpec((None, NPAIR, L, LANES), lambda b, c: (b, 0, c, 0))
    bwd = pl.BlockSpec((None, NPAIR, L, LANES), lambda b, c: (b, 0, nc - 1 - c, 0))
    st = pl.BlockSpec((None, 2, B_HEADS, B_HEAD_DIM, B_HEAD_DIM), lambda b, c: (b, 0, 0, 0, 0))
    seq_shape = jax.ShapeDtypeStruct((B, NPAIR, T, LANES), F32)
    return pl.pallas_call(
        functools.partial(_rwkv_scan_kernel, L=L),
        grid=(B, nc),
        in_specs=[fwd] * 6 + [bwd] * 6 + [pl.BlockSpec((NPAIR, 1, LANES), lambda b, c: (0, 0, 0)), st],
        out_specs=[fwd, bwd, st],
        out_shape=[seq_shape, seq_shape, jax.ShapeDtypeStruct((B, 2, B_HEADS, B_HEAD_DIM, B_HEAD_DIM), F32)],
        scratch_shapes=[pltpu.VMEM((2, B_HEADS, B_HEAD_DIM, B_HEAD_DIM), F32)],
        compiler_params=_cparams(("parallel", "arbitrary")))(
            r, v, kk, k, lws[0], aas[0], r, v, kk, k, lws[1], aas[1], k_a.reshape(NPAIR, 1, LANES), s0)


def _rwkv_post_kernel(y0_ref, y1_ref, r_ref, k_ref, v_ref, g_ref, rk_ref, lnw_ref, lnb_ref, o_ref):
    gsum = _pair_sum_matrix()
    inv_n = 1.0 / B_HEAD_DIM
    for p in range(NPAIR):
        y = y0_ref[p] + y1_ref[p]
        mean = _dot_hp(y, gsum) * inv_n
        yc = y - mean
        var = _dot_hp(yc * yc, gsum) * inv_n
        yn = yc * lax.rsqrt(var + B_GN_EPS) * lnw_ref[p] + lnb_ref[p]
        bonus = _dot_hp(r_ref[p] * k_ref[p] * rk_ref[p], gsum) * v_ref[p]
        o_ref[:, p * LANES:(p + 1) * LANES] = (yn + bonus) * g_ref[p]


def _rwkv_post(y0, y1, r, k, v, g, r_k, ln_w, ln_b, tt=256):
    B, _, T, _ = r.shape
    tt = min(tt, T)
    nt = T // tt
    seq = pl.BlockSpec((None, NPAIR, tt, LANES), lambda i: (i // nt, 0, i % nt, 0))
    par = pl.BlockSpec((NPAIR, 1, LANES), lambda i: (0, 0, 0))
    return pl.pallas_call(
        _rwkv_post_kernel, grid=(B * nt,),
        in_specs=[seq] * 6 + [par] * 3,
        out_specs=pl.BlockSpec((tt, B_W), lambda i: (i, 0)),
        out_shape=jax.ShapeDtypeStruct((B * T, B_W), F32),
        compiler_params=_cparams(("parallel",)))(
            y0, y1, r, k, v, g, r_k.reshape(NPAIR, 1, LANES), ln_w.reshape(NPAIR, 1, LANES),
            ln_b.reshape(NPAIR, 1, LANES))


def _rwkv_mixer(pb, B, T, prm, s0):
    r, k, v, kk, g, lw0, lw1, a0, a1 = _rwkv_prep(pb, B, T, prm['mu'], prm['w0'], prm['w2'], prm['a0'],
                                                   prm['a2'], prm['g2'], prm['k_k'])
    if s0 is None:
        s0 = jnp.zeros((B, 2, B_HEADS, B_HEAD_DIM, B_HEAD_DIM), F32)
    y0, y1, s_fin = _rwkv_scan(r, v, kk, k, (lw0, lw1), (a0, a1), prm['k_a'], s0)
    out = _rwkv_post(y0, y1, r, k, v, g, prm['r_k'], prm['ln_w'], prm['ln_b'])
    return out, s_fin


def _gdn_prep_kernel(x_ref, prev_ref, next_ref, cw_ref, o_ref, *, tt, T):
    c = pl.program_id(1)
    x = x_ref[...]
    xp, xn = _shifted(x, prev_ref, next_ref, tt, T)
    y = cw_ref[0:1, :] * xp + cw_ref[1:2, :] * x + cw_ref[2:3, :] * xn
    y = y * _sigmoid(y)

    @pl.when(c == 2)
    def _():
        o_ref[...] = y

    @pl.when(c < 2)
    def _():
        scale = jnp.where(c == 0, C_HEAD_DIM ** -0.5, 1.0).astype(F32)
        for h in range(C_HEADS):
            sl = slice(h * C_HEAD_DIM, (h + 1) * C_HEAD_DIM)
            yh = y[:, sl]
            o_ref[:, sl] = yh * lax.rsqrt(jnp.sum(yh * yh, axis=-1, keepdims=True) + 1e-6) * scale


def _gdn_prep(pc, T, conv_w, tt=256):
    M = pc.shape[0]
    tt = min(tt, T)
    in_specs = _halo_specs(tt, C_W, M, lambda i, c: c) + [pl.BlockSpec((3, C_W), lambda i, c: (0, c))]
    return pl.pallas_call(
        functools.partial(_gdn_prep_kernel, tt=tt, T=T),
        grid=(M // tt, 3), in_specs=in_specs,
        out_specs=pl.BlockSpec((tt, C_W), lambda i, c: (i, c)),
        out_shape=jax.ShapeDtypeStruct((M, 3 * C_W), F32),
        compiler_params=_cparams(("parallel", "arbitrary")))(pc, pc, pc, conv_w)


def _gdn_bg_kernel(x_ref, alog_ref, dtb_ref, o_ref):
    x = x_ref[...]
    lane = lax.broadcasted_iota(jnp.int32, x.shape, 1)
    beta = _sigmoid(x)
    g = -jnp.exp(alog_ref[...]) * _softplus(x + dtb_ref[...])
    o_ref[...] = jnp.where(lane < 2 * C_HEADS, beta, g)


def _gdn_bg(pc, a_log, dt_bias, tt=512):
    M = pc.shape[0]
    tt = min(tt, M)
    nb = 2 * C_HEADS
    pad = lambda p: jnp.concatenate([jnp.zeros((nb,), F32), p.reshape(nb), jnp.zeros((LANES - 2 * nb,), F32)]).reshape(1, LANES)
    return pl.pallas_call(
        _gdn_bg_kernel, grid=(M // tt,),
        in_specs=[pl.BlockSpec((tt, LANES), lambda i: (i, 4 * C_W // LANES)),
                  pl.BlockSpec((1, LANES), lambda i: (0, 0)), pl.BlockSpec((1, LANES), lambda i: (0, 0))],
        out_specs=pl.BlockSpec((tt, LANES), lambda i: (i, 0)),
        out_shape=jax.ShapeDtypeStruct((M, LANES), F32),
        compiler_params=_cparams(("parallel",)))(pc, pad(a_log), pad(dt_bias))


def _gdn_chains(qs, ks, vs, bcol, gcol, grow, L, rev):
    incl, strict, _ = _tri_masks(L, rev)
    i = lax.broadcasted_iota(jnp.int32, (L, L), 0)
    j = lax.broadcasted_iota(jnp.int32, (L, L), 1)
    cum_t = ((i >= j) if rev else (i <= j)).astype(F32)
    gc_cols = _dot_hp(incl.astype(F32), gcol)
    gc_rows = _dot_hp(grow, cum_t)
    last = 0 if rev else L - 1
    chains = []
    for h in range(len(qs)):
        gc = jnp.broadcast_to(gc_cols[:, h:h + 1], (L, LANES))
        beta = bcol[:, h:h + 1]
        kb = ks[h] * beta
        egc = jnp.exp(gc)
        g_end = gc[last:last + 1, :]
        chains.append(dict(
            q=qs[h], k=ks[h], incl=incl, strict=strict, kb=kb, qe=qs[h] * egc,
            decay=jnp.exp(jnp.where(incl, gc[:, :L] - gc_rows[h:h + 1, :], -1e30)),
            rhs=jnp.concatenate([vs[h] * beta, kb * egc], axis=1),
            kend=ks[h] * jnp.exp(g_end - gc), g_end=jnp.exp(g_end)))
    return chains


def _gdn_chunk(chains, states, L):
    r = lax.broadcasted_iota(jnp.int32, (L, L), 0)
    c = lax.broadcasted_iota(jnp.int32, (L, L), 1)
    eye_f = (r == c).astype(F32)
    kk = [_dot_hp(ch['kb'], ch['k'], NT) for ch in chains]
    qk = [_dot_hp(ch['q'], ch['k'], NT) for ch in chains]
    qs_s = [_dot_hp(ch['qe'], s) for ch, s in zip(chains, states)]
    tinv = _inv_unit_tri([jnp.where(ch['strict'], x * ch['decay'], 0.0) for ch, x in zip(chains, kk)], eye_f, L)
    sol = [_dot_hp(t, ch['rhs']) for t, ch in zip(tinv, chains)]
    v_new = [x[:, :C_HEAD_DIM] - _dot_hp(x[:, C_HEAD_DIM:], s) for x, s in zip(sol, states)]
    o = [a + _dot_hp(jnp.where(ch['incl'], x * ch['decay'], 0.0), vn)
         for a, x, ch, vn in zip(qs_s, qk, chains, v_new)]
    s_new = [s * ch['g_end'] + _dot_hp(ch['kend'], vn, TN) for s, ch, vn in zip(states, chains, v_new)]
    return o, s_new


def _gdn_scan_kernel(qf_ref, kf_ref, vf_ref, qr_ref, kr_ref, vr_ref, bf_ref, gf_ref, rf_ref, br_ref, gr_ref, rr_ref,
                     s0_ref, of_ref, or_ref, sf_ref, s_ref, *, L):
    c = pl.program_id(1)

    @pl.when(c == 0)
    def _():
        s_ref[...] = s0_ref[...]

    heads = range(C_HEADS)
    hs = [slice(h * C_HEAD_DIM, (h + 1) * C_HEAD_DIM) for h in heads]
    chains = _gdn_chains([qf_ref[:, sl] for sl in hs], [kf_ref[:, sl] for sl in hs], [vf_ref[:, sl] for sl in hs],
                         bf_ref[...], gf_ref[...], rf_ref[...], L, False)
    chains += _gdn_chains([qr_ref[:, sl] for sl in hs], [kr_ref[:, sl] for sl in hs], [vr_ref[:, sl] for sl in hs],
                          br_ref[...], gr_ref[...], rr_ref[...], L, True)
    o, s_new = _gdn_chunk(chains, [s_ref[d, h] for d in range(2) for h in heads], L)
    for h in heads:
        of_ref[:, hs[h]] = o[h]
        or_ref[:, hs[h]] = o[C_HEADS + h]
        s_ref[0, h] = s_new[h]
        s_ref[1, h] = s_new[C_HEADS + h]

    @pl.when(c == pl.num_programs(1) - 1)
    def _():
        sf_ref[...] = s_ref[...]


def _gdn_scan(qkv, bcols, gcols, grows, s0, B, T, L=CHUNK):
    nc = T // L
    fwd = lambda c: c
    bwd = lambda c: nc - 1 - c
    seq = lambda col, pos: pl.BlockSpec((L, C_W), lambda b, c: (b * nc + pos(c), col))
    small = lambda pos: pl.BlockSpec((L, C_HEADS), lambda b, c: (b * nc + pos(c), 0))
    rows = lambda pos: pl.BlockSpec((None, C_HEADS, L), lambda b, c: (b * nc + pos(c), 0, 0))
    st = pl.BlockSpec((None, 2, C_HEADS, C_HEAD_DIM, C_HEAD_DIM), lambda b, c: (b, 0, 0, 0, 0))
    return pl.pallas_call(
        functools.partial(_gdn_scan_kernel, L=L),
        grid=(B, nc),
        in_specs=[seq(0, fwd), seq(1, fwd), seq(2, fwd), seq(0, bwd), seq(1, bwd), seq(2, bwd),
                  small(fwd), small(fwd), rows(fwd), small(bwd), small(bwd), rows(bwd), st],
        out_specs=[pl.BlockSpec((L, C_W), lambda b, c: (b * nc + c, 0)),
                   pl.BlockSpec((L, C_W), lambda b, c: (b * nc + nc - 1 - c, 0)), st],
        out_shape=[jax.ShapeDtypeStruct((B * T, C_W), F32), jax.ShapeDtypeStruct((B * T, C_W), F32),
                   jax.ShapeDtypeStruct((B, 2, C_HEADS, C_HEAD_DIM, C_HEAD_DIM), F32)],
        scratch_shapes=[pltpu.VMEM((2, C_HEADS, C_HEAD_DIM, C_HEAD_DIM), F32)],
        compiler_params=_cparams(("parallel", "arbitrary")))(
            qkv, qkv, qkv, qkv, qkv, qkv, bcols[0], gcols[0], grows[0], bcols[1], gcols[1], grows[1], s0)


def _gdn_post_kernel(o0_ref, o1_ref, z_ref, g_ref, out_ref):
    for h in range(C_HEADS):
        sl = slice(h * C_HEAD_DIM, (h + 1) * C_HEAD_DIM)
        y = o0_ref[:, sl] + o1_ref[:, sl]
        y = y * lax.rsqrt(jnp.mean(y * y, axis=-1, keepdims=True) + NORM_EPS) * g_ref[...]
        z = z_ref[:, sl]
        out_ref[:, sl] = y * (z * _sigmoid(z))


def _gdn_post(o0, o1, pc, norm_g, tt=256):
    M = o0.shape[0]
    tt = min(tt, M)
    return pl.pallas_call(
        _gdn_post_kernel, grid=(M // tt,),
        in_specs=[pl.BlockSpec((tt, C_W), lambda i: (i, 0)), pl.BlockSpec((tt, C_W), lambda i: (i, 0)),
                  pl.BlockSpec((tt, C_W), lambda i: (i, 3)), pl.BlockSpec((1, C_HEAD_DIM), lambda i: (0, 0))],
        out_specs=pl.BlockSpec((tt, C_W), lambda i: (i, 0)),
        out_shape=jax.ShapeDtypeStruct((M, C_W), F32),
        compiler_params=_cparams(("parallel",)))(o0, o1, pc, norm_g.reshape(1, C_HEAD_DIM))


def _gdn_mixer(pc, B, T, prm, s0, L=CHUNK):
    nc = T // L
    qkv = _gdn_prep(pc, T, prm['conv'])
    bg = _gdn_bg(pc, prm['a_log'], prm['dt_bias'])
    if s0 is None:
        s0 = jnp.zeros((B, 2, C_HEADS, C_HEAD_DIM, C_HEAD_DIM), F32)
    bcols = [bg[:, d * C_HEADS:(d + 1) * C_HEADS] for d in range(2)]
    gcols = [bg[:, (2 + d) * C_HEADS:(3 + d) * C_HEADS] for d in range(2)]
    grows = [g.reshape(B * nc, L, C_HEADS).transpose(0, 2, 1) for g in gcols]
    o_fwd, o_bwd, s_fin = _gdn_scan(qkv, bcols, gcols, grows, s0, B, T)
    return _gdn_post(o_fwd, o_bwd, pc, prm['norm_g']), s_fin


def _top_values(tile, count, need_mask):
    rows = lax.broadcasted_iota(jnp.int32, tile.shape, 0).astype(F32)
    big = float(tile.shape[0])
    cur = tile
    vals = []
    for _ in range(count):
        m = jnp.max(cur, axis=0, keepdims=True)
        vals.append(m)
        idx = jnp.min(jnp.where(cur == m, rows, big), axis=0, keepdims=True)
        cur = jnp.where(rows == idx, -jnp.inf, cur)
    return vals, (cur == -jnp.inf) if need_mask else None


def _top_values_untied(tile, count):
    cur = tile
    vals = []
    for _ in range(count):
        m = jnp.max(cur, axis=0, keepdims=True)
        vals.append(m)
        cur = jnp.where(cur == m, -jnp.inf, cur)
    return vals, cur == -jnp.inf


PEER_EB = 1024
PEER_GRP = MXU_K // PEER_HALF
PEER_NGRP = 2 * PEER_HEADS // PEER_GRP


def _peer_topk_kernel(q_ref, kh_ref, kl_ref, thr_ref, e0_ref, e1_ref, s1_ref, sc_ref):
    rows_g = PEER_GRP * N_KEYS
    for g in range(PEER_NGRP):
        qh, ql = _split(q_ref[:, g * MXU_K:(g + 1) * MXU_K])
        kh = kh_ref[g * rows_g:(g + 1) * rows_g, :]
        kl = kl_ref[g * rows_g:(g + 1) * rows_g, :]
        sc_ref[g * rows_g:(g + 1) * rows_g, :] = _dg(kh, qh, NT) + (_dg(kh, ql, NT) + _dg(kl, qh, NT))

    def route(h, t0, t1, exact):
        if exact:
            a, ra = _top_values(t0, PEER_TOPK, True)
            b, rb = _top_values(t1, PEER_TOPK, True)
        else:
            a, ra = _top_values_untied(t0, PEER_TOPK)
            b, rb = _top_values_untied(t1, PEER_TOPK)
        s0 = jnp.where(ra, t0, -jnp.inf)
        s1 = jnp.where(rb, t1, -jnp.inf)
        amat = jnp.concatenate(a, axis=0)
        bmat = jnp.concatenate(b, axis=0)
        b_lo = bmat[:8]
        row8 = lax.broadcasted_iota(jnp.int32, b_lo.shape, 0)
        pieces = [a[0] + b_lo, a[0] + bmat[8:], a[1] + b_lo]
        n_pad = 0
        for r in range(2, 8):
            pieces.append(jnp.where(row8 < PEER_TOPK // (r + 1), a[r] + b_lo, -jnp.inf))
            n_pad += 8 - PEER_TOPK // (r + 1)
        pieces.append(amat[8:] + b[0])
        cand = jnp.concatenate(pieces, axis=0)
        tied = None
        if exact:
            best, _ = _top_values(cand, PEER_TOPK + 1, False)
        else:
            best, rc = _top_values_untied(cand, PEER_TOPK + 1)
            count = lambda mask: jnp.sum(jnp.where(mask, 1.0, 0.0), axis=0, keepdims=True)
            wrong = jnp.logical_or(jnp.logical_or(count(ra) != PEER_TOPK, count(rb) != PEER_TOPK),
                                   count(rc) != PEER_TOPK + 1 + n_pad)
            tied = jnp.max(jnp.where(wrong, 1.0, 0.0)) > 0.0
        z = jnp.zeros_like(best[0])
        for n in range(PEER_TOPK):
            z = z + jnp.exp(best[n] - best[0])
        tau = 0.5 * (best[PEER_TOPK - 1] + best[PEER_TOPK])
        thr = tau - s0
        e0 = jnp.exp(s0 - a[0]) * (0.5 / z)
        e1 = jnp.exp(s1 - b[0])
        for tc in range(t0.shape[1] // LANES):
            cols = slice(tc * LANES, (tc + 1) * LANES)
            thr_ref[h, tc] = thr[:, cols]
            e0_ref[h, tc] = e0[:, cols]
            e1_ref[h, tc] = e1[:, cols]
            s1_ref[h, tc] = s1[:, cols]
        return tied

    def head(h, carry):
        t0 = sc_ref[pl.ds(pl.multiple_of(h * 2 * N_KEYS, N_KEYS), N_KEYS), :]
        t1 = sc_ref[pl.ds(pl.multiple_of(h * 2 * N_KEYS + N_KEYS, N_KEYS), N_KEYS), :]
        tied = route(h, t0, t1, exact=False)
        pl.when(tied)(lambda: route(h, t0, t1, exact=True))
        return carry

    lax.fori_loop(0, PEER_HEADS, head, 0)


def _peer_topk(q, kdt_hi, kdt_lo, tt=256):
    M = q.shape[0]
    tt = min(tt, M)
    R = PEER_HEADS * 2 * N_KEYS
    big = pl.BlockSpec((PEER_HEADS, tt // LANES, N_KEYS, LANES), lambda i: (0, i, 0, 0))
    return pl.pallas_call(
        _peer_topk_kernel, grid=(M // tt,),
        in_specs=[pl.BlockSpec((tt, PEER_HEADS * PEER_QDIM), lambda i: (i, 0)),
                  pl.BlockSpec((R, MXU_K), lambda i: (0, 0)),
                  pl.BlockSpec((R, MXU_K), lambda i: (0, 0))],
        out_specs=[big] * 4,
        out_shape=[jax.ShapeDtypeStruct((PEER_HEADS, M // LANES, N_KEYS, LANES), F32)] * 4,
        scratch_shapes=[pltpu.VMEM((R, tt), F32)],
        compiler_params=_cparams(("parallel",)))(q, kdt_hi, kdt_lo)


def _peer_expert_kernel(xt_ref, u_ref, vt_ref, thr_ref, e0_ref, e1_ref, s1_ref, res_ref, m_ref, o_ref,
                        acc_ref, act0_ref, act1_ref, am0_ref, am1_ref, *, ni, gate_slot):
    e = pl.program_id(1)
    n_blocks = pl.num_programs(1) - 2
    tt = xt_ref.shape[1]
    even = e % 2 == 0
    odd = e % 2 == 1

    @pl.when(e == 0)
    def _():
        acc_ref[...] = jnp.zeros_like(acc_ref)

    def project(act_w):
        act_w[...] = _dg(u_ref[...], xt_ref[...])

    def weigh(act_r, am_w):
        first = (e - 1) * ni
        for ii in range(ni):
            i = first + ii
            rows = slice(ii * N_KEYS, (ii + 1) * N_KEYS)
            for tc in range(tt // LANES):
                cols = slice(tc * LANES, (tc + 1) * LANES)
                w = None
                for h in range(PEER_HEADS):
                    keep = s1_ref[h, tc] >= thr_ref[h, tc, pl.ds(i, 1), :]
                    contrib = jnp.where(keep, e1_ref[h, tc], 0.0) * e0_ref[h, tc, pl.ds(i, 1), :]
                    w = contrib if w is None else w + contrib
                a = act_r[rows, cols]
                a = a * (1.0 + lax.erf(a * (2.0 ** -0.5)))
                am_w[rows, cols] = (a * w).astype(BF16)

    def accumulate(am_r):
        acc_ref[...] += _dg(vt_ref[...], am_r[...])

    has_project = e < n_blocks
    has_weigh = jnp.logical_and(e >= 1, e <= n_blocks)
    has_accumulate = e >= 2
    pl.when(jnp.logical_and(has_project, even))(lambda: project(act0_ref))
    pl.when(jnp.logical_and(has_project, odd))(lambda: project(act1_ref))
    pl.when(jnp.logical_and(has_weigh, even))(lambda: weigh(act1_ref, am1_ref))
    pl.when(jnp.logical_and(has_weigh, odd))(lambda: weigh(act0_ref, am0_ref))
    pl.when(jnp.logical_and(has_accumulate, even))(lambda: accumulate(am0_ref))
    pl.when(jnp.logical_and(has_accumulate, odd))(lambda: accumulate(am1_ref))

    @pl.when(e == pl.num_programs(1) - 1)
    def _():
        o_ref[...] = res_ref[...] + m_ref[gate_slot:gate_slot + 1, :] * acc_ref[...].T


def _peer_experts(xn, u_bf, vt_bf, thr, e0, e1, s1m, res, mods, gate_slot, rows_per_mod, tt=512):
    M, D = res.shape
    tt = min(tt, M, rows_per_mod) if mods.shape[0] > 1 else min(tt, M)
    nb, _, eb = vt_bf.shape
    ni = eb // N_KEYS
    once = pl.Buffered(1)
    if mods.shape[0] > 1:
        mspec = pl.BlockSpec((None, 6, D), lambda i, e: ((i * tt) // rows_per_mod, 0, 0), pipeline_mode=once)
    else:
        mspec = pl.BlockSpec((None, 6, D), lambda i, e: (0, 0, 0), pipeline_mode=once)
    big = pl.BlockSpec((PEER_HEADS, tt // LANES, N_KEYS, LANES), lambda i, e: (0, i, 0, 0), pipeline_mode=once)
    return pl.pallas_call(
        functools.partial(_peer_expert_kernel, ni=ni, gate_slot=gate_slot),
        grid=(M // tt, nb + 2),
        in_specs=[pl.BlockSpec((D, tt), lambda i, e: (0, i), pipeline_mode=once),
                  pl.BlockSpec((eb, D), lambda i, e: (jnp.minimum(e, nb - 1), 0)),
                  pl.BlockSpec((None, D, eb), lambda i, e: (jnp.clip(e - 2, 0, nb - 1), 0, 0)),
                  big, big, big, big,
                  pl.BlockSpec((tt, D), lambda i, e: (i, 0), pipeline_mode=once), mspec],
        out_specs=pl.BlockSpec((tt, D), lambda i, e: (i, 0)),
        out_shape=jax.ShapeDtypeStruct((M, D), F32),
        scratch_shapes=[pltpu.VMEM((D, tt), F32), pltpu.VMEM((eb, tt), F32), pltpu.VMEM((eb, tt), F32),
                        pltpu.VMEM((eb, tt), BF16), pltpu.VMEM((eb, tt), BF16)],
        compiler_params=_cparams(("parallel", "arbitrary"), VMEM_LIMIT_BIG))(
            xn, u_bf, vt_bf, thr, e0, e1, s1m, res, mods)


def _peer(x, mods, rows_per_mod, norm_g, prm):
    q, xn = _mm_norm(x, prm['wq'], norm_g=norm_g, mods=mods, slots=(3, 4), rows_per_mod=rows_per_mod,
                     hp=True, emit_xn=BF16)
    thr, e0, e1, s1m = _peer_topk(q, prm['kdt_hi'], prm['kdt_lo'])
    return _peer_experts(xn.T, prm['u'], prm['vt'], thr, e0, e1, s1m, x, mods, 5, rows_per_mod)


def _peer_params(wq, keys, u_tab, v_tab):
    nset = 2 * PEER_HEADS
    kd = jnp.zeros((nset, N_KEYS, PEER_GRP, PEER_HALF), F32)
    sets = jnp.arange(nset)
    kd = kd.at[sets, :, sets % PEER_GRP, :].set(keys.reshape(nset, N_KEYS, PEER_HALF))
    kd_hi, kd_lo = _split(kd.reshape(nset * N_KEYS, MXU_K))
    vt = v_tab.astype(BF16).reshape(N_EXPERTS // PEER_EB, PEER_EB, -1).transpose(0, 2, 1)
    return dict(wq=_split(wq), kdt_hi=kd_hi, kdt_lo=kd_lo, u=u_tab.astype(BF16), vt=vt)


def _layer_even(x, mods, B, T, norm1, w_a, w_b, w_out, qk_gains, rwkv_prm, ctx):
    rpm = T
    pa = _mm_norm(x, w_a, norm_g=norm1, mods=mods, slots=(0, 1), rows_per_mod=rpm)
    pb = _mm_norm(x, w_b, norm_g=norm1, mods=mods, slots=(0, 1), rows_per_mod=rpm)
    nqk = A_HEADS + A_KV_HEADS
    assert A_HEADS // A_KV_HEADS == ATT_GROUP
    scale = A_HEAD_DIM ** -0.5
    if ctx is None:
        qk = _norm_rope(pa, nqk, T, gains=qk_gains)
        oa = _attention(B, T, T, A_HEADS, scale, (qk, 0), (qk, A_HEADS), (pa, nqk))
        ctx_out = (qk[:, A_Q:A_Q + A_KV], pa[:, A_Q + A_KV:])
        s0 = None
    else:
        cache_k, cache_v, s0 = ctx
        P = cache_k.shape[1]
        qk = _norm_rope(pa, nqk, T, gains=qk_gains, tables=_rope_tables(T, A_HEAD_DIM))
        k_all = jnp.concatenate([qk[:, A_Q:].reshape(B, T, A_KV), cache_k.reshape(B, P, A_KV)], axis=1)
        v_all = jnp.concatenate([pa[:, A_Q + A_KV:].reshape(B, T, A_KV), cache_v.reshape(B, P, A_KV)], axis=1)
        Tk = T + P
        oa = _attention(B, T, Tk, A_HEADS, scale, (qk, 0), (k_all.reshape(B * Tk, A_KV).astype(BF16), 0),
                        (v_all.reshape(B * Tk, A_KV).astype(BF16), 0))
        ctx_out = None
    ob, s_fin = _rwkv_mixer(pb, B, T, rwkv_prm, s0)
    x = _mm_res([oa, ob], w_out, x, mods, 2, rpm)
    return x, ctx_out, s_fin


D_QROPE_BLK = 0
D_KROPE_BLK = D_HEADS
D_NROPE_BLK = D_HEADS + 1
D_CKV_BLK = 12
D_QNOPE_BLK = 16
D_PROJ = 24 * LANES


def _mla_in_weight(w_d):
    K = w_d.shape[0]
    wq = w_d[:, :D_Q].reshape(K, D_HEADS, D_QK)
    z64 = jnp.zeros((K, D_HEADS, LANES - D_ROPE), w_d.dtype)
    q_rope = jnp.concatenate([wq[:, :, D_NOPE:], z64], axis=-1).reshape(K, D_HEADS * LANES)
    q_nope = wq[:, :, :D_NOPE].reshape(K, D_HEADS * LANES)
    k_rope = jnp.concatenate([w_d[:, D_Q + D_KV_RANK:], jnp.zeros((K, LANES - D_ROPE), w_d.dtype)], axis=-1)
    padz = jnp.zeros((K, (D_CKV_BLK - D_NROPE_BLK) * LANES), w_d.dtype)
    return jnp.concatenate([q_rope, k_rope, padz, w_d[:, D_Q:D_Q + D_KV_RANK], q_nope], axis=-1)


def _layer_odd(x, mods, B, T, norm1, w_c, w_d, w_out, gdn_prm, kv_norm_g, w_kv_b, ctx):
    rpm = T
    pc = _mm_norm(x, w_c, norm_g=norm1, mods=mods, slots=(0, 1), rows_per_mod=rpm)
    pd = _mm_norm(x, w_d, norm_g=norm1, mods=mods, slots=(0, 1), rows_per_mod=rpm)
    scale = D_QK ** -0.5
    ckv_col = D_CKV_BLK * LANES // D_KV_RANK
    kv, ckv_n = _mm_norm(pd, w_kv_b, k=D_KV_RANK, xcol=ckv_col, norm_g=kv_norm_g, emit_xn=F32)
    if ctx is None:
        od = _attention(B, T, T, D_HEADS, scale, (pd, D_QNOPE_BLK), (kv, 0), (kv, 0),
                        q2=(pd, D_QROPE_BLK), k2=(pd, D_KROPE_BLK))
        ctx_out = (ckv_n, pd[:, D_KROPE_BLK * LANES:D_KROPE_BLK * LANES + D_ROPE])
        s0 = None
    else:
        s0, c_ckv, c_krope = ctx
        P = c_ckv.shape[1]
        Tk = T + P
        roped = _norm_rope(pd, D_NROPE_BLK, T, tables=_rope_tables(T, D_ROPE))
        kv_ctx = _mm_norm(c_ckv.reshape(B * P, D_KV_RANK), w_kv_b)
        kv_all = jnp.concatenate([kv.reshape(B, T, -1), kv_ctx.reshape(B, P, -1)], axis=1).reshape(B * Tk, -1)
        kv_all = kv_all.astype(BF16)
        kr_lat = roped[:, D_KROPE_BLK * LANES:].reshape(B, T, LANES)
        kr_ctx = jnp.concatenate([c_krope, jnp.zeros((B, P, LANES - D_ROPE), F32)], axis=-1)
        kr_all = jnp.concatenate([kr_lat, kr_ctx], axis=1).reshape(B * Tk, LANES).astype(BF16)
        od = _attention(B, T, Tk, D_HEADS, scale, (pd, D_QNOPE_BLK), (kv_all, 0), (kv_all, 0),
                        q2=(roped, D_QROPE_BLK), k2=(kr_all, 0))
        ctx_out = None
    oc, s_fin = _gdn_mixer(pc, B, T, gdn_prm, s0)
    x = _mm_res([oc, od], w_out, x, mods, 2, rpm)
    return x, ctx_out, s_fin


def _modulation(cond, w_ada, b_ada):
    n = cond.shape[0]
    rows = ((n + 7) // 8) * 8
    a = cond * _sigmoid(cond)
    a = jnp.concatenate([a, jnp.zeros((rows - n, cond.shape[1]), F32)], axis=0)
    m = _mm_norm(a, _split(w_ada), hp=True, tn_cap=1024)[:n] + b_ada
    return m.reshape(n, 6, D_MODEL)


def kernel(x_prompt, x_sample, cache_attn_k, cache_attn_v, state_rwkv, state_gdn, cache_mla_ckv, cache_mla_krope,
           c, c_ctx, norm1_g, norm2_g, ada_w, ada_b, w_in_ab, w_out_ab, attn_q_norm, attn_k_norm,
           rwkv_mu, rwkv_w0, rwkv_w2, rwkv_a0, rwkv_a2, rwkv_g2, rwkv_k_k, rwkv_k_a, rwkv_r_k, rwkv_ln_w, rwkv_ln_b,
           w_in_cd, w_out_cd, gdn_conv, gdn_a_log, gdn_dt_bias, gdn_norm_g, mla_kv_norm_g, mla_w_kv_b,
           peer_wq, peer_keys, peer_u, peer_v, final_norm_g):
    Bp, Tp, D = x_prompt.shape
    Bs, Ts, _ = x_sample.shape
    depth = norm1_g.shape[0]
    xp = x_prompt.reshape(Bp * Tp, D)
    xs = x_sample.reshape(Bs * Ts, D)
    cond = jnp.concatenate([c_ctx[None, :], c], axis=0)
    outs = dict(k=[], v=[], rwkv=[], gdn=[], ckv=[], krope=[])
    for li in range(depth):
        j = li // 2
        mods = _modulation(cond, ada_w[li], ada_b[li])
        mods_p, mods_s = mods[:1], mods[1:]
        peer_prm = _peer_params(peer_wq[li], peer_keys[li], peer_u[li], peer_v[li])
        if li % 2 == 0:
            w_in = w_in_ab[j].astype(BF16)
            w_a, w_b = w_in[:, :A_IN], w_in[:, A_IN:]
            w_o = w_out_ab[j].astype(BF16)
            w_out = [w_o[:A_Q], w_o[A_Q:]]
            gains = jnp.concatenate([jnp.tile(attn_q_norm[j], A_HEADS), jnp.tile(attn_k_norm[j], A_KV_HEADS)])
            rp = dict(mu=rwkv_mu[j], w0=rwkv_w0[j], w2=rwkv_w2[j], a0=rwkv_a0[j], a2=rwkv_a2[j], g2=rwkv_g2[j],
                      k_k=rwkv_k_k[j], k_a=rwkv_k_a[j], r_k=rwkv_r_k[j], ln_w=rwkv_ln_w[j], ln_b=rwkv_ln_b[j])
            xp, (k_c, v_c), s_c = _layer_even(xp, mods_p, Bp, Tp, norm1_g[li], w_a, w_b, w_out, gains, rp, None)
            xs, _, _ = _layer_even(xs, mods_s, Bs, Ts, norm1_g[li], w_a, w_b, w_out, gains, rp,
                                   (cache_attn_k[:, j], cache_attn_v[:, j], state_rwkv[:, j]))
            outs['k'].append(k_c.reshape(Bp, Tp, A_KV_HEADS, A_HEAD_DIM))
            outs['v'].append(v_c.reshape(Bp, Tp, A_KV_HEADS, A_HEAD_DIM))
            outs['rwkv'].append(s_c)
        else:
            w_in = w_in_cd[j]
            w_c = jnp.concatenate([w_in[:, :C_IN], jnp.zeros((D, LANES - 4 * C_HEADS), F32)], axis=-1).astype(BF16)
            w_d = _mla_in_weight(w_in[:, C_IN:]).astype(BF16)
            w_o = w_out_cd[j].astype(BF16)
            w_out = [w_o[:C_W], w_o[C_W:]]
            gp = dict(conv=gdn_conv[j], a_log=gdn_a_log[j], dt_bias=gdn_dt_bias[j], norm_g=gdn_norm_g[j])
            w_kv_b = mla_w_kv_b[j].astype(BF16)
            xp, (ckv_c, kr_c), s_c = _layer_odd(xp, mods_p, Bp, Tp, norm1_g[li], w_c, w_d, w_out, gp,
                                                mla_kv_norm_g[j], w_kv_b, None)
            xs, _, _ = _layer_odd(xs, mods_s, Bs, Ts, norm1_g[li], w_c, w_d, w_out, gp, mla_kv_norm_g[j], w_kv_b,
                                  (state_gdn[:, j], cache_mla_ckv[:, j], cache_mla_krope[:, j]))
            outs['gdn'].append(s_c)
            outs['ckv'].append(ckv_c.reshape(Bp, Tp, D_KV_RANK))
            outs['krope'].append(kr_c.reshape(Bp, Tp, D_ROPE))
        xp = _peer(xp, mods_p, Tp, norm2_g[li], peer_prm)
        xs = _peer(xs, mods_s, Ts, norm2_g[li], peer_prm)
    y_prompt = _rmsnorm(xp, final_norm_g).reshape(Bp, Tp, D)
    y_sample = _rmsnorm(xs, final_norm_g).reshape(Bs, Ts, D)
    st = lambda name: jnp.stack(outs[name], axis=1)
    return (y_prompt, y_sample, st('k'), st('v'), st('rwkv'), st('gdn'), st('ckv'), st('krope'))
```

```python
import functools
import math

import numpy as np
import jax
import jax.numpy as jnp
from jax import lax
from jax.experimental import pallas as pl
from jax.experimental.pallas import tpu as pltpu

F32 = jnp.float32
BF16 = jnp.bfloat16

D_MODEL = 2048
GRID_W = 64
ROPE_THETA = 10000.0
NORM_EPS = 1e-6
A_HEADS, A_KV_HEADS, A_HEAD_DIM = 8, 2, 128
A_Q = A_HEADS * A_HEAD_DIM
A_KV = A_KV_HEADS * A_HEAD_DIM
A_IN = A_Q + 2 * A_KV
B_HEADS, B_HEAD_DIM = 16, 64
B_W = B_HEADS * B_HEAD_DIM
B_DECAY_LORA, B_AAA_LORA, B_GATE_LORA = 64, 64, 128
B_IN = 3 * B_W + 2 * B_DECAY_LORA + 2 * B_AAA_LORA + B_GATE_LORA
B_GN_EPS = 64e-5
C_HEADS, C_HEAD_DIM = 8, 128
C_W = C_HEADS * C_HEAD_DIM
C_IN = 4 * C_W + 4 * C_HEADS
D_HEADS, D_NOPE, D_ROPE, D_V = 8, 128, 64, 128
D_QK = D_NOPE + D_ROPE
D_KV_RANK = 512
D_Q = D_HEADS * D_QK
PEER_HEADS, N_KEYS, PEER_QDIM, PEER_TOPK = 8, 128, 128, 16
PEER_HALF = PEER_QDIM // 2
N_EXPERTS = N_KEYS * N_KEYS

LANES = 128
MXU_K = 256
VMEM_LIMIT = 48 * 1024 * 1024
VMEM_LIMIT_BIG = 56 * 1024 * 1024

CHUNK = 64

NN = ((1,), (0,))
NT = ((1,), (1,))
TN = ((0,), (0,))


def _dg(a, b, dims=NN):
    return lax.dot_general(a, b, (dims, ((), ())), preferred_element_type=F32)


def _split(x):
    hi = x.astype(BF16)
    lo = (x - hi.astype(F32)).astype(BF16)
    return hi, lo


def _dot_hp(a, b, dims=NN):
    ah, al = _split(a)
    bh, bl = _split(b)
    return _dg(ah, bh, dims) + (_dg(ah, bl, dims) + _dg(al, bh, dims))


def _dot_bf(a, b, dims=NN):
    return _dg(a.astype(BF16), b.astype(BF16), dims)


def _sigmoid(x):
    return 1.0 / (1.0 + jnp.exp(-x))


def _softplus(x):
    return jnp.maximum(x, 0.0) + jnp.log(1.0 + jnp.exp(-jnp.abs(x)))


def _cparams(sem, vmem=VMEM_LIMIT):
    return pltpu.CompilerParams(dimension_semantics=sem, vmem_limit_bytes=vmem)


def _pick_tile(n, cap):
    best = None
    t = LANES
    while t <= min(n, cap):
        if n % t == 0:
            best = t
        t += LANES
    return best if best is not None else n


def _mm_norm_kernel(*refs, has_norm, has_mod, shift_slot, scale_slot, hp, emit_xn):
    it = iter(refs)
    x_ref = next(it)
    g_ref = next(it) if has_norm else None
    m_ref = next(it) if has_mod else None
    wh_ref = next(it)
    wl_ref = next(it) if hp else None
    o_ref = next(it)
    xn_ref = next(it) if emit_xn else None
    xh_ref = next(it)
    xl_ref = next(it) if hp else None

    @pl.when(pl.program_id(1) == 0)
    def _():
        xv = x_ref[...].astype(F32)
        if has_norm:
            xv = xv * lax.rsqrt(jnp.mean(xv * xv, axis=-1, keepdims=True) + NORM_EPS) * g_ref[...]
        if has_mod:
            xv = xv * (1.0 + m_ref[scale_slot:scale_slot + 1, :]) + m_ref[shift_slot:shift_slot + 1, :]
        hi = xv.astype(BF16)
        xh_ref[...] = hi
        if hp:
            xl_ref[...] = (xv - hi.astype(F32)).astype(BF16)
        if emit_xn:
            xn_ref[...] = xv.astype(xn_ref.dtype)

    acc = _dg(xh_ref[...], wh_ref[...])
    if hp:
        acc = acc + (_dg(xh_ref[...], wl_ref[...]) + _dg(xl_ref[...], wh_ref[...]))
    o_ref[...] = acc


def _mm_norm(x, w, *, k=None, xcol=0, norm_g=None, mods=None, slots=(0, 1), rows_per_mod=None,
             hp=False, emit_xn=None, tm=512, tn_cap=1664):
    M = x.shape[0]
    K = k if k is not None else x.shape[1]
    ws = tuple(w) if isinstance(w, (tuple, list)) else (w,)
    N = ws[0].shape[1]
    tm = min(tm, M)
    if mods is not None and mods.shape[0] > 1:
        tm = min(tm, rows_per_mod)
    tn = _pick_tile(N, tn_cap)
    has_norm = norm_g is not None
    has_mod = mods is not None
    rpm = rows_per_mod if rows_per_mod is not None else M

    in_specs = [pl.BlockSpec((tm, K), lambda i, j: (i, xcol))]
    args = [x]
    if has_norm:
        in_specs.append(pl.BlockSpec((1, K), lambda i, j: (0, 0)))
        args.append(norm_g.reshape(1, K).astype(F32))
    if has_mod:
        if mods.shape[0] > 1:
            in_specs.append(pl.BlockSpec((None, 6, K), lambda i, j: ((i * tm) // rpm, 0, 0)))
        else:
            in_specs.append(pl.BlockSpec((None, 6, K), lambda i, j: (0, 0, 0)))
        args.append(mods)
    for wi in ws:
        in_specs.append(pl.BlockSpec((K, tn), lambda i, j: (0, j)))
        args.append(wi)
    out_shape = [jax.ShapeDtypeStruct((M, N), F32)]
    out_specs = [pl.BlockSpec((tm, tn), lambda i, j: (i, j))]
    if emit_xn is not None:
        out_shape.append(jax.ShapeDtypeStruct((M, K), emit_xn))
        out_specs.append(pl.BlockSpec((tm, K), lambda i, j: (i, 0)))
    scratch = [pltpu.VMEM((tm, K), BF16)]
    if hp:
        scratch.append(pltpu.VMEM((tm, K), BF16))
    kern = functools.partial(_mm_norm_kernel, has_norm=has_norm, has_mod=has_mod, shift_slot=slots[0],
                             scale_slot=slots[1], hp=hp, emit_xn=emit_xn is not None)
    outs = pl.pallas_call(
        kern, grid=(M // tm, N // tn), in_specs=in_specs, out_specs=out_specs, out_shape=out_shape,
        scratch_shapes=scratch, compiler_params=_cparams(("parallel", "arbitrary")))(*args)
    return outs if emit_xn is not None else outs[0]


def _mm_res_kernel(*refs, n_in, gate_slot):
    a_refs = refs[:n_in]
    w_refs = refs[n_in:2 * n_in]
    res_ref, m_ref, o_ref = refs[2 * n_in:]
    acc = None
    for a_ref, w_ref in zip(a_refs, w_refs):
        part = _dg(a_ref[...].astype(BF16), w_ref[...])
        acc = part if acc is None else acc + part
    o_ref[...] = res_ref[...] + m_ref[gate_slot:gate_slot + 1, :] * acc


def _mm_res(a_list, w_list, res, mods, gate_slot, rows_per_mod, tm=512, tn=1024):
    M, N = res.shape
    tm = min(tm, M, rows_per_mod) if mods.shape[0] > 1 else min(tm, M)
    tn = min(tn, N)
    n_in = len(a_list)
    in_specs, args = [], []
    for a in a_list:
        in_specs.append(pl.BlockSpec((tm, a.shape[1]), lambda i, j: (i, 0)))
        args.append(a)
    for w in w_list:
        in_specs.append(pl.BlockSpec((w.shape[0], tn), lambda i, j: (0, j)))
        args.append(w)
    in_specs.append(pl.BlockSpec((tm, tn), lambda i, j: (i, j)))
    args.append(res)
    if mods.shape[0] > 1:
        in_specs.append(pl.BlockSpec((None, 6, tn), lambda i, j: ((i * tm) // rows_per_mod, 0, j)))
    else:
        in_specs.append(pl.BlockSpec((None, 6, tn), lambda i, j: (0, 0, j)))
    args.append(mods)
    return pl.pallas_call(
        functools.partial(_mm_res_kernel, n_in=n_in, gate_slot=gate_slot),
        grid=(M // tm, N // tn), in_specs=in_specs,
        out_specs=pl.BlockSpec((tm, tn), lambda i, j: (i, j)),
        out_shape=jax.ShapeDtypeStruct((M, N), F32),
        compiler_params=_cparams(("parallel", "arbitrary")))(*args)


def _rmsnorm_kernel(x_ref, g_ref, o_ref):
    xv = x_ref[...]
    o_ref[...] = xv * lax.rsqrt(jnp.mean(xv * xv, axis=-1, keepdims=True) + NORM_EPS) * g_ref[...]


def _rmsnorm(x, g, tm=512):
    M, K = x.shape
    tm = min(tm, M)
    return pl.pallas_call(
        _rmsnorm_kernel, grid=(M // tm,),
        in_specs=[pl.BlockSpec((tm, K), lambda i: (i, 0)), pl.BlockSpec((1, K), lambda i: (0, 0))],
        out_specs=pl.BlockSpec((tm, K), lambda i: (i, 0)),
        out_shape=jax.ShapeDtypeStruct((M, K), F32),
        compiler_params=_cparams(("parallel",)))(x, g.reshape(1, K))


def _rope_tables(T, rot_dim):
    rows = T // GRID_W
    row = jnp.repeat(jnp.arange(rows, dtype=F32), GRID_W)
    col = jnp.tile(jnp.arange(GRID_W, dtype=F32), rows)
    n_freq = rot_dim // 4
    inv = ROPE_THETA ** (-jnp.arange(n_freq, dtype=F32) / n_freq)
    ang = jnp.concatenate([row[:, None] * inv, col[:, None] * inv], axis=-1)
    cos = jnp.repeat(jnp.cos(ang), 2, axis=-1)
    sin = jnp.repeat(jnp.sin(ang), 2, axis=-1) * jnp.tile(jnp.array([-1.0, 1.0], F32), rot_dim // 2)
    pad = LANES - rot_dim
    if pad:
        cos = jnp.concatenate([cos, jnp.ones((T, pad), F32)], axis=-1)
        sin = jnp.concatenate([sin, jnp.zeros((T, pad), F32)], axis=-1)
    return cos, sin


def _rope(y, cosf, sinf):
    lane = lax.broadcasted_iota(jnp.int32, y.shape, 1)
    even = (lane & 1) == 0
    width = y.shape[1]
    swap = jnp.where(even, pltpu.roll(y, width - 1, 1), pltpu.roll(y, 1, 1))
    return y * cosf + swap * sinf


def _norm_rope_kernel(*refs, nblk, norm, rope):
    it = iter(refs)
    x_ref = next(it)
    g_ref = next(it) if norm else None
    cos_ref = next(it) if rope else None
    sin_ref = next(it) if rope else None
    o_ref = next(it)
    for c in range(nblk):
        sl = slice(c * LANES, (c + 1) * LANES)
        y = x_ref[:, sl]
        if norm:
            y = y * lax.rsqrt(jnp.mean(y * y, axis=-1, keepdims=True) + NORM_EPS) * g_ref[:, sl]
        if rope:
            y = _rope(y, cos_ref[...], sin_ref[...])
        o_ref[:, sl] = y


def _norm_rope(x, nblk, T, gains=None, tables=None, tt=256):
    M = x.shape[0]
    tt = min(tt, T)
    W = nblk * LANES
    in_specs = [pl.BlockSpec((tt, W), lambda i: (i, 0))]
    args = [x]
    if gains is not None:
        in_specs.append(pl.BlockSpec((1, W), lambda i: (0, 0)))
        args.append(gains.reshape(1, W))
    if tables is not None:
        nt = T // tt
        for t in tables:
            in_specs.append(pl.BlockSpec((tt, LANES), lambda i: (i % nt, 0)))
            args.append(t)
    return pl.pallas_call(
        functools.partial(_norm_rope_kernel, nblk=nblk, norm=gains is not None, rope=tables is not None),
        grid=(M // tt,), in_specs=in_specs,
        out_specs=pl.BlockSpec((tt, W), lambda i: (i, 0)),
        out_shape=jax.ShapeDtypeStruct((M, W), F32),
        compiler_params=_cparams(("parallel",)))(*args)


ATT_GROUP = 4


def _attn_kernel(*refs, scale, latent):
    heads = range(ATT_GROUP)
    lane = lambda j: slice(j * LANES, (j + 1) * LANES)
    if latent:
        q_ref, q2_ref, kv_ref, k2_ref, o_ref = refs
        k2 = k2_ref[...]
        qs = [jnp.concatenate([q_ref[:, lane(j)], q2_ref[:, lane(j)]], axis=1) * scale for j in heads]
        ks = [jnp.concatenate([kv_ref[:, lane(2 * j)], k2], axis=1) for j in heads]
        vs = [kv_ref[:, lane(2 * j + 1)] for j in heads]
    else:
        q_ref, k_ref, v_ref, o_ref = refs
        qs = [q_ref[:, lane(j)] * scale for j in heads]
        ks = [k_ref[...]] * ATT_GROUP
        vs = [v_ref[...]] * ATT_GROUP
    s = [_dot_bf(q, k, NT) for q, k in zip(qs, ks)]
    p = [jnp.exp(x - jnp.max(x, axis=-1, keepdims=True)) for x in s]
    l = [jnp.sum(x, axis=-1, keepdims=True) for x in p]
    o = [_dot_bf(x, v) / n for x, v, n in zip(p, vs, l)]
    for j in heads:
        o_ref[:, lane(j)] = o[j].astype(o_ref.dtype)


def _attention(B, T, Tk, H, scale, q, k, v, q2=None, k2=None, tq=512):
    tq = min(tq, T)
    nq = T // tq
    latent = q2 is not None
    gw = ATT_GROUP * LANES

    def qspec(first):
        assert first % ATT_GROUP == 0
        return pl.BlockSpec((tq, gw), lambda b, g, i: (b * nq + i, first // ATT_GROUP + g))

    if latent:
        assert k[0] is v[0] and k[1] % (2 * ATT_GROUP) == 0
        in_specs = [qspec(q[1]), qspec(q2[1]),
                    pl.BlockSpec((Tk, 2 * gw), lambda b, g, i: (b, k[1] // (2 * ATT_GROUP) + g)),
                    pl.BlockSpec((Tk, LANES), lambda b, g, i: (b, k2[1]))]
        args = [q[0], q2[0], k[0], k2[0]]
    else:
        in_specs = [qspec(q[1]), pl.BlockSpec((Tk, LANES), lambda b, g, i: (b, k[1] + g)),
                    pl.BlockSpec((Tk, LANES), lambda b, g, i: (b, v[1] + g))]
        args = [q[0], k[0], v[0]]
    return pl.pallas_call(
        functools.partial(_attn_kernel, scale=scale, latent=latent),
        grid=(B, H // ATT_GROUP, nq), in_specs=in_specs,
        out_specs=pl.BlockSpec((tq, gw), lambda b, g, i: (b * nq + i, g)),
        out_shape=jax.ShapeDtypeStruct((B * T, H * LANES), BF16),
        compiler_params=_cparams(("parallel", "arbitrary", "arbitrary")))(*args)


def _tri_masks(L, rev):
    i = lax.broadcasted_iota(jnp.int32, (L, L), 0)
    j = lax.broadcasted_iota(jnp.int32, (L, L), 1)
    if rev:
        return j >= i, j > i, i == j
    return j <= i, j < i, i == j


def _inv_unit_tri(mats, eye_f, L):
    pows = [[-a for a in mats]]
    for _ in range(int(math.log2(L)) - 1):
        pows.append([_dot_bf(p, p) for p in pows[-1]])
    terms = [[eye_f + p for p in level] for level in pows]
    while len(terms) > 1:
        nxt = [[_dot_bf(x, y) for x, y in zip(terms[i], terms[i + 1])] for i in range(0, len(terms) - 1, 2)]
        if len(terms) % 2:
            nxt.append(terms[-1])
        terms = nxt
    x0 = terms[0]
    resid = [(eye_f - x) - _dot_hp(a, x) for a, x in zip(mats, x0)]
    return [x + _dot_bf(x, r) for x, r in zip(x0, resid)]


def _shifted(f, prev_ref, next_ref, tt, T):
    i = pl.program_id(0)
    first = (i * tt) % T == 0
    last = ((i + 1) * tt) % T == 0
    prev = jnp.where(first, 0.0, prev_ref[7:8, :])
    nxt = jnp.where(last, 0.0, next_ref[0:1, :])
    rows = lax.broadcasted_iota(jnp.int32, f.shape, 0)
    fp = jnp.where(rows == 0, prev, pltpu.roll(f, 1, 0))
    fn = jnp.where(rows == tt - 1, nxt, pltpu.roll(f, tt - 1, 0))
    return fp, fn


def _halo_specs(tt, width, M, col_fn):
    r8 = tt // 8
    last8 = M // 8 - 1
    return [
        pl.BlockSpec((tt, width), lambda *g: (g[0], col_fn(*g))),
        pl.BlockSpec((8, width), lambda *g: (jnp.maximum(g[0] * r8 - 1, 0), col_fn(*g))),
        pl.BlockSpec((8, width), lambda *g: (jnp.minimum((g[0] + 1) * r8, last8), col_fn(*g))),
    ]


NPAIR = B_HEADS // 2


def _pair_sum_matrix():
    r = lax.broadcasted_iota(jnp.int32, (LANES, LANES), 0) // B_HEAD_DIM
    c = lax.broadcasted_iota(jnp.int32, (LANES, LANES), 1) // B_HEAD_DIM
    return (r == c).astype(F32)


def _rwkv_prep_kernel(x_ref, prev_ref, next_ref, mu_ref, w0_ref, w2_ref, a0_ref, a2_ref, g2_ref, kk_ref_in,
                      r_o, k_o, v_o, kk_o, g_o, lw0_o, lw1_o, a0_o, a1_o, *, tt, T):
    f = x_ref[...]
    fp, fn = _shifted(f, prev_ref, next_ref, tt, T)
    f = f + mu_ref[0:1, :] * (fp - f) + mu_ref[1:2, :] * (fn - f)
    r = f[:, :B_W]
    k = f[:, B_W:2 * B_W]
    v = f[:, 2 * B_W:3 * B_W]
    off = 3 * B_W
    w_lo = f[:, off:off + 2 * B_DECAY_LORA]
    off += 2 * B_DECAY_LORA
    a_lo = f[:, off:off + 2 * B_AAA_LORA]
    off += 2 * B_AAA_LORA
    g_lo = f[:, off:]
    th = jnp.tanh(w_lo)
    logw, aa = [], []
    for d in range(2):
        wpre = w0_ref[d:d + 1, :] + _dot_hp(th[:, d * B_DECAY_LORA:(d + 1) * B_DECAY_LORA], w2_ref[d])
        wv = -_softplus(-wpre) - 0.5
        logw.append(-jnp.exp(wv))
        aa.append(_sigmoid(a0_ref[d:d + 1, :] + _dot_hp(a_lo[:, d * B_AAA_LORA:(d + 1) * B_AAA_LORA], a2_ref[d])))
    g = _dot_hp(_sigmoid(g_lo), g2_ref[...])
    kk = k * kk_ref_in[...]
    gsum = _pair_sum_matrix()
    for p in range(NPAIR):
        sl = slice(p * LANES, (p + 1) * LANES)
        kp = kk[:, sl]
        ssq = _dot_hp(kp * kp, gsum)
        kk_o[p] = kp / jnp.maximum(jnp.sqrt(ssq), 1e-12)
        r_o[p] = r[:, sl]
        k_o[p] = k[:, sl]
        v_o[p] = v[:, sl]
        g_o[p] = g[:, sl]
        lw0_o[p] = logw[0][:, sl]
        lw1_o[p] = logw[1][:, sl]
        a0_o[p] = aa[0][:, sl]
        a1_o[p] = aa[1][:, sl]


def _rwkv_prep(pb, B, T, mu, w0, w2, a0, a2, g2, k_k, tt=128):
    M = pb.shape[0]
    tt = min(tt, T)
    nt = T // tt
    full = lambda shape: pl.BlockSpec(shape, lambda i: tuple(0 for _ in shape))
    in_specs = _halo_specs(tt, B_IN, M, lambda i: 0) + [
        full((2, B_IN)), full((2, B_W)), full((2, B_DECAY_LORA, B_W)), full((2, B_W)),
        full((2, B_AAA_LORA, B_W)), full((B_GATE_LORA, B_W)), full((1, B_W))]
    ospec = pl.BlockSpec((None, NPAIR, tt, LANES), lambda i: (i // nt, 0, i % nt, 0))
    oshape = jax.ShapeDtypeStruct((B, NPAIR, T, LANES), F32)
    return pl.pallas_call(
        functools.partial(_rwkv_prep_kernel, tt=tt, T=T),
        grid=(M // tt,), in_specs=in_specs, out_specs=[ospec] * 9, out_shape=[oshape] * 9,
        compiler_params=_cparams(("parallel",)))(pb, pb, pb, mu, w0, w2, a0, a2, g2, k_k.reshape(1, B_W))


def _rwkv_chains(r2, lw2, kd2, v2, kk2, a2, L, rev):
    incl, strict, _ = _tri_masks(L, rev)
    incl_f = incl.astype(F32)
    i2 = lax.broadcasted_iota(jnp.int32, (L, 2 * L), 0)
    j2 = lax.broadcasted_iota(jnp.int32, (L, 2 * L), 1) % L
    incl2 = (j2 >= i2) if rev else (j2 <= i2)
    last = 0 if rev else L - 1
    halves = (slice(0, B_HEAD_DIM), slice(B_HEAD_DIM, 2 * B_HEAD_DIM))
    logp2 = [_dot_hp(incl_f, lw) for lw in lw2]
    chains = []
    for lp, lw, r, kd, v, kk, a in zip(logp2, lw2, r2, kd2, v2, kk2, a2):
        ninv = jnp.exp(-lp)
        b = kk * a
        kt = kk * jnp.exp(lp - lw)
        rt = r * jnp.exp(lp)
        bn, kdn = b * ninv, kd * ninv
        pl_row = lp[last:last + 1, :]
        rem = jnp.exp(pl_row - lp)
        bh, kh = b * rem, kd * rem
        dpl = jnp.exp(pl_row)
        for sl in halves:
            chains.append(dict(
                strict=strict, incl2=incl2, dec=dpl[:, sl], v=v[:, sl],
                lhs=jnp.concatenate([kt[:, sl], rt[:, sl]], axis=0),
                rhs=jnp.concatenate([bn[:, sl], kdn[:, sl]], axis=0),
                rem_rhs=jnp.concatenate([bh[:, sl], kh[:, sl]], axis=0)))
    return chains


def _rwkv_chunk(chains, states, L):
    row = lax.broadcasted_iota(jnp.int32, (L, L), 0)
    col = lax.broadcasted_iota(jnp.int32, (L, L), 1)
    eye_f = (row == col).astype(F32)
    gm = [_dot_hp(ch['lhs'], ch['rhs'], NT) for ch in chains]
    ls = [_dot_hp(ch['lhs'], s, NT) for ch, s in zip(chains, states)]
    akv = [_dot_hp(jnp.where(ch['strict'], g[:L, L:], 0.0), ch['v']) for g, ch in zip(gm, chains)]
    tinv = _inv_unit_tri([jnp.where(ch['strict'], g[:L, :L], 0.0) for g, ch in zip(gm, chains)], eye_f, L)
    u = [-_dot_hp(t, l[:L] + x) for t, l, x in zip(tinv, ls, akv)]
    uv = [jnp.concatenate([x, ch['v']], axis=0) for x, ch in zip(u, chains)]
    y = [l[L:] + _dot_hp(jnp.where(ch['incl2'], g[L:, :], 0.0), x) for l, g, x, ch in zip(ls, gm, uv, chains)]
    s_new = [s * ch['dec'] + _dot_hp(x, ch['rem_rhs'], TN) for s, ch, x in zip(states, chains, uv)]
    return y, s_new


def _rwkv_scan_kernel(rf_ref, vf_ref, kkf_ref, kf_ref, lwf_ref, af_ref, rr_ref, vr_ref, kkr_ref, kr_ref, lwr_ref,
                      ar_ref, ka_ref, s0_ref, yf_ref, yr_ref, sf_ref, s_ref, *, L):
    c = pl.program_id(1)

    @pl.when(c == 0)
    def _():
        s_ref[...] = s0_ref[...]

    pairs = range(NPAIR)
    chains = []
    for rev, (r_ref, v_ref, kk_ref, k_ref, lw_ref, a_ref) in enumerate(
            ((rf_ref, vf_ref, kkf_ref, kf_ref, lwf_ref, af_ref), (rr_ref, vr_ref, kkr_ref, kr_ref, lwr_ref, ar_ref))):
        a2 = [a_ref[p] for p in pairs]
        kd2 = [k_ref[p] * (1.0 + (a - 1.0) * ka_ref[p]) for p, a in zip(pairs, a2)]
        chains += _rwkv_chains([r_ref[p] for p in pairs], [lw_ref[p] for p in pairs], kd2,
                               [v_ref[p] for p in pairs], [kk_ref[p] for p in pairs], a2, L, bool(rev))
    y, s_new = _rwkv_chunk(chains, [s_ref[d, h] for d in range(2) for h in range(B_HEADS)], L)
    for p in pairs:
        yf_ref[p] = jnp.concatenate([y[2 * p], y[2 * p + 1]], axis=1)
        yr_ref[p] = jnp.concatenate([y[B_HEADS + 2 * p], y[B_HEADS + 2 * p + 1]], axis=1)
    for h in range(B_HEADS):
        s_ref[0, h] = s_new[h]
        s_ref[1, h] = s_new[B_HEADS + h]

    @pl.when(c == pl.num_programs(1) - 1)
    def _():
        sf_ref[...] = s_ref[...]


def _rwkv_scan(r, v, kk, k, lws, aas, k_a, s0, L=CHUNK):
    B, _, T, _ = r.shape
    nc = T // L
    fwd = pl.BlockSpec((None, NPAIR, L, LANES), lambda b, c: (b, 0, c, 0))
    bwd = pl.BlockSpec((None, NPAIR, L, LANES), lambda b, c: (b, 0, nc - 1 - c, 0))
    st = pl.BlockSpec((None, 2, B_HEADS, B_HEAD_DIM, B_HEAD_DIM), lambda b, c: (b, 0, 0, 0, 0))
    seq_shape = jax.ShapeDtypeStruct((B, NPAIR, T, LANES), F32)
    return pl.pallas_call(
        functools.partial(_rwkv_scan_kernel, L=L),
        grid=(B, nc),
        in_specs=[fwd] * 6 + [bwd] * 6 + [pl.BlockSpec((NPAIR, 1, LANES), lambda b, c: (0, 0, 0)), st],
        out_specs=[fwd, bwd, st],
        out_shape=[seq_shape, seq_shape, jax.ShapeDtypeStruct((B, 2, B_HEADS, B_HEAD_DIM, B_HEAD_DIM), F32)],
        scratch_shapes=[pltpu.VMEM((2, B_HEADS, B_HEAD_DIM, B_HEAD_DIM), F32)],
        compiler_params=_cparams(("parallel", "arbitrary")))(
            r, v, kk, k, lws[0], aas[0], r, v, kk, k, lws[1], aas[1], k_a.reshape(NPAIR, 1, LANES), s0)


def _rwkv_post_kernel(y0_ref, y1_ref, r_ref, k_ref, v_ref, g_ref, rk_ref, lnw_ref, lnb_ref, o_ref):
    gsum = _pair_sum_matrix()
    inv_n = 1.0 / B_HEAD_DIM
    for p in range(NPAIR):
        y = y0_ref[p] + y1_ref[p]
        mean = _dot_hp(y, gsum) * inv_n
        yc = y - mean
        var = _dot_hp(yc * yc, gsum) * inv_n
        yn = yc * lax.rsqrt(var + B_GN_EPS) * lnw_ref[p] + lnb_ref[p]
        bonus = _dot_hp(r_ref[p] * k_ref[p] * rk_ref[p], gsum) * v_ref[p]
        o_ref[:, p * LANES:(p + 1) * LANES] = ((yn + bonus) * g_ref[p]).astype(o_ref.dtype)


def _rwkv_post(y0, y1, r, k, v, g, r_k, ln_w, ln_b, tt=256):
    B, _, T, _ = r.shape
    tt = min(tt, T)
    nt = T // tt
    seq = pl.BlockSpec((None, NPAIR, tt, LANES), lambda i: (i // nt, 0, i % nt, 0))
    par = pl.BlockSpec((NPAIR, 1, LANES), lambda i: (0, 0, 0))
    return pl.pallas_call(
        _rwkv_post_kernel, grid=(B * nt,),
        in_specs=[seq] * 6 + [par] * 3,
        out_specs=pl.BlockSpec((tt, B_W), lambda i: (i, 0)),
        out_shape=jax.ShapeDtypeStruct((B * T, B_W), BF16),
        compiler_params=_cparams(("parallel",)))(
            y0, y1, r, k, v, g, r_k.reshape(NPAIR, 1, LANES), ln_w.reshape(NPAIR, 1, LANES),
            ln_b.reshape(NPAIR, 1, LANES))


def _rwkv_mixer(pb, B, T, prm, s0):
    r, k, v, kk, g, lw0, lw1, a0, a1 = _rwkv_prep(pb, B, T, prm['mu'], prm['w0'], prm['w2'], prm['a0'],
                                                   prm['a2'], prm['g2'], prm['k_k'])
    if s0 is None:
        s0 = jnp.zeros((B, 2, B_HEADS, B_HEAD_DIM, B_HEAD_DIM), F32)
    y0, y1, s_fin = _rwkv_scan(r, v, kk, k, (lw0, lw1), (a0, a1), prm['k_a'], s0)
    out = _rwkv_post(y0, y1, r, k, v, g, prm['r_k'], prm['ln_w'], prm['ln_b'])
    return out, s_fin


def _gdn_prep_kernel(x_ref, prev_ref, next_ref, cw_ref, o_ref, *, tt, T):
    c = pl.program_id(1)
    x = x_ref[...]
    xp, xn = _shifted(x, prev_ref, next_ref, tt, T)
    y = cw_ref[0:1, :] * xp + cw_ref[1:2, :] * x + cw_ref[2:3, :] * xn
    y = y * _sigmoid(y)

    @pl.when(c == 2)
    def _():
        o_ref[...] = y

    @pl.when(c < 2)
    def _():
        scale = jnp.where(c == 0, C_HEAD_DIM ** -0.5, 1.0).astype(F32)
        for h in range(C_HEADS):
            sl = slice(h * C_HEAD_DIM, (h + 1) * C_HEAD_DIM)
            yh = y[:, sl]
            o_ref[:, sl] = yh * lax.rsqrt(jnp.sum(yh * yh, axis=-1, keepdims=True) + 1e-6) * scale


def _gdn_prep(pc, T, conv_w, tt=256):
    M = pc.shape[0]
    tt = min(tt, T)
    in_specs = _halo_specs(tt, C_W, M, lambda i, c: c) + [pl.BlockSpec((3, C_W), lambda i, c: (0, c))]
    return pl.pallas_call(
        functools.partial(_gdn_prep_kernel, tt=tt, T=T),
        grid=(M // tt, 3), in_specs=in_specs,
        out_specs=pl.BlockSpec((tt, C_W), lambda i, c: (i, c)),
        out_shape=jax.ShapeDtypeStruct((M, 3 * C_W), F32),
        compiler_params=_cparams(("parallel", "arbitrary")))(pc, pc, pc, conv_w)


def _gdn_bg_kernel(x_ref, alog_ref, dtb_ref, o_ref):
    x = x_ref[...]
    lane = lax.broadcasted_iota(jnp.int32, x.shape, 1)
    beta = _sigmoid(x)
    g = -jnp.exp(alog_ref[...]) * _softplus(x + dtb_ref[...])
    o_ref[...] = jnp.where(lane < 2 * C_HEADS, beta, g)


def _gdn_bg(pc, a_log, dt_bias, tt=512):
    M = pc.shape[0]
    tt = min(tt, M)
    nb = 2 * C_HEADS
    pad = lambda p: jnp.concatenate([jnp.zeros((nb,), F32), p.reshape(nb), jnp.zeros((LANES - 2 * nb,), F32)]).reshape(1, LANES)
    return pl.pallas_call(
        _gdn_bg_kernel, grid=(M // tt,),
        in_specs=[pl.BlockSpec((tt, LANES), lambda i: (i, 4 * C_W // LANES)),
                  pl.BlockSpec((1, LANES), lambda i: (0, 0)), pl.BlockSpec((1, LANES), lambda i: (0, 0))],
        out_specs=pl.BlockSpec((tt, LANES), lambda i: (i, 0)),
        out_shape=jax.ShapeDtypeStruct((M, LANES), F32),
        compiler_params=_cparams(("parallel",)))(pc, pad(a_log), pad(dt_bias))


def _gdn_chains(qs, ks, vs, bcol, gcol, grow, L, rev):
    incl, strict, _ = _tri_masks(L, rev)
    i = lax.broadcasted_iota(jnp.int32, (L, L), 0)
    j = lax.broadcasted_iota(jnp.int32, (L, L), 1)
    cum_t = ((i >= j) if rev else (i <= j)).astype(F32)
    gc_cols = _dot_hp(incl.astype(F32), gcol)
    gc_rows = _dot_hp(grow, cum_t)
    last = 0 if rev else L - 1
    chains = []
    for h in range(len(qs)):
        gc = jnp.broadcast_to(gc_cols[:, h:h + 1], (L, LANES))
        beta = bcol[:, h:h + 1]
        kb = ks[h] * beta
        egc = jnp.exp(gc)
        g_end = gc[last:last + 1, :]
        chains.append(dict(
            q=qs[h], k=ks[h], incl=incl, strict=strict, kb=kb, qe=qs[h] * egc,
            decay=jnp.exp(jnp.where(incl, gc[:, :L] - gc_rows[h:h + 1, :], -1e30)),
            rhs=jnp.concatenate([vs[h] * beta, kb * egc], axis=1),
            kend=ks[h] * jnp.exp(g_end - gc), g_end=jnp.exp(g_end)))
    return chains


def _gdn_chunk(chains, states, L):
    r = lax.broadcasted_iota(jnp.int32, (L, L), 0)
    c = lax.broadcasted_iota(jnp.int32, (L, L), 1)
    eye_f = (r == c).astype(F32)
    kk = [_dot_hp(ch['kb'], ch['k'], NT) for ch in chains]
    qk = [_dot_hp(ch['q'], ch['k'], NT) for ch in chains]
    qs_s = [_dot_hp(ch['qe'], s) for ch, s in zip(chains, states)]
    tinv = _inv_unit_tri([jnp.where(ch['strict'], x * ch['decay'], 0.0) for ch, x in zip(chains, kk)], eye_f, L)
    sol = [_dot_hp(t, ch['rhs']) for t, ch in zip(tinv, chains)]
    v_new = [x[:, :C_HEAD_DIM] - _dot_hp(x[:, C_HEAD_DIM:], s) for x, s in zip(sol, states)]
    o = [a + _dot_hp(jnp.where(ch['incl'], x * ch['decay'], 0.0), vn)
         for a, x, ch, vn in zip(qs_s, qk, chains, v_new)]
    s_new = [s * ch['g_end'] + _dot_hp(ch['kend'], vn, TN) for s, ch, vn in zip(states, chains, v_new)]
    return o, s_new


def _gdn_scan_kernel(qf_ref, kf_ref, vf_ref, qr_ref, kr_ref, vr_ref, bf_ref, gf_ref, rf_ref, br_ref, gr_ref, rr_ref,
                     s0_ref, of_ref, or_ref, sf_ref, s_ref, *, L):
    c = pl.program_id(1)

    @pl.when(c == 0)
    def _():
        s_ref[...] = s0_ref[...]

    heads = range(C_HEADS)
    hs = [slice(h * C_HEAD_DIM, (h + 1) * C_HEAD_DIM) for h in heads]
    chains = _gdn_chains([qf_ref[:, sl] for sl in hs], [kf_ref[:, sl] for sl in hs], [vf_ref[:, sl] for sl in hs],
                         bf_ref[...], gf_ref[...], rf_ref[...], L, False)
    chains += _gdn_chains([qr_ref[:, sl] for sl in hs], [kr_ref[:, sl] for sl in hs], [vr_ref[:, sl] for sl in hs],
                          br_ref[...], gr_ref[...], rr_ref[...], L, True)
    o, s_new = _gdn_chunk(chains, [s_ref[d, h] for d in range(2) for h in heads], L)
    for h in heads:
        of_ref[:, hs[h]] = o[h]
        or_ref[:, hs[h]] = o[C_HEADS + h]
        s_ref[0, h] = s_new[h]
        s_ref[1, h] = s_new[C_HEADS + h]

    @pl.when(c == pl.num_programs(1) - 1)
    def _():
        sf_ref[...] = s_ref[...]


def _gdn_scan(qkv, bcols, gcols, grows, s0, B, T, L=CHUNK):
    nc = T // L
    fwd = lambda c: c
    bwd = lambda c: nc - 1 - c
    seq = lambda col, pos: pl.BlockSpec((L, C_W), lambda b, c: (b * nc + pos(c), col))
    small = lambda pos: pl.BlockSpec((L, C_HEADS), lambda b, c: (b * nc + pos(c), 0))
    rows = lambda pos: pl.BlockSpec((None, C_HEADS, L), lambda b, c: (b * nc + pos(c), 0, 0))
    st = pl.BlockSpec((None, 2, C_HEADS, C_HEAD_DIM, C_HEAD_DIM), lambda b, c: (b, 0, 0, 0, 0))
    return pl.pallas_call(
        functools.partial(_gdn_scan_kernel, L=L),
        grid=(B, nc),
        in_specs=[seq(0, fwd), seq(1, fwd), seq(2, fwd), seq(0, bwd), seq(1, bwd), seq(2, bwd),
                  small(fwd), small(fwd), rows(fwd), small(bwd), small(bwd), rows(bwd), st],
        out_specs=[pl.BlockSpec((L, C_W), lambda b, c: (b * nc + c, 0)),
                   pl.BlockSpec((L, C_W), lambda b, c: (b * nc + nc - 1 - c, 0)), st],
        out_shape=[jax.ShapeDtypeStruct((B * T, C_W), F32), jax.ShapeDtypeStruct((B * T, C_W), F32),
                   jax.ShapeDtypeStruct((B, 2, C_HEADS, C_HEAD_DIM, C_HEAD_DIM), F32)],
        scratch_shapes=[pltpu.VMEM((2, C_HEADS, C_HEAD_DIM, C_HEAD_DIM), F32)],
        compiler_params=_cparams(("parallel", "arbitrary")))(
            qkv, qkv, qkv, qkv, qkv, qkv, bcols[0], gcols[0], grows[0], bcols[1], gcols[1], grows[1], s0)


def _gdn_post_kernel(o0_ref, o1_ref, z_ref, g_ref, out_ref):
    for h in range(C_HEADS):
        sl = slice(h * C_HEAD_DIM, (h + 1) * C_HEAD_DIM)
        y = o0_ref[:, sl] + o1_ref[:, sl]
        y = y * lax.rsqrt(jnp.mean(y * y, axis=-1, keepdims=True) + NORM_EPS) * g_ref[...]
        z = z_ref[:, sl]
        out_ref[:, sl] = (y * (z * _sigmoid(z))).astype(out_ref.dtype)


def _gdn_post(o0, o1, pc, norm_g, tt=256):
    M = o0.shape[0]
    tt = min(tt, M)
    return pl.pallas_call(
        _gdn_post_kernel, grid=(M // tt,),
        in_specs=[pl.BlockSpec((tt, C_W), lambda i: (i, 0)), pl.BlockSpec((tt, C_W), lambda i: (i, 0)),
                  pl.BlockSpec((tt, C_W), lambda i: (i, 3)), pl.BlockSpec((1, C_HEAD_DIM), lambda i: (0, 0))],
        out_specs=pl.BlockSpec((tt, C_W), lambda i: (i, 0)),
        out_shape=jax.ShapeDtypeStruct((M, C_W), BF16),
        compiler_params=_cparams(("parallel",)))(o0, o1, pc, norm_g.reshape(1, C_HEAD_DIM))


def _gdn_mixer(pc, B, T, prm, s0, L=CHUNK):
    nc = T // L
    qkv = _gdn_prep(pc, T, prm['conv'])
    bg = _gdn_bg(pc, prm['a_log'], prm['dt_bias'])
    if s0 is None:
        s0 = jnp.zeros((B, 2, C_HEADS, C_HEAD_DIM, C_HEAD_DIM), F32)
    bcols = [bg[:, d * C_HEADS:(d + 1) * C_HEADS] for d in range(2)]
    gcols = [bg[:, (2 + d) * C_HEADS:(3 + d) * C_HEADS] for d in range(2)]
    grows = [g.reshape(B * nc, L, C_HEADS).transpose(0, 2, 1) for g in gcols]
    o_fwd, o_bwd, s_fin = _gdn_scan(qkv, bcols, gcols, grows, s0, B, T)
    return _gdn_post(o_fwd, o_bwd, pc, prm['norm_g']), s_fin


def _top_values(tile, count, need_mask):
    rows = lax.broadcasted_iota(jnp.int32, tile.shape, 0).astype(F32)
    big = float(tile.shape[0])
    cur = tile
    vals = []
    for _ in range(count):
        m = jnp.max(cur, axis=0, keepdims=True)
        vals.append(m)
        idx = jnp.min(jnp.where(cur == m, rows, big), axis=0, keepdims=True)
        cur = jnp.where(rows == idx, -jnp.inf, cur)
    return vals, (cur == -jnp.inf) if need_mask else None


def _top_values_untied(tile, count):
    cur = tile
    vals = []
    for _ in range(count):
        m = jnp.max(cur, axis=0, keepdims=True)
        vals.append(m)
        cur = jnp.where(cur == m, -jnp.inf, cur)
    return vals, cur == -jnp.inf


PEER_EB = 1024
PEER_GRP = MXU_K // PEER_HALF
PEER_NGRP = 2 * PEER_HEADS // PEER_GRP


def _peer_topk_kernel(q_ref, kh_ref, kl_ref, thr_ref, e0_ref, e1_ref, s1_ref, sc_ref):
    rows_g = PEER_GRP * N_KEYS
    for g in range(PEER_NGRP):
        qh, ql = _split(q_ref[:, g * MXU_K:(g + 1) * MXU_K])
        kh = kh_ref[g * rows_g:(g + 1) * rows_g, :]
        kl = kl_ref[g * rows_g:(g + 1) * rows_g, :]
        sc_ref[g * rows_g:(g + 1) * rows_g, :] = _dg(kh, qh, NT) + (_dg(kh, ql, NT) + _dg(kl, qh, NT))

    def route(h, t0, t1, exact):
        if exact:
            a, ra = _top_values(t0, PEER_TOPK, True)
            b, rb = _top_values(t1, PEER_TOPK, True)
        else:
            a, ra = _top_values_untied(t0, PEER_TOPK)
            b, rb = _top_values_untied(t1, PEER_TOPK)
        s0 = jnp.where(ra, t0, -jnp.inf)
        s1 = jnp.where(rb, t1, -jnp.inf)
        amat = jnp.concatenate(a, axis=0)
        bmat = jnp.concatenate(b, axis=0)
        b_lo = bmat[:8]
        row8 = lax.broadcasted_iota(jnp.int32, b_lo.shape, 0)
        pieces = [a[0] + b_lo, a[0] + bmat[8:], a[1] + b_lo]
        n_pad = 0
        for r in range(2, 8):
            pieces.append(jnp.where(row8 < PEER_TOPK // (r + 1), a[r] + b_lo, -jnp.inf))
            n_pad += 8 - PEER_TOPK // (r + 1)
        pieces.append(amat[8:] + b[0])
        cand = jnp.concatenate(pieces, axis=0)
        tied = None
        if exact:
            best, _ = _top_values(cand, PEER_TOPK + 1, False)
        else:
            best, rc = _top_values_untied(cand, PEER_TOPK + 1)
            count = lambda mask: jnp.sum(jnp.where(mask, 1.0, 0.0), axis=0, keepdims=True)
            wrong = jnp.logical_or(jnp.logical_or(count(ra) != PEER_TOPK, count(rb) != PEER_TOPK),
                                   count(rc) != PEER_TOPK + 1 + n_pad)
            tied = jnp.max(jnp.where(wrong, 1.0, 0.0)) > 0.0
        z = jnp.zeros_like(best[0])
        for n in range(PEER_TOPK):
            z = z + jnp.exp(best[n] - best[0])
        tau = 0.5 * (best[PEER_TOPK - 1] + best[PEER_TOPK])
        thr = tau - s0
        e0 = jnp.exp(s0 - a[0]) * (0.5 / z)
        e1 = jnp.exp(s1 - b[0])
        for tc in range(t0.shape[1] // LANES):
            cols = slice(tc * LANES, (tc + 1) * LANES)
            thr_ref[h, tc] = thr[:, cols]
            e0_ref[h, tc] = e0[:, cols]
            e1_ref[h, tc] = e1[:, cols]
            s1_ref[h, tc] = s1[:, cols]
        return tied

    def head(h, carry):
        t0 = sc_ref[pl.ds(pl.multiple_of(h * 2 * N_KEYS, N_KEYS), N_KEYS), :]
        t1 = sc_ref[pl.ds(pl.multiple_of(h * 2 * N_KEYS + N_KEYS, N_KEYS), N_KEYS), :]
        tied = route(h, t0, t1, exact=False)
        pl.when(tied)(lambda: route(h, t0, t1, exact=True))
        return carry

    lax.fori_loop(0, PEER_HEADS, head, 0)


def _peer_topk(q, kdt_hi, kdt_lo, tt=256):
    M = q.shape[0]
    tt = min(tt, M)
    R = PEER_HEADS * 2 * N_KEYS
    big = pl.BlockSpec((PEER_HEADS, tt // LANES, N_KEYS, LANES), lambda i: (0, i, 0, 0))
    return pl.pallas_call(
        _peer_topk_kernel, grid=(M // tt,),
        in_specs=[pl.BlockSpec((tt, PEER_HEADS * PEER_QDIM), lambda i: (i, 0)),
                  pl.BlockSpec((R, MXU_K), lambda i: (0, 0)),
                  pl.BlockSpec((R, MXU_K), lambda i: (0, 0))],
        out_specs=[big] * 4,
        out_shape=[jax.ShapeDtypeStruct((PEER_HEADS, M // LANES, N_KEYS, LANES), F32)] * 4,
        scratch_shapes=[pltpu.VMEM((R, tt), F32)],
        compiler_params=_cparams(("parallel",)))(q, kdt_hi, kdt_lo)


def _peer_expert_kernel(xt_ref, u_ref, vt_ref, thr_ref, e0_ref, e1_ref, s1_ref, res_ref, m_ref, o_ref,
                        acc_ref, act0_ref, act1_ref, am0_ref, am1_ref, *, ni, gate_slot):
    e = pl.program_id(1)
    n_blocks = pl.num_programs(1) - 2
    tt = xt_ref.shape[1]
    even = e % 2 == 0
    odd = e % 2 == 1

    @pl.when(e == 0)
    def _():
        acc_ref[...] = jnp.zeros_like(acc_ref)

    def project(act_w):
        act_w[...] = _dg(u_ref[...], xt_ref[...])

    def weigh(act_r, am_w):
        first = (e - 1) * ni
        for ii in range(ni):
            i = first + ii
            rows = slice(ii * N_KEYS, (ii + 1) * N_KEYS)
            for tc in range(tt // LANES):
                cols = slice(tc * LANES, (tc + 1) * LANES)
                w = None
                for h in range(PEER_HEADS):
                    keep = s1_ref[h, tc] >= thr_ref[h, tc, pl.ds(i, 1), :]
                    contrib = jnp.where(keep, e1_ref[h, tc], 0.0) * e0_ref[h, tc, pl.ds(i, 1), :]
                    w = contrib if w is None else w + contrib
                a = act_r[rows, cols]
                a = a * (1.0 + lax.erf(a * (2.0 ** -0.5)))
                am_w[rows, cols] = (a * w).astype(BF16)

    def accumulate(am_r):
        acc_ref[...] += _dg(vt_ref[...], am_r[...])

    has_project = e < n_blocks
    has_weigh = jnp.logical_and(e >= 1, e <= n_blocks)
    has_accumulate = e >= 2
    pl.when(jnp.logical_and(has_project, even))(lambda: project(act0_ref))
    pl.when(jnp.logical_and(has_project, odd))(lambda: project(act1_ref))
    pl.when(jnp.logical_and(has_weigh, even))(lambda: weigh(act1_ref, am1_ref))
    pl.when(jnp.logical_and(has_weigh, odd))(lambda: weigh(act0_ref, am0_ref))
    pl.when(jnp.logical_and(has_accumulate, even))(lambda: accumulate(am0_ref))
    pl.when(jnp.logical_and(has_accumulate, odd))(lambda: accumulate(am1_ref))

    @pl.when(e == pl.num_programs(1) - 1)
    def _():
        o_ref[...] = res_ref[...] + m_ref[gate_slot:gate_slot + 1, :] * acc_ref[...].T


def _peer_experts(xn, u_bf, vt_bf, thr, e0, e1, s1m, res, mods, gate_slot, rows_per_mod, tt=512):
    M, D = res.shape
    tt = min(tt, M, rows_per_mod) if mods.shape[0] > 1 else min(tt, M)
    nb, _, eb = vt_bf.shape
    ni = eb // N_KEYS
    once = pl.Buffered(1)
    if mods.shape[0] > 1:
        mspec = pl.BlockSpec((None, 6, D), lambda i, e: ((i * tt) // rows_per_mod, 0, 0), pipeline_mode=once)
    else:
        mspec = pl.BlockSpec((None, 6, D), lambda i, e: (0, 0, 0), pipeline_mode=once)
    big = pl.BlockSpec((PEER_HEADS, tt // LANES, N_KEYS, LANES), lambda i, e: (0, i, 0, 0), pipeline_mode=once)
    return pl.pallas_call(
        functools.partial(_peer_expert_kernel, ni=ni, gate_slot=gate_slot),
        grid=(M // tt, nb + 2),
        in_specs=[pl.BlockSpec((D, tt), lambda i, e: (0, i), pipeline_mode=once),
                  pl.BlockSpec((eb, D), lambda i, e: (jnp.minimum(e, nb - 1), 0)),
                  pl.BlockSpec((None, D, eb), lambda i, e: (jnp.clip(e - 2, 0, nb - 1), 0, 0)),
                  big, big, big, big,
                  pl.BlockSpec((tt, D), lambda i, e: (i, 0), pipeline_mode=once), mspec],
        out_specs=pl.BlockSpec((tt, D), lambda i, e: (i, 0)),
        out_shape=jax.ShapeDtypeStruct((M, D), F32),
        scratch_shapes=[pltpu.VMEM((D, tt), F32), pltpu.VMEM((eb, tt), F32), pltpu.VMEM((eb, tt), F32),
                        pltpu.VMEM((eb, tt), BF16), pltpu.VMEM((eb, tt), BF16)],
        compiler_params=_cparams(("parallel", "arbitrary"), VMEM_LIMIT_BIG))(
            xn, u_bf, vt_bf, thr, e0, e1, s1m, res, mods)


def _peer(x, mods, rows_per_mod, norm_g, prm):
    q, xn = _mm_norm(x, prm['wq'], norm_g=norm_g, mods=mods, slots=(3, 4), rows_per_mod=rows_per_mod,
                     hp=True, emit_xn=BF16)
    thr, e0, e1, s1m = _peer_topk(q, prm['kdt_hi'], prm['kdt_lo'])
    return _peer_experts(xn.T, prm['u'], prm['vt'], thr, e0, e1, s1m, x, mods, 5, rows_per_mod)


def _peer_params(wq, keys, u_tab, v_tab):
    nset = 2 * PEER_HEADS
    kd = jnp.zeros((nset, N_KEYS, PEER_GRP, PEER_HALF), F32)
    sets = jnp.arange(nset)
    kd = kd.at[sets, :, sets % PEER_GRP, :].set(keys.reshape(nset, N_KEYS, PEER_HALF))
    kd_hi, kd_lo = _split(kd.reshape(nset * N_KEYS, MXU_K))
    vt = v_tab.astype(BF16).reshape(N_EXPERTS // PEER_EB, PEER_EB, -1).transpose(0, 2, 1)
    return dict(wq=_split(wq), kdt_hi=kd_hi, kdt_lo=kd_lo, u=u_tab.astype(BF16), vt=vt)


def _layer_even(x, mods, B, T, norm1, w_a, w_b, w_out, qk_gains, rwkv_prm, ctx):
    rpm = T
    pa = _mm_norm(x, w_a, norm_g=norm1, mods=mods, slots=(0, 1), rows_per_mod=rpm)
    pb = _mm_norm(x, w_b, norm_g=norm1, mods=mods, slots=(0, 1), rows_per_mod=rpm)
    nqk = A_HEADS + A_KV_HEADS
    assert A_HEADS // A_KV_HEADS == ATT_GROUP
    scale = A_HEAD_DIM ** -0.5
    if ctx is None:
        qk = _norm_rope(pa, nqk, T, gains=qk_gains)
        oa = _attention(B, T, T, A_HEADS, scale, (qk, 0), (qk, A_HEADS), (pa, nqk))
        ctx_out = (qk[:, A_Q:A_Q + A_KV], pa[:, A_Q + A_KV:])
        s0 = None
    else:
        cache_k, cache_v, s0 = ctx
        P = cache_k.shape[1]
        qk = _norm_rope(pa, nqk, T, gains=qk_gains, tables=_rope_tables(T, A_HEAD_DIM))
        k_all = jnp.concatenate([qk[:, A_Q:].reshape(B, T, A_KV), cache_k.reshape(B, P, A_KV)], axis=1)
        v_all = jnp.concatenate([pa[:, A_Q + A_KV:].reshape(B, T, A_KV), cache_v.reshape(B, P, A_KV)], axis=1)
        Tk = T + P
        oa = _attention(B, T, Tk, A_HEADS, scale, (qk, 0), (k_all.reshape(B * Tk, A_KV).astype(BF16), 0),
                        (v_all.reshape(B * Tk, A_KV).astype(BF16), 0))
        ctx_out = None
    ob, s_fin = _rwkv_mixer(pb, B, T, rwkv_prm, s0)
    x = _mm_res([oa, ob], w_out, x, mods, 2, rpm)
    return x, ctx_out, s_fin


D_QROPE_BLK = 0
D_KROPE_BLK = D_HEADS
D_NROPE_BLK = D_HEADS + 1
D_CKV_BLK = 12
D_QNOPE_BLK = 16
D_PROJ = 24 * LANES


def _mla_in_weight(w_d):
    K = w_d.shape[0]
    wq = w_d[:, :D_Q].reshape(K, D_HEADS, D_QK)
    z64 = jnp.zeros((K, D_HEADS, LANES - D_ROPE), w_d.dtype)
    q_rope = jnp.concatenate([wq[:, :, D_NOPE:], z64], axis=-1).reshape(K, D_HEADS * LANES)
    q_nope = wq[:, :, :D_NOPE].reshape(K, D_HEADS * LANES)
    k_rope = jnp.concatenate([w_d[:, D_Q + D_KV_RANK:], jnp.zeros((K, LANES - D_ROPE), w_d.dtype)], axis=-1)
    padz = jnp.zeros((K, (D_CKV_BLK - D_NROPE_BLK) * LANES), w_d.dtype)
    return jnp.concatenate([q_rope, k_rope, padz, w_d[:, D_Q:D_Q + D_KV_RANK], q_nope], axis=-1)


def _layer_odd(x, mods, B, T, norm1, w_c, w_d, w_out, gdn_prm, kv_norm_g, w_kv_b, ctx):
    rpm = T
    pc = _mm_norm(x, w_c, norm_g=norm1, mods=mods, slots=(0, 1), rows_per_mod=rpm)
    pd = _mm_norm(x, w_d, norm_g=norm1, mods=mods, slots=(0, 1), rows_per_mod=rpm)
    scale = D_QK ** -0.5
    ckv_col = D_CKV_BLK * LANES // D_KV_RANK
    kv, ckv_n = _mm_norm(pd, w_kv_b, k=D_KV_RANK, xcol=ckv_col, norm_g=kv_norm_g, emit_xn=F32)
    if ctx is None:
        od = _attention(B, T, T, D_HEADS, scale, (pd, D_QNOPE_BLK), (kv, 0), (kv, 0),
                        q2=(pd, D_QROPE_BLK), k2=(pd, D_KROPE_BLK))
        ctx_out = (ckv_n, pd[:, D_KROPE_BLK * LANES:D_KROPE_BLK * LANES + D_ROPE])
        s0 = None
    else:
        s0, c_ckv, c_krope = ctx
        P = c_ckv.shape[1]
        Tk = T + P
        roped = _norm_rope(pd, D_NROPE_BLK, T, tables=_rope_tables(T, D_ROPE))
        kv_ctx = _mm_norm(c_ckv.reshape(B * P, D_KV_RANK), w_kv_b)
        kv_all = jnp.concatenate([kv.reshape(B, T, -1), kv_ctx.reshape(B, P, -1)], axis=1).reshape(B * Tk, -1)
        kv_all = kv_all.astype(BF16)
        kr_lat = roped[:, D_KROPE_BLK * LANES:].reshape(B, T, LANES)
        kr_ctx = jnp.concatenate([c_krope, jnp.zeros((B, P, LANES - D_ROPE), F32)], axis=-1)
        kr_all = jnp.concatenate([kr_lat, kr_ctx], axis=1).reshape(B * Tk, LANES).astype(BF16)
        od = _attention(B, T, Tk, D_HEADS, scale, (pd, D_QNOPE_BLK), (kv_all, 0), (kv_all, 0),
                        q2=(roped, D_QROPE_BLK), k2=(kr_all, 0))
        ctx_out = None
    oc, s_fin = _gdn_mixer(pc, B, T, gdn_prm, s0)
    x = _mm_res([oc, od], w_out, x, mods, 2, rpm)
    return x, ctx_out, s_fin


def _modulation(cond, w_ada, b_ada):
    n = cond.shape[0]
    rows = ((n + 7) // 8) * 8
    a = cond * _sigmoid(cond)
    a = jnp.concatenate([a, jnp.zeros((rows - n, cond.shape[1]), F32)], axis=0)
    m = _mm_norm(a, _split(w_ada), hp=True, tn_cap=1024)[:n] + b_ada
    return m.reshape(n, 6, D_MODEL)


def kernel(x_prompt, x_sample, cache_attn_k, cache_attn_v, state_rwkv, state_gdn, cache_mla_ckv, cache_mla_krope,
           c, c_ctx, norm1_g, norm2_g, ada_w, ada_b, w_in_ab, w_out_ab, attn_q_norm, attn_k_norm,
           rwkv_mu, rwkv_w0, rwkv_w2, rwkv_a0, rwkv_a2, rwkv_g2, rwkv_k_k, rwkv_k_a, rwkv_r_k, rwkv_ln_w, rwkv_ln_b,
           w_in_cd, w_out_cd, gdn_conv, gdn_a_log, gdn_dt_bias, gdn_norm_g, mla_kv_norm_g, mla_w_kv_b,
           peer_wq, peer_keys, peer_u, peer_v, final_norm_g):
    Bp, Tp, D = x_prompt.shape
    Bs, Ts, _ = x_sample.shape
    depth = norm1_g.shape[0]
    xp = x_prompt.reshape(Bp * Tp, D)
    xs = x_sample.reshape(Bs * Ts, D)
    cond = jnp.concatenate([c_ctx[None, :], c], axis=0)
    outs = dict(k=[], v=[], rwkv=[], gdn=[], ckv=[], krope=[])
    for li in range(depth):
        j = li // 2
        mods = _modulation(cond, ada_w[li], ada_b[li])
        mods_p, mods_s = mods[:1], mods[1:]
        peer_prm = _peer_params(peer_wq[li], peer_keys[li], peer_u[li], peer_v[li])
        if li % 2 == 0:
            w_in = w_in_ab[j].astype(BF16)
            w_a, w_b = w_in[:, :A_IN], w_in[:, A_IN:]
            w_o = w_out_ab[j].astype(BF16)
            w_out = [w_o[:A_Q], w_o[A_Q:]]
            gains = jnp.concatenate([jnp.tile(attn_q_norm[j], A_HEADS), jnp.tile(attn_k_norm[j], A_KV_HEADS)])
            rp = dict(mu=rwkv_mu[j], w0=rwkv_w0[j], w2=rwkv_w2[j], a0=rwkv_a0[j], a2=rwkv_a2[j], g2=rwkv_g2[j],
                      k_k=rwkv_k_k[j], k_a=rwkv_k_a[j], r_k=rwkv_r_k[j], ln_w=rwkv_ln_w[j], ln_b=rwkv_ln_b[j])
            xp, (k_c, v_c), s_c = _layer_even(xp, mods_p, Bp, Tp, norm1_g[li], w_a, w_b, w_out, gains, rp, None)
            xs, _, _ = _layer_even(xs, mods_s, Bs, Ts, norm1_g[li], w_a, w_b, w_out, gains, rp,
                                   (cache_attn_k[:, j], cache_attn_v[:, j], state_rwkv[:, j]))
            outs['k'].append(k_c.reshape(Bp, Tp, A_KV_HEADS, A_HEAD_DIM))
            outs['v'].append(v_c.reshape(Bp, Tp, A_KV_HEADS, A_HEAD_DIM))
            outs['rwkv'].append(s_c)
        else:
            w_in = w_in_cd[j]
            w_c = jnp.concatenate([w_in[:, :C_IN], jnp.zeros((D, LANES - 4 * C_HEADS), F32)], axis=-1).astype(BF16)
            w_d = _mla_in_weight(w_in[:, C_IN:]).astype(BF16)
            w_o = w_out_cd[j].astype(BF16)
            w_out = [w_o[:C_W], w_o[C_W:]]
            gp = dict(conv=gdn_conv[j], a_log=gdn_a_log[j], dt_bias=gdn_dt_bias[j], norm_g=gdn_norm_g[j])
            w_kv_b = mla_w_kv_b[j].astype(BF16)
            xp, (ckv_c, kr_c), s_c = _layer_odd(xp, mods_p, Bp, Tp, norm1_g[li], w_c, w_d, w_out, gp,
                                                mla_kv_norm_g[j], w_kv_b, None)
            xs, _, _ = _layer_odd(xs, mods_s, Bs, Ts, norm1_g[li], w_c, w_d, w_out, gp, mla_kv_norm_g[j], w_kv_b,
                                  (state_gdn[:, j], cache_mla_ckv[:, j], cache_mla_krope[:, j]))
            outs['gdn'].append(s_c)
            outs['ckv'].append(ckv_c.reshape(Bp, Tp, D_KV_RANK))
            outs['krope'].append(kr_c.reshape(Bp, Tp, D_ROPE))
        xp = _peer(xp, mods_p, Tp, norm2_g[li], peer_prm)
        xs = _peer(xs, mods_s, Ts, norm2_g[li], peer_prm)
    y_prompt = _rmsnorm(xp, final_norm_g).reshape(Bp, Tp, D)
    y_sample = _rmsnorm(xs, final_norm_g).reshape(Bs, Ts, D)
    st = lambda name: jnp.stack(outs[name], axis=1)
    return (y_prompt, y_sample, st('k'), st('v'), st('rwkv'), st('gdn'), st('ckv'), st('krope'))
```
